```python
import math, functools
import jax, jax.numpy as jnp
from jax import lax
import numpy as np

D_MODEL = 1024
BATCH = 8
SEQ = 8192
DEPTH = 2

ATT_HEADS = 8
ATT_HEAD_DIM = 64
ATT_WIDTH = ATT_HEADS * ATT_HEAD_DIM
ATT_PATTERNS = ((128, 1), (512, 4), (2048, 16))
ATT_BLOCK = 128
SSD_HEADS = 8
SSD_HEAD_DIM = 64
SSD_WIDTH = SSD_HEADS * SSD_HEAD_DIM
SSD_GROUPS = 2
SSD_STATE = 128
SSD_CHUNK = 128
SSD_CONV_CH = SSD_WIDTH + 2 * SSD_GROUPS * SSD_STATE
DT_MIN = 0.001
DT_MAX = 0.1
LRU_WIDTH = 512
LRU_BLOCKS = 8
LRU_BLOCK_W = LRU_WIDTH // LRU_BLOCKS
LRU_C = 8.0
CONV_WIDTH = 4
D_MIX = ATT_WIDTH + SSD_WIDTH + LRU_WIDTH
IN_COLS = 3 * ATT_WIDTH + SSD_WIDTH + SSD_CONV_CH + SSD_HEADS + 2 * LRU_WIDTH
D_FF = ((8 * D_MODEL + 3 * 256 - 1) // (3 * 256)) * 256
NORM_EPS = 1e-6
SSD_NORM_EPS = 1e-5

kernel_name = 'hybrid_dilated_attn_ssd_rglru_block'


def rmsnorm(x, g, eps=NORM_EPS):
    xf = x.astype(jnp.float32)
    y = xf * lax.rsqrt(jnp.mean(xf * xf, axis=-1, keepdims=True) + eps)
    return (y * g.astype(jnp.float32)).astype(x.dtype)


def causal_depthwise_conv(x, w, b):
    k_width, s = w.shape[0], x.shape[1]
    xp = jnp.pad(x, ((0, 0), (k_width - 1, 0), (0, 0)))
    y = b + xp[:, k_width - 1:] * w[k_width - 1]
    for k in range(k_width - 1):
        y = y + xp[:, k:k + s] * w[k]
    return y


def split_cols(proj):
    sizes = (ATT_WIDTH, ATT_WIDTH, ATT_WIDTH, SSD_WIDTH, SSD_CONV_CH, SSD_HEADS, LRU_WIDTH, LRU_WIDTH)
    offsets = np.cumsum(sizes)[:-1].tolist()
    return jnp.split(proj, offsets, axis=-1)


def dilated_branch(q, k, v, window, dilation, slopes):
    b, s, h, dh = q.shape
    length = s // dilation
    nb = -(-length // ATT_BLOCK)
    pad = nb * ATT_BLOCK - length

    def to_blocks(t):
        t = t.reshape(b, length, dilation, h, dh).transpose(0, 2, 3, 1, 4)
        t = jnp.pad(t, ((0, 0), (0, 0), (0, 0), (0, pad), (0, 0)))
        return t.reshape(b, dilation, h, nb, ATT_BLOCK, dh)

    def with_prev(t):
        prev = jnp.pad(t[:, :, :, :-1], ((0, 0), (0, 0), (0, 0), (1, 0), (0, 0), (0, 0)))
        return jnp.concatenate([prev, t], axis=4)

    qb, kb, vb = to_blocks(q), to_blocks(k), to_blocks(v)
    kk, vv = with_prev(kb), with_prev(vb)
    scores = jnp.einsum('brhnqd,brhnkd->brhnqk', qb, kk,
                        preferred_element_type=jnp.float32) * (ATT_HEAD_DIM ** -0.5)
    qi = jnp.arange(ATT_BLOCK)[:, None]
    ki = jnp.arange(2 * ATT_BLOCK)[None, :]
    dist = ATT_BLOCK + qi - ki
    band = (dist >= 0) & (dist <= window // dilation)
    valid = band[None] & ((jnp.arange(nb)[:, None, None] > 0) | (ki[None] >= ATT_BLOCK))
    alibi = -slopes[:, None, None] * (dilation * dist).astype(jnp.float32)
    scores = scores + alibi[None, None, :, None]
    scores = jnp.where(valid[None, None, None], scores, -jnp.inf)
    m = jnp.max(scores, axis=-1)
    p = jnp.exp(scores - m[..., None])
    den = jnp.sum(p, axis=-1)
    num = jnp.einsum('brhnqk,brhnkd->brhnqd', p, vv.astype(jnp.float32))

    def from_blocks(t):
        t = t.reshape(b, dilation, h, nb * ATT_BLOCK, *t.shape[5:])[:, :, :, :length]
        t = jnp.moveaxis(t, 3, 1)
        return t.reshape(b, s, h, *t.shape[4:])

    return from_blocks(num), from_blocks(m), from_blocks(den)


def dilated_attention(q, k, v):
    b, s, _ = q.shape
    shp = (b, s, ATT_HEADS, ATT_HEAD_DIM)
    q, k, v = q.reshape(shp), k.reshape(shp), v.reshape(shp)
    slopes = jnp.exp2(-8.0 * jnp.arange(1, ATT_HEADS + 1, dtype=jnp.float32) / ATT_HEADS)
    branches = [dilated_branch(q, k, v, w, d, slopes) for (w, d) in ATT_PATTERNS]
    m_all = functools.reduce(jnp.maximum, [br[1] for br in branches])
    num = jnp.zeros(shp, jnp.float32)
    den = jnp.zeros(shp[:3], jnp.float32)
    for n_g, m_g, d_g in branches:
        e = jnp.exp(m_g - m_all)
        num = num + n_g * e[..., None]
        den = den + d_g * e
    return (num / den[..., None]).reshape(b, s, ATT_WIDTH)


def ssd_chunked_scan(x, dt, a, bm, cm):
    b, s, h, p = x.shape
    n = bm.shape[-1]
    q = SSD_CHUNK
    nc = s // q
    rep = h // bm.shape[2]
    x = x.reshape(b, nc, q, h, p)
    dt = dt.reshape(b, nc, q, h)
    bh = jnp.repeat(bm, rep, axis=2).reshape(b, nc, q, h, n)
    ch = jnp.repeat(cm, rep, axis=2).reshape(b, nc, q, h, n)
    acs = jnp.cumsum(dt * a, axis=2)
    seg = acs[:, :, :, None, :] - acs[:, :, None, :, :]
    causal = jnp.tril(jnp.ones((q, q), dtype=bool))[None, None, :, :, None]
    lmat = jnp.exp(jnp.where(causal, seg, -jnp.inf))
    scores = jnp.einsum('bcihn,bcjhn->bcijh', ch, bh) * lmat * dt[:, :, None, :, :]
    y_diag = jnp.einsum('bcijh,bcjhp->bcihp', scores, x)
    decay_to_end = jnp.exp(acs[:, :, -1:, :] - acs)
    states = jnp.einsum('bcjhn,bcjh,bcjhp->bchpn', bh, decay_to_end * dt, x)
    chunk_decay = jnp.exp(acs[:, :, -1, :])

    def step(carry, inp):
        st, dc = inp
        return dc[:, :, None, None] * carry + st, carry

    _, prev = lax.scan(step, jnp.zeros((b, h, p, n), x.dtype),
                       (jnp.moveaxis(states, 1, 0), jnp.moveaxis(chunk_decay, 1, 0)))
    prev = jnp.moveaxis(prev, 0, 1)
    y_off = jnp.einsum('bcihn,bchpn,bcih->bcihp', ch, prev, jnp.exp(acs))
    return (y_diag + y_off).reshape(b, s, h, p)


def ssd_mixer(z, xbc, dt_raw, conv_w, conv_b, dt_bias, a_log, d_skip, norm_w):
    b, s, _ = z.shape
    f32 = jnp.float32
    xbc = jax.nn.silu(causal_depthwise_conv(xbc, conv_w, conv_b)).astype(f32)
    xs, bm, cm = jnp.split(xbc, [SSD_WIDTH, SSD_WIDTH + SSD_GROUPS * SSD_STATE], axis=-1)
    xs = xs.reshape(b, s, SSD_HEADS, SSD_HEAD_DIM)
    bm = bm.reshape(b, s, SSD_GROUPS, SSD_STATE)
    cm = cm.reshape(b, s, SSD_GROUPS, SSD_STATE)
    dt = jax.nn.softplus(dt_raw.astype(f32) + dt_bias.astype(f32))
    a = -jnp.exp(a_log.astype(f32))
    y = ssd_chunked_scan(xs, dt, a, bm, cm) + d_skip.astype(f32)[:, None] * xs
    y = y.reshape(b, s, SSD_WIDTH) * jax.nn.silu(z.astype(f32))
    yg = y.reshape(b, s, SSD_GROUPS, SSD_WIDTH // SSD_GROUPS)
    yg = yg * lax.rsqrt(jnp.mean(yg * yg, axis=-1, keepdims=True) + SSD_NORM_EPS)
    return yg.reshape(b, s, SSD_WIDTH) * norm_w.astype(f32)


def rglru_mixer(gate_in, x_in, conv_w, conv_b, wa, ba, wx, bx, lam):
    b, s, _ = x_in.shape
    f32 = jnp.float32
    gate = jax.nn.gelu(gate_in.astype(f32))
    xc = causal_depthwise_conv(x_in, conv_w, conv_b).astype(f32)
    xb = xc.reshape(b, s, LRU_BLOCKS, LRU_BLOCK_W)
    r = jax.nn.sigmoid(jnp.einsum('bsnc,ncd->bsnd', xb, wa.astype(f32)).reshape(b, s, LRU_WIDTH) + ba.astype(f32))
    i = jax.nn.sigmoid(jnp.einsum('bsnc,ncd->bsnd', xb, wx.astype(f32)).reshape(b, s, LRU_WIDTH) + bx.astype(f32))
    log_a = -LRU_C * r * jax.nn.softplus(-lam.astype(f32))
    a = jnp.exp(log_a)
    u = jnp.sqrt(-jnp.expm1(2.0 * log_a)) * (i * xc)

    def combine(left, right):
        a_l, h_l = left
        a_r, h_r = right
        return a_l * a_r, a_r * h_l + h_r

    _, h = lax.associative_scan(combine, (a, u), axis=1)
    return h * gate


def swiglu(h, w_gate, w_up, w_down):
    return (jax.nn.silu(h @ w_gate) * (h @ w_up)) @ w_down


def hybrid_layer(x, norm_mix, w_in, ssd_conv_w, ssd_conv_b, ssd_dt_bias, ssd_a_log, ssd_d, ssd_norm,
                 lru_conv_w, lru_conv_b, lru_wa, lru_ba, lru_wx, lru_bx, lru_lambda, w_out,
                 norm_ffn, w_gate, w_up, w_down):
    h = rmsnorm(x, norm_mix)
    q, k, v, z, xbc, dt_raw, g_lru, x_lru = split_cols(h @ w_in)
    att = dilated_attention(q, k, v).astype(x.dtype)
    ssd = ssd_mixer(z, xbc, dt_raw, ssd_conv_w, ssd_conv_b, ssd_dt_bias, ssd_a_log, ssd_d, ssd_norm).astype(x.dtype)
    lru = rglru_mixer(g_lru, x_lru, lru_conv_w, lru_conv_b, lru_wa, lru_ba, lru_wx, lru_bx, lru_lambda).astype(x.dtype)
    x = x + jnp.concatenate([att, ssd, lru], axis=-1) @ w_out
    x = x + swiglu(rmsnorm(x, norm_ffn), w_gate, w_up, w_down)
    return x


def _fwd_setup_inputs(seed: int = 0) -> dict:
    key = jax.random.key(seed)
    ks = jax.random.split(key, 24)
    f32 = jnp.float32
    L = DEPTH

    def nrm(k, shape, scale):
        return scale * jax.random.normal(k, shape, f32)

    x = nrm(ks[0], (BATCH, SEQ, D_MODEL), 1.0)
    norm_mix = 1.0 + nrm(ks[1], (L, D_MODEL), 0.05)
    w_in = nrm(ks[2], (L, D_MODEL, IN_COLS), D_MODEL ** -0.5)
    ssd_conv_w = nrm(ks[3], (L, CONV_WIDTH, SSD_CONV_CH), CONV_WIDTH ** -0.5)
    ssd_conv_b = nrm(ks[4], (L, SSD_CONV_CH), 0.02)
    dt0 = jnp.exp(jax.random.uniform(ks[5], (L, SSD_HEADS), f32, math.log(DT_MIN), math.log(DT_MAX)))
    ssd_dt_bias = dt0 + jnp.log(-jnp.expm1(-dt0))
    ssd_a_log = jnp.log(jax.random.uniform(ks[6], (L, SSD_HEADS), f32, 1.0, 16.0))
    ssd_d = 1.0 + nrm(ks[7], (L, SSD_HEADS), 0.1)
    ssd_norm = 1.0 + nrm(ks[8], (L, SSD_WIDTH), 0.05)
    lru_conv_w = nrm(ks[9], (L, CONV_WIDTH, LRU_WIDTH), CONV_WIDTH ** -0.5)
    lru_conv_b = nrm(ks[10], (L, LRU_WIDTH), 0.02)
    lru_wa = nrm(ks[11], (L, LRU_BLOCKS, LRU_BLOCK_W, LRU_BLOCK_W), LRU_BLOCK_W ** -0.5)
    lru_ba = nrm(ks[12], (L, LRU_WIDTH), 0.02)
    lru_wx = nrm(ks[13], (L, LRU_BLOCKS, LRU_BLOCK_W, LRU_BLOCK_W), LRU_BLOCK_W ** -0.5)
    lru_bx = nrm(ks[14], (L, LRU_WIDTH), 0.02)
    a_c = jax.random.uniform(ks[15], (L, LRU_WIDTH), f32, 0.9, 0.999)
    a_base = a_c ** (1.0 / LRU_C)
    lru_lambda = jnp.log(a_base) - jnp.log1p(-a_base)
    w_out = nrm(ks[16], (L, D_MIX, D_MODEL), D_MIX ** -0.5)
    norm_ffn = 1.0 + nrm(ks[17], (L, D_MODEL), 0.05)
    w_gate = nrm(ks[18], (L, D_MODEL, D_FF), D_MODEL ** -0.5)
    w_up = nrm(ks[19], (L, D_MODEL, D_FF), D_MODEL ** -0.5)
    w_down = nrm(ks[20], (L, D_FF, D_MODEL), D_FF ** -0.5)
    norm_final = 1.0 + nrm(ks[21], (D_MODEL,), 0.05)
    return {'x': x, 'norm_mix': norm_mix, 'w_in': w_in, 'ssd_conv_w': ssd_conv_w, 'ssd_conv_b': ssd_conv_b,
            'ssd_dt_bias': ssd_dt_bias, 'ssd_a_log': ssd_a_log, 'ssd_d': ssd_d, 'ssd_norm': ssd_norm,
            'lru_conv_w': lru_conv_w, 'lru_conv_b': lru_conv_b, 'lru_wa': lru_wa, 'lru_ba': lru_ba,
            'lru_wx': lru_wx, 'lru_bx': lru_bx, 'lru_lambda': lru_lambda, 'w_out': w_out,
            'norm_ffn': norm_ffn, 'w_gate': w_gate, 'w_up': w_up, 'w_down': w_down, 'norm_final': norm_final}


def _fwd_reference(x, norm_mix, w_in, ssd_conv_w, ssd_conv_b, ssd_dt_bias, ssd_a_log, ssd_d, ssd_norm,
              lru_conv_w, lru_conv_b, lru_wa, lru_ba, lru_wx, lru_bx, lru_lambda, w_out,
              norm_ffn, w_gate, w_up, w_down, norm_final):
    for l in range(DEPTH):
        x = hybrid_layer(x, norm_mix[l], w_in[l], ssd_conv_w[l], ssd_conv_b[l], ssd_dt_bias[l], ssd_a_log[l],
                         ssd_d[l], ssd_norm[l], lru_conv_w[l], lru_conv_b[l], lru_wa[l], lru_ba[l],
                         lru_wx[l], lru_bx[l], lru_lambda[l], w_out[l], norm_ffn[l], w_gate[l], w_up[l], w_down[l])
    return rmsnorm(x, norm_final)


import jax as _jax
import jax.numpy as _jnp

TWIN_FORMAT = 'train_step'
FWD_PARAMS = ['x', 'norm_mix', 'w_in', 'ssd_conv_w', 'ssd_conv_b', 'ssd_dt_bias', 'ssd_a_log', 'ssd_d', 'ssd_norm', 'lru_conv_w', 'lru_conv_b', 'lru_wa', 'lru_ba', 'lru_wx', 'lru_bx', 'lru_lambda', 'w_out', 'norm_ffn', 'w_gate', 'w_up', 'w_down', 'norm_final']
TWIN_WEIGHTS = ['norm_mix', 'w_in', 'ssd_conv_w', 'ssd_conv_b', 'ssd_dt_bias', 'ssd_a_log', 'ssd_d', 'ssd_norm', 'lru_conv_w', 'lru_conv_b', 'lru_wa', 'lru_ba', 'lru_wx', 'lru_bx', 'lru_lambda', 'w_out', 'norm_ffn', 'w_gate', 'w_up', 'w_down', 'norm_final']
TWIN_DIFF_INPUT = 'x'
TWIN_INPUTS = ['x', 'norm_mix', 'w_in', 'ssd_conv_w', 'ssd_conv_b', 'ssd_dt_bias', 'ssd_a_log', 'ssd_d', 'ssd_norm', 'lru_conv_w', 'lru_conv_b', 'lru_wa', 'lru_ba', 'lru_wx', 'lru_bx', 'lru_lambda', 'w_out', 'norm_ffn', 'w_gate', 'w_up', 'w_down', 'norm_final', 'loss_target', 'm_norm_mix', 'm_w_in', 'm_ssd_conv_w', 'm_ssd_conv_b', 'm_ssd_dt_bias', 'm_ssd_a_log', 'm_ssd_d', 'm_ssd_norm', 'm_lru_conv_w', 'm_lru_conv_b', 'm_lru_wa', 'm_lru_ba', 'm_lru_wx', 'm_lru_bx', 'm_lru_lambda', 'm_w_out', 'm_norm_ffn', 'm_w_gate', 'm_w_up', 'm_w_down', 'm_norm_final', 'v_norm_mix', 'v_w_in', 'v_ssd_conv_w', 'v_ssd_conv_b', 'v_ssd_dt_bias', 'v_ssd_a_log', 'v_ssd_d', 'v_ssd_norm', 'v_lru_conv_w', 'v_lru_conv_b', 'v_lru_wa', 'v_lru_ba', 'v_lru_wx', 'v_lru_bx', 'v_lru_lambda', 'v_w_out', 'v_norm_ffn', 'v_w_gate', 'v_w_up', 'v_w_down', 'v_norm_final']
TWIN_OUTPUTS = ['loss', 'grad_x', 'grad_norm_mix', 'grad_w_in', 'grad_ssd_conv_w', 'grad_ssd_conv_b', 'grad_ssd_dt_bias', 'grad_ssd_a_log', 'grad_ssd_d', 'grad_ssd_norm', 'grad_lru_conv_w', 'grad_lru_conv_b', 'grad_lru_wa', 'grad_lru_ba', 'grad_lru_wx', 'grad_lru_bx', 'grad_lru_lambda', 'grad_w_out', 'grad_norm_ffn', 'grad_w_gate', 'grad_w_up', 'grad_w_down', 'grad_norm_final', 'delta_norm_mix', 'delta_w_in', 'delta_ssd_conv_w', 'delta_ssd_conv_b', 'delta_ssd_dt_bias', 'delta_ssd_a_log', 'delta_ssd_d', 'delta_ssd_norm', 'delta_lru_conv_w', 'delta_lru_conv_b', 'delta_lru_wa', 'delta_lru_ba', 'delta_lru_wx', 'delta_lru_bx', 'delta_lru_lambda', 'delta_w_out', 'delta_norm_ffn', 'delta_w_gate', 'delta_w_up', 'delta_w_down', 'delta_norm_final', 'new_m_norm_mix', 'new_m_w_in', 'new_m_ssd_conv_w', 'new_m_ssd_conv_b', 'new_m_ssd_dt_bias', 'new_m_ssd_a_log', 'new_m_ssd_d', 'new_m_ssd_norm', 'new_m_lru_conv_w', 'new_m_lru_conv_b', 'new_m_lru_wa', 'new_m_lru_ba', 'new_m_lru_wx', 'new_m_lru_bx', 'new_m_lru_lambda', 'new_m_w_out', 'new_m_norm_ffn', 'new_m_w_gate', 'new_m_w_up', 'new_m_w_down', 'new_m_norm_final', 'new_v_norm_mix', 'new_v_w_in', 'new_v_ssd_conv_w', 'new_v_ssd_conv_b', 'new_v_ssd_dt_bias', 'new_v_ssd_a_log', 'new_v_ssd_d', 'new_v_ssd_norm', 'new_v_lru_conv_w', 'new_v_lru_conv_b', 'new_v_lru_wa', 'new_v_lru_ba', 'new_v_lru_wx', 'new_v_lru_bx', 'new_v_lru_lambda', 'new_v_w_out', 'new_v_norm_ffn', 'new_v_w_gate', 'new_v_w_up', 'new_v_w_down', 'new_v_norm_final']
TWIN_LEAF_KINDS = {'loss': 'loss', 'grad_x': 'grad_x', 'grad_norm_mix': 'grad_w', 'grad_w_in': 'grad_w', 'grad_ssd_conv_w': 'grad_w', 'grad_ssd_conv_b': 'grad_w', 'grad_ssd_dt_bias': 'grad_w', 'grad_ssd_a_log': 'grad_w', 'grad_ssd_d': 'grad_w', 'grad_ssd_norm': 'grad_w', 'grad_lru_conv_w': 'grad_w', 'grad_lru_conv_b': 'grad_w', 'grad_lru_wa': 'grad_w', 'grad_lru_ba': 'grad_w', 'grad_lru_wx': 'grad_w', 'grad_lru_bx': 'grad_w', 'grad_lru_lambda': 'grad_w', 'grad_w_out': 'grad_w', 'grad_norm_ffn': 'grad_w', 'grad_w_gate': 'grad_w', 'grad_w_up': 'grad_w', 'grad_w_down': 'grad_w', 'grad_norm_final': 'grad_w', 'delta_norm_mix': 'delta_w', 'delta_w_in': 'delta_w', 'delta_ssd_conv_w': 'delta_w', 'delta_ssd_conv_b': 'delta_w', 'delta_ssd_dt_bias': 'delta_w', 'delta_ssd_a_log': 'delta_w', 'delta_ssd_d': 'delta_w', 'delta_ssd_norm': 'delta_w', 'delta_lru_conv_w': 'delta_w', 'delta_lru_conv_b': 'delta_w', 'delta_lru_wa': 'delta_w', 'delta_lru_ba': 'delta_w', 'delta_lru_wx': 'delta_w', 'delta_lru_bx': 'delta_w', 'delta_lru_lambda': 'delta_w', 'delta_w_out': 'delta_w', 'delta_norm_ffn': 'delta_w', 'delta_w_gate': 'delta_w', 'delta_w_up': 'delta_w', 'delta_w_down': 'delta_w', 'delta_norm_final': 'delta_w', 'new_m_norm_mix': 'new_m', 'new_m_w_in': 'new_m', 'new_m_ssd_conv_w': 'new_m', 'new_m_ssd_conv_b': 'new_m', 'new_m_ssd_dt_bias': 'new_m', 'new_m_ssd_a_log': 'new_m', 'new_m_ssd_d': 'new_m', 'new_m_ssd_norm': 'new_m', 'new_m_lru_conv_w': 'new_m', 'new_m_lru_conv_b': 'new_m', 'new_m_lru_wa': 'new_m', 'new_m_lru_ba': 'new_m', 'new_m_lru_wx': 'new_m', 'new_m_lru_bx': 'new_m', 'new_m_lru_lambda': 'new_m', 'new_m_w_out': 'new_m', 'new_m_norm_ffn': 'new_m', 'new_m_w_gate': 'new_m', 'new_m_w_up': 'new_m', 'new_m_w_down': 'new_m', 'new_m_norm_final': 'new_m', 'new_v_norm_mix': 'new_v', 'new_v_w_in': 'new_v', 'new_v_ssd_conv_w': 'new_v', 'new_v_ssd_conv_b': 'new_v', 'new_v_ssd_dt_bias': 'new_v', 'new_v_ssd_a_log': 'new_v', 'new_v_ssd_d': 'new_v', 'new_v_ssd_norm': 'new_v', 'new_v_lru_conv_w': 'new_v', 'new_v_lru_conv_b': 'new_v', 'new_v_lru_wa': 'new_v', 'new_v_lru_ba': 'new_v', 'new_v_lru_wx': 'new_v', 'new_v_lru_bx': 'new_v', 'new_v_lru_lambda': 'new_v', 'new_v_w_out': 'new_v', 'new_v_norm_ffn': 'new_v', 'new_v_w_gate': 'new_v', 'new_v_w_up': 'new_v', 'new_v_w_down': 'new_v', 'new_v_norm_final': 'new_v'}


def _forward(args):
    return _fwd_reference(*[args[k] for k in FWD_PARAMS])


def _output_shape():
    def fwd():
        inp = _fwd_setup_inputs(0)
        return _fwd_reference(*[inp[k] for k in FWD_PARAMS])
    out = _jax.eval_shape(fwd)
    return out.shape, out.dtype

N_MICROBATCH = 1
ADAM_LR = 0.001
ADAM_B1 = 0.9
ADAM_B2 = 0.999
ADAM_EPS = 1e-08
ADAM_WD = 0.01
ADAM_STEP = 10
PER_EXAMPLE_BATCH_AXIS = {'x': 0, 'loss_target': 0}
SHARED_INPUTS = []
_WEIGHT_DTYPES = {'norm_mix': _jnp.float32, 'w_in': _jnp.float32, 'ssd_conv_w': _jnp.float32, 'ssd_conv_b': _jnp.float32, 'ssd_dt_bias': _jnp.float32, 'ssd_a_log': _jnp.float32, 'ssd_d': _jnp.float32, 'ssd_norm': _jnp.float32, 'lru_conv_w': _jnp.float32, 'lru_conv_b': _jnp.float32, 'lru_wa': _jnp.float32, 'lru_ba': _jnp.float32, 'lru_wx': _jnp.float32, 'lru_bx': _jnp.float32, 'lru_lambda': _jnp.float32, 'w_out': _jnp.float32, 'norm_ffn': _jnp.float32, 'w_gate': _jnp.float32, 'w_up': _jnp.float32, 'w_down': _jnp.float32, 'norm_final': _jnp.float32}
MOMENT_SCALE = {'norm_mix': 2.461891e-01, 'w_in': 1.189613e-01, 'ssd_conv_w': 1.510074e-01, 'ssd_conv_b': 2.113536e-01, 'ssd_dt_bias': 5.608904e-01, 'ssd_a_log': 4.320615e-01, 'ssd_d': 1.564330e+00, 'ssd_norm': 1.916497e-01, 'lru_conv_w': 1.115734e-01, 'lru_conv_b': 1.128640e+00, 'lru_wa': 4.043635e-02, 'lru_ba': 3.399352e-02, 'lru_wx': 7.398772e-02, 'lru_bx': 3.718023e-02, 'lru_lambda': 6.505489e-02, 'w_out': 1.724087e-01, 'norm_ffn': 1.681760e-01, 'w_gate': 7.189760e-02, 'w_up': 6.976358e-02, 'w_down': 1.156066e-01, 'norm_final': 6.421724e+01}


def _to_microbatches(a, axis):
    t = _jnp.moveaxis(a, axis, 0)
    t = t.reshape((N_MICROBATCH, t.shape[0] // N_MICROBATCH) + t.shape[1:])
    return _jnp.moveaxis(t, 1, axis + 1)


def setup_inputs(seed: int = 0) -> dict:
    inp = _fwd_setup_inputs(seed)
    key = _jax.random.fold_in(_jax.random.key(seed), 7919)
    shape, _ = _output_shape()
    out = dict(inp)
    out["loss_target"] = _jax.random.normal(_jax.random.fold_in(key, 0), shape, _jnp.float32)
    for i, name in enumerate(TWIN_WEIGHTS):
        w = inp[name].astype(_jnp.float32)
        if MOMENT_SCALE is None:
            s = _jnp.sqrt(_jnp.mean(_jnp.square(w)) + 1e-30)
        else:
            s = MOMENT_SCALE[name]
        km, kv = _jax.random.split(_jax.random.fold_in(key, i + 1))
        out[name] = w
        out["m_" + name] = s * _jax.random.normal(km, w.shape, _jnp.float32)
        out["v_" + name] = (s * s) * _jax.random.uniform(kv, w.shape, _jnp.float32, 0.5, 1.5)
    if N_MICROBATCH > 1:
        for name, axis in PER_EXAMPLE_BATCH_AXIS.items():
            out[name] = _to_microbatches(out[name], axis)
    return {'x': out['x'], 'norm_mix': out['norm_mix'], 'w_in': out['w_in'], 'ssd_conv_w': out['ssd_conv_w'], 'ssd_conv_b': out['ssd_conv_b'], 'ssd_dt_bias': out['ssd_dt_bias'], 'ssd_a_log': out['ssd_a_log'], 'ssd_d': out['ssd_d'], 'ssd_norm': out['ssd_norm'], 'lru_conv_w': out['lru_conv_w'], 'lru_conv_b': out['lru_conv_b'], 'lru_wa': out['lru_wa'], 'lru_ba': out['lru_ba'], 'lru_wx': out['lru_wx'], 'lru_bx': out['lru_bx'], 'lru_lambda': out['lru_lambda'], 'w_out': out['w_out'], 'norm_ffn': out['norm_ffn'], 'w_gate': out['w_gate'], 'w_up': out['w_up'], 'w_down': out['w_down'], 'norm_final': out['norm_final'], 'loss_target': out['loss_target'], 'm_norm_mix': out['m_norm_mix'], 'm_w_in': out['m_w_in'], 'm_ssd_conv_w': out['m_ssd_conv_w'], 'm_ssd_conv_b': out['m_ssd_conv_b'], 'm_ssd_dt_bias': out['m_ssd_dt_bias'], 'm_ssd_a_log': out['m_ssd_a_log'], 'm_ssd_d': out['m_ssd_d'], 'm_ssd_norm': out['m_ssd_norm'], 'm_lru_conv_w': out['m_lru_conv_w'], 'm_lru_conv_b': out['m_lru_conv_b'], 'm_lru_wa': out['m_lru_wa'], 'm_lru_ba': out['m_lru_ba'], 'm_lru_wx': out['m_lru_wx'], 'm_lru_bx': out['m_lru_bx'], 'm_lru_lambda': out['m_lru_lambda'], 'm_w_out': out['m_w_out'], 'm_norm_ffn': out['m_norm_ffn'], 'm_w_gate': out['m_w_gate'], 'm_w_up': out['m_w_up'], 'm_w_down': out['m_w_down'], 'm_norm_final': out['m_norm_final'], 'v_norm_mix': out['v_norm_mix'], 'v_w_in': out['v_w_in'], 'v_ssd_conv_w': out['v_ssd_conv_w'], 'v_ssd_conv_b': out['v_ssd_conv_b'], 'v_ssd_dt_bias': out['v_ssd_dt_bias'], 'v_ssd_a_log': out['v_ssd_a_log'], 'v_ssd_d': out['v_ssd_d'], 'v_ssd_norm': out['v_ssd_norm'], 'v_lru_conv_w': out['v_lru_conv_w'], 'v_lru_conv_b': out['v_lru_conv_b'], 'v_lru_wa': out['v_lru_wa'], 'v_lru_ba': out['v_lru_ba'], 'v_lru_wx': out['v_lru_wx'], 'v_lru_bx': out['v_lru_bx'], 'v_lru_lambda': out['v_lru_lambda'], 'v_w_out': out['v_w_out'], 'v_norm_ffn': out['v_norm_ffn'], 'v_w_gate': out['v_w_gate'], 'v_w_up': out['v_w_up'], 'v_w_down': out['v_w_down'], 'v_norm_final': out['v_norm_final']}


def _loss(weights, diff, rest, loss_target):
    with _jax.named_scope("forward"):
        args = {**rest, TWIN_DIFF_INPUT: diff, **{k: w.astype(_WEIGHT_DTYPES[k]) for k, w in weights.items()}}
        y = _forward(args)
    with _jax.named_scope("loss_head"):
        err = _jnp.square(y.astype(_jnp.float32) - loss_target)
        return 0.5 * _jnp.sum(_jnp.mean(err, axis=-1)) if err.ndim else 0.5 * err


def _adamw(w, g, m, v):
    m = ADAM_B1 * m + (1.0 - ADAM_B1) * g
    v = ADAM_B2 * v + (1.0 - ADAM_B2) * _jnp.square(g)
    m_hat = m / (1.0 - ADAM_B1 ** ADAM_STEP)
    v_hat = v / (1.0 - ADAM_B2 ** ADAM_STEP)
    delta = -ADAM_LR * (m_hat / (_jnp.sqrt(v_hat) + ADAM_EPS) + ADAM_WD * w)
    return delta, m, v


def reference(x, norm_mix, w_in, ssd_conv_w, ssd_conv_b, ssd_dt_bias, ssd_a_log, ssd_d, ssd_norm, lru_conv_w, lru_conv_b, lru_wa, lru_ba, lru_wx, lru_bx, lru_lambda, w_out, norm_ffn, w_gate, w_up, w_down, norm_final, loss_target, m_norm_mix, m_w_in, m_ssd_conv_w, m_ssd_conv_b, m_ssd_dt_bias, m_ssd_a_log, m_ssd_d, m_ssd_norm, m_lru_conv_w, m_lru_conv_b, m_lru_wa, m_lru_ba, m_lru_wx, m_lru_bx, m_lru_lambda, m_w_out, m_norm_ffn, m_w_gate, m_w_up, m_w_down, m_norm_final, v_norm_mix, v_w_in, v_ssd_conv_w, v_ssd_conv_b, v_ssd_dt_bias, v_ssd_a_log, v_ssd_d, v_ssd_norm, v_lru_conv_w, v_lru_conv_b, v_lru_wa, v_lru_ba, v_lru_wx, v_lru_bx, v_lru_lambda, v_w_out, v_norm_ffn, v_w_gate, v_w_up, v_w_down, v_norm_final):
    given = dict(x=x, norm_mix=norm_mix, w_in=w_in, ssd_conv_w=ssd_conv_w, ssd_conv_b=ssd_conv_b, ssd_dt_bias=ssd_dt_bias, ssd_a_log=ssd_a_log, ssd_d=ssd_d, ssd_norm=ssd_norm, lru_conv_w=lru_conv_w, lru_conv_b=lru_conv_b, lru_wa=lru_wa, lru_ba=lru_ba, lru_wx=lru_wx, lru_bx=lru_bx, lru_lambda=lru_lambda, w_out=w_out, norm_ffn=norm_ffn, w_gate=w_gate, w_up=w_up, w_down=w_down, norm_final=norm_final, loss_target=loss_target, m_norm_mix=m_norm_mix, m_w_in=m_w_in, m_ssd_conv_w=m_ssd_conv_w, m_ssd_conv_b=m_ssd_conv_b, m_ssd_dt_bias=m_ssd_dt_bias, m_ssd_a_log=m_ssd_a_log, m_ssd_d=m_ssd_d, m_ssd_norm=m_ssd_norm, m_lru_conv_w=m_lru_conv_w, m_lru_conv_b=m_lru_conv_b, m_lru_wa=m_lru_wa, m_lru_ba=m_lru_ba, m_lru_wx=m_lru_wx, m_lru_bx=m_lru_bx, m_lru_lambda=m_lru_lambda, m_w_out=m_w_out, m_norm_ffn=m_norm_ffn, m_w_gate=m_w_gate, m_w_up=m_w_up, m_w_down=m_w_down, m_norm_final=m_norm_final, v_norm_mix=v_norm_mix, v_w_in=v_w_in, v_ssd_conv_w=v_ssd_conv_w, v_ssd_conv_b=v_ssd_conv_b, v_ssd_dt_bias=v_ssd_dt_bias, v_ssd_a_log=v_ssd_a_log, v_ssd_d=v_ssd_d, v_ssd_norm=v_ssd_norm, v_lru_conv_w=v_lru_conv_w, v_lru_conv_b=v_lru_conv_b, v_lru_wa=v_lru_wa, v_lru_ba=v_lru_ba, v_lru_wx=v_lru_wx, v_lru_bx=v_lru_bx, v_lru_lambda=v_lru_lambda, v_w_out=v_w_out, v_norm_ffn=v_norm_ffn, v_w_gate=v_w_gate, v_w_up=v_w_up, v_w_down=v_w_down, v_norm_final=v_norm_final)
    weights = {n: given[n] for n in TWIN_WEIGHTS}
    shared = {n: given[n] for n in SHARED_INPUTS}
    per_example = {n: given[n] for n in ['x']}
    grad_fn = _jax.value_and_grad(_loss, argnums=(0, 1))

    def one_microbatch(ex, loss_target):
        ex = dict(ex)
        diff = ex.pop(TWIN_DIFF_INPUT)
        return grad_fn(weights, diff, {**shared, **ex}, loss_target)

    if N_MICROBATCH == 1:
        loss, (grad_w, grad_x) = one_microbatch(per_example, given["loss_target"])
    else:
        def body(carry, xs):
            loss_sum, grad_sum = carry
            l_k, (gw_k, gx_k) = one_microbatch(xs[0], xs[1])
            with _jax.named_scope("update"):
                return (loss_sum + l_k, _jax.tree.map(_jnp.add, grad_sum, gw_k)), gx_k

        init = (_jnp.zeros((), _jnp.float32), _jax.tree.map(_jnp.zeros_like, weights))
        (loss, grad_w), grad_x = _jax.lax.scan(body, init, (per_example, given["loss_target"]))
    with _jax.named_scope("update"):
        delta_w, new_m, new_v = {}, {}, {}
        for n in TWIN_WEIGHTS:
            delta_w[n], new_m[n], new_v[n] = _adamw(weights[n], grad_w[n], given["m_" + n], given["v_" + n])
    return (loss, grad_x, *[grad_w[n] for n in TWIN_WEIGHTS], *[delta_w[n] for n in TWIN_WEIGHTS],
            *[new_m[n] for n in TWIN_WEIGHTS], *[new_v[n] for n in TWIN_WEIGHTS])
```

```python
import functools
import math

import jax
import jax.numpy as jnp
from jax import lax
from jax.experimental import pallas as pl
from jax.experimental.pallas import tpu as pltpu

F32 = jnp.float32
BF16 = jnp.bfloat16

N_DEV = 8
DEPTH = 2
D_MODEL = 1024
ATT_W = 512
HEAD_DIM = 64
N_HEADS = 8
ATT_BLOCK = 128
ATT_DILATIONS = (1, 4, 16)
SSD_W = 512
SSD_STATE = 128
SSD_CONV = 1024
SSD_CHUNK = 128
LRU_W = 512
LRU_C = 8.0
D_MIX = 1536
D_FF = 2816
IN_COLS = 4104
REST_COLS = 2688
NORM_EPS = 1e-6
SSD_NORM_EPS = 1e-5
NEG = -1e30

ADAM_LR = 0.001
ADAM_B1 = 0.9
ADAM_B2 = 0.999
ADAM_EPS = 1e-08
ADAM_WD = 0.01
ADAM_STEP = 10

LANES = 128
VMEM_LIMIT = 52 * 1024 * 1024
HI = lax.Precision.HIGHEST


def _sigmoid(x):
    return 1.0 / (1.0 + jnp.exp(-x))


def _silu(x):
    return x * _sigmoid(x)


def _dsilu(x):
    s = _sigmoid(x)
    return s * (1.0 + x * (1.0 - s))


def _softplus(x):
    return jnp.maximum(x, 0.0) + jnp.log(1.0 + jnp.exp(-jnp.abs(x)))


_GELU_C = math.sqrt(2.0 / math.pi)


def _gelu(x):
    return 0.5 * x * (1.0 + jnp.tanh(_GELU_C * (x + 0.044715 * x * x * x)))


def _dgelu(x):
    t = jnp.tanh(_GELU_C * (x + 0.044715 * x * x * x))
    return 0.5 * (1.0 + t) + 0.5 * x * (1.0 - t * t) * _GELU_C * (1.0 + 3.0 * 0.044715 * x * x)


def _dot(a, b):
    return jnp.dot(a, b, preferred_element_type=F32)


def _dot_nt(a, b):
    return lax.dot_general(a, b, (((1,), (1,)), ((), ())), preferred_element_type=F32)


def _dot_tn(a, b):
    return lax.dot_general(a, b, (((0,), (0,)), ((), ())), preferred_element_type=F32)


def _dot_hi(a, b):
    return jnp.dot(a, b, preferred_element_type=F32, precision=HI)


def _iota(shape, axis):
    return lax.broadcasted_iota(jnp.int32, shape, axis)


def _shift_down(x, s, prev8):
    xs = pltpu.roll(x, s, 0)
    ps = pltpu.roll(prev8, s, 0)
    top = jnp.concatenate([ps, x[8:]], axis=0)
    return jnp.where(_iota(x.shape, 0) < s, top, xs)


def _shift_up(x, s, next8):
    tm = x.shape[0]
    xs = pltpu.roll(x, tm - s, 0)
    ns = pltpu.roll(next8, 8 - s, 0)
    bottom = jnp.concatenate([x[:tm - 8], ns], axis=0)
    return jnp.where(_iota(x.shape, 0) >= tm - s, bottom, xs)


def _expand_mat():
    return jnp.where(_iota((LANES, SSD_W), 1) // HEAD_DIM == _iota((LANES, SSD_W), 0), 1.0, 0.0).astype(F32)


def _reduce_mat():
    return jnp.where(_iota((SSD_W, LANES), 0) // HEAD_DIM == _iota((SSD_W, LANES), 1), 1.0, 0.0).astype(F32)


def _params(n_grid):
    return pltpu.CompilerParams(dimension_semantics=("arbitrary",) * n_grid, vmem_limit_bytes=VMEM_LIMIT)


def _rowwise(name, body, n_rows, tm, ins, outs, scratch=(), reverse=False):
    nt = n_rows // tm
    assert nt * tm == n_rows and tm % 8 == 0
    r8 = tm // 8
    last8 = n_rows // 8 - 1

    def pos(s):
        return (nt - 1 - s) if reverse else s

    in_specs, args = [], []
    for spec in ins:
        kind, arr = spec[0], spec[1]
        args.append(arr)
        if kind == 'row':
            in_specs.append(pl.BlockSpec((tm, arr.shape[1]), lambda s: (pos(s), 0)))
        elif kind == 'col':
            in_specs.append(pl.BlockSpec((tm, spec[2]), functools.partial(lambda s, j: (pos(s), j), j=spec[3])))
        elif kind == 'full':
            in_specs.append(pl.BlockSpec(arr.shape, functools.partial(lambda s, n: (0,) * n, n=arr.ndim)))
        elif kind == 'prev8':
            in_specs.append(pl.BlockSpec((8, spec[2]), functools.partial(
                lambda s, j: (jnp.maximum(pos(s) * r8 - 1, 0), j), j=spec[3])))
        elif kind == 'next8':
            in_specs.append(pl.BlockSpec((8, spec[2]), functools.partial(
                lambda s, j: (jnp.minimum((pos(s) + 1) * r8, last8), j), j=spec[3])))
        else:
            raise ValueError(kind)
    out_specs, out_shape, acc_idx = [], [], []
    for k, spec in enumerate(outs):
        if spec[0] == 'row':
            out_specs.append(pl.BlockSpec((tm, spec[1]), lambda s: (pos(s), 0)))
            out_shape.append(jax.ShapeDtypeStruct((n_rows, spec[1]), spec[2]))
        else:
            out_specs.append(pl.BlockSpec(spec[1], lambda s: (0, 0)))
            out_shape.append(jax.ShapeDtypeStruct(spec[1], spec[2]))
            acc_idx.append(k)
    n_in, n_out = len(ins), len(outs)

    def kern(*refs):
        s = pl.program_id(0)
        in_refs, out_refs, scr = refs[:n_in], refs[n_in:n_in + n_out], refs[n_in + n_out:]

        @pl.when(s == 0)
        def _():
            for k in acc_idx:
                out_refs[k][...] = jnp.zeros(out_refs[k].shape, out_refs[k].dtype)

        body(pos(s), nt, in_refs, out_refs, scr)

    res = pl.pallas_call(kern, name=name, grid=(nt,), in_specs=in_specs, out_specs=out_specs,
                         out_shape=out_shape, scratch_shapes=list(scratch), compiler_params=_params(1))(*args)
    return res


def _mm(name, a_list, b, *, res=None, out_dtype=F32, tm=512, tn=None):
    n_rows = a_list[0].shape[0]
    k_total, n = b.shape
    ks = [a.shape[1] for a in a_list]
    assert sum(ks) == k_total
    tn = n if tn is None else tn
    assert n_rows % tm == 0 and n % tn == 0
    na = len(a_list)

    def kern(*refs):
        a_refs, b_ref, o_ref = refs[:na], refs[na], refs[-1]
        acc, off = None, 0
        for a_ref, kp in zip(a_refs, ks):
            part = _dot(a_ref[...].astype(BF16), b_ref[off:off + kp, :])
            acc = part if acc is None else acc + part
            off += kp
        if res is not None:
            acc = acc + refs[na + 1][...]
        o_ref[...] = acc.astype(out_dtype)

    in_specs = [pl.BlockSpec((tm, kp), lambda i, j: (i, 0)) for kp in ks]
    in_specs.append(pl.BlockSpec((k_total, tn), lambda i, j: (0, j)))
    args = list(a_list) + [b]
    if res is not None:
        in_specs.append(pl.BlockSpec((tm, tn), lambda i, j: (i, j)))
        args.append(res)
    return pl.pallas_call(kern, name=name, grid=(n_rows // tm, n // tn), in_specs=in_specs,
                          out_specs=pl.BlockSpec((tm, tn), lambda i, j: (i, j)),
                          out_shape=jax.ShapeDtypeStruct((n_rows, n), out_dtype),
                          compiler_params=_params(2))(*args)


def _mm_tn(name, a, g, *, a_col=None, tk=None, tn=None, tt=512):
    n_rows = a.shape[0]
    k = a.shape[1] if a_col is None else a_col[0]
    a_j = 0 if a_col is None else a_col[1]
    n = g.shape[1]
    tk = k if tk is None else tk
    tn = n if tn is None else tn
    assert k % tk == 0 and n % tn == 0 and n_rows % tt == 0
    kb = k // tk

    def kern(a_ref, g_ref, o_ref):
        t = pl.program_id(2)

        @pl.when(t == 0)
        def _():
            o_ref[...] = jnp.zeros(o_ref.shape, F32)

        o_ref[...] += _dot_tn(a_ref[...].astype(BF16), g_ref[...].astype(BF16))

    return pl.pallas_call(
        kern, name=name, grid=(kb, n // tn, n_rows // tt),
        in_specs=[pl.BlockSpec((tt, tk), lambda i, j, t: (t, a_j * kb + i)),
                  pl.BlockSpec((tt, tn), lambda i, j, t: (t, j))],
        out_specs=pl.BlockSpec((tk, tn), lambda i, j, t: (i, j)),
        out_shape=jax.ShapeDtypeStruct((k, n), F32), compiler_params=_params(3))(a, g)


def _rmsnorm_fwd(name, x, g):
    def body(i, nt, ins, outs, scr):
        xv = ins[0][...]
        rstd = lax.rsqrt(jnp.mean(xv * xv, axis=-1, keepdims=True) + NORM_EPS)
        outs[0][...] = (xv * rstd * ins[1][...]).astype(BF16)

    return _rowwise(name, body, x.shape[0], 512, [('row', x), ('full', g)], [('row', x.shape[1], BF16)])[0]


def _rmsnorm_bwd(name, dh, x, g, dres):
    d = x.shape[1]

    def body(i, nt, ins, outs, scr):
        dy, xv, gv, dr = ins[0][...], ins[1][...], ins[2][...], ins[3][...]
        rstd = lax.rsqrt(jnp.mean(xv * xv, axis=-1, keepdims=True) + NORM_EPS)
        xhat = xv * rstd
        outs[1][0:1, :] += jnp.sum(dy * xhat, axis=0, keepdims=True)
        dxh = dy * gv
        outs[0][...] = dr + rstd * (dxh - xhat * jnp.mean(dxh * xhat, axis=-1, keepdims=True))

    return _rowwise(name, body, x.shape[0], 512, [('row', dh), ('row', x), ('full', g), ('row', dres)],
                    [('row', d, F32), ('acc', (8, d), F32)])


def _final_loss(x, g, target):
    d = x.shape[1]

    def body(i, nt, ins, outs, scr):
        xv, gv, tv = ins[0][...], ins[1][...], ins[2][...]
        rstd = lax.rsqrt(jnp.mean(xv * xv, axis=-1, keepdims=True) + NORM_EPS)
        xhat = xv * rstd
        err = xhat * gv - tv
        row_loss = 0.5 * jnp.mean(err * err, axis=-1, keepdims=True)
        outs[1][...] += jnp.sum(row_loss, axis=0, keepdims=True)
        dy = err * (1.0 / d)
        outs[2][0:1, :] += jnp.sum(dy * xhat, axis=0, keepdims=True)
        dxh = dy * gv
        outs[0][...] = rstd * (dxh - xhat * jnp.mean(dxh * xhat, axis=-1, keepdims=True))

    return _rowwise("final_loss", body, x.shape[0], 512, [('row', x), ('full', g), ('row', target)],
                    [('row', d, F32), ('acc', (8, LANES), F32), ('acc', (8, d), F32)])


def _conv_fwd(name, src, width, idx, w, b):
    def body(i, nt, ins, outs, scr):
        xv = ins[0][...]
        prev = jnp.where(i > 0, ins[1][...], 0.0)
        wv = ins[2][...]
        y = ins[3][...] + wv[3:4, :] * xv
        for s in (1, 2, 3):
            y = y + wv[3 - s:4 - s, :] * _shift_down(xv, s, prev)
        outs[0][...] = y

    return _rowwise(name, body, src.shape[0], 512,
                    [('col', src, width, idx), ('prev8', src, width, idx), ('full', w), ('full', b)],
                    [('row', width, F32)])[0]


def _conv_bwd(name, dpre, src, width, idx, w):
    def body(i, nt, ins, outs, scr):
        dy = ins[0][...]
        nxt = jnp.where(i < nt - 1, ins[1][...], 0.0)
        xv = ins[2][...]
        prev = jnp.where(i > 0, ins[3][...], 0.0)
        wv = ins[4][...]
        dx = wv[3:4, :] * dy
        outs[1][3:4, :] += jnp.sum(dy * xv, axis=0, keepdims=True)
        outs[1][4:5, :] += jnp.sum(dy, axis=0, keepdims=True)
        for s in (1, 2, 3):
            dx = dx + wv[3 - s:4 - s, :] * _shift_up(dy, s, nxt)
            outs[1][3 - s:4 - s, :] += jnp.sum(dy * _shift_down(xv, s, prev), axis=0, keepdims=True)
        outs[0][...] = dx

    return _rowwise(name, body, src.shape[0], 512,
                    [('row', dpre), ('next8', dpre, width, 0), ('col', src, width, idx),
                     ('prev8', src, width, idx), ('full', w)],
                    [('row', width, F32), ('acc', (8, width), F32)])


def _att_bias(n, dil, head, transposed_first_block):
    qi = _iota((ATT_BLOCK, 2 * ATT_BLOCK), 0)
    ki = _iota((ATT_BLOCK, 2 * ATT_BLOCK), 1)
    dist = ATT_BLOCK + qi - ki
    valid = (dist >= 0) & (dist <= ATT_BLOCK) & ((n > 0) | (ki >= ATT_BLOCK))
    slope = 2.0 ** (-(head + 1))
    return jnp.where(valid, (-slope * dil) * dist.astype(F32), NEG)


def _head_mask():
    lane = _iota((ATT_BLOCK, LANES), 1)
    return lane < HEAD_DIM


def _att_fwd(dil, qkv_v, stats):
    n_l = qkv_v.shape[0]
    nb = n_l // ATT_BLOCK
    first = stats is None
    scale = HEAD_DIM ** -0.5

    def kern(*refs):
        n = pl.program_id(1)
        q_ref, kc_ref, kp_ref, vc_ref, vp_ref = refs[:5]
        if first:
            m_out, l_out, a_out = refs[5:]
        else:
            m_in, l_in, a_in, m_out, l_out, a_out = refs[5:]
        low = _head_mask()
        for p in range(N_HEADS // 2):
            sl = slice(LANES * p, LANES * (p + 1))
            q2 = q_ref[:, sl]
            k2 = jnp.concatenate([kp_ref[:, sl], kc_ref[:, sl]], axis=0).astype(BF16)
            v2 = jnp.concatenate([vp_ref[:, sl], vc_ref[:, sl]], axis=0).astype(BF16)
            res = []
            for e in range(2):
                keep = low if e == 0 else jnp.logical_not(low)
                qe = jnp.where(keep, q2, 0.0).astype(BF16)
                s = _dot_nt(qe, k2) * scale + _att_bias(n, dil, 2 * p + e, False)
                m_blk = jnp.max(s, axis=-1, keepdims=True)
                if first:
                    m_new = m_blk
                    pe = jnp.exp(s - m_new)
                    l_new = jnp.sum(pe, axis=-1, keepdims=True)
                    a_new = _dot(pe.astype(BF16), v2)
                else:
                    c0 = HEAD_DIM * e
                    m_old = m_in[:, LANES * p + c0:LANES * p + c0 + 1]
                    l_old = l_in[:, LANES * p + c0:LANES * p + c0 + 1]
                    m_new = jnp.maximum(m_old, m_blk)
                    pe = jnp.exp(s - m_new)
                    alpha = jnp.exp(m_old - m_new)
                    l_new = alpha * l_old + jnp.sum(pe, axis=-1, keepdims=True)
                    a_new = alpha * a_in[:, sl] + _dot(pe.astype(BF16), v2)
                res.append((m_new, l_new, a_new))
            m_out[:, sl] = jnp.where(low, res[0][0], res[1][0])
            l_out[:, sl] = jnp.where(low, res[0][1], res[1][1])
            a_out[:, sl] = jnp.where(low, res[0][2], res[1][2])

    blk = (ATT_BLOCK, ATT_W)
    in_specs = [pl.BlockSpec(blk, lambda r, n: (n, 3 * r)),
                pl.BlockSpec(blk, lambda r, n: (n, 3 * r + 1)),
                pl.BlockSpec(blk, lambda r, n: (jnp.maximum(n - 1, 0), 3 * r + 1)),
                pl.BlockSpec(blk, lambda r, n: (n, 3 * r + 2)),
                pl.BlockSpec(blk, lambda r, n: (jnp.maximum(n - 1, 0), 3 * r + 2))]
    args = [qkv_v] * 5
    st_spec = pl.BlockSpec(blk, lambda r, n: (n, r))
    if not first:
        in_specs += [st_spec] * 3
        args += list(stats)
    shp = jax.ShapeDtypeStruct((n_l, dil * ATT_W), F32)
    return pl.pallas_call(kern, name="att_fwd_d%d" % dil, grid=(dil, nb), in_specs=in_specs,
                          out_specs=[st_spec] * 3, out_shape=[shp] * 3, compiler_params=_params(2))(*args)


def _att_finish(m, l, acc):
    def body(i, nt, ins, outs, scr):
        lv = ins[1][...]
        outs[0][...] = ins[2][...] / lv
        outs[1][...] = ins[0][...] + jnp.log(lv)

    return _rowwise("att_finish", body, m.shape[0], 512, [('row', m), ('row', l), ('row', acc)],
                    [('row', ATT_W, F32), ('row', ATT_W, F32)])


def _att_delta(d_out, out):
    def body(i, nt, ins, outs, scr):
        prod = ins[0][...] * ins[1][...]
        ones = jnp.where(_iota((ATT_W, ATT_W), 0) // HEAD_DIM == _iota((ATT_W, ATT_W), 1) // HEAD_DIM,
                         1.0, 0.0).astype(F32)
        outs[0][...] = _dot_hi(prod, ones)

    return _rowwise("att_delta", body, out.shape[0], 512, [('col', d_out, ATT_W, 0), ('row', out)],
                    [('row', ATT_W, F32)])[0]


def _att_bwd_dq(dil, qkv_v, do_v, lse_v, delta_v, dq_in):
    n_l = qkv_v.shape[0]
    nb = n_l // ATT_BLOCK
    first = dq_in is None
    scale = HEAD_DIM ** -0.5

    def kern(*refs):
        n = pl.program_id(1)
        q_ref, kc_ref, kp_ref, vc_ref, vp_ref, do_ref, lse_ref, dl_ref = refs[:8]
        dq_out = refs[-1]
        low = _head_mask()
        for p in range(N_HEADS // 2):
            sl = slice(LANES * p, LANES * (p + 1))
            q2 = q_ref[:, sl]
            do2 = do_ref[:, sl]
            k2 = jnp.concatenate([kp_ref[:, sl], kc_ref[:, sl]], axis=0).astype(BF16)
            v2 = jnp.concatenate([vp_ref[:, sl], vc_ref[:, sl]], axis=0).astype(BF16)
            res = []
            for e in range(2):
                keep = low if e == 0 else jnp.logical_not(low)
                c0 = LANES * p + HEAD_DIM * e
                qe = jnp.where(keep, q2, 0.0).astype(BF16)
                doe = jnp.where(keep, do2, 0.0).astype(BF16)
                s = _dot_nt(qe, k2) * scale + _att_bias(n, dil, 2 * p + e, False)
                pe = jnp.exp(s - lse_ref[:, c0:c0 + 1])
                dp = _dot_nt(doe, v2)
                ds = pe * (dp - dl_ref[:, c0:c0 + 1])
                res.append(_dot(ds.astype(BF16), k2) * scale)
            dq = jnp.where(low, res[0], res[1])
            if not first:
                dq = dq + refs[8][:, sl]
            dq_out[:, sl] = dq

    blk = (ATT_BLOCK, ATT_W)
    st_spec = pl.BlockSpec(blk, lambda r, n: (n, r))
    in_specs = [pl.BlockSpec(blk, lambda r, n: (n, 3 * r)),
                pl.BlockSpec(blk, lambda r, n: (n, 3 * r + 1)),
                pl.BlockSpec(blk, lambda r, n: (jnp.maximum(n - 1, 0), 3 * r + 1)),
                pl.BlockSpec(blk, lambda r, n: (n, 3 * r + 2)),
                pl.BlockSpec(blk, lambda r, n: (jnp.maximum(n - 1, 0), 3 * r + 2)),
                st_spec, st_spec, st_spec]
    args = [qkv_v] * 5 + [do_v, lse_v, delta_v]
    if not first:
        in_specs.append(st_spec)
        args.append(dq_in)
    return pl.pallas_call(kern, name="att_bwd_dq_d%d" % dil, grid=(dil, nb), in_specs=in_specs,
                          out_specs=st_spec, out_shape=jax.ShapeDtypeStruct((n_l, dil * ATT_W), F32),
                          compiler_params=_params(2))(*args)


def _att_bwd_dkv(dil, qkv_v, do_v, lse_v, delta_v, dkv_in):
    n_l = qkv_v.shape[0]
    nb = n_l // ATT_BLOCK
    first = dkv_in is None
    scale = HEAD_DIM ** -0.5

    def kern(*refs):
        j = pl.program_id(1)
        k_ref, v_ref, qc_ref, qn_ref, doc_ref, don_ref, lc_ref, ln_ref, dc_ref, dn_ref = refs[:10]
        dk_out, dv_out = refs[-2:]
        low = _head_mask()
        row = _iota((2 * ATT_BLOCK, ATT_BLOCK), 0)
        key = _iota((2 * ATT_BLOCK, ATT_BLOCK), 1)
        dist = row - key
        valid = (dist >= 0) & (dist <= ATT_BLOCK) & ((row < ATT_BLOCK) | (j < nb - 1))
        for p in range(N_HEADS // 2):
            sl = slice(LANES * p, LANES * (p + 1))
            k2 = k_ref[:, sl]
            v2 = v_ref[:, sl]
            q2 = jnp.concatenate([qc_ref[:, sl], qn_ref[:, sl]], axis=0)
            do2 = jnp.concatenate([doc_ref[:, sl], don_ref[:, sl]], axis=0)
            lse2 = jnp.concatenate([lc_ref[:, sl], ln_ref[:, sl]], axis=0)
            dl2 = jnp.concatenate([dc_ref[:, sl], dn_ref[:, sl]], axis=0)
            low2 = _iota((2 * ATT_BLOCK, LANES), 1) < HEAD_DIM
            dks, dvs = [], []
            for e in range(2):
                keep = low if e == 0 else jnp.logical_not(low)
                keep2 = low2 if e == 0 else jnp.logical_not(low2)
                c0 = HEAD_DIM * e
                slope = 2.0 ** (-(2 * p + e + 1))
                bias = jnp.where(valid, (-slope * dil) * dist.astype(F32), NEG)
                ke = jnp.where(keep, k2, 0.0).astype(BF16)
                ve = jnp.where(keep, v2, 0.0).astype(BF16)
                q2b = q2.astype(BF16)
                do2b = do2.astype(BF16)
                s = _dot_nt(q2b, ke) * scale + bias
                pe = jnp.exp(s - lse2[:, c0:c0 + 1])
                dp = _dot_nt(do2b, ve)
                ds = pe * (dp - dl2[:, c0:c0 + 1])
                dvs.append(_dot_tn(pe.astype(BF16), jnp.where(keep2, do2, 0.0).astype(BF16)))
                dks.append(_dot_tn(ds.astype(BF16), jnp.where(keep2, q2, 0.0).astype(BF16)) * scale)
            dk = jnp.where(low, dks[0], dks[1])
            dv = jnp.where(low, dvs[0], dvs[1])
            if not first:
                dk = dk + refs[10][:, sl]
                dv = dv + refs[11][:, sl]
            dk_out[:, sl] = dk
            dv_out[:, sl] = dv

    blk = (ATT_BLOCK, ATT_W)
    cur = pl.BlockSpec(blk, lambda r, j: (j, r))
    nxt = pl.BlockSpec(blk, lambda r, j: (jnp.minimum(j + 1, nb - 1), r))
    in_specs = [pl.BlockSpec(blk, lambda r, j: (j, 3 * r + 1)),
                pl.BlockSpec(blk, lambda r, j: (j, 3 * r + 2)),
                pl.BlockSpec(blk, lambda r, j: (j, 3 * r)),
                pl.BlockSpec(blk, lambda r, j: (jnp.minimum(j + 1, nb - 1), 3 * r)),
                cur, nxt, cur, nxt, cur, nxt]
    args = [qkv_v] * 4 + [do_v, do_v, lse_v, lse_v, delta_v, delta_v]
    if not first:
        in_specs += [cur, cur]
        args += list(dkv_in)
    shp = jax.ShapeDtypeStruct((n_l, dil * ATT_W), F32)
    return pl.pallas_call(kern, name="att_bwd_dkv_d%d" % dil, grid=(dil, nb), in_specs=in_specs,
                          out_specs=[cur, cur], out_shape=[shp, shp], compiler_params=_params(2))(*args)


def _attention_fwd(qkv):
    t = qkv.shape[0]
    stats = None
    for dil in ATT_DILATIONS:
        if stats is not None:
            stats = [s.reshape(t // dil, dil * ATT_W) for s in stats]
        stats = _att_fwd(dil, qkv.reshape(t // dil, dil * D_MIX), stats)
    m, l, acc = [s.reshape(t, ATT_W) for s in stats]
    return _att_finish(m, l, acc)


def _attention_bwd(qkv, d_mix, out, lse):
    t = qkv.shape[0]
    delta = _att_delta(d_mix, out)
    d_att = d_mix[:, :ATT_W]
    dq, dkv = None, None
    for dil in ATT_DILATIONS:
        view = lambda a: a.reshape(t // dil, dil * a.shape[1])
        qkv_v = view(qkv)
        do_v, lse_v, dl_v = view(d_att), view(lse), view(delta)
        dq = _att_bwd_dq(dil, qkv_v, do_v, lse_v, dl_v, None if dq is None else view(dq.reshape(t, ATT_W)))
        dkv = _att_bwd_dkv(dil, qkv_v, do_v, lse_v, dl_v,
                           None if dkv is None else [view(a.reshape(t, ATT_W)) for a in dkv])
    return dq.reshape(t, ATT_W), dkv[0].reshape(t, ATT_W), dkv[1].reshape(t, ATT_W)


def _ssd_chunk_common(pre, dtraw, bias_row, alog_row):
    q = SSD_CHUNK
    act = _silu(pre)
    lane = _iota((q, LANES), 1)
    dt = jnp.where(lane < N_HEADS, _softplus(dtraw + bias_row), 0.0)
    a_row = -jnp.exp(alog_row)
    tril = jnp.where(_iota((q, q), 0) >= _iota((q, q), 1), 1.0, 0.0).astype(F32)
    cs = _dot_hi(tril, dt * a_row)
    cs_last = cs[q - 1:q, :]
    return act, dt, a_row, tril, cs, cs_last


def _ssd_lmat(cs, cs_t, h):
    q = SSD_CHUNK
    seg = cs[:, h:h + 1] - cs_t[h:h + 1, :]
    causal = _iota((q, q), 0) >= _iota((q, q), 1)
    return jnp.exp(jnp.where(causal, seg, NEG))


def _ssd_gate_norm(y, z, norm_w):
    sz = _silu(z)
    yg = y * sz
    half = SSD_W // 2
    outs, rss = [], []
    for g in range(2):
        part = yg[:, half * g:half * (g + 1)]
        rs = lax.rsqrt(jnp.mean(part * part, axis=-1, keepdims=True) + SSD_NORM_EPS)
        outs.append(part * rs)
        rss.append(rs)
    yn = jnp.concatenate(outs, axis=1)
    return sz, yn, rss, yn * norm_w


def _ssd_fwd(pre, rest, dt_bias, a_log, d_skip, norm_w):
    t = pre.shape[0]
    q = SSD_CHUNK

    def body(c, nc, ins, outs, scr):
        pre_ref, z_ref, dtr_ref, bias_ref, alog_ref, dsk_ref, nw_ref = ins
        out_ref, y_ref, sp_ref = outs
        s_ref = scr[0]

        @pl.when(c == 0)
        def _():
            s_ref[...] = jnp.zeros(s_ref.shape, F32)

        act, dt, a_row, tril, cs, cs_last = _ssd_chunk_common(pre_ref[...], dtr_ref[...], bias_ref[...],
                                                               alog_ref[...])
        x = act[:, :SSD_W]
        cs_t = cs.T
        e_col = jnp.exp(cs)
        w = jnp.exp(cs_last - cs) * dt
        expand = _expand_mat()
        w_x = _dot_hi(w, expand)
        dt_x = _dot_hi(dt, expand)
        e_x = _dot_hi(e_col, expand)
        d_x = _dot_hi(dsk_ref[...], expand)
        cd_x = _dot_hi(jnp.exp(cs_last), expand)
        s_prev = s_ref[...]
        sp_ref[...] = s_prev
        xw = (x * w_x).astype(BF16)
        xd = (x * dt_x).astype(BF16)
        low = _head_mask()
        y_parts, s_parts = [], []
        for g in range(2):
            bg = act[:, SSD_W + SSD_STATE * g:SSD_W + SSD_STATE * (g + 1)].astype(BF16)
            cg = act[:, SSD_W + 2 * SSD_STATE + SSD_STATE * g:SSD_W + 2 * SSD_STATE + SSD_STATE * (g + 1)].astype(BF16)
            gsl = slice(256 * g, 256 * (g + 1))
            gmat = _dot_nt(cg, bg)
            s_parts.append(_dot_tn(bg, xw[:, gsl]))
            y0 = _dot(cg, s_prev[:, gsl].astype(BF16))
            for pp in range(2):
                pair = 2 * g + pp
                psl = slice(LANES * pair, LANES * (pair + 1))
                yd = []
                for e in range(2):
                    h = 2 * pair + e
                    mh = (gmat * _ssd_lmat(cs, cs_t, h)).astype(BF16)
                    yd.append(_dot(mh, xd[:, psl]))
                y_parts.append(jnp.where(low, yd[0], yd[1]) + e_x[:, psl] * y0[:, LANES * pp:LANES * (pp + 1)])
        y = jnp.concatenate(y_parts, axis=1) + d_x * x
        s_ref[...] = cd_x * s_prev + jnp.concatenate(s_parts, axis=1)
        y_ref[...] = y
        out_ref[...] = _ssd_gate_norm(y, z_ref[...], nw_ref[...])[3]

    return _rowwise("ssd_fwd", body, t, q,
                    [('row', pre), ('col', rest, SSD_W, 2), ('col', rest, LANES, 20), ('full', dt_bias),
                     ('full', a_log), ('full', d_skip), ('full', norm_w)],
                    [('row', SSD_W, F32), ('row', SSD_W, F32), ('row', SSD_W, F32)],
                    scratch=[pltpu.VMEM((SSD_STATE, SSD_W), F32)])


def _ssd_bwd(pre, rest, y, s_prev_all, d_mix, dt_bias, a_log, d_skip, norm_w):
    t = pre.shape[0]
    q = SSD_CHUNK

    def body(c, nc, ins, outs, scr):
        pre_ref, z_ref, dtr_ref, y_ref, sp_ref, do_ref, bias_ref, alog_ref, dsk_ref, nw_ref = ins
        dpre_ref, dz_ref, ddt_ref, a128_ref, a512_ref = outs
        ds_ref = scr[0]

        @pl.when(c == nc - 1)
        def _():
            ds_ref[...] = jnp.zeros(ds_ref.shape, F32)

        pre_v = pre_ref[...]
        dtr = dtr_ref[...]
        act, dt, a_row, tril, cs, cs_last = _ssd_chunk_common(pre_v, dtr, bias_ref[...], alog_ref[...])
        x = act[:, :SSD_W]
        cs_t = cs.T
        e_col = jnp.exp(cs)
        decay_end = jnp.exp(cs_last - cs)
        w = decay_end * dt
        cd = jnp.exp(cs_last)
        expand = _expand_mat()
        reduce = _reduce_mat()
        w_x = _dot_hi(w, expand)
        dt_x = _dot_hi(dt, expand)
        e_x = _dot_hi(e_col, expand)
        d_x = _dot_hi(dsk_ref[...], expand)
        cd_x = _dot_hi(cd, expand)
        s_prev = sp_ref[...]
        d_s = ds_ref[...]
        xw = (x * w_x).astype(BF16)
        xd = (x * dt_x).astype(BF16)
        low = _head_mask()
        lane = _iota((q, LANES), 1)
        sub = _iota((q, LANES), 0)

        yv, zv, nw = y_ref[...], z_ref[...], nw_ref[...]
        d_out = do_ref[...]
        sz, yn, rss, _ = _ssd_gate_norm(yv, zv, nw)
        a512_ref[0:1, :] += jnp.sum(d_out * yn, axis=0, keepdims=True)
        dyn = d_out * nw
        half = SSD_W // 2
        dyg_parts = []
        for g in range(2):
            hs = slice(half * g, half * (g + 1))
            dyg_parts.append(rss[g] * (dyn[:, hs] - yn[:, hs] * jnp.mean(dyn[:, hs] * yn[:, hs], axis=-1,
                                                                          keepdims=True)))
        dyg = jnp.concatenate(dyg_parts, axis=1)
        dy = dyg * sz
        dz_ref[...] = dyg * yv * _dsilu(zv)

        a128_ref[2:3, :] += _dot_hi(jnp.sum(dy * x, axis=0, keepdims=True), reduce)
        dx = d_x * dy

        dy0 = e_x * dy
        dyb = dy.astype(BF16)
        dcs = jnp.zeros((q, LANES), F32)
        dcs_rows = jnp.zeros((q, LANES), F32)
        ddt = jnp.zeros((q, LANES), F32)
        ds_prev_parts, z_parts, db_parts, dc_parts, dxd_parts, y0_parts = [], [], [], [], [], []
        for g in range(2):
            bg = act[:, SSD_W + SSD_STATE * g:SSD_W + SSD_STATE * (g + 1)].astype(BF16)
            cg = act[:, SSD_W + 2 * SSD_STATE + SSD_STATE * g:SSD_W + 2 * SSD_STATE + SSD_STATE * (g + 1)].astype(BF16)
            gsl = slice(256 * g, 256 * (g + 1))
            spg = s_prev[:, gsl].astype(BF16)
            dsg = d_s[:, gsl].astype(BF16)
            dy0g = dy0[:, gsl].astype(BF16)
            gmat = _dot_nt(cg, bg)
            y0_parts.append(_dot(cg, spg))
            dc_g = _dot_nt(dy0g, spg)
            ds_prev_parts.append(_dot_tn(cg, dy0g))
            z_parts.append(_dot(bg, dsg))
            db_g = _dot_nt(xw[:, gsl], dsg)
            dg_acc = jnp.zeros((q, q), F32)
            for pp in range(2):
                pair = 2 * g + pp
                psl = slice(LANES * pair, LANES * (pair + 1))
                dxd_e = []
                for e in range(2):
                    h = 2 * pair + e
                    keep = low if e == 0 else jnp.logical_not(low)
                    lm = _ssd_lmat(cs, cs_t, h)
                    mh = gmat * lm
                    dm = _dot_nt(jnp.where(keep, dy[:, psl], 0.0).astype(BF16), xd[:, psl])
                    dxd_e.append(_dot_tn(mh.astype(BF16), dyb[:, psl]))
                    wm = dm * mh
                    dcs = dcs + jnp.where(lane == h, jnp.sum(wm, axis=1, keepdims=True), 0.0)
                    dcs_rows = dcs_rows - jnp.where(sub == h, jnp.sum(wm, axis=0, keepdims=True), 0.0)
                    dg_acc = dg_acc + dm * lm
                dxd_parts.append(jnp.where(low, dxd_e[0], dxd_e[1]))
            dgb = dg_acc.astype(BF16)
            dc_parts.append(dc_g + _dot(dgb, bg))
            db_parts.append(db_g + _dot_tn(dgb, cg))
        y0 = jnp.concatenate(y0_parts, axis=1)
        zmat = jnp.concatenate(z_parts, axis=1)
        dxd = jnp.concatenate(dxd_parts, axis=1)
        ds_prev = jnp.concatenate(ds_prev_parts, axis=1) + cd_x * d_s
        ds_ref[...] = ds_prev

        dcs = dcs + _dot_hi(dy * y0, reduce) * e_col
        dcd = _dot_hi(jnp.sum(d_s * s_prev, axis=0, keepdims=True), reduce)
        dlast = dcd * cd
        dx = dx + w_x * zmat + dxd * dt_x
        dw = _dot_hi(zmat * x, reduce)
        ddt = ddt + dw * decay_end + _dot_hi(dxd * x, reduce)
        dwl = dw * w
        dcs = dcs - dwl
        dlast = dlast + jnp.sum(dwl, axis=0, keepdims=True)
        dcs = dcs + dcs_rows.T + jnp.where(sub == q - 1, dlast, 0.0)
        dda = _dot_hi(tril.T, dcs)
        ddt = ddt + dda * a_row
        a128_ref[1:2, :] += jnp.sum(dda * dt, axis=0, keepdims=True) * a_row
        draw = jnp.where(lane < N_HEADS, ddt * _sigmoid(dtr + bias_ref[...]), 0.0)
        a128_ref[0:1, :] += jnp.sum(draw, axis=0, keepdims=True)
        ddt_ref[...] = draw
        dact = jnp.concatenate([dx] + db_parts + dc_parts, axis=1)
        dpre_ref[...] = dact * _dsilu(pre_v)

    return _rowwise("ssd_bwd", body, t, q,
                    [('row', pre), ('col', rest, SSD_W, 2), ('col', rest, LANES, 20), ('row', y),
                     ('row', s_prev_all), ('col', d_mix, SSD_W, 1), ('full', dt_bias), ('full', a_log),
                     ('full', d_skip), ('full', norm_w)],
                    [('row', SSD_CONV, F32), ('row', SSD_W, F32), ('row', LANES, F32),
                     ('acc', (8, LANES), F32), ('acc', (8, SSD_W), F32)],
                    scratch=[pltpu.VMEM((SSD_STATE, SSD_W), F32)], reverse=True)


LRU_TM = 256


def _lru_gates(xc, wa, ba, wx, bx, lam):
    xb = xc.astype(BF16)
    r = _sigmoid(_dot(xb, wa) + ba)
    i = _sigmoid(_dot(xb, wx) + bx)
    sp = _softplus(-lam)
    a = jnp.exp(-LRU_C * r * sp)
    mult = jnp.sqrt(1.0 - a * a)
    return r, i, sp, a, mult


def _lru_fwd(xc, rest, wa, ba, wx, bx, lam):
    def body(i, nt, ins, outs, scr):
        xc_ref, g_ref, wa_ref, ba_ref, wx_ref, bx_ref, lam_ref = ins
        carry = scr[0]

        @pl.when(i == 0)
        def _():
            carry[...] = jnp.zeros(carry.shape, F32)

        xv = xc_ref[...]
        r, ig, sp, a, mult = _lru_gates(xv, wa_ref[...], ba_ref[...], wx_ref[...], bx_ref[...], lam_ref[...])
        u = mult * (ig * xv)
        row = _iota(a.shape, 0)
        s = 1
        while s < LRU_TM:
            a_sh = jnp.where(row >= s, pltpu.roll(a, s, 0), 1.0)
            u_sh = jnp.where(row >= s, pltpu.roll(u, s, 0), 0.0)
            u = a * u_sh + u
            a = a * a_sh
            s *= 2
        h = u + a * carry[0:1, :]
        carry[0:1, :] = h[LRU_TM - 1:LRU_TM, :]
        outs[1][...] = h
        outs[0][...] = h * _gelu(g_ref[...])

    return _rowwise("lru_fwd", body, xc.shape[0], LRU_TM,
                    [('row', xc), ('col', rest, LRU_W, 3), ('full', wa), ('full', ba), ('full', wx),
                     ('full', bx), ('full', lam)],
                    [('row', LRU_W, F32), ('row', LRU_W, F32)], scratch=[pltpu.VMEM((8, LRU_W), F32)])


def _lru_bwd(xc, rest, h, d_mix, wa, ba, wx, bx, lam, wa_t, wx_t):
    def body(i, nt, ins, outs, scr):
        xc_ref, g_ref, h_ref, hp_ref, do_ref, wa_ref, ba_ref, wx_ref, bx_ref, lam_ref, wat_ref, wxt_ref = ins
        dxc_ref, dg_ref, dza_ref, dzi_ref, acc_ref = outs
        carry = scr[0]

        @pl.when(i == nt - 1)
        def _():
            carry[...] = jnp.zeros(carry.shape, F32)

        xv, gv, hv, d_out = xc_ref[...], g_ref[...], h_ref[...], do_ref[...]
        r, ig, sp, a, mult = _lru_gates(xv, wa_ref[...], ba_ref[...], wx_ref[...], bx_ref[...], lam_ref[...])
        dg_ref[...] = d_out * hv * _dgelu(gv)
        gsum = d_out * _gelu(gv)
        row = _iota(a.shape, 0)
        b = jnp.where(row < LRU_TM - 1, pltpu.roll(a, LRU_TM - 1, 0), 1.0)
        s = 1
        while s < LRU_TM:
            keep = row < LRU_TM - s
            b_sh = jnp.where(keep, pltpu.roll(b, LRU_TM - s, 0), 1.0)
            g_sh = jnp.where(keep, pltpu.roll(gsum, LRU_TM - s, 0), 0.0)
            gsum = gsum + b * g_sh
            b = b * b_sh
            s *= 2
        dh = gsum + b * carry[0:1, :]
        carry[0:1, :] = a[0:1, :] * dh[0:1, :]
        h_prev = _shift_down(hv, 1, jnp.where(i > 0, hp_ref[...], 0.0))
        du = dh
        dmult = du * ig * xv
        di = du * mult * xv
        dxc = du * mult * ig
        da = dh * h_prev - dmult * a / mult
        dlog = da * a
        dr = dlog * (-LRU_C) * sp
        acc_ref[2:3, :] += jnp.sum(dlog * (-LRU_C) * r, axis=0, keepdims=True)
        dza = dr * r * (1.0 - r)
        dzi = di * ig * (1.0 - ig)
        acc_ref[0:1, :] += jnp.sum(dza, axis=0, keepdims=True)
        acc_ref[1:2, :] += jnp.sum(dzi, axis=0, keepdims=True)
        dzab, dzib = dza.astype(BF16), dzi.astype(BF16)
        dza_ref[...] = dzab
        dzi_ref[...] = dzib
        dxc_ref[...] = dxc + _dot(dzab, wat_ref[...]) + _dot(dzib, wxt_ref[...])

    return _rowwise("lru_bwd", body, xc.shape[0], LRU_TM,
                    [('row', xc), ('col', rest, LRU_W, 3), ('row', h), ('prev8', h, LRU_W, 0),
                     ('col', d_mix, LRU_W, 2), ('full', wa), ('full', ba), ('full', wx), ('full', bx),
                     ('full', lam), ('full', wa_t), ('full', wx_t)],
                    [('row', LRU_W, F32), ('row', LRU_W, F32), ('row', LRU_W, BF16), ('row', LRU_W, BF16),
                     ('acc', (8, LRU_W), F32)],
                    scratch=[pltpu.VMEM((8, LRU_W), F32)], reverse=True)


def _swiglu_act(gu):
    def body(i, nt, ins, outs, scr):
        outs[0][...] = (_silu(ins[0][...]) * ins[1][...]).astype(BF16)

    return _rowwise("swiglu_act", body, gu.shape[0], 512, [('col', gu, D_FF, 0), ('col', gu, D_FF, 1)],
                    [('row', D_FF, BF16)])[0]


def _swiglu_bwd(gu, dact):
    def body(i, nt, ins, outs, scr):
        gv, uv, da = ins[0][...], ins[1][...], ins[2][...]
        outs[0][:, :D_FF] = (da * uv * _dsilu(gv)).astype(BF16)
        outs[0][:, D_FF:] = (da * _silu(gv)).astype(BF16)

    return _rowwise("swiglu_bwd", body, gu.shape[0], 256,
                    [('col', gu, D_FF, 0), ('col', gu, D_FF, 1), ('row', dact)], [('row', 2 * D_FF, BF16)])[0]


def _layer_fwd(x, w, l):
    tag = "_l%d" % l
    h = _rmsnorm_fwd("norm_mix" + tag, x, w['norm_mix'][l])
    qkv = _mm("proj_qkv" + tag, [h], w['w_qkv'][l], tn=768)
    rest = _mm("proj_rest" + tag, [h], w['w_rest'][l], tn=896)
    att, lse = _attention_fwd(qkv)
    pre = _conv_fwd("ssd_conv" + tag, rest, SSD_CONV, 0, w['ssd_conv_w'][l], w['ssd_conv_b'][l])
    ssd, y, s_prev = _ssd_fwd(pre, rest, w['ssd_dt_bias'][l], w['ssd_a_log'][l], w['ssd_d'][l], w['ssd_norm'][l])
    xc = _conv_fwd("lru_conv" + tag, rest, LRU_W, 4, w['lru_conv_w'][l], w['lru_conv_b'][l])
    lru, hl = _lru_fwd(xc, rest, w['lru_wa'][l], w['lru_ba'][l], w['lru_wx'][l], w['lru_bx'][l], w['lru_lambda'][l])
    x_mid = _mm("proj_out" + tag, [att, ssd, lru], w['w_out'][l], res=x, tn=512)
    h2 = _rmsnorm_fwd("norm_ffn" + tag, x_mid, w['norm_ffn'][l])
    gu = _mm("proj_gu" + tag, [h2], w['w_gu'][l], tn=1408)
    act = _swiglu_act(gu)
    x_next = _mm("proj_down" + tag, [act], w['w_down'][l], res=x_mid, tn=512)
    saved = dict(x=x, h=h, qkv=qkv, rest=rest, att=att, lse=lse, pre=pre, ssd=ssd, y=y, s_prev=s_prev, xc=xc,
                 lru=lru, hl=hl, x_mid=x_mid, h2=h2, gu=gu, act=act)
    return x_next, saved


def _layer_bwd(dx_next, sv, w, l):
    tag = "_l%d_b" % l
    t = dx_next.shape[0]
    g = {}
    dact = _mm("d_act" + tag, [dx_next], w['w_down_t'][l], tn=1408)
    g['w_down'] = _mm_tn("dw_down" + tag, sv['act'], dx_next, tk=1408)
    dgu = _swiglu_bwd(sv['gu'], dact)
    dh2 = _mm("d_h2" + tag, [dgu], w['w_gu_t'][l], tn=512, tm=256)
    dw_gu = _mm_tn("dw_gu" + tag, sv['h2'], dgu, tn=1408)
    g['w_gate'], g['w_up'] = dw_gu[:, :D_FF], dw_gu[:, D_FF:]
    dx_mid, acc = _rmsnorm_bwd("norm_ffn" + tag, dh2, sv['x_mid'], w['norm_ffn'][l], dx_next)
    g['norm_ffn'] = acc[0]
    d_mix = _mm("d_mix" + tag, [dx_mid], w['w_out_t'][l], tn=768)
    g['w_out'] = jnp.concatenate([_mm_tn("dw_out%d" % k + tag, a, dx_mid)
                                  for k, a in enumerate((sv['att'], sv['ssd'], sv['lru']))], axis=0)
    dxc, dgl, dza, dzi, acc = _lru_bwd(sv['xc'], sv['rest'], sv['hl'], d_mix, w['lru_wa'][l], w['lru_ba'][l],
                                       w['lru_wx'][l], w['lru_bx'][l], w['lru_lambda'][l],
                                       w['lru_wa_t'][l], w['lru_wx_t'][l])
    g['lru_ba'], g['lru_bx'] = acc[0], acc[1]
    g['lru_lambda'] = acc[2] * (-_sigmoid(-w['lru_lambda'][l][0]))
    g['lru_wa'] = _diag_blocks(_mm_tn("dw_lru_a" + tag, sv['xc'], dza))
    g['lru_wx'] = _diag_blocks(_mm_tn("dw_lru_x" + tag, sv['xc'], dzi))
    dxl, acc = _conv_bwd("lru_conv" + tag, dxc, sv['rest'], LRU_W, 4, w['lru_conv_w'][l])
    g['lru_conv_w'], g['lru_conv_b'] = acc[:4], acc[4]
    dpre, dz, ddt, a128, a512 = _ssd_bwd(sv['pre'], sv['rest'], sv['y'], sv['s_prev'], d_mix, w['ssd_dt_bias'][l],
                                         w['ssd_a_log'][l], w['ssd_d'][l], w['ssd_norm'][l])
    g['ssd_dt_bias'], g['ssd_a_log'], g['ssd_d'] = a128[0, :N_HEADS], a128[1, :N_HEADS], a128[2, :N_HEADS]
    g['ssd_norm'] = a512[0]
    dxbc, acc = _conv_bwd("ssd_conv" + tag, dpre, sv['rest'], SSD_CONV, 0, w['ssd_conv_w'][l])
    g['ssd_conv_w'], g['ssd_conv_b'] = acc[:4], acc[4]
    dq, dk, dv = _attention_bwd(sv['qkv'], d_mix, sv['att'], sv['lse'])
    pieces = [dq, dk, dv, dxbc, dz, dgl, dxl, ddt]
    dh = _mm("d_h" + tag, pieces, w['w_in_t'][l], tn=512)
    dws = [_mm_tn("dw_in%d" % k + tag, sv['h'], p) for k, p in enumerate(pieces)]
    g['w_in'] = jnp.concatenate([dws[0], dws[1], dws[2], dws[4], dws[3], dws[7][:, :N_HEADS], dws[5], dws[6]], axis=1)
    dx, acc = _rmsnorm_bwd("norm_mix" + tag, dh, sv['x'], w['norm_mix'][l], dx_mid)
    g['norm_mix'] = acc[0]
    return dx, g


def _diag_blocks(m):
    return jnp.stack([m[64 * n:64 * (n + 1), 64 * n:64 * (n + 1)] for n in range(8)])


def _block_diag(w):
    eye = jnp.eye(8, dtype=w.dtype)
    return (w[:, :, None, :] * eye[:, None, :, None]).reshape(512, 512)


_ANY = pl.BlockSpec(memory_space=pl.ANY)
_MESH = pl.DeviceIdType.MESH


def _all_gather(name, x):
    def body(x_ref, out_ref, send_sems, recv_sems, local_sem):
        x_, y_, c_ = lax.axis_index("x"), lax.axis_index("y"), lax.axis_index("c")
        me, sibling = (x_, y_, c_), (x_, y_, 1 - c_)
        chips = [(1 - x_, y_), (x_, 1 - y_), (1 - x_, 1 - y_)]

        def slot(px, py, pc):
            return out_ref.at[4 * px + 2 * py + pc]

        def copy(k, block, to, src=None):
            return pltpu.make_async_remote_copy(
                src_ref=slot(*block) if src is None else src, dst_ref=slot(*block),
                send_sem=send_sems.at[k], recv_sem=recv_sems.at[k], device_id=to, device_id_type=_MESH)

        mine = pltpu.make_async_copy(x_ref, slot(*me), local_sem)
        mine.start()
        first = [copy(0, me, sibling, src=x_ref)]
        first += [copy(1 + j, me, (*chip, c_), src=x_ref) for j, chip in enumerate(chips)]
        for cp in first:
            cp.start()
        passed = [copy(4 + j, (*chip, c_), sibling) for j, chip in enumerate(chips)]
        for j, chip in enumerate(chips):
            copy(1 + j, (*chip, c_), me).wait_recv()
            passed[j].start()
        copy(0, sibling, me).wait_recv()
        for j, chip in enumerate(chips):
            copy(4 + j, (*chip, 1 - c_), me).wait_recv()
        for cp in first + passed:
            cp.wait_send()
        mine.wait()

    return pl.pallas_call(
        body, name=name, out_shape=jax.ShapeDtypeStruct((N_DEV,) + x.shape, x.dtype),
        in_specs=[_ANY], out_specs=_ANY,
        scratch_shapes=[pltpu.SemaphoreType.DMA((7,)), pltpu.SemaphoreType.DMA((7,)), pltpu.SemaphoreType.DMA],
    )(x)


def _all_to_all(name, x):
    def body(x_ref, out_ref, send_sems, recv_sems, local_sem):
        x_, y_, c_ = lax.axis_index("x"), lax.axis_index("y"), lax.axis_index("c")
        me = 4 * x_ + 2 * y_ + c_

        def peer(k):
            return ((1 - x_) if k & 4 else x_, (1 - y_) if k & 2 else y_, (1 - c_) if k & 1 else c_)

        def copy(k):
            px, py, pc = peer(k)
            return pltpu.make_async_remote_copy(
                src_ref=x_ref.at[4 * px + 2 * py + pc], dst_ref=out_ref.at[me],
                send_sem=send_sems.at[k - 1], recv_sem=recv_sems.at[k - 1],
                device_id=(px, py, pc), device_id_type=_MESH)

        def arrival(k):
            px, py, pc = peer(k)
            return pltpu.make_async_remote_copy(
                src_ref=x_ref.at[me], dst_ref=out_ref.at[4 * px + 2 * py + pc],
                send_sem=send_sems.at[k - 1], recv_sem=recv_sems.at[k - 1],
                device_id=(px, py, pc), device_id_type=_MESH)

        mine = pltpu.make_async_copy(x_ref.at[me], out_ref.at[me], local_sem)
        mine.start()
        copies = [copy(k) for k in range(1, N_DEV)]
        for cp in copies:
            cp.start()
        for k in range(1, N_DEV):
            arrival(k).wait_recv()
        for cp in copies:
            cp.wait_send()
        mine.wait()

    return pl.pallas_call(
        body, name=name, out_shape=jax.ShapeDtypeStruct(x.shape, x.dtype),
        in_specs=[_ANY], out_specs=_ANY,
        scratch_shapes=[pltpu.SemaphoreType.DMA((7,)), pltpu.SemaphoreType.DMA((7,)), pltpu.SemaphoreType.DMA],
    )(x)


def _adamw(name, w, m, v, g, tr):
    s_parts, r, c = g.shape
    assert r % tr == 0

    def kern(w_ref, m_ref, v_ref, g_ref, go_ref, d_ref, mo_ref, vo_ref):
        gs = g_ref[0].astype(F32)
        for s in range(1, s_parts):
            gs = gs + g_ref[s].astype(F32)
        wv = w_ref[...]
        m2 = ADAM_B1 * m_ref[...] + (1.0 - ADAM_B1) * gs
        v2 = ADAM_B2 * v_ref[...] + (1.0 - ADAM_B2) * (gs * gs)
        m_hat = m2 / (1.0 - ADAM_B1 ** ADAM_STEP)
        v_hat = v2 / (1.0 - ADAM_B2 ** ADAM_STEP)
        go_ref[...] = gs
        d_ref[...] = -ADAM_LR * (m_hat / (jnp.sqrt(v_hat) + ADAM_EPS) + ADAM_WD * wv)
        mo_ref[...] = m2
        vo_ref[...] = v2

    spec = pl.BlockSpec((tr, c), lambda i: (i, 0))
    shp = jax.ShapeDtypeStruct((r, c), F32)
    return pl.pallas_call(kern, name=name, grid=(r // tr,),
                          in_specs=[spec, spec, spec, pl.BlockSpec((s_parts, tr, c), lambda i: (0, i, 0))],
                          out_specs=[spec] * 4, out_shape=[shp] * 4, compiler_params=_params(1))(w, m, v, g)


def _pack(arrs, rows, lead=0):
    parts = []
    for a in arrs:
        flat = a.reshape(a.shape[:lead] + (-1,))
        pad = (-flat.shape[-1]) % LANES
        if pad:
            flat = jnp.pad(flat, [(0, 0)] * lead + [(0, pad)])
        parts.append(flat)
    flat = jnp.concatenate(parts, axis=-1)
    pad = rows * LANES - flat.shape[-1]
    assert pad >= 0
    if pad:
        flat = jnp.pad(flat, [(0, 0)] * lead + [(0, pad)])
    return flat.reshape(flat.shape[:lead] + (rows, LANES))


def _unpack(buf, shapes, lead=0):
    flat = buf.reshape(buf.shape[:lead] + (-1,))
    out, off = [], 0
    for shp in shapes:
        n = math.prod(shp)
        out.append(flat[..., off:off + n].reshape(buf.shape[:lead] + tuple(shp)))
        off += n + ((-n) % LANES)
    return out


BIG = ('w_in', 'w_out', 'w_gate', 'w_up', 'w_down')
BIG_SHARD_SHAPES = ((DEPTH, D_MODEL, IN_COLS // N_DEV), (DEPTH, D_MIX // N_DEV, D_MODEL),
                    (DEPTH, D_MODEL, D_FF // N_DEV), (DEPTH, D_MODEL, D_FF // N_DEV),
                    (DEPTH, D_FF // N_DEV, D_MODEL))
BIG_ROWS = 28672
BIG_TILE = 1024
CONV = ('ssd_conv_w', 'lru_conv_w')
CONV_SHARD_SHAPES = ((DEPTH, 4, SSD_CONV // N_DEV), (DEPTH, 4, LRU_W // N_DEV))
CONV_ROWS = 16
SMALL = ('norm_mix', 'ssd_conv_b', 'ssd_dt_bias', 'ssd_a_log', 'ssd_d', 'ssd_norm', 'lru_conv_b', 'lru_wa',
         'lru_ba', 'lru_wx', 'lru_bx', 'lru_lambda', 'norm_ffn', 'norm_final')
SMALL_ROWS = 1280
SMALL_TILE = 256
WEIGHTS = ('norm_mix', 'w_in', 'ssd_conv_w', 'ssd_conv_b', 'ssd_dt_bias', 'ssd_a_log', 'ssd_d', 'ssd_norm',
           'lru_conv_w', 'lru_conv_b', 'lru_wa', 'lru_ba', 'lru_wx', 'lru_bx', 'lru_lambda', 'w_out', 'norm_ffn',
           'w_gate', 'w_up', 'w_down', 'norm_final')


def _join_cols(a):
    return jnp.transpose(a, (1, 2, 0, 3)).reshape(a.shape[1], a.shape[2], -1)


def _join_rows(a):
    return jnp.transpose(a, (1, 0, 2, 3)).reshape(a.shape[1], -1, a.shape[3])


def _split_cols(a):
    l, r, c = a.shape
    return jnp.transpose(a.reshape(l, r, N_DEV, c // N_DEV), (2, 0, 1, 3))


def _split_rows(a):
    l, r, c = a.shape
    return jnp.transpose(a.reshape(l, N_DEV, r // N_DEV, c), (1, 0, 2, 3))


def _prepare_weights(p, full):
    w = {}
    w_in = full['w_in']
    w_qkv = w_in[:, :, :D_MIX]
    dt_cols = jnp.pad(w_in[:, :, 3072:3080], ((0, 0), (0, 0), (0, LANES - N_HEADS)))
    w_rest = jnp.concatenate([w_in[:, :, 2048:3072], w_in[:, :, 1536:2048], w_in[:, :, 3080:3592],
                              w_in[:, :, 3592:4104], dt_cols], axis=2)
    w['w_qkv'], w['w_rest'] = w_qkv, w_rest
    w['w_in_t'] = jnp.transpose(jnp.concatenate([w_qkv, w_rest], axis=2), (0, 2, 1))
    w['w_out'] = full['w_out']
    w['w_out_t'] = jnp.transpose(full['w_out'], (0, 2, 1))
    w['w_gu'] = jnp.concatenate([full['w_gate'], full['w_up']], axis=2)
    w['w_gu_t'] = jnp.transpose(w['w_gu'], (0, 2, 1))
    w['w_down'] = full['w_down']
    w['w_down_t'] = jnp.transpose(full['w_down'], (0, 2, 1))
    for k in ('norm_mix', 'ssd_conv_b', 'ssd_norm', 'lru_conv_b', 'lru_ba', 'lru_bx', 'lru_lambda', 'norm_ffn'):
        w[k] = p[k][:, None, :]
    for k in ('ssd_dt_bias', 'ssd_a_log', 'ssd_d'):
        w[k] = jnp.pad(p[k], ((0, 0), (0, LANES - N_HEADS)))[:, None, :]
    for k in CONV:
        w[k] = jnp.pad(full[k], ((0, 0), (0, 4), (0, 0)))
    for k in ('lru_wa', 'lru_wx'):
        bd = jnp.stack([_block_diag(p[k][l]) for l in range(DEPTH)]).astype(BF16)
        w[k] = bd
        w[k + '_t'] = jnp.transpose(bd, (0, 2, 1))
    return w


def _local_step(x, target, w, norm_final):
    saved = []
    for l in range(DEPTH):
        x, sv = _layer_fwd(x, w, l)
        saved.append(sv)
    dx, loss_acc, dgf = _final_loss(x, norm_final[None, :], target)
    grads = [None] * DEPTH
    for l in reversed(range(DEPTH)):
        dx, grads[l] = _layer_bwd(dx, saved[l], w, l)
    g = {k: jnp.stack([grads[l][k] for l in range(DEPTH)]) for k in grads[0]}
    g['norm_final'] = dgf[0]
    return loss_acc[0, 0], dx, g


def kernel(x, norm_mix, w_in, ssd_conv_w, ssd_conv_b, ssd_dt_bias, ssd_a_log, ssd_d, ssd_norm, lru_conv_w, lru_conv_b, lru_wa, lru_ba, lru_wx, lru_bx, lru_lambda, w_out, norm_ffn, w_gate, w_up, w_down, norm_final, loss_target, m_norm_mix, m_w_in, m_ssd_conv_w, m_ssd_conv_b, m_ssd_dt_bias, m_ssd_a_log, m_ssd_d, m_ssd_norm, m_lru_conv_w, m_lru_conv_b, m_lru_wa, m_lru_ba, m_lru_wx, m_lru_bx, m_lru_lambda, m_w_out, m_norm_ffn, m_w_gate, m_w_up, m_w_down, m_norm_final, v_norm_mix, v_w_in, v_ssd_conv_w, v_ssd_conv_b, v_ssd_dt_bias, v_ssd_a_log, v_ssd_d, v_ssd_norm, v_lru_conv_w, v_lru_conv_b, v_lru_wa, v_lru_ba, v_lru_wx, v_lru_bx, v_lru_lambda, v_w_out, v_norm_ffn, v_w_gate, v_w_up, v_w_down, v_norm_final):
    args = (norm_mix, w_in, ssd_conv_w, ssd_conv_b, ssd_dt_bias, ssd_a_log, ssd_d, ssd_norm, lru_conv_w, lru_conv_b, lru_wa, lru_ba, lru_wx, lru_bx, lru_lambda, w_out, norm_ffn, w_gate, w_up, w_down, norm_final)
    margs = (m_norm_mix, m_w_in, m_ssd_conv_w, m_ssd_conv_b, m_ssd_dt_bias, m_ssd_a_log, m_ssd_d, m_ssd_norm, m_lru_conv_w, m_lru_conv_b, m_lru_wa, m_lru_ba, m_lru_wx, m_lru_bx, m_lru_lambda, m_w_out, m_norm_ffn, m_w_gate, m_w_up, m_w_down, m_norm_final)
    vargs = (v_norm_mix, v_w_in, v_ssd_conv_w, v_ssd_conv_b, v_ssd_dt_bias, v_ssd_a_log, v_ssd_d, v_ssd_norm, v_lru_conv_w, v_lru_conv_b, v_lru_wa, v_lru_ba, v_lru_wx, v_lru_bx, v_lru_lambda, v_w_out, v_norm_ffn, v_w_gate, v_w_up, v_w_down, v_norm_final)
    p = dict(zip(WEIGHTS, args))
    pm = dict(zip(WEIGHTS, margs))
    pv = dict(zip(WEIGHTS, vargs))

    big = _all_gather("gather_big", _pack([p[k] for k in BIG], BIG_ROWS).astype(BF16))
    conv = _all_gather("gather_conv", _pack([p[k] for k in CONV], CONV_ROWS))
    full = dict(zip(BIG, _unpack(big, BIG_SHARD_SHAPES, lead=1)))
    full['w_in'], full['w_gate'], full['w_up'] = (_join_cols(full[k]) for k in ('w_in', 'w_gate', 'w_up'))
    full['w_out'], full['w_down'] = _join_rows(full['w_out']), _join_rows(full['w_down'])
    for k, a in zip(CONV, _unpack(conv, CONV_SHARD_SHAPES, lead=1)):
        full[k] = _join_cols(a)
    w = _prepare_weights(p, full)

    loss_local, dx, g = _local_step(x[0], loss_target[0], w, norm_final)
    loss = lax.psum(loss_local, ("x", "y", "c"))

    small_g = _all_gather("gather_small_grads", _pack([g[k] for k in SMALL + CONV], SMALL_ROWS))
    zeros = [jnp.zeros_like(g[k]) for k in CONV]
    res_small = _adamw("adamw_small", _pack([p[k] for k in SMALL] + zeros, SMALL_ROWS),
                       _pack([pm[k] for k in SMALL] + zeros, SMALL_ROWS),
                       _pack([pv[k] for k in SMALL] + zeros, SMALL_ROWS), small_g, SMALL_TILE)
    small_shapes = [g[k].shape for k in SMALL + CONV]
    out = {kind: {} for kind in range(4)}
    for kind in range(4):
        for k, a in zip(SMALL + CONV, _unpack(res_small[kind], small_shapes)):
            out[kind][k] = a
    me = 4 * lax.axis_index("x") + 2 * lax.axis_index("y") + lax.axis_index("c")
    conv_g = []
    for k, shp in zip(CONV, CONV_SHARD_SHAPES):
        conv_g.append(lax.dynamic_slice_in_dim(out[0][k], me * shp[2], shp[2], axis=2))
    res_conv = _adamw("adamw_conv", _pack([p[k] for k in CONV], CONV_ROWS), _pack([pm[k] for k in CONV], CONV_ROWS),
                      _pack([pv[k] for k in CONV], CONV_ROWS), _pack(conv_g, CONV_ROWS)[None], CONV_ROWS)
    for kind in range(4):
        for k, a in zip(CONV, _unpack(res_conv[kind], CONV_SHARD_SHAPES)):
            out[kind][k] = a

    dest = [_split_cols(g['w_in']), _split_rows(g['w_out']), _split_cols(g['w_gate']), _split_cols(g['w_up']),
            _split_rows(g['w_down'])]
    parts = _all_to_all("exchange_big_grads", _pack(dest, BIG_ROWS, lead=1).astype(BF16))
    res_big = _adamw("adamw_big", _pack([p[k] for k in BIG], BIG_ROWS), _pack([pm[k] for k in BIG], BIG_ROWS),
                     _pack([pv[k] for k in BIG], BIG_ROWS), parts, BIG_TILE)
    for kind in range(4):
        for k, a in zip(BIG, _unpack(res_big[kind], BIG_SHARD_SHAPES)):
            out[kind][k] = a

    outs = [loss, dx[None]]
    for kind in range(4):
        outs += [out[kind][k] for k in WEIGHTS]
    return tuple(outs)
```

```python
import functools
import math

import jax
import jax.numpy as jnp
from jax import lax
from jax.experimental import pallas as pl
from jax.experimental.pallas import tpu as pltpu

F32 = jnp.float32
BF16 = jnp.bfloat16

N_DEV = 8
DEPTH = 2
D_MODEL = 1024
ATT_W = 512
HEAD_DIM = 64
N_HEADS = 8
ATT_BLOCK = 128
ATT_DILATIONS = (1, 4, 16)
SSD_W = 512
SSD_STATE = 128
SSD_CONV = 1024
SSD_CHUNK = 128
LRU_W = 512
LRU_C = 8.0
D_MIX = 1536
D_FF = 2816
FF_SHARD = D_FF // N_DEV
FF_SHARD_P = 384
D_FFP = N_DEV * FF_SHARD_P
IN_COLS = 4104
IN_SHARD = IN_COLS // N_DEV
IN_WIN = 640
IN_COLS_P = 4224
REST_COLS = 2688
NORM_EPS = 1e-6
SSD_NORM_EPS = 1e-5
NEG = -1e30

ADAM_LR = 0.001
ADAM_B1 = 0.9
ADAM_B2 = 0.999
ADAM_EPS = 1e-08
ADAM_WD = 0.01
ADAM_STEP = 10

LANES = 128
VMEM_LIMIT = 52 * 1024 * 1024
HI = lax.Precision.HIGHEST


def _sigmoid(x):
    return 1.0 / (1.0 + jnp.exp(-x))


def _silu(x):
    return x * _sigmoid(x)


def _dsilu(x):
    s = _sigmoid(x)
    return s * (1.0 + x * (1.0 - s))


def _softplus(x):
    return jnp.maximum(x, 0.0) + jnp.log(1.0 + jnp.exp(-jnp.abs(x)))


_GELU_C = math.sqrt(2.0 / math.pi)


def _gelu(x):
    return 0.5 * x * (1.0 + jnp.tanh(_GELU_C * (x + 0.044715 * x * x * x)))


def _dgelu(x):
    t = jnp.tanh(_GELU_C * (x + 0.044715 * x * x * x))
    return 0.5 * (1.0 + t) + 0.5 * x * (1.0 - t * t) * _GELU_C * (1.0 + 3.0 * 0.044715 * x * x)


def _dot(a, b):
    return jnp.dot(a, b, preferred_element_type=F32)


def _dot_nt(a, b):
    return lax.dot_general(a, b, (((1,), (1,)), ((), ())), preferred_element_type=F32)


def _dot_tn(a, b):
    return lax.dot_general(a, b, (((0,), (0,)), ((), ())), preferred_element_type=F32)


def _dot_hi(a, b):
    return jnp.dot(a, b, preferred_element_type=F32, precision=HI)


def _iota(shape, axis):
    return lax.broadcasted_iota(jnp.int32, shape, axis)


def _shift_down(x, s, prev8):
    xs = pltpu.roll(x, s, 0)
    ps = pltpu.roll(prev8, s, 0)
    top = jnp.concatenate([ps, x[8:]], axis=0)
    return jnp.where(_iota(x.shape, 0) < s, top, xs)


def _shift_up(x, s, next8):
    tm = x.shape[0]
    xs = pltpu.roll(x, tm - s, 0)
    ns = pltpu.roll(next8, 8 - s, 0)
    bottom = jnp.concatenate([x[:tm - 8], ns], axis=0)
    return jnp.where(_iota(x.shape, 0) >= tm - s, bottom, xs)


def _expand_mat():
    return jnp.where(_iota((LANES, SSD_W), 1) // HEAD_DIM == _iota((LANES, SSD_W), 0), 1.0, 0.0).astype(F32)


def _reduce_mat():
    return jnp.where(_iota((SSD_W, LANES), 0) // HEAD_DIM == _iota((SSD_W, LANES), 1), 1.0, 0.0).astype(F32)


def _params(n_grid):
    return pltpu.CompilerParams(dimension_semantics=("arbitrary",) * n_grid, vmem_limit_bytes=VMEM_LIMIT)


def _rowwise(name, body, n_rows, tm, ins, outs, scratch=(), reverse=False):
    nt = n_rows // tm
    assert nt * tm == n_rows and tm % 8 == 0
    r8 = tm // 8
    last8 = n_rows // 8 - 1

    def pos(s):
        return (nt - 1 - s) if reverse else s

    in_specs, args = [], []
    for spec in ins:
        kind, arr = spec[0], spec[1]
        args.append(arr)
        if kind == 'row':
            in_specs.append(pl.BlockSpec((tm, arr.shape[1]), lambda s: (pos(s), 0)))
        elif kind == 'col':
            in_specs.append(pl.BlockSpec((tm, spec[2]), functools.partial(lambda s, j: (pos(s), j), j=spec[3])))
        elif kind == 'full':
            in_specs.append(pl.BlockSpec(arr.shape, functools.partial(lambda s, n: (0,) * n, n=arr.ndim)))
        elif kind == 'prev8':
            in_specs.append(pl.BlockSpec((8, spec[2]), functools.partial(
                lambda s, j: (jnp.maximum(pos(s) * r8 - 1, 0), j), j=spec[3])))
        elif kind == 'next8':
            in_specs.append(pl.BlockSpec((8, spec[2]), functools.partial(
                lambda s, j: (jnp.minimum((pos(s) + 1) * r8, last8), j), j=spec[3])))
        else:
            raise ValueError(kind)
    out_specs, out_shape, acc_idx = [], [], []
    for k, spec in enumerate(outs):
        if spec[0] == 'row':
            out_specs.append(pl.BlockSpec((tm, spec[1]), lambda s: (pos(s), 0)))
            out_shape.append(jax.ShapeDtypeStruct((n_rows, spec[1]), spec[2]))
        else:
            out_specs.append(pl.BlockSpec(spec[1], lambda s: (0, 0)))
            out_shape.append(jax.ShapeDtypeStruct(spec[1], spec[2]))
            acc_idx.append(k)
    n_in, n_out = len(ins), len(outs)

    def kern(*refs):
        s = pl.program_id(0)
        in_refs, out_refs, scr = refs[:n_in], refs[n_in:n_in + n_out], refs[n_in + n_out:]

        @pl.when(s == 0)
        def _():
            for k in acc_idx:
                out_refs[k][...] = jnp.zeros(out_refs[k].shape, out_refs[k].dtype)

        body(pos(s), nt, in_refs, out_refs, scr)

    res = pl.pallas_call(kern, name=name, grid=(nt,), in_specs=in_specs, out_specs=out_specs,
                         out_shape=out_shape, scratch_shapes=list(scratch), compiler_params=_params(1))(*args)
    return res


def _mm(name, a_list, b, *, res=None, out_dtype=F32, tm=512, tn=None):
    n_rows = a_list[0].shape[0]
    k_total, n = b.shape
    ks = [a.shape[1] for a in a_list]
    assert sum(ks) == k_total
    tn = n if tn is None else tn
    assert n_rows % tm == 0 and n % tn == 0
    na = len(a_list)

    def kern(*refs):
        a_refs, b_ref, o_ref = refs[:na], refs[na], refs[-1]
        acc, off = None, 0
        for a_ref, kp in zip(a_refs, ks):
            part = _dot(a_ref[...].astype(BF16), b_ref[off:off + kp, :])
            acc = part if acc is None else acc + part
            off += kp
        if res is not None:
            acc = acc + refs[na + 1][...]
        o_ref[...] = acc.astype(out_dtype)

    in_specs = [pl.BlockSpec((tm, kp), lambda i, j: (i, 0)) for kp in ks]
    in_specs.append(pl.BlockSpec((k_total, tn), lambda i, j: (0, j)))
    args = list(a_list) + [b]
    if res is not None:
        in_specs.append(pl.BlockSpec((tm, tn), lambda i, j: (i, j)))
        args.append(res)
    return pl.pallas_call(kern, name=name, grid=(n_rows // tm, n // tn), in_specs=in_specs,
                          out_specs=pl.BlockSpec((tm, tn), lambda i, j: (i, j)),
                          out_shape=jax.ShapeDtypeStruct((n_rows, n), out_dtype),
                          compiler_params=_params(2))(*args)


def _mm_tn(name, a, g, *, a_col=None, tk=None, tn=None, tt=512):
    n_rows = a.shape[0]
    k = a.shape[1] if a_col is None else a_col[0]
    a_j = 0 if a_col is None else a_col[1]
    n = g.shape[1]
    tk = k if tk is None else tk
    tn = n if tn is None else tn
    assert k % tk == 0 and n % tn == 0 and n_rows % tt == 0
    kb = k // tk

    def kern(a_ref, g_ref, o_ref):
        t = pl.program_id(2)

        @pl.when(t == 0)
        def _():
            o_ref[...] = jnp.zeros(o_ref.shape, F32)

        o_ref[...] += _dot_tn(a_ref[...].astype(BF16), g_ref[...].astype(BF16))

    return pl.pallas_call(
        kern, name=name, grid=(kb, n // tn, n_rows // tt),
        in_specs=[pl.BlockSpec((tt, tk), lambda i, j, t: (t, a_j * kb + i)),
                  pl.BlockSpec((tt, tn), lambda i, j, t: (t, j))],
        out_specs=pl.BlockSpec((tk, tn), lambda i, j, t: (i, j)),
        out_shape=jax.ShapeDtypeStruct((k, n), F32), compiler_params=_params(3))(a, g)


def _rmsnorm_fwd(name, x, g):
    def body(i, nt, ins, outs, scr):
        xv = ins[0][...]
        rstd = lax.rsqrt(jnp.mean(xv * xv, axis=-1, keepdims=True) + NORM_EPS)
        outs[0][...] = (xv * rstd * ins[1][...]).astype(BF16)

    return _rowwise(name, body, x.shape[0], 512, [('row', x), ('full', g)], [('row', x.shape[1], BF16)])[0]


def _rmsnorm_bwd(name, dh, x, g, dres):
    d = x.shape[1]

    def body(i, nt, ins, outs, scr):
        dy, xv, gv, dr = ins[0][...], ins[1][...], ins[2][...], ins[3][...]
        rstd = lax.rsqrt(jnp.mean(xv * xv, axis=-1, keepdims=True) + NORM_EPS)
        xhat = xv * rstd
        outs[1][0:1, :] += jnp.sum(dy * xhat, axis=0, keepdims=True)
        dxh = dy * gv
        outs[0][...] = dr + rstd * (dxh - xhat * jnp.mean(dxh * xhat, axis=-1, keepdims=True))

    return _rowwise(name, body, x.shape[0], 512, [('row', dh), ('row', x), ('full', g), ('row', dres)],
                    [('row', d, F32), ('acc', (8, d), F32)])


def _final_loss(x, g, target):
    d = x.shape[1]

    def body(i, nt, ins, outs, scr):
        xv, gv, tv = ins[0][...], ins[1][...], ins[2][...]
        rstd = lax.rsqrt(jnp.mean(xv * xv, axis=-1, keepdims=True) + NORM_EPS)
        xhat = xv * rstd
        err = xhat * gv - tv
        row_loss = 0.5 * jnp.mean(err * err, axis=-1, keepdims=True)
        outs[1][...] += jnp.sum(row_loss, axis=0, keepdims=True)
        dy = err * (1.0 / d)
        outs[2][0:1, :] += jnp.sum(dy * xhat, axis=0, keepdims=True)
        dxh = dy * gv
        outs[0][...] = rstd * (dxh - xhat * jnp.mean(dxh * xhat, axis=-1, keepdims=True))

    return _rowwise("final_loss", body, x.shape[0], 512, [('row', x), ('full', g), ('row', target)],
                    [('row', d, F32), ('acc', (8, LANES), F32), ('acc', (8, d), F32)])


def _conv_fwd(name, src, width, idx, w, b):
    def body(i, nt, ins, outs, scr):
        xv = ins[0][...]
        prev = jnp.where(i > 0, ins[1][...], 0.0)
        wv = ins[2][...]
        y = ins[3][...] + wv[3:4, :] * xv
        for s in (1, 2, 3):
            y = y + wv[3 - s:4 - s, :] * _shift_down(xv, s, prev)
        outs[0][...] = y

    return _rowwise(name, body, src.shape[0], 512,
                    [('col', src, width, idx), ('prev8', src, width, idx), ('full', w), ('full', b)],
                    [('row', width, F32)])[0]


def _conv_bwd(name, dpre, src, width, idx, w):
    def body(i, nt, ins, outs, scr):
        dy = ins[0][...]
        nxt = jnp.where(i < nt - 1, ins[1][...], 0.0)
        xv = ins[2][...]
        prev = jnp.where(i > 0, ins[3][...], 0.0)
        wv = ins[4][...]
        dx = wv[3:4, :] * dy
        outs[1][3:4, :] += jnp.sum(dy * xv, axis=0, keepdims=True)
        outs[1][4:5, :] += jnp.sum(dy, axis=0, keepdims=True)
        for s in (1, 2, 3):
            dx = dx + wv[3 - s:4 - s, :] * _shift_up(dy, s, nxt)
            outs[1][3 - s:4 - s, :] += jnp.sum(dy * _shift_down(xv, s, prev), axis=0, keepdims=True)
        outs[0][...] = dx

    return _rowwise(name, body, src.shape[0], 512,
                    [('row', dpre), ('next8', dpre, width, 0), ('col', src, width, idx),
                     ('prev8', src, width, idx), ('full', w)],
                    [('row', width, F32), ('acc', (8, width), F32)])


def _att_bias(n, dil, head, transposed_first_block):
    qi = _iota((ATT_BLOCK, 2 * ATT_BLOCK), 0)
    ki = _iota((ATT_BLOCK, 2 * ATT_BLOCK), 1)
    dist = ATT_BLOCK + qi - ki
    valid = (dist >= 0) & (dist <= ATT_BLOCK) & ((n > 0) | (ki >= ATT_BLOCK))
    slope = 2.0 ** (-(head + 1))
    return jnp.where(valid, (-slope * dil) * dist.astype(F32), NEG)


def _head_mask():
    lane = _iota((ATT_BLOCK, LANES), 1)
    return lane < HEAD_DIM


def _att_fwd(dil, qkv_v, stats):
    n_l = qkv_v.shape[0]
    nb = n_l // ATT_BLOCK
    first = stats is None
    scale = HEAD_DIM ** -0.5

    def kern(*refs):
        n = pl.program_id(1)
        q_ref, kc_ref, kp_ref, vc_ref, vp_ref = refs[:5]
        if first:
            m_out, l_out, a_out = refs[5:]
        else:
            m_in, l_in, a_in, m_out, l_out, a_out = refs[5:]
        low = _head_mask()
        for p in range(N_HEADS // 2):
            sl = slice(LANES * p, LANES * (p + 1))
            q2 = q_ref[:, sl]
            k2 = jnp.concatenate([kp_ref[:, sl], kc_ref[:, sl]], axis=0).astype(BF16)
            v2 = jnp.concatenate([vp_ref[:, sl], vc_ref[:, sl]], axis=0).astype(BF16)
            res = []
            for e in range(2):
                keep = low if e == 0 else jnp.logical_not(low)
                qe = jnp.where(keep, q2, 0.0).astype(BF16)
                s = _dot_nt(qe, k2) * scale + _att_bias(n, dil, 2 * p + e, False)
                m_blk = jnp.max(s, axis=-1, keepdims=True)
                if first:
                    m_new = m_blk
                    pe = jnp.exp(s - m_new)
                    l_new = jnp.sum(pe, axis=-1, keepdims=True)
                    a_new = _dot(pe.astype(BF16), v2)
                else:
                    c0 = HEAD_DIM * e
                    m_old = m_in[:, LANES * p + c0:LANES * p + c0 + 1]
                    l_old = l_in[:, LANES * p + c0:LANES * p + c0 + 1]
                    m_new = jnp.maximum(m_old, m_blk)
                    pe = jnp.exp(s - m_new)
                    alpha = jnp.exp(m_old - m_new)
                    l_new = alpha * l_old + jnp.sum(pe, axis=-1, keepdims=True)
                    a_new = alpha * a_in[:, sl] + _dot(pe.astype(BF16), v2)
                res.append((m_new, l_new, a_new))
            m_out[:, sl] = jnp.where(low, res[0][0], res[1][0])
            l_out[:, sl] = jnp.where(low, res[0][1], res[1][1])
            a_out[:, sl] = jnp.where(low, res[0][2], res[1][2])

    blk = (ATT_BLOCK, ATT_W)
    in_specs = [pl.BlockSpec(blk, lambda r, n: (n, 3 * r)),
                pl.BlockSpec(blk, lambda r, n: (n, 3 * r + 1)),
                pl.BlockSpec(blk, lambda r, n: (jnp.maximum(n - 1, 0), 3 * r + 1)),
                pl.BlockSpec(blk, lambda r, n: (n, 3 * r + 2)),
                pl.BlockSpec(blk, lambda r, n: (jnp.maximum(n - 1, 0), 3 * r + 2))]
    args = [qkv_v] * 5
    st_spec = pl.BlockSpec(blk, lambda r, n: (n, r))
    if not first:
        in_specs += [st_spec] * 3
        args += list(stats)
    shp = jax.ShapeDtypeStruct((n_l, dil * ATT_W), F32)
    return pl.pallas_call(kern, name="att_fwd_d%d" % dil, grid=(dil, nb), in_specs=in_specs,
                          out_specs=[st_spec] * 3, out_shape=[shp] * 3, compiler_params=_params(2))(*args)


def _att_finish(m, l, acc):
    def body(i, nt, ins, outs, scr):
        lv = ins[1][...]
        outs[0][...] = ins[2][...] / lv
        outs[1][...] = ins[0][...] + jnp.log(lv)

    return _rowwise("att_finish", body, m.shape[0], 512, [('row', m), ('row', l), ('row', acc)],
                    [('row', ATT_W, F32), ('row', ATT_W, F32)])


def _att_delta(d_out, out):
    def body(i, nt, ins, outs, scr):
        prod = ins[0][...] * ins[1][...]
        ones = jnp.where(_iota((ATT_W, ATT_W), 0) // HEAD_DIM == _iota((ATT_W, ATT_W), 1) // HEAD_DIM,
                         1.0, 0.0).astype(F32)
        outs[0][...] = _dot_hi(prod, ones)

    return _rowwise("att_delta", body, out.shape[0], 512, [('col', d_out, ATT_W, 0), ('row', out)],
                    [('row', ATT_W, F32)])[0]


def _att_bwd_dq(dil, qkv_v, do_v, lse_v, delta_v, dq_in):
    n_l = qkv_v.shape[0]
    nb = n_l // ATT_BLOCK
    first = dq_in is None
    scale = HEAD_DIM ** -0.5

    def kern(*refs):
        n = pl.program_id(1)
        q_ref, kc_ref, kp_ref, vc_ref, vp_ref, do_ref, lse_ref, dl_ref = refs[:8]
        dq_out = refs[-1]
        low = _head_mask()
        for p in range(N_HEADS // 2):
            sl = slice(LANES * p, LANES * (p + 1))
            q2 = q_ref[:, sl]
            do2 = do_ref[:, sl]
            k2 = jnp.concatenate([kp_ref[:, sl], kc_ref[:, sl]], axis=0).astype(BF16)
            v2 = jnp.concatenate([vp_ref[:, sl], vc_ref[:, sl]], axis=0).astype(BF16)
            res = []
            for e in range(2):
                keep = low if e == 0 else jnp.logical_not(low)
                c0 = LANES * p + HEAD_DIM * e
                qe = jnp.where(keep, q2, 0.0).astype(BF16)
                doe = jnp.where(keep, do2, 0.0).astype(BF16)
                s = _dot_nt(qe, k2) * scale + _att_bias(n, dil, 2 * p + e, False)
                pe = jnp.exp(s - lse_ref[:, c0:c0 + 1])
                dp = _dot_nt(doe, v2)
                ds = pe * (dp - dl_ref[:, c0:c0 + 1])
                res.append(_dot(ds.astype(BF16), k2) * scale)
            dq = jnp.where(low, res[0], res[1])
            if not first:
                dq = dq + refs[8][:, sl]
            dq_out[:, sl] = dq

    blk = (ATT_BLOCK, ATT_W)
    st_spec = pl.BlockSpec(blk, lambda r, n: (n, r))
    in_specs = [pl.BlockSpec(blk, lambda r, n: (n, 3 * r)),
                pl.BlockSpec(blk, lambda r, n: (n, 3 * r + 1)),
                pl.BlockSpec(blk, lambda r, n: (jnp.maximum(n - 1, 0), 3 * r + 1)),
                pl.BlockSpec(blk, lambda r, n: (n, 3 * r + 2)),
                pl.BlockSpec(blk, lambda r, n: (jnp.maximum(n - 1, 0), 3 * r + 2)),
                st_spec, st_spec, st_spec]
    args = [qkv_v] * 5 + [do_v, lse_v, delta_v]
    if not first:
        in_specs.append(st_spec)
        args.append(dq_in)
    return pl.pallas_call(kern, name="att_bwd_dq_d%d" % dil, grid=(dil, nb), in_specs=in_specs,
                          out_specs=st_spec, out_shape=jax.ShapeDtypeStruct((n_l, dil * ATT_W), F32),
                          compiler_params=_params(2))(*args)


def _att_bwd_dkv(dil, qkv_v, do_v, lse_v, delta_v, dkv_in):
    n_l = qkv_v.shape[0]
    nb = n_l // ATT_BLOCK
    first = dkv_in is None
    scale = HEAD_DIM ** -0.5

    def kern(*refs):
        j = pl.program_id(1)
        k_ref, v_ref, qc_ref, qn_ref, doc_ref, don_ref, lc_ref, ln_ref, dc_ref, dn_ref = refs[:10]
        dk_out, dv_out = refs[-2:]
        low = _head_mask()
        row = _iota((2 * ATT_BLOCK, ATT_BLOCK), 0)
        key = _iota((2 * ATT_BLOCK, ATT_BLOCK), 1)
        dist = row - key
        valid = (dist >= 0) & (dist <= ATT_BLOCK) & ((row < ATT_BLOCK) | (j < nb - 1))
        for p in range(N_HEADS // 2):
            sl = slice(LANES * p, LANES * (p + 1))
            k2 = k_ref[:, sl]
            v2 = v_ref[:, sl]
            q2 = jnp.concatenate([qc_ref[:, sl], qn_ref[:, sl]], axis=0)
            do2 = jnp.concatenate([doc_ref[:, sl], don_ref[:, sl]], axis=0)
            lse2 = jnp.concatenate([lc_ref[:, sl], ln_ref[:, sl]], axis=0)
            dl2 = jnp.concatenate([dc_ref[:, sl], dn_ref[:, sl]], axis=0)
            low2 = _iota((2 * ATT_BLOCK, LANES), 1) < HEAD_DIM
            dks, dvs = [], []
            for e in range(2):
                keep = low if e == 0 else jnp.logical_not(low)
                keep2 = low2 if e == 0 else jnp.logical_not(low2)
                c0 = HEAD_DIM * e
                slope = 2.0 ** (-(2 * p + e + 1))
                bias = jnp.where(valid, (-slope * dil) * dist.astype(F32), NEG)
                ke = jnp.where(keep, k2, 0.0).astype(BF16)
                ve = jnp.where(keep, v2, 0.0).astype(BF16)
                q2b = q2.astype(BF16)
                do2b = do2.astype(BF16)
                s = _dot_nt(q2b, ke) * scale + bias
                pe = jnp.exp(s - lse2[:, c0:c0 + 1])
                dp = _dot_nt(do2b, ve)
                ds = pe * (dp - dl2[:, c0:c0 + 1])
                dvs.append(_dot_tn(pe.astype(BF16), jnp.where(keep2, do2, 0.0).astype(BF16)))
                dks.append(_dot_tn(ds.astype(BF16), jnp.where(keep2, q2, 0.0).astype(BF16)) * scale)
            dk = jnp.where(low, dks[0], dks[1])
            dv = jnp.where(low, dvs[0], dvs[1])
            if not first:
                dk = dk + refs[10][:, sl]
                dv = dv + refs[11][:, sl]
            dk_out[:, sl] = dk
            dv_out[:, sl] = dv

    blk = (ATT_BLOCK, ATT_W)
    cur = pl.BlockSpec(blk, lambda r, j: (j, r))
    nxt = pl.BlockSpec(blk, lambda r, j: (jnp.minimum(j + 1, nb - 1), r))
    in_specs = [pl.BlockSpec(blk, lambda r, j: (j, 3 * r + 1)),
                pl.BlockSpec(blk, lambda r, j: (j, 3 * r + 2)),
                pl.BlockSpec(blk, lambda r, j: (j, 3 * r)),
                pl.BlockSpec(blk, lambda r, j: (jnp.minimum(j + 1, nb - 1), 3 * r)),
                cur, nxt, cur, nxt, cur, nxt]
    args = [qkv_v] * 4 + [do_v, do_v, lse_v, lse_v, delta_v, delta_v]
    if not first:
        in_specs += [cur, cur]
        args += list(dkv_in)
    shp = jax.ShapeDtypeStruct((n_l, dil * ATT_W), F32)
    return pl.pallas_call(kern, name="att_bwd_dkv_d%d" % dil, grid=(dil, nb), in_specs=in_specs,
                          out_specs=[cur, cur], out_shape=[shp, shp], compiler_params=_params(2))(*args)


def _attention_fwd(qkv):
    t = qkv.shape[0]
    stats = None
    for dil in ATT_DILATIONS:
        if stats is not None:
            stats = [s.reshape(t // dil, dil * ATT_W) for s in stats]
        stats = _att_fwd(dil, qkv.reshape(t // dil, dil * D_MIX), stats)
    m, l, acc = [s.reshape(t, ATT_W) for s in stats]
    return _att_finish(m, l, acc)


def _attention_bwd(qkv, d_mix, out, lse):
    t = qkv.shape[0]
    delta = _att_delta(d_mix, out)
    d_att = d_mix[:, :ATT_W]
    dq, dkv = None, None
    for dil in ATT_DILATIONS:
        view = lambda a: a.reshape(t // dil, dil * a.shape[1])
        qkv_v = view(qkv)
        do_v, lse_v, dl_v = view(d_att), view(lse), view(delta)
        dq = _att_bwd_dq(dil, qkv_v, do_v, lse_v, dl_v, None if dq is None else view(dq.reshape(t, ATT_W)))
        dkv = _att_bwd_dkv(dil, qkv_v, do_v, lse_v, dl_v,
                           None if dkv is None else [view(a.reshape(t, ATT_W)) for a in dkv])
    return dq.reshape(t, ATT_W), dkv[0].reshape(t, ATT_W), dkv[1].reshape(t, ATT_W)


def _ssd_chunk_common(pre, dtraw, bias_row, alog_row):
    q = SSD_CHUNK
    act = _silu(pre)
    lane = _iota((q, LANES), 1)
    dt = jnp.where(lane < N_HEADS, _softplus(dtraw + bias_row), 0.0)
    a_row = -jnp.exp(alog_row)
    tril = jnp.where(_iota((q, q), 0) >= _iota((q, q), 1), 1.0, 0.0).astype(F32)
    cs = _dot_hi(tril, dt * a_row)
    cs_last = cs[q - 1:q, :]
    return act, dt, a_row, tril, cs, cs_last


def _ssd_lmat(cs, cs_t, h):
    q = SSD_CHUNK
    seg = cs[:, h:h + 1] - cs_t[h:h + 1, :]
    causal = _iota((q, q), 0) >= _iota((q, q), 1)
    return jnp.exp(jnp.where(causal, seg, NEG))


def _ssd_gate_norm(y, z, norm_w):
    sz = _silu(z)
    yg = y * sz
    half = SSD_W // 2
    outs, rss = [], []
    for g in range(2):
        part = yg[:, half * g:half * (g + 1)]
        rs = lax.rsqrt(jnp.mean(part * part, axis=-1, keepdims=True) + SSD_NORM_EPS)
        outs.append(part * rs)
        rss.append(rs)
    yn = jnp.concatenate(outs, axis=1)
    return sz, yn, rss, yn * norm_w


def _ssd_fwd(pre, rest, dt_bias, a_log, d_skip, norm_w):
    t = pre.shape[0]
    q = SSD_CHUNK

    def body(c, nc, ins, outs, scr):
        pre_ref, z_ref, dtr_ref, bias_ref, alog_ref, dsk_ref, nw_ref = ins
        out_ref, y_ref, sp_ref = outs
        s_ref = scr[0]

        @pl.when(c == 0)
        def _():
            s_ref[...] = jnp.zeros(s_ref.shape, F32)

        act, dt, a_row, tril, cs, cs_last = _ssd_chunk_common(pre_ref[...], dtr_ref[...], bias_ref[...],
                                                               alog_ref[...])
        x = act[:, :SSD_W]
        cs_t = cs.T
        e_col = jnp.exp(cs)
        w = jnp.exp(cs_last - cs) * dt
        expand = _expand_mat()
        w_x = _dot_hi(w, expand)
        dt_x = _dot_hi(dt, expand)
        e_x = _dot_hi(e_col, expand)
        d_x = _dot_hi(dsk_ref[...], expand)
        cd_x = _dot_hi(jnp.exp(cs_last), expand)
        s_prev = s_ref[...]
        sp_ref[...] = s_prev
        xw = (x * w_x).astype(BF16)
        xd = (x * dt_x).astype(BF16)
        low = _head_mask()
        y_parts, s_parts = [], []
        for g in range(2):
            bg = act[:, SSD_W + SSD_STATE * g:SSD_W + SSD_STATE * (g + 1)].astype(BF16)
            cg = act[:, SSD_W + 2 * SSD_STATE + SSD_STATE * g:SSD_W + 2 * SSD_STATE + SSD_STATE * (g + 1)].astype(BF16)
            gsl = slice(256 * g, 256 * (g + 1))
            gmat = _dot_nt(cg, bg)
            s_parts.append(_dot_tn(bg, xw[:, gsl]))
            y0 = _dot(cg, s_prev[:, gsl].astype(BF16))
            for pp in range(2):
                pair = 2 * g + pp
                psl = slice(LANES * pair, LANES * (pair + 1))
                yd = []
                for e in range(2):
                    h = 2 * pair + e
                    mh = (gmat * _ssd_lmat(cs, cs_t, h)).astype(BF16)
                    yd.append(_dot(mh, xd[:, psl]))
                y_parts.append(jnp.where(low, yd[0], yd[1]) + e_x[:, psl] * y0[:, LANES * pp:LANES * (pp + 1)])
        y = jnp.concatenate(y_parts, axis=1) + d_x * x
        s_ref[...] = cd_x * s_prev + jnp.concatenate(s_parts, axis=1)
        y_ref[...] = y
        out_ref[...] = _ssd_gate_norm(y, z_ref[...], nw_ref[...])[3]

    return _rowwise("ssd_fwd", body, t, q,
                    [('row', pre), ('col', rest, SSD_W, 2), ('col', rest, LANES, 20), ('full', dt_bias),
                     ('full', a_log), ('full', d_skip), ('full', norm_w)],
                    [('row', SSD_W, F32), ('row', SSD_W, F32), ('row', SSD_W, F32)],
                    scratch=[pltpu.VMEM((SSD_STATE, SSD_W), F32)])


def _ssd_bwd(pre, rest, y, s_prev_all, d_mix, dt_bias, a_log, d_skip, norm_w):
    t = pre.shape[0]
    q = SSD_CHUNK

    def body(c, nc, ins, outs, scr):
        pre_ref, z_ref, dtr_ref, y_ref, sp_ref, do_ref, bias_ref, alog_ref, dsk_ref, nw_ref = ins
        dpre_ref, dz_ref, ddt_ref, a128_ref, a512_ref = outs
        ds_ref = scr[0]

        @pl.when(c == nc - 1)
        def _():
            ds_ref[...] = jnp.zeros(ds_ref.shape, F32)

        pre_v = pre_ref[...]
        dtr = dtr_ref[...]
        act, dt, a_row, tril, cs, cs_last = _ssd_chunk_common(pre_v, dtr, bias_ref[...], alog_ref[...])
        x = act[:, :SSD_W]
        cs_t = cs.T
        e_col = jnp.exp(cs)
        decay_end = jnp.exp(cs_last - cs)
        w = decay_end * dt
        cd = jnp.exp(cs_last)
        expand = _expand_mat()
        reduce = _reduce_mat()
        w_x = _dot_hi(w, expand)
        dt_x = _dot_hi(dt, expand)
        e_x = _dot_hi(e_col, expand)
        d_x = _dot_hi(dsk_ref[...], expand)
        cd_x = _dot_hi(cd, expand)
        s_prev = sp_ref[...]
        d_s = ds_ref[...]
        xw = (x * w_x).astype(BF16)
        xd = (x * dt_x).astype(BF16)
        low = _head_mask()
        lane = _iota((q, LANES), 1)
        sub = _iota((q, LANES), 0)

        yv, zv, nw = y_ref[...], z_ref[...], nw_ref[...]
        d_out = do_ref[...]
        sz, yn, rss, _ = _ssd_gate_norm(yv, zv, nw)
        a512_ref[0:1, :] += jnp.sum(d_out * yn, axis=0, keepdims=True)
        dyn = d_out * nw
        half = SSD_W // 2
        dyg_parts = []
        for g in range(2):
            hs = slice(half * g, half * (g + 1))
            dyg_parts.append(rss[g] * (dyn[:, hs] - yn[:, hs] * jnp.mean(dyn[:, hs] * yn[:, hs], axis=-1,
                                                                          keepdims=True)))
        dyg = jnp.concatenate(dyg_parts, axis=1)
        dy = dyg * sz
        dz_ref[...] = dyg * yv * _dsilu(zv)

        a128_ref[2:3, :] += _dot_hi(jnp.sum(dy * x, axis=0, keepdims=True), reduce)
        dx = d_x * dy

        dy0 = e_x * dy
        dyb = dy.astype(BF16)
        dcs = jnp.zeros((q, LANES), F32)
        dcs_rows = jnp.zeros((q, LANES), F32)
        ddt = jnp.zeros((q, LANES), F32)
        ds_prev_parts, z_parts, db_parts, dc_parts, dxd_parts, y0_parts = [], [], [], [], [], []
        for g in range(2):
            bg = act[:, SSD_W + SSD_STATE * g:SSD_W + SSD_STATE * (g + 1)].astype(BF16)
            cg = act[:, SSD_W + 2 * SSD_STATE + SSD_STATE * g:SSD_W + 2 * SSD_STATE + SSD_STATE * (g + 1)].astype(BF16)
            gsl = slice(256 * g, 256 * (g + 1))
            spg = s_prev[:, gsl].astype(BF16)
            dsg = d_s[:, gsl].astype(BF16)
            dy0g = dy0[:, gsl].astype(BF16)
            gmat = _dot_nt(cg, bg)
            y0_parts.append(_dot(cg, spg))
            dc_g = _dot_nt(dy0g, spg)
            ds_prev_parts.append(_dot_tn(cg, dy0g))
            z_parts.append(_dot(bg, dsg))
            db_g = _dot_nt(xw[:, gsl], dsg)
            dg_acc = jnp.zeros((q, q), F32)
            for pp in range(2):
                pair = 2 * g + pp
                psl = slice(LANES * pair, LANES * (pair + 1))
                dxd_e = []
                for e in range(2):
                    h = 2 * pair + e
                    keep = low if e == 0 else jnp.logical_not(low)
                    lm = _ssd_lmat(cs, cs_t, h)
                    mh = gmat * lm
                    dm = _dot_nt(jnp.where(keep, dy[:, psl], 0.0).astype(BF16), xd[:, psl])
                    dxd_e.append(_dot_tn(mh.astype(BF16), dyb[:, psl]))
                    wm = dm * mh
                    dcs = dcs + jnp.where(lane == h, jnp.sum(wm, axis=1, keepdims=True), 0.0)
                    dcs_rows = dcs_rows - jnp.where(sub == h, jnp.sum(wm, axis=0, keepdims=True), 0.0)
                    dg_acc = dg_acc + dm * lm
                dxd_parts.append(jnp.where(low, dxd_e[0], dxd_e[1]))
            dgb = dg_acc.astype(BF16)
            dc_parts.append(dc_g + _dot(dgb, bg))
            db_parts.append(db_g + _dot_tn(dgb, cg))
        y0 = jnp.concatenate(y0_parts, axis=1)
        zmat = jnp.concatenate(z_parts, axis=1)
        dxd = jnp.concatenate(dxd_parts, axis=1)
        ds_prev = jnp.concatenate(ds_prev_parts, axis=1) + cd_x * d_s
        ds_ref[...] = ds_prev

        dcs = dcs + _dot_hi(dy * y0, reduce) * e_col
        dcd = _dot_hi(jnp.sum(d_s * s_prev, axis=0, keepdims=True), reduce)
        dlast = dcd * cd
        dx = dx + w_x * zmat + dxd * dt_x
        dw = _dot_hi(zmat * x, reduce)
        ddt = ddt + dw * decay_end + _dot_hi(dxd * x, reduce)
        dwl = dw * w
        dcs = dcs - dwl
        dlast = dlast + jnp.sum(dwl, axis=0, keepdims=True)
        dcs = dcs + dcs_rows.T + jnp.where(sub == q - 1, dlast, 0.0)
        dda = _dot_hi(tril.T, dcs)
        ddt = ddt + dda * a_row
        a128_ref[1:2, :] += jnp.sum(dda * dt, axis=0, keepdims=True) * a_row
        draw = jnp.where(lane < N_HEADS, ddt * _sigmoid(dtr + bias_ref[...]), 0.0)
        a128_ref[0:1, :] += jnp.sum(draw, axis=0, keepdims=True)
        ddt_ref[...] = draw
        dact = jnp.concatenate([dx] + db_parts + dc_parts, axis=1)
        dpre_ref[...] = dact * _dsilu(pre_v)

    return _rowwise("ssd_bwd", body, t, q,
                    [('row', pre), ('col', rest, SSD_W, 2), ('col', rest, LANES, 20), ('row', y),
                     ('row', s_prev_all), ('col', d_mix, SSD_W, 1), ('full', dt_bias), ('full', a_log),
                     ('full', d_skip), ('full', norm_w)],
                    [('row', SSD_CONV, F32), ('row', SSD_W, F32), ('row', LANES, F32),
                     ('acc', (8, LANES), F32), ('acc', (8, SSD_W), F32)],
                    scratch=[pltpu.VMEM((SSD_STATE, SSD_W), F32)], reverse=True)


LRU_TM = 256


def _lru_gates(xc, wa, ba, wx, bx, lam):
    xb = xc.astype(BF16)
    r = _sigmoid(_dot(xb, wa) + ba)
    i = _sigmoid(_dot(xb, wx) + bx)
    sp = _softplus(-lam)
    a = jnp.exp(-LRU_C * r * sp)
    mult = jnp.sqrt(1.0 - a * a)
    return r, i, sp, a, mult


def _lru_fwd(xc, rest, wa, ba, wx, bx, lam):
    def body(i, nt, ins, outs, scr):
        xc_ref, g_ref, wa_ref, ba_ref, wx_ref, bx_ref, lam_ref = ins
        carry = scr[0]

        @pl.when(i == 0)
        def _():
            carry[...] = jnp.zeros(carry.shape, F32)

        xv = xc_ref[...]
        r, ig, sp, a, mult = _lru_gates(xv, wa_ref[...], ba_ref[...], wx_ref[...], bx_ref[...], lam_ref[...])
        u = mult * (ig * xv)
        row = _iota(a.shape, 0)
        s = 1
        while s < LRU_TM:
            a_sh = jnp.where(row >= s, pltpu.roll(a, s, 0), 1.0)
            u_sh = jnp.where(row >= s, pltpu.roll(u, s, 0), 0.0)
            u = a * u_sh + u
            a = a * a_sh
            s *= 2
        h = u + a * carry[0:1, :]
        carry[0:1, :] = h[LRU_TM - 1:LRU_TM, :]
        outs[1][...] = h
        outs[0][...] = h * _gelu(g_ref[...])

    return _rowwise("lru_fwd", body, xc.shape[0], LRU_TM,
                    [('row', xc), ('col', rest, LRU_W, 3), ('full', wa), ('full', ba), ('full', wx),
                     ('full', bx), ('full', lam)],
                    [('row', LRU_W, F32), ('row', LRU_W, F32)], scratch=[pltpu.VMEM((8, LRU_W), F32)])


def _lru_bwd(xc, rest, h, d_mix, wa, ba, wx, bx, lam, wa_t, wx_t):
    def body(i, nt, ins, outs, scr):
        xc_ref, g_ref, h_ref, hp_ref, do_ref, wa_ref, ba_ref, wx_ref, bx_ref, lam_ref, wat_ref, wxt_ref = ins
        dxc_ref, dg_ref, dza_ref, dzi_ref, acc_ref = outs
        carry = scr[0]

        @pl.when(i == nt - 1)
        def _():
            carry[...] = jnp.zeros(carry.shape, F32)

        xv, gv, hv, d_out = xc_ref[...], g_ref[...], h_ref[...], do_ref[...]
        r, ig, sp, a, mult = _lru_gates(xv, wa_ref[...], ba_ref[...], wx_ref[...], bx_ref[...], lam_ref[...])
        dg_ref[...] = d_out * hv * _dgelu(gv)
        gsum = d_out * _gelu(gv)
        row = _iota(a.shape, 0)
        b = jnp.where(row < LRU_TM - 1, pltpu.roll(a, LRU_TM - 1, 0), 1.0)
        s = 1
        while s < LRU_TM:
            keep = row < LRU_TM - s
            b_sh = jnp.where(keep, pltpu.roll(b, LRU_TM - s, 0), 1.0)
            g_sh = jnp.where(keep, pltpu.roll(gsum, LRU_TM - s, 0), 0.0)
            gsum = gsum + b * g_sh
            b = b * b_sh
            s *= 2
        dh = gsum + b * carry[0:1, :]
        carry[0:1, :] = a[0:1, :] * dh[0:1, :]
        h_prev = _shift_down(hv, 1, jnp.where(i > 0, hp_ref[...], 0.0))
        du = dh
        dmult = du * ig * xv
        di = du * mult * xv
        dxc = du * mult * ig
        da = dh * h_prev - dmult * a / mult
        dlog = da * a
        dr = dlog * (-LRU_C) * sp
        acc_ref[2:3, :] += jnp.sum(dlog * (-LRU_C) * r, axis=0, keepdims=True)
        dza = dr * r * (1.0 - r)
        dzi = di * ig * (1.0 - ig)
        acc_ref[0:1, :] += jnp.sum(dza, axis=0, keepdims=True)
        acc_ref[1:2, :] += jnp.sum(dzi, axis=0, keepdims=True)
        dzab, dzib = dza.astype(BF16), dzi.astype(BF16)
        dza_ref[...] = dzab
        dzi_ref[...] = dzib
        dxc_ref[...] = dxc + _dot(dzab, wat_ref[...]) + _dot(dzib, wxt_ref[...])

    return _rowwise("lru_bwd", body, xc.shape[0], LRU_TM,
                    [('row', xc), ('col', rest, LRU_W, 3), ('row', h), ('prev8', h, LRU_W, 0),
                     ('col', d_mix, LRU_W, 2), ('full', wa), ('full', ba), ('full', wx), ('full', bx),
                     ('full', lam), ('full', wa_t), ('full', wx_t)],
                    [('row', LRU_W, F32), ('row', LRU_W, F32), ('row', LRU_W, BF16), ('row', LRU_W, BF16),
                     ('acc', (8, LRU_W), F32)],
                    scratch=[pltpu.VMEM((8, LRU_W), F32)], reverse=True)


def _swiglu_act(gu):
    def body(i, nt, ins, outs, scr):
        outs[0][...] = (_silu(ins[0][...]) * ins[1][...]).astype(BF16)

    return _rowwise("swiglu_act", body, gu.shape[0], 512, [('col', gu, D_FFP, 0), ('col', gu, D_FFP, 1)],
                    [('row', D_FFP, BF16)])[0]


def _swiglu_bwd(gu, dact):
    def body(i, nt, ins, outs, scr):
        gv, uv, da = ins[0][...], ins[1][...], ins[2][...]
        outs[0][:, :D_FFP] = (da * uv * _dsilu(gv)).astype(BF16)
        outs[0][:, D_FFP:] = (da * _silu(gv)).astype(BF16)

    return _rowwise("swiglu_bwd", body, gu.shape[0], 256,
                    [('col', gu, D_FFP, 0), ('col', gu, D_FFP, 1), ('row', dact)], [('row', 2 * D_FFP, BF16)])[0]


def _layer_fwd(x, w, l):
    tag = "_l%d" % l
    h = _rmsnorm_fwd("norm_mix" + tag, x, w['norm_mix'][l])
    qkv = _mm("proj_qkv" + tag, [h], w['w_qkv'][l], tn=768)
    rest = _mm("proj_rest" + tag, [h], w['w_rest'][l], tn=896)
    att, lse = _attention_fwd(qkv)
    pre = _conv_fwd("ssd_conv" + tag, rest, SSD_CONV, 0, w['ssd_conv_w'][l], w['ssd_conv_b'][l])
    ssd, y, s_prev = _ssd_fwd(pre, rest, w['ssd_dt_bias'][l], w['ssd_a_log'][l], w['ssd_d'][l], w['ssd_norm'][l])
    xc = _conv_fwd("lru_conv" + tag, rest, LRU_W, 4, w['lru_conv_w'][l], w['lru_conv_b'][l])
    lru, hl = _lru_fwd(xc, rest, w['lru_wa'][l], w['lru_ba'][l], w['lru_wx'][l], w['lru_bx'][l], w['lru_lambda'][l])
    x_mid = _mm("proj_out" + tag, [att, ssd, lru], w['w_out'][l], res=x, tn=512)
    h2 = _rmsnorm_fwd("norm_ffn" + tag, x_mid, w['norm_ffn'][l])
    gu = _mm("proj_gu" + tag, [h2], w['w_gu'][l], tn=1536)
    act = _swiglu_act(gu)
    x_next = _mm("proj_down" + tag, [act], w['w_down'][l], res=x_mid, tn=512)
    saved = dict(x=x, h=h, qkv=qkv, rest=rest, att=att, lse=lse, pre=pre, ssd=ssd, y=y, s_prev=s_prev, xc=xc,
                 lru=lru, hl=hl, x_mid=x_mid, h2=h2, gu=gu, act=act)
    return x_next, saved


def _layer_bwd(dx_next, sv, w, l):
    tag = "_l%d_b" % l
    t = dx_next.shape[0]
    g = {}
    dact = _mm("d_act" + tag, [dx_next], w['w_down_t'][l], tn=1536)
    g['w_down'] = _mm_tn("dw_down" + tag, sv['act'], dx_next, tk=1536)
    dgu = _swiglu_bwd(sv['gu'], dact)
    dh2 = _mm("d_h2" + tag, [dgu], w['w_gu_t'][l], tn=512, tm=256)
    dw_gu = _mm_tn("dw_gu" + tag, sv['h2'], dgu, tn=1536)
    g['w_gate'], g['w_up'] = dw_gu[:, :D_FFP], dw_gu[:, D_FFP:]
    dx_mid, acc = _rmsnorm_bwd("norm_ffn" + tag, dh2, sv['x_mid'], w['norm_ffn'][l], dx_next)
    g['norm_ffn'] = acc[0]
    d_mix = _mm("d_mix" + tag, [dx_mid], w['w_out_t'][l], tn=768)
    g['w_out'] = jnp.concatenate([_mm_tn("dw_out%d" % k + tag, a, dx_mid)
                                  for k, a in enumerate((sv['att'], sv['ssd'], sv['lru']))], axis=0)
    dxc, dgl, dza, dzi, acc = _lru_bwd(sv['xc'], sv['rest'], sv['hl'], d_mix, w['lru_wa'][l], w['lru_ba'][l],
                                       w['lru_wx'][l], w['lru_bx'][l], w['lru_lambda'][l],
                                       w['lru_wa_t'][l], w['lru_wx_t'][l])
    g['lru_ba'], g['lru_bx'] = acc[0], acc[1]
    g['lru_lambda'] = acc[2] * (-_sigmoid(-w['lru_lambda'][l][0]))
    g['lru_wa'] = _diag_blocks(_mm_tn("dw_lru_a" + tag, sv['xc'], dza))
    g['lru_wx'] = _diag_blocks(_mm_tn("dw_lru_x" + tag, sv['xc'], dzi))
    dxl, acc = _conv_bwd("lru_conv" + tag, dxc, sv['rest'], LRU_W, 4, w['lru_conv_w'][l])
    g['lru_conv_w'], g['lru_conv_b'] = acc[:4], acc[4]
    dpre, dz, ddt, a128, a512 = _ssd_bwd(sv['pre'], sv['rest'], sv['y'], sv['s_prev'], d_mix, w['ssd_dt_bias'][l],
                                         w['ssd_a_log'][l], w['ssd_d'][l], w['ssd_norm'][l])
    g['ssd_dt_bias'], g['ssd_a_log'], g['ssd_d'] = a128[0, :N_HEADS], a128[1, :N_HEADS], a128[2, :N_HEADS]
    g['ssd_norm'] = a512[0]
    dxbc, acc = _conv_bwd("ssd_conv" + tag, dpre, sv['rest'], SSD_CONV, 0, w['ssd_conv_w'][l])
    g['ssd_conv_w'], g['ssd_conv_b'] = acc[:4], acc[4]
    dq, dk, dv = _attention_bwd(sv['qkv'], d_mix, sv['att'], sv['lse'])
    pieces = [dq, dk, dv, dxbc, dz, dgl, dxl, ddt]
    dh = _mm("d_h" + tag, pieces, w['w_in_t'][l], tn=512)
    dws = [_mm_tn("dw_in%d" % k + tag, sv['h'], p) for k, p in enumerate(pieces)]
    g['w_in'] = jnp.concatenate([dws[0], dws[1], dws[2], dws[4], dws[3], _dt_tile_place(dws[7]), dws[5], dws[6]],
                                axis=1)
    dx, acc = _rmsnorm_bwd("norm_mix" + tag, dh, sv['x'], w['norm_mix'][l], dx_mid)
    g['norm_mix'] = acc[0]
    return dx, g


def _diag_blocks(m):
    return jnp.stack([m[64 * n:64 * (n + 1), 64 * n:64 * (n + 1)] for n in range(8)])


def _block_diag(w):
    eye = jnp.eye(8, dtype=w.dtype)
    return (w[:, :, None, :] * eye[:, None, :, None]).reshape(512, 512)


_ANY = pl.BlockSpec(memory_space=pl.ANY)
_MESH = pl.DeviceIdType.MESH


def _all_gather(name, xs):
    n = len(xs)

    def body(*refs):
        x_refs, out_refs = refs[:n], refs[n:2 * n]
        send_sems, recv_sems, local_sems = refs[2 * n:]
        x_, y_, c_ = lax.axis_index("x"), lax.axis_index("y"), lax.axis_index("c")
        me, sibling = (x_, y_, c_), (x_, y_, 1 - c_)
        chips = [(1 - x_, y_), (x_, 1 - y_), (1 - x_, 1 - y_)]

        def slot(a, px, py, pc):
            return out_refs[a].at[4 * px + 2 * py + pc]

        def copy(a, k, block, to, src=None):
            return pltpu.make_async_remote_copy(
                src_ref=slot(a, *block) if src is None else src, dst_ref=slot(a, *block),
                send_sem=send_sems.at[a, k], recv_sem=recv_sems.at[a, k], device_id=to, device_id_type=_MESH)

        mine = [pltpu.make_async_copy(x_refs[a], slot(a, *me), local_sems.at[a]) for a in range(n)]
        for cp in mine:
            cp.start()
        first = []
        for a in range(n):
            first.append(copy(a, 0, me, sibling, src=x_refs[a]))
            first += [copy(a, 1 + j, me, (*chip, c_), src=x_refs[a]) for j, chip in enumerate(chips)]
        for cp in first:
            cp.start()
        passed = []
        for j, chip in enumerate(chips):
            for a in range(n):
                copy(a, 1 + j, (*chip, c_), me).wait_recv()
                fwd = copy(a, 4 + j, (*chip, c_), sibling)
                fwd.start()
                passed.append(fwd)
        for a in range(n):
            copy(a, 0, sibling, me).wait_recv()
            for j, chip in enumerate(chips):
                copy(a, 4 + j, (*chip, 1 - c_), me).wait_recv()
        for cp in first + passed:
            cp.wait_send()
        for cp in mine:
            cp.wait()

    return pl.pallas_call(
        body, name=name, out_shape=[jax.ShapeDtypeStruct((N_DEV,) + x.shape, x.dtype) for x in xs],
        in_specs=[_ANY] * n, out_specs=[_ANY] * n,
        scratch_shapes=[pltpu.SemaphoreType.DMA((n, 7)), pltpu.SemaphoreType.DMA((n, 7)),
                        pltpu.SemaphoreType.DMA((n,))],
    )(*xs)


def _all_to_all(name, xs):
    n = len(xs)

    def body(*refs):
        x_refs, out_refs = refs[:n], refs[n:2 * n]
        send_sems, recv_sems, local_sems = refs[2 * n:]
        x_, y_, c_ = lax.axis_index("x"), lax.axis_index("y"), lax.axis_index("c")
        me = 4 * x_ + 2 * y_ + c_

        def peer(k):
            return ((1 - x_) if k & 4 else x_, (1 - y_) if k & 2 else y_, (1 - c_) if k & 1 else c_)

        def copy(a, k):
            px, py, pc = peer(k)
            return pltpu.make_async_remote_copy(
                src_ref=x_refs[a].at[4 * px + 2 * py + pc], dst_ref=out_refs[a].at[me],
                send_sem=send_sems.at[a, k - 1], recv_sem=recv_sems.at[a, k - 1],
                device_id=(px, py, pc), device_id_type=_MESH)

        def arrival(a, k):
            px, py, pc = peer(k)
            return pltpu.make_async_remote_copy(
                src_ref=x_refs[a].at[me], dst_ref=out_refs[a].at[4 * px + 2 * py + pc],
                send_sem=send_sems.at[a, k - 1], recv_sem=recv_sems.at[a, k - 1],
                device_id=(px, py, pc), device_id_type=_MESH)

        mine = [pltpu.make_async_copy(x_refs[a].at[me], out_refs[a].at[me], local_sems.at[a]) for a in range(n)]
        for cp in mine:
            cp.start()
        copies = [copy(a, k) for a in range(n) for k in range(1, N_DEV)]
        for cp in copies:
            cp.start()
        for a in range(n):
            for k in range(1, N_DEV):
                arrival(a, k).wait_recv()
        for cp in copies:
            cp.wait_send()
        for cp in mine:
            cp.wait()

    return pl.pallas_call(
        body, name=name, out_shape=[jax.ShapeDtypeStruct(x.shape, x.dtype) for x in xs],
        in_specs=[_ANY] * n, out_specs=[_ANY] * n,
        scratch_shapes=[pltpu.SemaphoreType.DMA((n, 7)), pltpu.SemaphoreType.DMA((n, 7)),
                        pltpu.SemaphoreType.DMA((n,))],
    )(*xs)


def _window_offset():
    me = 4 * lax.axis_index("x") + 2 * lax.axis_index("y") + lax.axis_index("c")
    return jnp.where(me < 6, me, me + 120)


def _place_w_in(shard):
    def body(i, nt, ins, outs, scr):
        outs[0][...] = pltpu.roll(ins[0][...], _window_offset(), 1).astype(BF16)

    return _rowwise("place_w_in", body, shard.shape[0], 256, [('row', shard)], [('row', IN_WIN, BF16)])[0]


def _adamw(name, w, m, v, g, tr, from_window=False):
    s_parts, r, c = g.shape
    assert r % tr == 0

    def kern(w_ref, m_ref, v_ref, g_ref, go_ref, d_ref, mo_ref, vo_ref):
        gs = g_ref[0].astype(F32)
        for s in range(1, s_parts):
            gs = gs + g_ref[s].astype(F32)
        if from_window:
            gs = pltpu.roll(gs, c - _window_offset(), 1)
        wv = w_ref[...]
        m2 = ADAM_B1 * m_ref[...] + (1.0 - ADAM_B1) * gs
        v2 = ADAM_B2 * v_ref[...] + (1.0 - ADAM_B2) * (gs * gs)
        m_hat = m2 / (1.0 - ADAM_B1 ** ADAM_STEP)
        v_hat = v2 / (1.0 - ADAM_B2 ** ADAM_STEP)
        go_ref[...] = gs
        d_ref[...] = -ADAM_LR * (m_hat / (jnp.sqrt(v_hat) + ADAM_EPS) + ADAM_WD * wv)
        mo_ref[...] = m2
        vo_ref[...] = v2

    spec = pl.BlockSpec((tr, c), lambda i: (i, 0))
    shp = jax.ShapeDtypeStruct((r, c), F32)
    return pl.pallas_call(kern, name=name, grid=(r // tr,),
                          in_specs=[spec, spec, spec, pl.BlockSpec((s_parts, tr, c), lambda i: (0, i, 0))],
                          out_specs=[spec] * 4, out_shape=[shp] * 4, compiler_params=_params(1))(w, m, v, g)


def _pack(arrs, rows, lead=0):
    parts = []
    for a in arrs:
        flat = a.reshape(a.shape[:lead] + (-1,))
        pad = (-flat.shape[-1]) % LANES
        if pad:
            flat = jnp.pad(flat, [(0, 0)] * lead + [(0, pad)])
        parts.append(flat)
    flat = jnp.concatenate(parts, axis=-1)
    pad = rows * LANES - flat.shape[-1]
    assert pad >= 0
    if pad:
        flat = jnp.pad(flat, [(0, 0)] * lead + [(0, pad)])
    return flat.reshape(flat.shape[:lead] + (rows, LANES))


def _unpack(buf, shapes, lead=0):
    flat = buf.reshape(buf.shape[:lead] + (-1,))
    out, off = [], 0
    for shp in shapes:
        n = math.prod(shp)
        out.append(flat[..., off:off + n].reshape(buf.shape[:lead] + tuple(shp)))
        off += n + ((-n) % LANES)
    return out


BIG = ('w_in', 'w_out', 'w_gate', 'w_up', 'w_down')
CONV =('ssd_conv_w', 'lru_conv_w')
CONV_SHARD_SHAPES = ((DEPTH, 4, SSD_CONV // N_DEV), (DEPTH, 4, LRU_W // N_DEV))
CONV_ROWS = 16
SMALL = ('norm_mix', 'ssd_conv_b', 'ssd_dt_bias', 'ssd_a_log', 'ssd_d', 'ssd_norm', 'lru_conv_b', 'lru_wa',
         'lru_ba', 'lru_wx', 'lru_bx', 'lru_lambda', 'norm_ffn', 'norm_final')
SMALL_ROWS = 1280
SMALL_TILE = 256
WEIGHTS = ('norm_mix', 'w_in', 'ssd_conv_w', 'ssd_conv_b', 'ssd_dt_bias', 'ssd_a_log', 'ssd_d', 'ssd_norm',
           'lru_conv_w', 'lru_conv_b', 'lru_wa', 'lru_ba', 'lru_wx', 'lru_bx', 'lru_lambda', 'w_out', 'norm_ffn',
           'w_gate', 'w_up', 'w_down', 'norm_final')


def _join_cols(a):
    return jnp.transpose(a, (1, 2, 0, 3)).reshape(a.shape[1], a.shape[2], -1)


def _join_rows(a):
    return jnp.transpose(a, (1, 0, 2, 3)).reshape(a.shape[1], -1, a.shape[3])


def _split_cols(a):
    l, r, c = a.shape
    return jnp.transpose(a.reshape(l, r, N_DEV, c // N_DEV), (2, 0, 1, 3)).reshape(N_DEV, l * r, c // N_DEV)


def _split_rows(a):
    l, r, c = a.shape
    return jnp.transpose(a.reshape(l, N_DEV, r // N_DEV, c), (1, 0, 2, 3)).reshape(N_DEV, l * r // N_DEV, c)


def _shard_form(k, a):
    if k == 'w_in':
        return jnp.pad(a.reshape(-1, IN_SHARD), ((0, 0), (0, IN_WIN - IN_SHARD)))
    if k in ('w_gate', 'w_up'):
        return jnp.pad(a.reshape(-1, FF_SHARD), ((0, 0), (0, FF_SHARD_P - FF_SHARD)))
    if k == 'w_down':
        return jnp.pad(a, ((0, 0), (0, FF_SHARD_P - FF_SHARD), (0, 0))).reshape(-1, D_MODEL)
    return a.reshape(-1, D_MODEL)


def _shard_back(k, a):
    if k == 'w_in':
        return a[:, :IN_SHARD].reshape(DEPTH, D_MODEL, IN_SHARD)
    if k in ('w_gate', 'w_up'):
        return a[:, :FF_SHARD].reshape(DEPTH, D_MODEL, FF_SHARD)
    if k == 'w_down':
        return a.reshape(DEPTH, FF_SHARD_P, D_MODEL)[:, :FF_SHARD]
    return a.reshape(DEPTH, D_MIX // N_DEV, D_MODEL)


def _dt_tile_place(a):
    zeros = jnp.zeros(a.shape[:-1] + (LANES - N_HEADS,), a.dtype)
    return jnp.concatenate([a[..., :6], zeros, a[..., 6:8]], axis=-1)


def _dt_tile_heads(tile):
    zeros = jnp.zeros(tile.shape[:-1] + (LANES - N_HEADS,), tile.dtype)
    return jnp.concatenate([tile[..., :6], tile[..., 126:128], zeros], axis=-1)


def _layout_from_windows(win):
    r = win.shape[1]
    main = jnp.concatenate([win[j][:, :512] for j in range(N_DEV)] + [jnp.zeros((r, LANES), win.dtype)], axis=1)
    gap = jnp.zeros((r, 384), win.dtype)
    tails = [jnp.zeros((r, 512), win.dtype)]
    for j in range(N_DEV - 1):
        tails += [win[j][:, 512:], gap]
    tails.append(win[N_DEV - 1][:, 512:])
    return main + jnp.concatenate(tails, axis=1)


def _prepare_weights(p, full):
    w = {}
    w_in = full['w_in']
    w_qkv = w_in[:, :, :D_MIX]
    dt_cols = _dt_tile_heads(w_in[:, :, 3072:3200])
    w_rest = jnp.concatenate([w_in[:, :, 2048:3072], w_in[:, :, 1536:2048], w_in[:, :, 3200:3712],
                              w_in[:, :, 3712:4224], dt_cols], axis=2)
    w['w_qkv'], w['w_rest'] = w_qkv, w_rest
    w['w_in_t'] = jnp.transpose(jnp.concatenate([w_qkv, w_rest], axis=2), (0, 2, 1))
    w['w_out'] = full['w_out']
    w['w_out_t'] = jnp.transpose(full['w_out'], (0, 2, 1))
    w['w_gu'] = jnp.concatenate([full['w_gate'], full['w_up']], axis=2)
    w['w_gu_t'] = jnp.transpose(w['w_gu'], (0, 2, 1))
    w['w_down'] = full['w_down']
    w['w_down_t'] = jnp.transpose(full['w_down'], (0, 2, 1))
    for k in ('norm_mix', 'ssd_conv_b', 'ssd_norm', 'lru_conv_b', 'lru_ba', 'lru_bx', 'lru_lambda', 'norm_ffn'):
        w[k] = p[k][:, None, :]
    for k in ('ssd_dt_bias', 'ssd_a_log', 'ssd_d'):
        w[k] = jnp.pad(p[k], ((0, 0), (0, LANES - N_HEADS)))[:, None, :]
    for k in CONV:
        w[k] = jnp.pad(full[k], ((0, 0), (0, 4), (0, 0)))
    for k in ('lru_wa', 'lru_wx'):
        bd = jnp.stack([_block_diag(p[k][l]) for l in range(DEPTH)]).astype(BF16)
        w[k] = bd
        w[k + '_t'] = jnp.transpose(bd, (0, 2, 1))
    return w


def _local_step(x, target, w, norm_final):
    saved = []
    for l in range(DEPTH):
        x, sv = _layer_fwd(x, w, l)
        saved.append(sv)
    dx, loss_acc, dgf = _final_loss(x, norm_final[None, :], target)
    grads = [None] * DEPTH
    for l in reversed(range(DEPTH)):
        dx, grads[l] = _layer_bwd(dx, saved[l], w, l)
    g = {k: jnp.stack([grads[l][k] for l in range(DEPTH)]) for k in grads[0]}
    g['norm_final'] = dgf[0]
    return loss_acc[0, 0], dx, g


def kernel(x, norm_mix, w_in, ssd_conv_w, ssd_conv_b, ssd_dt_bias, ssd_a_log, ssd_d, ssd_norm, lru_conv_w, lru_conv_b, lru_wa, lru_ba, lru_wx, lru_bx, lru_lambda, w_out, norm_ffn, w_gate, w_up, w_down, norm_final, loss_target, m_norm_mix, m_w_in, m_ssd_conv_w, m_ssd_conv_b, m_ssd_dt_bias, m_ssd_a_log, m_ssd_d, m_ssd_norm, m_lru_conv_w, m_lru_conv_b, m_lru_wa, m_lru_ba, m_lru_wx, m_lru_bx, m_lru_lambda, m_w_out, m_norm_ffn, m_w_gate, m_w_up, m_w_down, m_norm_final, v_norm_mix, v_w_in, v_ssd_conv_w, v_ssd_conv_b, v_ssd_dt_bias, v_ssd_a_log, v_ssd_d, v_ssd_norm, v_lru_conv_w, v_lru_conv_b, v_lru_wa, v_lru_ba, v_lru_wx, v_lru_bx, v_lru_lambda, v_w_out, v_norm_ffn, v_w_gate, v_w_up, v_w_down, v_norm_final):
    args = (norm_mix, w_in, ssd_conv_w, ssd_conv_b, ssd_dt_bias, ssd_a_log, ssd_d, ssd_norm, lru_conv_w, lru_conv_b, lru_wa, lru_ba, lru_wx, lru_bx, lru_lambda, w_out, norm_ffn, w_gate, w_up, w_down, norm_final)
    margs = (m_norm_mix, m_w_in, m_ssd_conv_w, m_ssd_conv_b, m_ssd_dt_bias, m_ssd_a_log, m_ssd_d, m_ssd_norm, m_lru_conv_w, m_lru_conv_b, m_lru_wa, m_lru_ba, m_lru_wx, m_lru_bx, m_lru_lambda, m_w_out, m_norm_ffn, m_w_gate, m_w_up, m_w_down, m_norm_final)
    vargs = (v_norm_mix, v_w_in, v_ssd_conv_w, v_ssd_conv_b, v_ssd_dt_bias, v_ssd_a_log, v_ssd_d, v_ssd_norm, v_lru_conv_w, v_lru_conv_b, v_lru_wa, v_lru_ba, v_lru_wx, v_lru_bx, v_lru_lambda, v_w_out, v_norm_ffn, v_w_gate, v_w_up, v_w_down, v_norm_final)
    p = dict(zip(WEIGHTS, args))
    pm = dict(zip(WEIGHTS, margs))
    pv = dict(zip(WEIGHTS, vargs))

    forms = {k: _shard_form(k, p[k]) for k in BIG}
    send = [_place_w_in(forms['w_in'])] + [forms[k].astype(BF16) for k in BIG[1:]]
    got = _all_gather("gather_weights", send + [_pack([p[k] for k in CONV], CONV_ROWS)])
    full = {'w_in': _layout_from_windows(got[0]).reshape(DEPTH, D_MODEL, IN_COLS_P),
            'w_out': _join_rows(got[1].reshape(N_DEV, DEPTH, D_MIX // N_DEV, D_MODEL)),
            'w_gate': _join_cols(got[2].reshape(N_DEV, DEPTH, D_MODEL, FF_SHARD_P)),
            'w_up': _join_cols(got[3].reshape(N_DEV, DEPTH, D_MODEL, FF_SHARD_P)),
            'w_down': _join_rows(got[4].reshape(N_DEV, DEPTH, FF_SHARD_P, D_MODEL))}
    for k, a in zip(CONV, _unpack(got[5], CONV_SHARD_SHAPES, lead=1)):
        full[k] = _join_cols(a)
    w = _prepare_weights(p, full)

    loss_local, dx, g = _local_step(x[0], loss_target[0], w, norm_final)
    loss = lax.psum(loss_local, ("x", "y", "c"))

    small_g = _all_gather("gather_small_grads", [_pack([g[k] for k in SMALL + CONV], SMALL_ROWS)])[0]
    zeros = [jnp.zeros_like(g[k]) for k in CONV]
    res_small = _adamw("adamw_small", _pack([p[k] for k in SMALL] + zeros, SMALL_ROWS),
                       _pack([pm[k] for k in SMALL] + zeros, SMALL_ROWS),
                       _pack([pv[k] for k in SMALL] + zeros, SMALL_ROWS), small_g, SMALL_TILE)
    small_shapes = [g[k].shape for k in SMALL + CONV]
    out = {kind: {} for kind in range(4)}
    for kind in range(4):
        for k, a in zip(SMALL + CONV, _unpack(res_small[kind], small_shapes)):
            out[kind][k] = a
    me = 4 * lax.axis_index("x") + 2 * lax.axis_index("y") + lax.axis_index("c")
    conv_g = []
    for k, shp in zip(CONV, CONV_SHARD_SHAPES):
        conv_g.append(lax.dynamic_slice_in_dim(out[0][k], me * shp[2], shp[2], axis=2))
    res_conv = _adamw("adamw_conv", _pack([p[k] for k in CONV], CONV_ROWS), _pack([pm[k] for k in CONV], CONV_ROWS),
                      _pack([pv[k] for k in CONV], CONV_ROWS), _pack(conv_g, CONV_ROWS)[None], CONV_ROWS)
    for kind in range(4):
        for k, a in zip(CONV, _unpack(res_conv[kind], CONV_SHARD_SHAPES)):
            out[kind][k] = a

    g_in = g['w_in'].reshape(DEPTH * D_MODEL, IN_COLS_P)
    dest = [jnp.stack([g_in[:, 512 * j:512 * j + IN_WIN] for j in range(N_DEV)]), _split_rows(g['w_out']),
            _split_cols(g['w_gate']), _split_cols(g['w_up']), _split_rows(g['w_down'])]
    parts = _all_to_all("exchange_big_grads", [a.astype(BF16) for a in dest])
    tiles = {'w_in': 256, 'w_out': 128, 'w_gate': 512, 'w_up': 512, 'w_down': 256}
    for k, part in zip(BIG, parts):
        res = _adamw("adamw_" + k, forms[k], _shard_form(k, pm[k]), _shard_form(k, pv[k]), part, tiles[k],
                     from_window=(k == 'w_in'))
        for kind in range(4):
            out[kind][k] = _shard_back(k, res[kind])

    outs = [loss, dx[None]]
    for kind in range(4):
        outs += [out[kind][k] for k in WEIGHTS]
    return tuple(outs)
```

```python
import functools
import math

import jax
import jax.numpy as jnp
from jax import lax
from jax.experimental import pallas as pl
from jax.experimental.pallas import tpu as pltpu

F32 = jnp.float32
BF16 = jnp.bfloat16

N_DEV = 8
DEPTH = 2
D_MODEL = 1024
ATT_W = 512
HEAD_DIM = 64
N_HEADS = 8
ATT_BLOCK = 128
ATT_DILATIONS = (1, 4, 16)
SSD_W = 512
SSD_STATE = 128
SSD_CONV = 1024
SSD_CHUNK = 128
LRU_W = 512
LRU_C = 8.0
D_MIX = 1536
D_FF = 2816
FF_SHARD = D_FF // N_DEV
FF_SHARD_P = 384
D_FFP = N_DEV * FF_SHARD_P
IN_COLS = 4104
IN_SHARD = IN_COLS // N_DEV
IN_WIN = 640
IN_COLS_P = 4224
REST_COLS = 2688
NORM_EPS = 1e-6
SSD_NORM_EPS = 1e-5
NEG = -1e30

ADAM_LR = 0.001
ADAM_B1 = 0.9
ADAM_B2 = 0.999
ADAM_EPS = 1e-08
ADAM_WD = 0.01
ADAM_STEP = 10

LANES = 128
VMEM_LIMIT = 52 * 1024 * 1024
HI = lax.Precision.HIGHEST


def _sigmoid(x):
    return 1.0 / (1.0 + jnp.exp(-x))


def _silu(x):
    return x * _sigmoid(x)


def _dsilu(x):
    s = _sigmoid(x)
    return s * (1.0 + x * (1.0 - s))


def _softplus(x):
    return jnp.maximum(x, 0.0) + jnp.log(1.0 + jnp.exp(-jnp.abs(x)))


_GELU_C = math.sqrt(2.0 / math.pi)


def _gelu(x):
    return 0.5 * x * (1.0 + jnp.tanh(_GELU_C * (x + 0.044715 * x * x * x)))


def _dgelu(x):
    t = jnp.tanh(_GELU_C * (x + 0.044715 * x * x * x))
    return 0.5 * (1.0 + t) + 0.5 * x * (1.0 - t * t) * _GELU_C * (1.0 + 3.0 * 0.044715 * x * x)


def _dot(a, b):
    return jnp.dot(a, b, preferred_element_type=F32)


def _dot_nt(a, b):
    return lax.dot_general(a, b, (((1,), (1,)), ((), ())), preferred_element_type=F32)


def _dot_tn(a, b):
    return lax.dot_general(a, b, (((0,), (0,)), ((), ())), preferred_element_type=F32)


def _dot_hi(a, b):
    return jnp.dot(a, b, preferred_element_type=F32, precision=HI)


def _iota(shape, axis):
    return lax.broadcasted_iota(jnp.int32, shape, axis)


def _shift_down(x, s, prev8):
    xs = pltpu.roll(x, s, 0)
    ps = pltpu.roll(prev8, s, 0)
    top = jnp.concatenate([ps, x[8:]], axis=0)
    return jnp.where(_iota(x.shape, 0) < s, top, xs)


def _shift_up(x, s, next8):
    tm = x.shape[0]
    xs = pltpu.roll(x, tm - s, 0)
    ns = pltpu.roll(next8, 8 - s, 0)
    bottom = jnp.concatenate([x[:tm - 8], ns], axis=0)
    return jnp.where(_iota(x.shape, 0) >= tm - s, bottom, xs)


def _expand_mat():
    return jnp.where(_iota((LANES, SSD_W), 1) // HEAD_DIM == _iota((LANES, SSD_W), 0), 1.0, 0.0).astype(F32)


def _reduce_mat():
    return jnp.where(_iota((SSD_W, LANES), 0) // HEAD_DIM == _iota((SSD_W, LANES), 1), 1.0, 0.0).astype(F32)


def _params(n_grid):
    return pltpu.CompilerParams(dimension_semantics=("arbitrary",) * n_grid, vmem_limit_bytes=VMEM_LIMIT)


def _rowwise(name, body, n_rows, tm, ins, outs, scratch=(), reverse=False):
    nt = n_rows // tm
    assert nt * tm == n_rows and tm % 8 == 0
    r8 = tm // 8
    last8 = n_rows // 8 - 1

    def pos(s):
        return (nt - 1 - s) if reverse else s

    in_specs, args = [], []
    for spec in ins:
        kind, arr = spec[0], spec[1]
        args.append(arr)
        if kind == 'row':
            in_specs.append(pl.BlockSpec((tm, arr.shape[1]), lambda s: (pos(s), 0)))
        elif kind == 'col':
            in_specs.append(pl.BlockSpec((tm, spec[2]), functools.partial(lambda s, j: (pos(s), j), j=spec[3])))
        elif kind == 'full':
            in_specs.append(pl.BlockSpec(arr.shape, functools.partial(lambda s, n: (0,) * n, n=arr.ndim)))
        elif kind == 'prev8':
            in_specs.append(pl.BlockSpec((8, spec[2]), functools.partial(
                lambda s, j: (jnp.maximum(pos(s) * r8 - 1, 0), j), j=spec[3])))
        elif kind == 'next8':
            in_specs.append(pl.BlockSpec((8, spec[2]), functools.partial(
                lambda s, j: (jnp.minimum((pos(s) + 1) * r8, last8), j), j=spec[3])))
        else:
            raise ValueError(kind)
    out_specs, out_shape, acc_idx = [], [], []
    for k, spec in enumerate(outs):
        if spec[0] == 'row':
            out_specs.append(pl.BlockSpec((tm, spec[1]), lambda s: (pos(s), 0)))
            out_shape.append(jax.ShapeDtypeStruct((n_rows, spec[1]), spec[2]))
        else:
            out_specs.append(pl.BlockSpec(spec[1], lambda s: (0, 0)))
            out_shape.append(jax.ShapeDtypeStruct(spec[1], spec[2]))
            acc_idx.append(k)
    n_in, n_out = len(ins), len(outs)

    def kern(*refs):
        s = pl.program_id(0)
        in_refs, out_refs, scr = refs[:n_in], refs[n_in:n_in + n_out], refs[n_in + n_out:]

        @pl.when(s == 0)
        def _():
            for k in acc_idx:
                out_refs[k][...] = jnp.zeros(out_refs[k].shape, out_refs[k].dtype)

        body(pos(s), nt, in_refs, out_refs, scr)

    res = pl.pallas_call(kern, name=name, grid=(nt,), in_specs=in_specs, out_specs=out_specs,
                         out_shape=out_shape, scratch_shapes=list(scratch), compiler_params=_params(1))(*args)
    return res


def _mm(name, a_list, b, *, res=None, out_dtype=F32, tm=512, tn=None):
    n_rows = a_list[0].shape[0]
    k_total, n = b.shape
    ks = [a.shape[1] for a in a_list]
    assert sum(ks) == k_total
    tn = n if tn is None else tn
    assert n_rows % tm == 0 and n % tn == 0
    na = len(a_list)

    def kern(*refs):
        a_refs, b_ref, o_ref = refs[:na], refs[na], refs[-1]
        acc, off = None, 0
        for a_ref, kp in zip(a_refs, ks):
            part = _dot(a_ref[...].astype(BF16), b_ref[off:off + kp, :])
            acc = part if acc is None else acc + part
            off += kp
        if res is not None:
            acc = acc + refs[na + 1][...]
        o_ref[...] = acc.astype(out_dtype)

    in_specs = [pl.BlockSpec((tm, kp), lambda i, j: (i, 0)) for kp in ks]
    in_specs.append(pl.BlockSpec((k_total, tn), lambda i, j: (0, j)))
    args = list(a_list) + [b]
    if res is not None:
        in_specs.append(pl.BlockSpec((tm, tn), lambda i, j: (i, j)))
        args.append(res)
    return pl.pallas_call(kern, name=name, grid=(n_rows // tm, n // tn), in_specs=in_specs,
                          out_specs=pl.BlockSpec((tm, tn), lambda i, j: (i, j)),
                          out_shape=jax.ShapeDtypeStruct((n_rows, n), out_dtype),
                          compiler_params=_params(2))(*args)


def _mm_tn(name, a, g, *, a_col=None, g_col=None, tk=None, tn=None, tt=512):
    n_rows = a.shape[0]
    k = a.shape[1] if a_col is None else a_col[0]
    a_j = 0 if a_col is None else a_col[1]
    n = g.shape[1] if g_col is None else g_col[0]
    g_j = 0 if g_col is None else g_col[1]
    tk = k if tk is None else tk
    tn = n if tn is None else tn
    assert k % tk == 0 and n % tn == 0 and n_rows % tt == 0
    kb = k // tk
    nbk = n // tn

    def kern(a_ref, g_ref, o_ref):
        t = pl.program_id(2)

        @pl.when(t == 0)
        def _():
            o_ref[...] = jnp.zeros(o_ref.shape, F32)

        o_ref[...] += _dot_tn(a_ref[...].astype(BF16), g_ref[...].astype(BF16))

    return pl.pallas_call(
        kern, name=name, grid=(kb, n // tn, n_rows // tt),
        in_specs=[pl.BlockSpec((tt, tk), lambda i, j, t: (t, a_j * kb + i)),
                  pl.BlockSpec((tt, tn), lambda i, j, t: (t, g_j * nbk + j))],
        out_specs=pl.BlockSpec((tk, tn), lambda i, j, t: (i, j)),
        out_shape=jax.ShapeDtypeStruct((k, n), F32), compiler_params=_params(3))(a, g)


def _rmsnorm_fwd(name, x, g):
    def body(i, nt, ins, outs, scr):
        xv = ins[0][...]
        rstd = lax.rsqrt(jnp.mean(xv * xv, axis=-1, keepdims=True) + NORM_EPS)
        outs[0][...] = (xv * rstd * ins[1][...]).astype(BF16)

    return _rowwise(name, body, x.shape[0], 512, [('row', x), ('full', g)], [('row', x.shape[1], BF16)])[0]


def _rmsnorm_bwd(name, dh, x, g, dres):
    d = x.shape[1]

    def body(i, nt, ins, outs, scr):
        dy, xv, gv, dr = ins[0][...], ins[1][...], ins[2][...], ins[3][...]
        rstd = lax.rsqrt(jnp.mean(xv * xv, axis=-1, keepdims=True) + NORM_EPS)
        xhat = xv * rstd
        outs[1][0:1, :] += jnp.sum(dy * xhat, axis=0, keepdims=True)
        dxh = dy * gv
        outs[0][...] = dr + rstd * (dxh - xhat * jnp.mean(dxh * xhat, axis=-1, keepdims=True))

    return _rowwise(name, body, x.shape[0], 512, [('row', dh), ('row', x), ('full', g), ('row', dres)],
                    [('row', d, F32), ('acc', (8, d), F32)])


def _final_loss(x, g, target):
    d = x.shape[1]

    def body(i, nt, ins, outs, scr):
        xv, gv, tv = ins[0][...], ins[1][...], ins[2][...]
        rstd = lax.rsqrt(jnp.mean(xv * xv, axis=-1, keepdims=True) + NORM_EPS)
        xhat = xv * rstd
        err = xhat * gv - tv
        row_loss = 0.5 * jnp.mean(err * err, axis=-1, keepdims=True)
        outs[1][...] += jnp.sum(row_loss, axis=0, keepdims=True)
        dy = err * (1.0 / d)
        outs[2][0:1, :] += jnp.sum(dy * xhat, axis=0, keepdims=True)
        dxh = dy * gv
        outs[0][...] = rstd * (dxh - xhat * jnp.mean(dxh * xhat, axis=-1, keepdims=True))

    return _rowwise("final_loss", body, x.shape[0], 512, [('row', x), ('full', g), ('row', target)],
                    [('row', d, F32), ('acc', (8, LANES), F32), ('acc', (8, d), F32)])


def _conv_fwd(name, src, width, idx, w, b):
    def body(i, nt, ins, outs, scr):
        xv = ins[0][...]
        prev = jnp.where(i > 0, ins[1][...], 0.0)
        wv = ins[2][...]
        y = ins[3][...] + wv[3:4, :] * xv
        for s in (1, 2, 3):
            y = y + wv[3 - s:4 - s, :] * _shift_down(xv, s, prev)
        outs[0][...] = y

    return _rowwise(name, body, src.shape[0], 512,
                    [('col', src, width, idx), ('prev8', src, width, idx), ('full', w), ('full', b)],
                    [('row', width, F32)])[0]


def _conv_bwd(name, dpre, src, width, idx, w):
    def body(i, nt, ins, outs, scr):
        dy = ins[0][...]
        nxt = jnp.where(i < nt - 1, ins[1][...], 0.0)
        xv = ins[2][...]
        prev = jnp.where(i > 0, ins[3][...], 0.0)
        wv = ins[4][...]
        dx = wv[3:4, :] * dy
        outs[1][3:4, :] += jnp.sum(dy * xv, axis=0, keepdims=True)
        outs[1][4:5, :] += jnp.sum(dy, axis=0, keepdims=True)
        for s in (1, 2, 3):
            dx = dx + wv[3 - s:4 - s, :] * _shift_up(dy, s, nxt)
            outs[1][3 - s:4 - s, :] += jnp.sum(dy * _shift_down(xv, s, prev), axis=0, keepdims=True)
        outs[0][...] = dx

    return _rowwise(name, body, src.shape[0], 512,
                    [('row', dpre), ('next8', dpre, width, 0), ('col', src, width, idx),
                     ('prev8', src, width, idx), ('full', w)],
                    [('row', width, F32), ('acc', (8, width), F32)])


ATT_STEP_BLOCKS = 4


def _att_bias(not_first, dil, head):
    qi = _iota((ATT_BLOCK, 2 * ATT_BLOCK), 0)
    ki = _iota((ATT_BLOCK, 2 * ATT_BLOCK), 1)
    dist = ATT_BLOCK + qi - ki
    valid = (dist >= 0) & (dist <= ATT_BLOCK) & (not_first | (ki >= ATT_BLOCK))
    slope = 2.0 ** (-(head + 1))
    return jnp.where(valid, (-slope * dil) * dist.astype(F32), NEG)


def _head_mask():
    lane = _iota((ATT_BLOCK, LANES), 1)
    return lane < HEAD_DIM


def _att_q_specs(dil, nb, bq):
    big = (bq * ATT_BLOCK, ATT_W)
    one = (ATT_BLOCK, ATT_W)
    specs = [pl.BlockSpec(big, lambda r, n: (n, 3 * r)),
             pl.BlockSpec(big, lambda r, n: (n, 3 * r + 1)),
             pl.BlockSpec(one, lambda r, n: (jnp.maximum(n * bq - 1, 0), 3 * r + 1)),
             pl.BlockSpec(big, lambda r, n: (n, 3 * r + 2)),
             pl.BlockSpec(one, lambda r, n: (jnp.maximum(n * bq - 1, 0), 3 * r + 2))]
    wide = pl.BlockSpec(big, lambda r, n: (n, r))
    stat = pl.BlockSpec((bq * ATT_BLOCK, LANES), lambda r, n: (n, r))
    return specs, wide, stat


def _att_fwd(dil, qkv_v, stats):
    n_l = qkv_v.shape[0]
    nb = n_l // ATT_BLOCK
    bq = min(ATT_STEP_BLOCKS, nb)
    first = stats is None
    scale = HEAD_DIM ** -0.5

    def kern(*refs):
        n = pl.program_id(1)
        q_ref, kc_ref, kp_ref, vc_ref, vp_ref = refs[:5]
        if first:
            m_out, l_out, a_out = refs[5:]
        else:
            m_in, l_in, a_in, m_out, l_out, a_out = refs[5:]
        low = _head_mask()
        lane = _iota((ATT_BLOCK, LANES), 1)
        for b in range(bq):
            rows = slice(ATT_BLOCK * b, ATT_BLOCK * (b + 1))
            prev = slice(ATT_BLOCK * (b - 1), ATT_BLOCK * b)
            not_first = (n * bq + b) > 0
            m_acc = jnp.zeros((ATT_BLOCK, LANES), F32)
            l_acc = jnp.zeros((ATT_BLOCK, LANES), F32)
            for p in range(N_HEADS // 2):
                sl = slice(LANES * p, LANES * (p + 1))
                q2 = q_ref[rows, sl]
                k_prev = kp_ref[:, sl] if b == 0 else kc_ref[prev, sl]
                v_prev = vp_ref[:, sl] if b == 0 else vc_ref[prev, sl]
                k2 = jnp.concatenate([k_prev, kc_ref[rows, sl]], axis=0).astype(BF16)
                v2 = jnp.concatenate([v_prev, vc_ref[rows, sl]], axis=0).astype(BF16)
                res = []
                for e in range(2):
                    h = 2 * p + e
                    keep = low if e == 0 else jnp.logical_not(low)
                    qe = jnp.where(keep, q2, 0.0).astype(BF16)
                    s = _dot_nt(qe, k2) * scale + _att_bias(not_first, dil, h)
                    m_blk = jnp.max(s, axis=-1, keepdims=True)
                    if first:
                        m_new = m_blk
                        pe = jnp.exp(s - m_new)
                        l_new = jnp.sum(pe, axis=-1, keepdims=True)
                        a_new = _dot(pe.astype(BF16), v2)
                    else:
                        m_old = m_in[rows, h:h + 1]
                        l_old = l_in[rows, h:h + 1]
                        m_new = jnp.maximum(m_old, m_blk)
                        pe = jnp.exp(s - m_new)
                        alpha = jnp.exp(m_old - m_new)
                        l_new = alpha * l_old + jnp.sum(pe, axis=-1, keepdims=True)
                        a_new = alpha * a_in[rows, sl] + _dot(pe.astype(BF16), v2)
                    m_acc = jnp.where(lane == h, m_new, m_acc)
                    l_acc = jnp.where(lane == h, l_new, l_acc)
                    res.append(a_new)
                a_out[rows, sl] = jnp.where(low, res[0], res[1])
            m_out[rows, :] = m_acc
            l_out[rows, :] = l_acc

    specs, wide, stat = _att_q_specs(dil, nb, bq)
    args = [qkv_v] * 5
    if not first:
        specs = specs + [stat, stat, wide]
        args += list(stats)
    shp_s = jax.ShapeDtypeStruct((n_l, dil * LANES), F32)
    shp_a = jax.ShapeDtypeStruct((n_l, dil * ATT_W), F32)
    return pl.pallas_call(kern, name="att_fwd_d%d" % dil, grid=(dil, nb // bq), in_specs=specs,
                          out_specs=[stat, stat, wide], out_shape=[shp_s, shp_s, shp_a],
                          compiler_params=_params(2))(*args)


def _att_finish(m, l, acc):
    def body(i, nt, ins, outs, scr):
        lv = ins[1][...]
        real = _iota(lv.shape, 1) < N_HEADS
        outs[0][...] = ins[2][...] / _dot_hi(lv, _expand_mat())
        outs[1][...] = jnp.where(real, ins[0][...] + jnp.log(jnp.where(real, lv, 1.0)), 0.0)

    return _rowwise("att_finish", body, m.shape[0], 512, [('row', m), ('row', l), ('row', acc)],
                    [('row', ATT_W, F32), ('row', LANES, F32)])


def _att_delta(d_att, out):
    def body(i, nt, ins, outs, scr):
        outs[0][...] = _dot_hi(ins[0][...] * ins[1][...], _reduce_mat())

    return _rowwise("att_delta", body, out.shape[0], 512, [('row', d_att), ('row', out)],
                    [('row', LANES, F32)])[0]


def _att_bwd_dq(dil, qkv_v, do_v, lse_v, delta_v, dq_in):
    n_l = qkv_v.shape[0]
    nb = n_l // ATT_BLOCK
    bq = min(ATT_STEP_BLOCKS, nb)
    first = dq_in is None
    scale = HEAD_DIM ** -0.5

    def kern(*refs):
        n = pl.program_id(1)
        q_ref, kc_ref, kp_ref, vc_ref, vp_ref, do_ref, lse_ref, dl_ref = refs[:8]
        dq_out = refs[-1]
        low = _head_mask()
        for b in range(bq):
            rows = slice(ATT_BLOCK * b, ATT_BLOCK * (b + 1))
            prev = slice(ATT_BLOCK * (b - 1), ATT_BLOCK * b)
            not_first = (n * bq + b) > 0
            for p in range(N_HEADS // 2):
                sl = slice(LANES * p, LANES * (p + 1))
                q2 = q_ref[rows, sl]
                do2 = do_ref[rows, sl]
                k_prev = kp_ref[:, sl] if b == 0 else kc_ref[prev, sl]
                v_prev = vp_ref[:, sl] if b == 0 else vc_ref[prev, sl]
                k2 = jnp.concatenate([k_prev, kc_ref[rows, sl]], axis=0).astype(BF16)
                v2 = jnp.concatenate([v_prev, vc_ref[rows, sl]], axis=0).astype(BF16)
                res = []
                for e in range(2):
                    h = 2 * p + e
                    keep = low if e == 0 else jnp.logical_not(low)
                    qe = jnp.where(keep, q2, 0.0).astype(BF16)
                    doe = jnp.where(keep, do2, 0.0).astype(BF16)
                    s = _dot_nt(qe, k2) * scale + _att_bias(not_first, dil, h)
                    pe = jnp.exp(s - lse_ref[rows, h:h + 1])
                    dp = _dot_nt(doe, v2)
                    ds = pe * (dp - dl_ref[rows, h:h + 1])
                    res.append(_dot(ds.astype(BF16), k2) * scale)
                dq = jnp.where(low, res[0], res[1])
                if not first:
                    dq = dq + refs[8][rows, sl]
                dq_out[rows, sl] = dq

    specs, wide, stat = _att_q_specs(dil, nb, bq)
    specs = specs + [wide, stat, stat]
    args = [qkv_v] * 5 + [do_v, lse_v, delta_v]
    if not first:
        specs.append(wide)
        args.append(dq_in)
    return pl.pallas_call(kern, name="att_bwd_dq_d%d" % dil, grid=(dil, nb // bq), in_specs=specs,
                          out_specs=wide, out_shape=jax.ShapeDtypeStruct((n_l, dil * ATT_W), F32),
                          compiler_params=_params(2))(*args)


def _att_bwd_dkv(dil, qkv_v, do_v, lse_v, delta_v, dkv_in):
    n_l = qkv_v.shape[0]
    nb = n_l // ATT_BLOCK
    bq = min(ATT_STEP_BLOCKS, nb)
    steps = nb // bq
    first = dkv_in is None
    scale = HEAD_DIM ** -0.5

    def kern(*refs):
        j = pl.program_id(1)
        k_ref, v_ref, qc_ref, qn_ref, doc_ref, don_ref, lc_ref, ln_ref, dc_ref, dn_ref = refs[:10]
        dk_out, dv_out = refs[-2:]
        low = _head_mask()
        row = _iota((2 * ATT_BLOCK, ATT_BLOCK), 0)
        key = _iota((2 * ATT_BLOCK, ATT_BLOCK), 1)
        dist = row - key
        low2 = _iota((2 * ATT_BLOCK, LANES), 1) < HEAD_DIM
        for b in range(bq):
            rows = slice(ATT_BLOCK * b, ATT_BLOCK * (b + 1))
            nrows = slice(ATT_BLOCK * (b + 1), ATT_BLOCK * (b + 2))
            inner = b < bq - 1
            has_next = True if inner else (j < steps - 1)
            valid = (dist >= 0) & (dist <= ATT_BLOCK) & ((row < ATT_BLOCK) | has_next)
            lse2 = jnp.concatenate([lc_ref[rows, :], lc_ref[nrows, :] if inner else ln_ref[...]], axis=0)
            dl2 = jnp.concatenate([dc_ref[rows, :], dc_ref[nrows, :] if inner else dn_ref[...]], axis=0)
            for p in range(N_HEADS // 2):
                sl = slice(LANES * p, LANES * (p + 1))
                k2 = k_ref[rows, sl]
                v2 = v_ref[rows, sl]
                q2 = jnp.concatenate([qc_ref[rows, sl], qc_ref[nrows, sl] if inner else qn_ref[:, sl]], axis=0)
                do2 = jnp.concatenate([doc_ref[rows, sl], doc_ref[nrows, sl] if inner else don_ref[:, sl]], axis=0)
                q2b = q2.astype(BF16)
                do2b = do2.astype(BF16)
                dks, dvs = [], []
                for e in range(2):
                    h = 2 * p + e
                    keep = low if e == 0 else jnp.logical_not(low)
                    keep2 = low2 if e == 0 else jnp.logical_not(low2)
                    slope = 2.0 ** (-(h + 1))
                    bias = jnp.where(valid, (-slope * dil) * dist.astype(F32), NEG)
                    ke = jnp.where(keep, k2, 0.0).astype(BF16)
                    ve = jnp.where(keep, v2, 0.0).astype(BF16)
                    s = _dot_nt(q2b, ke) * scale + bias
                    pe = jnp.exp(s - lse2[:, h:h + 1])
                    dp = _dot_nt(do2b, ve)
                    ds = pe * (dp - dl2[:, h:h + 1])
                    dvs.append(_dot_tn(pe.astype(BF16), jnp.where(keep2, do2, 0.0).astype(BF16)))
                    dks.append(_dot_tn(ds.astype(BF16), jnp.where(keep2, q2, 0.0).astype(BF16)) * scale)
                dk = jnp.where(low, dks[0], dks[1])
                dv = jnp.where(low, dvs[0], dvs[1])
                if not first:
                    dk = dk + refs[10][rows, sl]
                    dv = dv + refs[11][rows, sl]
                dk_out[rows, sl] = dk
                dv_out[rows, sl] = dv

    big = (bq * ATT_BLOCK, ATT_W)
    one = (ATT_BLOCK, ATT_W)

    def nxt_idx(j):
        return jnp.minimum((j + 1) * bq, nb - 1)

    cur = pl.BlockSpec(big, lambda r, j: (j, r))
    nxt = pl.BlockSpec(one, lambda r, j: (nxt_idx(j), r))
    cur_s = pl.BlockSpec((bq * ATT_BLOCK, LANES), lambda r, j: (j, r))
    nxt_s = pl.BlockSpec((ATT_BLOCK, LANES), lambda r, j: (nxt_idx(j), r))
    in_specs = [pl.BlockSpec(big, lambda r, j: (j, 3 * r + 1)),
                pl.BlockSpec(big, lambda r, j: (j, 3 * r + 2)),
                pl.BlockSpec(big, lambda r, j: (j, 3 * r)),
                pl.BlockSpec(one, lambda r, j: (nxt_idx(j), 3 * r)),
                cur, nxt, cur_s, nxt_s, cur_s, nxt_s]
    args = [qkv_v] * 4 + [do_v, do_v, lse_v, lse_v, delta_v, delta_v]
    if not first:
        in_specs += [cur, cur]
        args += list(dkv_in)
    shp = jax.ShapeDtypeStruct((n_l, dil * ATT_W), F32)
    return pl.pallas_call(kern, name="att_bwd_dkv_d%d" % dil, grid=(dil, steps), in_specs=in_specs,
                          out_specs=[cur, cur], out_shape=[shp, shp], compiler_params=_params(2))(*args)


def _attention_fwd(qkv):
    t = qkv.shape[0]
    stats = None
    for dil in ATT_DILATIONS:
        if stats is not None:
            stats = [s.reshape(t // dil, -1) for s in stats]
        stats = _att_fwd(dil, qkv.reshape(t // dil, dil * D_MIX), stats)
    m, l, acc = [s.reshape(t, -1) for s in stats]
    return _att_finish(m, l, acc)


def _attention_bwd(qkv, d_att, out, lse):
    t = qkv.shape[0]
    delta = _att_delta(d_att, out)
    dq, dkv = None, None
    for dil in ATT_DILATIONS:
        view = lambda a: a.reshape(t // dil, -1)
        qkv_v = view(qkv)
        do_v, lse_v, dl_v = view(d_att), view(lse), view(delta)
        dq = _att_bwd_dq(dil, qkv_v, do_v, lse_v, dl_v, None if dq is None else view(dq))
        dkv = _att_bwd_dkv(dil, qkv_v, do_v, lse_v, dl_v, None if dkv is None else [view(a) for a in dkv])
    return dq.reshape(t, ATT_W), dkv[0].reshape(t, ATT_W), dkv[1].reshape(t, ATT_W)


def _ssd_chunk_common(pre, dtraw, bias_row, alog_row):
    q = SSD_CHUNK
    act = _silu(pre)
    lane = _iota((q, LANES), 1)
    dt = jnp.where(lane < N_HEADS, _softplus(dtraw + bias_row), 0.0)
    a_row = -jnp.exp(alog_row)
    tril = jnp.where(_iota((q, q), 0) >= _iota((q, q), 1), 1.0, 0.0).astype(F32)
    cs = _dot_hi(tril, dt * a_row)
    cs_last = cs[q - 1:q, :]
    return act, dt, a_row, tril, cs, cs_last


def _ssd_lmat(cs, cs_t, h):
    q = SSD_CHUNK
    seg = cs[:, h:h + 1] - cs_t[h:h + 1, :]
    causal = _iota((q, q), 0) >= _iota((q, q), 1)
    return jnp.exp(jnp.where(causal, seg, NEG))


def _ssd_gate_norm(y, z, norm_w):
    sz = _silu(z)
    yg = y * sz
    half = SSD_W // 2
    outs, rss = [], []
    for g in range(2):
        part = yg[:, half * g:half * (g + 1)]
        rs = lax.rsqrt(jnp.mean(part * part, axis=-1, keepdims=True) + SSD_NORM_EPS)
        outs.append(part * rs)
        rss.append(rs)
    yn = jnp.concatenate(outs, axis=1)
    return sz, yn, rss, yn * norm_w


def _ssd_fwd(pre, rest, dt_bias, a_log, d_skip, norm_w):
    t = pre.shape[0]
    q = SSD_CHUNK

    def body(c, nc, ins, outs, scr):
        pre_ref, z_ref, dtr_ref, bias_ref, alog_ref, dsk_ref, nw_ref = ins
        out_ref, y_ref, sp_ref = outs
        s_ref = scr[0]

        @pl.when(c == 0)
        def _():
            s_ref[...] = jnp.zeros(s_ref.shape, F32)

        act, dt, a_row, tril, cs, cs_last = _ssd_chunk_common(pre_ref[...], dtr_ref[...], bias_ref[...],
                                                               alog_ref[...])
        x = act[:, :SSD_W]
        cs_t = cs.T
        e_col = jnp.exp(cs)
        w = jnp.exp(cs_last - cs) * dt
        expand = _expand_mat()
        w_x = _dot_hi(w, expand)
        dt_x = _dot_hi(dt, expand)
        e_x = _dot_hi(e_col, expand)
        d_x = _dot_hi(dsk_ref[...], expand)
        cd_x = _dot_hi(jnp.exp(cs_last), expand)
        s_prev = s_ref[...]
        sp_ref[...] = s_prev
        xw = (x * w_x).astype(BF16)
        xd = (x * dt_x).astype(BF16)
        low = _head_mask()
        y_parts, s_parts = [], []
        for g in range(2):
            bg = act[:, SSD_W + SSD_STATE * g:SSD_W + SSD_STATE * (g + 1)].astype(BF16)
            cg = act[:, SSD_W + 2 * SSD_STATE + SSD_STATE * g:SSD_W + 2 * SSD_STATE + SSD_STATE * (g + 1)].astype(BF16)
            gsl = slice(256 * g, 256 * (g + 1))
            gmat = _dot_nt(cg, bg)
            s_parts.append(_dot_tn(bg, xw[:, gsl]))
            y0 = _dot(cg, s_prev[:, gsl].astype(BF16))
            for pp in range(2):
                pair = 2 * g + pp
                psl = slice(LANES * pair, LANES * (pair + 1))
                yd = []
                for e in range(2):
                    h = 2 * pair + e
                    mh = (gmat * _ssd_lmat(cs, cs_t, h)).astype(BF16)
                    yd.append(_dot(mh, xd[:, psl]))
                y_parts.append(jnp.where(low, yd[0], yd[1]) + e_x[:, psl] * y0[:, LANES * pp:LANES * (pp + 1)])
        y = jnp.concatenate(y_parts, axis=1) + d_x * x
        s_ref[...] = cd_x * s_prev + jnp.concatenate(s_parts, axis=1)
        y_ref[...] = y
        out_ref[...] = _ssd_gate_norm(y, z_ref[...], nw_ref[...])[3]

    return _rowwise("ssd_fwd", body, t, q,
                    [('row', pre), ('col', rest, SSD_W, 2), ('col', rest, LANES, 20), ('full', dt_bias),
                     ('full', a_log), ('full', d_skip), ('full', norm_w)],
                    [('row', SSD_W, F32), ('row', SSD_W, F32), ('row', SSD_W, F32)],
                    scratch=[pltpu.VMEM((SSD_STATE, SSD_W), F32)])


def _ssd_bwd(pre, rest, y, s_prev_all, d_mix, dt_bias, a_log, d_skip, norm_w):
    t = pre.shape[0]
    q = SSD_CHUNK

    def body(c, nc, ins, outs, scr):
        pre_ref, z_ref, dtr_ref, y_ref, sp_ref, do_ref, bias_ref, alog_ref, dsk_ref, nw_ref = ins
        dpre_ref, dz_ref, ddt_ref, a128_ref, a512_ref = outs
        ds_ref = scr[0]

        @pl.when(c == nc - 1)
        def _():
            ds_ref[...] = jnp.zeros(ds_ref.shape, F32)

        pre_v = pre_ref[...]
        dtr = dtr_ref[...]
        act, dt, a_row, tril, cs, cs_last = _ssd_chunk_common(pre_v, dtr, bias_ref[...], alog_ref[...])
        x = act[:, :SSD_W]
        cs_t = cs.T
        e_col = jnp.exp(cs)
        decay_end = jnp.exp(cs_last - cs)
        w = decay_end * dt
        cd = jnp.exp(cs_last)
        expand = _expand_mat()
        reduce = _reduce_mat()
        w_x = _dot_hi(w, expand)
        dt_x = _dot_hi(dt, expand)
        e_x = _dot_hi(e_col, expand)
        d_x = _dot_hi(dsk_ref[...], expand)
        cd_x = _dot_hi(cd, expand)
        s_prev = sp_ref[...]
        d_s = ds_ref[...]
        xw = (x * w_x).astype(BF16)
        xd = (x * dt_x).astype(BF16)
        low = _head_mask()
        lane = _iota((q, LANES), 1)
        sub = _iota((q, LANES), 0)

        yv, zv, nw = y_ref[...], z_ref[...], nw_ref[...]
        d_out = do_ref[...]
        sz, yn, rss, _ = _ssd_gate_norm(yv, zv, nw)
        a512_ref[0:1, :] += jnp.sum(d_out * yn, axis=0, keepdims=True)
        dyn = d_out * nw
        half = SSD_W // 2
        dyg_parts = []
        for g in range(2):
            hs = slice(half * g, half * (g + 1))
            dyg_parts.append(rss[g] * (dyn[:, hs] - yn[:, hs] * jnp.mean(dyn[:, hs] * yn[:, hs], axis=-1,
                                                                          keepdims=True)))
        dyg = jnp.concatenate(dyg_parts, axis=1)
        dy = dyg * sz
        dz_ref[...] = dyg * yv * _dsilu(zv)

        a128_ref[2:3, :] += _dot_hi(jnp.sum(dy * x, axis=0, keepdims=True), reduce)
        dx = d_x * dy

        dy0 = e_x * dy
        dyb = dy.astype(BF16)
        dcs = jnp.zeros((q, LANES), F32)
        dcs_rows = jnp.zeros((q, LANES), F32)
        ddt = jnp.zeros((q, LANES), F32)
        ds_prev_parts, z_parts, db_parts, dc_parts, dxd_parts, y0_parts = [], [], [], [], [], []
        for g in range(2):
            bg = act[:, SSD_W + SSD_STATE * g:SSD_W + SSD_STATE * (g + 1)].astype(BF16)
            cg = act[:, SSD_W + 2 * SSD_STATE + SSD_STATE * g:SSD_W + 2 * SSD_STATE + SSD_STATE * (g + 1)].astype(BF16)
            gsl = slice(256 * g, 256 * (g + 1))
            spg = s_prev[:, gsl].astype(BF16)
            dsg = d_s[:, gsl].astype(BF16)
            dy0g = dy0[:, gsl].astype(BF16)
            gmat = _dot_nt(cg, bg)
            y0_parts.append(_dot(cg, spg))
            dc_g = _dot_nt(dy0g, spg)
            ds_prev_parts.append(_dot_tn(cg, dy0g))
            z_parts.append(_dot(bg, dsg))
            db_g = _dot_nt(xw[:, gsl], dsg)
            dg_acc = jnp.zeros((q, q), F32)
            for pp in range(2):
                pair = 2 * g + pp
                psl = slice(LANES * pair, LANES * (pair + 1))
                dxd_e = []
                for e in range(2):
                    h = 2 * pair + e
                    keep = low if e == 0 else jnp.logical_not(low)
                    lm = _ssd_lmat(cs, cs_t, h)
                    mh = gmat * lm
                    dm = _dot_nt(jnp.where(keep, dy[:, psl], 0.0).astype(BF16), xd[:, psl])
                    dxd_e.append(_dot_tn(mh.astype(BF16), dyb[:, psl]))
                    wm = dm * mh
                    dcs = dcs + jnp.where(lane == h, jnp.sum(wm, axis=1, keepdims=True), 0.0)
                    dcs_rows = dcs_rows - jnp.where(sub == h, jnp.sum(wm, axis=0, keepdims=True), 0.0)
                    dg_acc = dg_acc + dm * lm
                dxd_parts.append(jnp.where(low, dxd_e[0], dxd_e[1]))
            dgb = dg_acc.astype(BF16)
            dc_parts.append(dc_g + _dot(dgb, bg))
            db_parts.append(db_g + _dot_tn(dgb, cg))
        y0 = jnp.concatenate(y0_parts, axis=1)
        zmat = jnp.concatenate(z_parts, axis=1)
        dxd = jnp.concatenate(dxd_parts, axis=1)
        ds_prev = jnp.concatenate(ds_prev_parts, axis=1) + cd_x * d_s
        ds_ref[...] = ds_prev

        dcs = dcs + _dot_hi(dy * y0, reduce) * e_col
        dcd = _dot_hi(jnp.sum(d_s * s_prev, axis=0, keepdims=True), reduce)
        dlast = dcd * cd
        dx = dx + w_x * zmat + dxd * dt_x
        dw = _dot_hi(zmat * x, reduce)
        ddt = ddt + dw * decay_end + _dot_hi(dxd * x, reduce)
        dwl = dw * w
        dcs = dcs - dwl
        dlast = dlast + jnp.sum(dwl, axis=0, keepdims=True)
        dcs = dcs + dcs_rows.T + jnp.where(sub == q - 1, dlast, 0.0)
        dda = _dot_hi(tril.T, dcs)
        ddt = ddt + dda * a_row
        a128_ref[1:2, :] += jnp.sum(dda * dt, axis=0, keepdims=True) * a_row
        draw = jnp.where(lane < N_HEADS, ddt * _sigmoid(dtr + bias_ref[...]), 0.0)
        a128_ref[0:1, :] += jnp.sum(draw, axis=0, keepdims=True)
        ddt_ref[...] = draw
        dact = jnp.concatenate([dx] + db_parts + dc_parts, axis=1)
        dpre_ref[...] = dact * _dsilu(pre_v)

    return _rowwise("ssd_bwd", body, t, q,
                    [('row', pre), ('col', rest, SSD_W, 2), ('col', rest, LANES, 20), ('row', y),
                     ('row', s_prev_all), ('col', d_mix, SSD_W, 0), ('full', dt_bias), ('full', a_log),
                     ('full', d_skip), ('full', norm_w)],
                    [('row', SSD_CONV, F32), ('row', SSD_W, F32), ('row', LANES, F32),
                     ('acc', (8, LANES), F32), ('acc', (8, SSD_W), F32)],
                    scratch=[pltpu.VMEM((SSD_STATE, SSD_W), F32)], reverse=True)


LRU_TM = 256


def _lru_gates(xc, wa, ba, wx, bx, lam):
    xb = xc.astype(BF16)
    r = _sigmoid(_dot(xb, wa) + ba)
    i = _sigmoid(_dot(xb, wx) + bx)
    sp = _softplus(-lam)
    a = jnp.exp(-LRU_C * r * sp)
    mult = jnp.sqrt(1.0 - a * a)
    return r, i, sp, a, mult


def _lru_fwd(xc, rest, wa, ba, wx, bx, lam):
    def body(i, nt, ins, outs, scr):
        xc_ref, g_ref, wa_ref, ba_ref, wx_ref, bx_ref, lam_ref = ins
        carry = scr[0]

        @pl.when(i == 0)
        def _():
            carry[...] = jnp.zeros(carry.shape, F32)

        xv = xc_ref[...]
        r, ig, sp, a, mult = _lru_gates(xv, wa_ref[...], ba_ref[...], wx_ref[...], bx_ref[...], lam_ref[...])
        u = mult * (ig * xv)
        row = _iota(a.shape, 0)
        s = 1
        while s < LRU_TM:
            a_sh = jnp.where(row >= s, pltpu.roll(a, s, 0), 1.0)
            u_sh = jnp.where(row >= s, pltpu.roll(u, s, 0), 0.0)
            u = a * u_sh + u
            a = a * a_sh
            s *= 2
        h = u + a * carry[0:1, :]
        carry[0:1, :] = h[LRU_TM - 1:LRU_TM, :]
        outs[1][...] = h
        outs[0][...] = h * _gelu(g_ref[...])

    return _rowwise("lru_fwd", body, xc.shape[0], LRU_TM,
                    [('row', xc), ('col', rest, LRU_W, 3), ('full', wa), ('full', ba), ('full', wx),
                     ('full', bx), ('full', lam)],
                    [('row', LRU_W, F32), ('row', LRU_W, F32)], scratch=[pltpu.VMEM((8, LRU_W), F32)])


def _lru_bwd(xc, rest, h, d_mix, wa, ba, wx, bx, lam, wa_t, wx_t):
    def body(i, nt, ins, outs, scr):
        xc_ref, g_ref, h_ref, hp_ref, do_ref, wa_ref, ba_ref, wx_ref, bx_ref, lam_ref, wat_ref, wxt_ref = ins
        dxc_ref, dg_ref, dza_ref, dzi_ref, acc_ref = outs
        carry = scr[0]

        @pl.when(i == nt - 1)
        def _():
            carry[...] = jnp.zeros(carry.shape, F32)

        xv, gv, hv, d_out = xc_ref[...], g_ref[...], h_ref[...], do_ref[...]
        r, ig, sp, a, mult = _lru_gates(xv, wa_ref[...], ba_ref[...], wx_ref[...], bx_ref[...], lam_ref[...])
        dg_ref[...] = d_out * hv * _dgelu(gv)
        gsum = d_out * _gelu(gv)
        row = _iota(a.shape, 0)
        b = jnp.where(row < LRU_TM - 1, pltpu.roll(a, LRU_TM - 1, 0), 1.0)
        s = 1
        while s < LRU_TM:
            keep = row < LRU_TM - s
            b_sh = jnp.where(keep, pltpu.roll(b, LRU_TM - s, 0), 1.0)
            g_sh = jnp.where(keep, pltpu.roll(gsum, LRU_TM - s, 0), 0.0)
            gsum = gsum + b * g_sh
            b = b * b_sh
            s *= 2
        dh = gsum + b * carry[0:1, :]
        carry[0:1, :] = a[0:1, :] * dh[0:1, :]
        h_prev = _shift_down(hv, 1, jnp.where(i > 0, hp_ref[...], 0.0))
        du = dh
        dmult = du * ig * xv
        di = du * mult * xv
        dxc = du * mult * ig
        da = dh * h_prev - dmult * a / mult
        dlog = da * a
        dr = dlog * (-LRU_C) * sp
        acc_ref[2:3, :] += jnp.sum(dlog * (-LRU_C) * r, axis=0, keepdims=True)
        dza = dr * r * (1.0 - r)
        dzi = di * ig * (1.0 - ig)
        acc_ref[0:1, :] += jnp.sum(dza, axis=0, keepdims=True)
        acc_ref[1:2, :] += jnp.sum(dzi, axis=0, keepdims=True)
        dzab, dzib = dza.astype(BF16), dzi.astype(BF16)
        dza_ref[...] = dzab
        dzi_ref[...] = dzib
        dxc_ref[...] = dxc + _dot(dzab, wat_ref[...]) + _dot(dzib, wxt_ref[...])

    return _rowwise("lru_bwd", body, xc.shape[0], LRU_TM,
                    [('row', xc), ('col', rest, LRU_W, 3), ('row', h), ('prev8', h, LRU_W, 0),
                     ('col', d_mix, LRU_W, 1), ('full', wa), ('full', ba), ('full', wx), ('full', bx),
                     ('full', lam), ('full', wa_t), ('full', wx_t)],
                    [('row', LRU_W, F32), ('row', LRU_W, F32), ('row', LRU_W, BF16), ('row', LRU_W, BF16),
                     ('acc', (8, LRU_W), F32)],
                    scratch=[pltpu.VMEM((8, LRU_W), F32)], reverse=True)


def _swiglu_act(gu):
    def body(i, nt, ins, outs, scr):
        outs[0][...] = (_silu(ins[0][...]) * ins[1][...]).astype(BF16)

    return _rowwise("swiglu_act", body, gu.shape[0], 512, [('col', gu, D_FFP, 0), ('col', gu, D_FFP, 1)],
                    [('row', D_FFP, BF16)])[0]


def _swiglu_bwd(gu, dact):
    def body(i, nt, ins, outs, scr):
        gv, uv, da = ins[0][...], ins[1][...], ins[2][...]
        outs[0][:, :D_FFP] = (da * uv * _dsilu(gv)).astype(BF16)
        outs[0][:, D_FFP:] = (da * _silu(gv)).astype(BF16)

    return _rowwise("swiglu_bwd", body, gu.shape[0], 256,
                    [('col', gu, D_FFP, 0), ('col', gu, D_FFP, 1), ('row', dact)], [('row', 2 * D_FFP, BF16)])[0]


def _layer_fwd(x, w, l):
    tag = "_l%d" % l
    h = _rmsnorm_fwd("norm_mix" + tag, x, w['norm_mix'][l])
    qkv = _mm("proj_qkv" + tag, [h], w['w_qkv'][l], tn=768)
    rest = _mm("proj_rest" + tag, [h], w['w_rest'][l], tn=896)
    att, lse = _attention_fwd(qkv)
    pre = _conv_fwd("ssd_conv" + tag, rest, SSD_CONV, 0, w['ssd_conv_w'][l], w['ssd_conv_b'][l])
    ssd, y, s_prev = _ssd_fwd(pre, rest, w['ssd_dt_bias'][l], w['ssd_a_log'][l], w['ssd_d'][l], w['ssd_norm'][l])
    xc = _conv_fwd("lru_conv" + tag, rest, LRU_W, 4, w['lru_conv_w'][l], w['lru_conv_b'][l])
    lru, hl = _lru_fwd(xc, rest, w['lru_wa'][l], w['lru_ba'][l], w['lru_wx'][l], w['lru_bx'][l], w['lru_lambda'][l])
    x_mid = _mm("proj_out" + tag, [att, ssd, lru], w['w_out'][l], res=x, tn=512)
    h2 = _rmsnorm_fwd("norm_ffn" + tag, x_mid, w['norm_ffn'][l])
    gu = _mm("proj_gu" + tag, [h2], w['w_gu'][l], tn=1536)
    act = _swiglu_act(gu)
    x_next = _mm("proj_down" + tag, [act], w['w_down'][l], res=x_mid, tn=512)
    saved = dict(x=x, h=h, qkv=qkv, rest=rest, att=att, lse=lse, pre=pre, ssd=ssd, y=y, s_prev=s_prev, xc=xc,
                 lru=lru, hl=hl, x_mid=x_mid, h2=h2, gu=gu, act=act)
    return x_next, saved


def _layer_bwd(dx_next, sv, w, l):
    tag = "_l%d_b" % l
    t = dx_next.shape[0]
    g = {}
    dact = _mm("d_act" + tag, [dx_next], w['w_down_t'][l], tn=1536)
    g['w_down'] = _mm_tn("dw_down" + tag, sv['act'], dx_next, tk=1536)
    dgu = _swiglu_bwd(sv['gu'], dact)
    dh2 = _mm("d_h2" + tag, [dgu], w['w_gu_t'][l], tn=512, tm=256)
    g['w_gate'] = _mm_tn("dw_gate" + tag, sv['h2'], dgu, g_col=(D_FFP, 0), tn=1536)
    g['w_up'] = _mm_tn("dw_up" + tag, sv['h2'], dgu, g_col=(D_FFP, 1), tn=1536)
    dx_mid, acc = _rmsnorm_bwd("norm_ffn" + tag, dh2, sv['x_mid'], w['norm_ffn'][l], dx_next)
    g['norm_ffn'] = acc[0]
    d_att = _mm("d_att" + tag, [dx_mid], w['w_out_t'][l][:, :ATT_W], tn=512)
    d_mix = _mm("d_mix" + tag, [dx_mid], w['w_out_t'][l][:, ATT_W:], tn=512)
    g['w_out'] = jnp.concatenate([_mm_tn("dw_out%d" % k + tag, a, dx_mid)
                                  for k, a in enumerate((sv['att'], sv['ssd'], sv['lru']))], axis=0)
    dxc, dgl, dza, dzi, acc = _lru_bwd(sv['xc'], sv['rest'], sv['hl'], d_mix, w['lru_wa'][l], w['lru_ba'][l],
                                       w['lru_wx'][l], w['lru_bx'][l], w['lru_lambda'][l],
                                       w['lru_wa_t'][l], w['lru_wx_t'][l])
    g['lru_ba'], g['lru_bx'] = acc[0], acc[1]
    g['lru_lambda'] = acc[2] * (-_sigmoid(-w['lru_lambda'][l][0]))
    g['lru_wa'] = _diag_blocks(_mm_tn("dw_lru_a" + tag, sv['xc'], dza))
    g['lru_wx'] = _diag_blocks(_mm_tn("dw_lru_x" + tag, sv['xc'], dzi))
    dxl, acc = _conv_bwd("lru_conv" + tag, dxc, sv['rest'], LRU_W, 4, w['lru_conv_w'][l])
    g['lru_conv_w'], g['lru_conv_b'] = acc[:4], acc[4]
    dpre, dz, ddt, a128, a512 = _ssd_bwd(sv['pre'], sv['rest'], sv['y'], sv['s_prev'], d_mix, w['ssd_dt_bias'][l],
                                         w['ssd_a_log'][l], w['ssd_d'][l], w['ssd_norm'][l])
    g['ssd_dt_bias'], g['ssd_a_log'], g['ssd_d'] = a128[0, :N_HEADS], a128[1, :N_HEADS], a128[2, :N_HEADS]
    g['ssd_norm'] = a512[0]
    dxbc, acc = _conv_bwd("ssd_conv" + tag, dpre, sv['rest'], SSD_CONV, 0, w['ssd_conv_w'][l])
    g['ssd_conv_w'], g['ssd_conv_b'] = acc[:4], acc[4]
    dq, dk, dv = _attention_bwd(sv['qkv'], d_att, sv['att'], sv['lse'])
    pieces = [dq, dk, dv, dxbc, dz, dgl, dxl, ddt]
    dh = _mm("d_h" + tag, pieces, w['w_in_t'][l], tn=512)
    dws = [_mm_tn("dw_in%d" % k + tag, sv['h'], p) for k, p in enumerate(pieces)]
    g['w_in'] = jnp.concatenate([dws[0], dws[1], dws[2], dws[4], dws[3], _dt_tile_place(dws[7]), dws[5], dws[6]],
                                axis=1)
    dx, acc = _rmsnorm_bwd("norm_mix" + tag, dh, sv['x'], w['norm_mix'][l], dx_mid)
    g['norm_mix'] = acc[0]
    return dx, g


def _diag_blocks(m):
    return jnp.stack([m[64 * n:64 * (n + 1), 64 * n:64 * (n + 1)] for n in range(8)])


def _block_diag(w):
    eye = jnp.eye(8, dtype=w.dtype)
    return (w[:, :, None, :] * eye[:, None, :, None]).reshape(512, 512)


_ANY = pl.BlockSpec(memory_space=pl.ANY)
_MESH = pl.DeviceIdType.MESH


def _all_gather(name, xs):
    n = len(xs)

    def body(*refs):
        x_refs, out_refs = refs[:n], refs[n:2 * n]
        send_sems, recv_sems, local_sems = refs[2 * n:]
        x_, y_, c_ = lax.axis_index("x"), lax.axis_index("y"), lax.axis_index("c")
        me, sibling = (x_, y_, c_), (x_, y_, 1 - c_)
        chips = [(1 - x_, y_), (x_, 1 - y_), (1 - x_, 1 - y_)]

        def slot(a, px, py, pc):
            return out_refs[a].at[4 * px + 2 * py + pc]

        def copy(a, k, block, to, src=None):
            return pltpu.make_async_remote_copy(
                src_ref=slot(a, *block) if src is None else src, dst_ref=slot(a, *block),
                send_sem=send_sems.at[a, k], recv_sem=recv_sems.at[a, k], device_id=to, device_id_type=_MESH)

        mine = [pltpu.make_async_copy(x_refs[a], slot(a, *me), local_sems.at[a]) for a in range(n)]
        for cp in mine:
            cp.start()
        first = []
        for a in range(n):
            first.append(copy(a, 0, me, sibling, src=x_refs[a]))
            first += [copy(a, 1 + j, me, (*chip, c_), src=x_refs[a]) for j, chip in enumerate(chips)]
        for cp in first:
            cp.start()
        passed = []
        for j, chip in enumerate(chips):
            for a in range(n):
                copy(a, 1 + j, (*chip, c_), me).wait_recv()
                fwd = copy(a, 4 + j, (*chip, c_), sibling)
                fwd.start()
                passed.append(fwd)
        for a in range(n):
            copy(a, 0, sibling, me).wait_recv()
            for j, chip in enumerate(chips):
                copy(a, 4 + j, (*chip, 1 - c_), me).wait_recv()
        for cp in first + passed:
            cp.wait_send()
        for cp in mine:
            cp.wait()

    return pl.pallas_call(
        body, name=name, out_shape=[jax.ShapeDtypeStruct((N_DEV,) + x.shape, x.dtype) for x in xs],
        in_specs=[_ANY] * n, out_specs=[_ANY] * n,
        scratch_shapes=[pltpu.SemaphoreType.DMA((n, 7)), pltpu.SemaphoreType.DMA((n, 7)),
                        pltpu.SemaphoreType.DMA((n,))],
    )(*xs)


def _all_to_all(name, xs):
    n = len(xs)

    def body(*refs):
        x_refs, out_refs = refs[:n], refs[n:2 * n]
        send_sems, recv_sems, local_sems = refs[2 * n:]
        x_, y_, c_ = lax.axis_index("x"), lax.axis_index("y"), lax.axis_index("c")
        me = 4 * x_ + 2 * y_ + c_

        def peer(k):
            return ((1 - x_) if k & 4 else x_, (1 - y_) if k & 2 else y_, (1 - c_) if k & 1 else c_)

        def copy(a, k):
            px, py, pc = peer(k)
            return pltpu.make_async_remote_copy(
                src_ref=x_refs[a].at[4 * px + 2 * py + pc], dst_ref=out_refs[a].at[me],
                send_sem=send_sems.at[a, k - 1], recv_sem=recv_sems.at[a, k - 1],
                device_id=(px, py, pc), device_id_type=_MESH)

        def arrival(a, k):
            px, py, pc = peer(k)
            return pltpu.make_async_remote_copy(
                src_ref=x_refs[a].at[me], dst_ref=out_refs[a].at[4 * px + 2 * py + pc],
                send_sem=send_sems.at[a, k - 1], recv_sem=recv_sems.at[a, k - 1],
                device_id=(px, py, pc), device_id_type=_MESH)

        mine = [pltpu.make_async_copy(x_refs[a].at[me], out_refs[a].at[me], local_sems.at[a]) for a in range(n)]
        for cp in mine:
            cp.start()
        copies = [copy(a, k) for a in range(n) for k in range(1, N_DEV)]
        for cp in copies:
            cp.start()
        for a in range(n):
            for k in range(1, N_DEV):
                arrival(a, k).wait_recv()
        for cp in copies:
            cp.wait_send()
        for cp in mine:
            cp.wait()

    return pl.pallas_call(
        body, name=name, out_shape=[jax.ShapeDtypeStruct(x.shape, x.dtype) for x in xs],
        in_specs=[_ANY] * n, out_specs=[_ANY] * n,
        scratch_shapes=[pltpu.SemaphoreType.DMA((n, 7)), pltpu.SemaphoreType.DMA((n, 7)),
                        pltpu.SemaphoreType.DMA((n,))],
    )(*xs)


def _window_offset():
    me = 4 * lax.axis_index("x") + 2 * lax.axis_index("y") + lax.axis_index("c")
    return jnp.where(me < 6, me, me + 120)


def _place_w_in(shard):
    def body(i, nt, ins, outs, scr):
        outs[0][...] = pltpu.roll(ins[0][...], _window_offset(), 1).astype(BF16)

    return _rowwise("place_w_in", body, shard.shape[0], 256, [('row', shard)], [('row', IN_WIN, BF16)])[0]


def _adamw(name, w, m, v, g, tr, from_window=False):
    s_parts, r, c = g.shape
    assert r % tr == 0

    def kern(w_ref, m_ref, v_ref, g_ref, go_ref, d_ref, mo_ref, vo_ref):
        gs = g_ref[0].astype(F32)
        for s in range(1, s_parts):
            gs = gs + g_ref[s].astype(F32)
        if from_window:
            gs = pltpu.roll(gs, c - _window_offset(), 1)
        wv = w_ref[...]
        m2 = ADAM_B1 * m_ref[...] + (1.0 - ADAM_B1) * gs
        v2 = ADAM_B2 * v_ref[...] + (1.0 - ADAM_B2) * (gs * gs)
        m_hat = m2 / (1.0 - ADAM_B1 ** ADAM_STEP)
        v_hat = v2 / (1.0 - ADAM_B2 ** ADAM_STEP)
        go_ref[...] = gs
        d_ref[...] = -ADAM_LR * (m_hat / (jnp.sqrt(v_hat) + ADAM_EPS) + ADAM_WD * wv)
        mo_ref[...] = m2
        vo_ref[...] = v2

    spec = pl.BlockSpec((tr, c), lambda i: (i, 0))
    shp = jax.ShapeDtypeStruct((r, c), F32)
    return pl.pallas_call(kern, name=name, grid=(r // tr,),
                          in_specs=[spec, spec, spec, pl.BlockSpec((s_parts, tr, c), lambda i: (0, i, 0))],
                          out_specs=[spec] * 4, out_shape=[shp] * 4, compiler_params=_params(1))(w, m, v, g)


def _pack(arrs, rows, lead=0):
    parts = []
    for a in arrs:
        flat = a.reshape(a.shape[:lead] + (-1,))
        pad = (-flat.shape[-1]) % LANES
        if pad:
            flat = jnp.pad(flat, [(0, 0)] * lead + [(0, pad)])
        parts.append(flat)
    flat = jnp.concatenate(parts, axis=-1)
    pad = rows * LANES - flat.shape[-1]
    assert pad >= 0
    if pad:
        flat = jnp.pad(flat, [(0, 0)] * lead + [(0, pad)])
    return flat.reshape(flat.shape[:lead] + (rows, LANES))


def _unpack(buf, shapes, lead=0):
    flat = buf.reshape(buf.shape[:lead] + (-1,))
    out, off = [], 0
    for shp in shapes:
        n = math.prod(shp)
        out.append(flat[..., off:off + n].reshape(buf.shape[:lead] + tuple(shp)))
        off += n + ((-n) % LANES)
    return out


BIG = ('w_in', 'w_out', 'w_gate', 'w_up', 'w_down')
CONV =('ssd_conv_w', 'lru_conv_w')
CONV_SHARD_SHAPES = ((DEPTH, 4, SSD_CONV // N_DEV), (DEPTH, 4, LRU_W // N_DEV))
CONV_ROWS = 16
SMALL = ('norm_mix', 'ssd_conv_b', 'ssd_dt_bias', 'ssd_a_log', 'ssd_d', 'ssd_norm', 'lru_conv_b', 'lru_wa',
         'lru_ba', 'lru_wx', 'lru_bx', 'lru_lambda', 'norm_ffn', 'norm_final')
SMALL_ROWS = 1280
SMALL_TILE = 256
WEIGHTS = ('norm_mix', 'w_in', 'ssd_conv_w', 'ssd_conv_b', 'ssd_dt_bias', 'ssd_a_log', 'ssd_d', 'ssd_norm',
           'lru_conv_w', 'lru_conv_b', 'lru_wa', 'lru_ba', 'lru_wx', 'lru_bx', 'lru_lambda', 'w_out', 'norm_ffn',
           'w_gate', 'w_up', 'w_down', 'norm_final')


def _join_cols(a):
    return jnp.transpose(a, (1, 2, 0, 3)).reshape(a.shape[1], a.shape[2], -1)


def _join_rows(a):
    return jnp.transpose(a, (1, 0, 2, 3)).reshape(a.shape[1], -1, a.shape[3])


def _split_cols(a):
    l, r, c = a.shape
    return jnp.transpose(a.reshape(l, r, N_DEV, c // N_DEV), (2, 0, 1, 3)).reshape(N_DEV, l * r, c // N_DEV)


def _split_rows(a):
    l, r, c = a.shape
    return jnp.transpose(a.reshape(l, N_DEV, r // N_DEV, c), (1, 0, 2, 3)).reshape(N_DEV, l * r // N_DEV, c)


def _shard_form(k, a):
    if k == 'w_in':
        return jnp.pad(a.reshape(-1, IN_SHARD), ((0, 0), (0, IN_WIN - IN_SHARD)))
    if k in ('w_gate', 'w_up'):
        return jnp.pad(a.reshape(-1, FF_SHARD), ((0, 0), (0, FF_SHARD_P - FF_SHARD)))
    if k == 'w_down':
        return jnp.pad(a, ((0, 0), (0, FF_SHARD_P - FF_SHARD), (0, 0))).reshape(-1, D_MODEL)
    return a.reshape(-1, D_MODEL)


def _shard_back(k, a):
    if k == 'w_in':
        return a[:, :IN_SHARD].reshape(DEPTH, D_MODEL, IN_SHARD)
    if k in ('w_gate', 'w_up'):
        return a[:, :FF_SHARD].reshape(DEPTH, D_MODEL, FF_SHARD)
    if k == 'w_down':
        return a.reshape(DEPTH, FF_SHARD_P, D_MODEL)[:, :FF_SHARD]
    return a.reshape(DEPTH, D_MIX // N_DEV, D_MODEL)


def _dt_tile_place(a):
    zeros = jnp.zeros(a.shape[:-1] + (LANES - N_HEADS,), a.dtype)
    return jnp.concatenate([a[..., :6], zeros, a[..., 6:8]], axis=-1)


def _dt_tile_heads(tile):
    zeros = jnp.zeros(tile.shape[:-1] + (LANES - N_HEADS,), tile.dtype)
    return jnp.concatenate([tile[..., :6], tile[..., 126:128], zeros], axis=-1)


def _layout_from_windows(win):
    r = win.shape[1]
    main = jnp.concatenate([win[j][:, :512] for j in range(N_DEV)] + [jnp.zeros((r, LANES), win.dtype)], axis=1)
    gap = jnp.zeros((r, 384), win.dtype)
    tails = [jnp.zeros((r, 512), win.dtype)]
    for j in range(N_DEV - 1):
        tails += [win[j][:, 512:], gap]
    tails.append(win[N_DEV - 1][:, 512:])
    return main + jnp.concatenate(tails, axis=1)


def _prepare_weights(p, full):
    w = {}
    w_in = full['w_in']
    w_qkv = w_in[:, :, :D_MIX]
    dt_cols = _dt_tile_heads(w_in[:, :, 3072:3200])
    w_rest = jnp.concatenate([w_in[:, :, 2048:3072], w_in[:, :, 1536:2048], w_in[:, :, 3200:3712],
                              w_in[:, :, 3712:4224], dt_cols], axis=2)
    w['w_qkv'], w['w_rest'] = w_qkv, w_rest
    w['w_in_t'] = jnp.transpose(jnp.concatenate([w_qkv, w_rest], axis=2), (0, 2, 1))
    w['w_out'] = full['w_out']
    w['w_out_t'] = jnp.transpose(full['w_out'], (0, 2, 1))
    w['w_gu'] = jnp.concatenate([full['w_gate'], full['w_up']], axis=2)
    w['w_gu_t'] = jnp.transpose(w['w_gu'], (0, 2, 1))
    w['w_down'] = full['w_down']
    w['w_down_t'] = jnp.transpose(full['w_down'], (0, 2, 1))
    for k in ('norm_mix', 'ssd_conv_b', 'ssd_norm', 'lru_conv_b', 'lru_ba', 'lru_bx', 'lru_lambda', 'norm_ffn'):
        w[k] = p[k][:, None, :]
    for k in ('ssd_dt_bias', 'ssd_a_log', 'ssd_d'):
        w[k] = jnp.pad(p[k], ((0, 0), (0, LANES - N_HEADS)))[:, None, :]
    for k in CONV:
        w[k] = jnp.pad(full[k], ((0, 0), (0, 4), (0, 0)))
    for k in ('lru_wa', 'lru_wx'):
        bd = jnp.stack([_block_diag(p[k][l]) for l in range(DEPTH)]).astype(BF16)
        w[k] = bd
        w[k + '_t'] = jnp.transpose(bd, (0, 2, 1))
    return w


def _local_step(x, target, w, norm_final):
    saved = []
    for l in range(DEPTH):
        x, sv = _layer_fwd(x, w, l)
        saved.append(sv)
    dx, loss_acc, dgf = _final_loss(x, norm_final[None, :], target)
    grads = [None] * DEPTH
    for l in reversed(range(DEPTH)):
        dx, grads[l] = _layer_bwd(dx, saved[l], w, l)
    g = {k: jnp.stack([grads[l][k] for l in range(DEPTH)]) for k in grads[0]}
    g['norm_final'] = dgf[0]
    return loss_acc[0, 0], dx, g


def kernel(x, norm_mix, w_in, ssd_conv_w, ssd_conv_b, ssd_dt_bias, ssd_a_log, ssd_d, ssd_norm, lru_conv_w, lru_conv_b, lru_wa, lru_ba, lru_wx, lru_bx, lru_lambda, w_out, norm_ffn, w_gate, w_up, w_down, norm_final, loss_target, m_norm_mix, m_w_in, m_ssd_conv_w, m_ssd_conv_b, m_ssd_dt_bias, m_ssd_a_log, m_ssd_d, m_ssd_norm, m_lru_conv_w, m_lru_conv_b, m_lru_wa, m_lru_ba, m_lru_wx, m_lru_bx, m_lru_lambda, m_w_out, m_norm_ffn, m_w_gate, m_w_up, m_w_down, m_norm_final, v_norm_mix, v_w_in, v_ssd_conv_w, v_ssd_conv_b, v_ssd_dt_bias, v_ssd_a_log, v_ssd_d, v_ssd_norm, v_lru_conv_w, v_lru_conv_b, v_lru_wa, v_lru_ba, v_lru_wx, v_lru_bx, v_lru_lambda, v_w_out, v_norm_ffn, v_w_gate, v_w_up, v_w_down, v_norm_final):
    args = (norm_mix, w_in, ssd_conv_w, ssd_conv_b, ssd_dt_bias, ssd_a_log, ssd_d, ssd_norm, lru_conv_w, lru_conv_b, lru_wa, lru_ba, lru_wx, lru_bx, lru_lambda, w_out, norm_ffn, w_gate, w_up, w_down, norm_final)
    margs = (m_norm_mix, m_w_in, m_ssd_conv_w, m_ssd_conv_b, m_ssd_dt_bias, m_ssd_a_log, m_ssd_d, m_ssd_norm, m_lru_conv_w, m_lru_conv_b, m_lru_wa, m_lru_ba, m_lru_wx, m_lru_bx, m_lru_lambda, m_w_out, m_norm_ffn, m_w_gate, m_w_up, m_w_down, m_norm_final)
    vargs = (v_norm_mix, v_w_in, v_ssd_conv_w, v_ssd_conv_b, v_ssd_dt_bias, v_ssd_a_log, v_ssd_d, v_ssd_norm, v_lru_conv_w, v_lru_conv_b, v_lru_wa, v_lru_ba, v_lru_wx, v_lru_bx, v_lru_lambda, v_w_out, v_norm_ffn, v_w_gate, v_w_up, v_w_down, v_norm_final)
    p = dict(zip(WEIGHTS, args))
    pm = dict(zip(WEIGHTS, margs))
    pv = dict(zip(WEIGHTS, vargs))

    forms = {k: _shard_form(k, p[k]) for k in BIG}
    send = [_place_w_in(forms['w_in'])] + [forms[k].astype(BF16) for k in BIG[1:]]
    got = _all_gather("gather_weights", send + [_pack([p[k] for k in CONV], CONV_ROWS)])
    full = {'w_in': _layout_from_windows(got[0]).reshape(DEPTH, D_MODEL, IN_COLS_P),
            'w_out': _join_rows(got[1].reshape(N_DEV, DEPTH, D_MIX // N_DEV, D_MODEL)),
            'w_gate': _join_cols(got[2].reshape(N_DEV, DEPTH, D_MODEL, FF_SHARD_P)),
            'w_up': _join_cols(got[3].reshape(N_DEV, DEPTH, D_MODEL, FF_SHARD_P)),
            'w_down': _join_rows(got[4].reshape(N_DEV, DEPTH, FF_SHARD_P, D_MODEL))}
    for k, a in zip(CONV, _unpack(got[5], CONV_SHARD_SHAPES, lead=1)):
        full[k] = _join_cols(a)
    w = _prepare_weights(p, full)

    loss_local, dx, g = _local_step(x[0], loss_target[0], w, norm_final)
    loss = lax.psum(loss_local, ("x", "y", "c"))

    small_g = _all_gather("gather_small_grads", [_pack([g[k] for k in SMALL + CONV], SMALL_ROWS)])[0]
    zeros = [jnp.zeros_like(g[k]) for k in CONV]
    res_small = _adamw("adamw_small", _pack([p[k] for k in SMALL] + zeros, SMALL_ROWS),
                       _pack([pm[k] for k in SMALL] + zeros, SMALL_ROWS),
                       _pack([pv[k] for k in SMALL] + zeros, SMALL_ROWS), small_g, SMALL_TILE)
    small_shapes = [g[k].shape for k in SMALL + CONV]
    out = {kind: {} for kind in range(4)}
    for kind in range(4):
        for k, a in zip(SMALL + CONV, _unpack(res_small[kind], small_shapes)):
            out[kind][k] = a
    me = 4 * lax.axis_index("x") + 2 * lax.axis_index("y") + lax.axis_index("c")
    conv_g = []
    for k, shp in zip(CONV, CONV_SHARD_SHAPES):
        conv_g.append(lax.dynamic_slice_in_dim(out[0][k], me * shp[2], shp[2], axis=2))
    res_conv = _adamw("adamw_conv", _pack([p[k] for k in CONV], CONV_ROWS), _pack([pm[k] for k in CONV], CONV_ROWS),
                      _pack([pv[k] for k in CONV], CONV_ROWS), _pack(conv_g, CONV_ROWS)[None], CONV_ROWS)
    for kind in range(4):
        for k, a in zip(CONV, _unpack(res_conv[kind], CONV_SHARD_SHAPES)):
            out[kind][k] = a

    g_in = g['w_in'].reshape(DEPTH * D_MODEL, IN_COLS_P)
    dest = [jnp.stack([g_in[:, 512 * j:512 * j + IN_WIN] for j in range(N_DEV)]), _split_rows(g['w_out']),
            _split_cols(g['w_gate']), _split_cols(g['w_up']), _split_rows(g['w_down'])]
    parts = _all_to_all("exchange_big_grads", [a.astype(BF16) for a in dest])
    tiles = {'w_in': 256, 'w_out': 128, 'w_gate': 512, 'w_up': 512, 'w_down': 256}
    for k, part in zip(BIG, parts):
        res = _adamw("adamw_" + k, forms[k], _shard_form(k, pm[k]), _shard_form(k, pv[k]), part, tiles[k],
                     from_window=(k == 'w_in'))
        for kind in range(4):
            out[kind][k] = _shard_back(k, res[kind])

    outs = [loss, dx[None]]
    for kind in range(4):
        outs += [out[kind][k] for k in WEIGHTS]
    return tuple(outs)
```

```python
import functools
import math

import jax
import jax.numpy as jnp
from jax import lax
from jax.experimental import pallas as pl
from jax.experimental.pallas import tpu as pltpu

F32 = jnp.float32
BF16 = jnp.bfloat16

N_DEV = 8
DEPTH = 2
D_MODEL = 1024
ATT_W = 512
HEAD_DIM = 64
N_HEADS = 8
ATT_BLOCK = 128
ATT_DILATIONS = (16, 4, 1)
SSD_W = 512
SSD_STATE = 128
SSD_CONV = 1024
SSD_CHUNK = 128
LRU_W = 512
LRU_C = 8.0
D_MIX = 1536
D_FF = 2816
FF_SHARD = D_FF // N_DEV
FF_SHARD_P = 384
D_FFP = N_DEV * FF_SHARD_P
IN_COLS = 4104
IN_SHARD = IN_COLS // N_DEV
IN_WIN = 640
IN_COLS_P = 4224
REST_COLS = 2688
NORM_EPS = 1e-6
SSD_NORM_EPS = 1e-5
NEG = -1e30

ADAM_LR = 0.001
ADAM_B1 = 0.9
ADAM_B2 = 0.999
ADAM_EPS = 1e-08
ADAM_WD = 0.01
ADAM_STEP = 10

LANES = 128
VMEM_LIMIT = 52 * 1024 * 1024
HI = lax.Precision.HIGHEST


def _sigmoid(x):
    return 1.0 / (1.0 + jnp.exp(-x))


def _silu(x):
    return x * _sigmoid(x)


def _dsilu(x):
    s = _sigmoid(x)
    return s * (1.0 + x * (1.0 - s))


def _softplus(x):
    return jnp.maximum(x, 0.0) + jnp.log(1.0 + jnp.exp(-jnp.abs(x)))


_GELU_C = math.sqrt(2.0 / math.pi)


def _gelu(x):
    return 0.5 * x * (1.0 + jnp.tanh(_GELU_C * (x + 0.044715 * x * x * x)))


def _dgelu(x):
    t = jnp.tanh(_GELU_C * (x + 0.044715 * x * x * x))
    return 0.5 * (1.0 + t) + 0.5 * x * (1.0 - t * t) * _GELU_C * (1.0 + 3.0 * 0.044715 * x * x)


def _dot(a, b):
    return jnp.dot(a, b, preferred_element_type=F32)


def _dot_nt(a, b):
    return lax.dot_general(a, b, (((1,), (1,)), ((), ())), preferred_element_type=F32)


def _dot_tn(a, b):
    return lax.dot_general(a, b, (((0,), (0,)), ((), ())), preferred_element_type=F32)


def _dot_hi(a, b):
    return jnp.dot(a, b, preferred_element_type=F32, precision=HI)


def _iota(shape, axis):
    return lax.broadcasted_iota(jnp.int32, shape, axis)


def _shift_down(x, s, prev8):
    xs = pltpu.roll(x, s, 0)
    ps = pltpu.roll(prev8, s, 0)
    top = jnp.concatenate([ps, x[8:]], axis=0)
    return jnp.where(_iota(x.shape, 0) < s, top, xs)


def _shift_up(x, s, next8):
    tm = x.shape[0]
    xs = pltpu.roll(x, tm - s, 0)
    ns = pltpu.roll(next8, 8 - s, 0)
    bottom = jnp.concatenate([x[:tm - 8], ns], axis=0)
    return jnp.where(_iota(x.shape, 0) >= tm - s, bottom, xs)


def _expand_mat():
    return jnp.where(_iota((LANES, SSD_W), 1) // HEAD_DIM == _iota((LANES, SSD_W), 0), 1.0, 0.0).astype(F32)


def _reduce_mat():
    return jnp.where(_iota((SSD_W, LANES), 0) // HEAD_DIM == _iota((SSD_W, LANES), 1), 1.0, 0.0).astype(F32)


def _params(n_grid):
    return pltpu.CompilerParams(dimension_semantics=("arbitrary",) * n_grid, vmem_limit_bytes=VMEM_LIMIT)


def _rowwise(name, body, n_rows, tm, ins, outs, scratch=(), reverse=False):
    nt = n_rows // tm
    assert nt * tm == n_rows and tm % 8 == 0
    r8 = tm // 8
    last8 = n_rows // 8 - 1

    def pos(s):
        return (nt - 1 - s) if reverse else s

    in_specs, args = [], []
    for spec in ins:
        kind, arr = spec[0], spec[1]
        args.append(arr)
        if kind == 'row':
            in_specs.append(pl.BlockSpec((tm, arr.shape[1]), lambda s: (pos(s), 0)))
        elif kind == 'col':
            in_specs.append(pl.BlockSpec((tm, spec[2]), functools.partial(lambda s, j: (pos(s), j), j=spec[3])))
        elif kind == 'full':
            in_specs.append(pl.BlockSpec(arr.shape, functools.partial(lambda s, n: (0,) * n, n=arr.ndim)))
        elif kind == 'prev8':
            in_specs.append(pl.BlockSpec((8, spec[2]), functools.partial(
                lambda s, j: (jnp.maximum(pos(s) * r8 - 1, 0), j), j=spec[3])))
        elif kind == 'next8':
            in_specs.append(pl.BlockSpec((8, spec[2]), functools.partial(
                lambda s, j: (jnp.minimum((pos(s) + 1) * r8, last8), j), j=spec[3])))
        else:
            raise ValueError(kind)
    out_specs, out_shape, acc_idx = [], [], []
    for k, spec in enumerate(outs):
        if spec[0] == 'row':
            out_specs.append(pl.BlockSpec((tm, spec[1]), lambda s: (pos(s), 0)))
            out_shape.append(jax.ShapeDtypeStruct((n_rows, spec[1]), spec[2]))
        else:
            out_specs.append(pl.BlockSpec(spec[1], lambda s: (0, 0)))
            out_shape.append(jax.ShapeDtypeStruct(spec[1], spec[2]))
            acc_idx.append(k)
    n_in, n_out = len(ins), len(outs)

    def kern(*refs):
        s = pl.program_id(0)
        in_refs, out_refs, scr = refs[:n_in], refs[n_in:n_in + n_out], refs[n_in + n_out:]

        @pl.when(s == 0)
        def _():
            for k in acc_idx:
                out_refs[k][...] = jnp.zeros(out_refs[k].shape, out_refs[k].dtype)

        body(pos(s), nt, in_refs, out_refs, scr)

    res = pl.pallas_call(kern, name=name, grid=(nt,), in_specs=in_specs, out_specs=out_specs,
                         out_shape=out_shape, scratch_shapes=list(scratch), compiler_params=_params(1))(*args)
    return res


def _mm(name, a_list, b, *, res=None, out_dtype=F32, tm=512, tn=None):
    n_rows = a_list[0].shape[0]
    k_total, n = b.shape
    ks = [a.shape[1] for a in a_list]
    assert sum(ks) == k_total
    tn = n if tn is None else tn
    assert n_rows % tm == 0 and n % tn == 0
    na = len(a_list)

    def kern(*refs):
        a_refs, b_ref, o_ref = refs[:na], refs[na], refs[-1]
        acc, off = None, 0
        for a_ref, kp in zip(a_refs, ks):
            part = _dot(a_ref[...].astype(BF16), b_ref[off:off + kp, :])
            acc = part if acc is None else acc + part
            off += kp
        if res is not None:
            acc = acc + refs[na + 1][...]
        o_ref[...] = acc.astype(out_dtype)

    in_specs = [pl.BlockSpec((tm, kp), lambda i, j: (i, 0)) for kp in ks]
    in_specs.append(pl.BlockSpec((k_total, tn), lambda i, j: (0, j)))
    args = list(a_list) + [b]
    if res is not None:
        in_specs.append(pl.BlockSpec((tm, tn), lambda i, j: (i, j)))
        args.append(res)
    return pl.pallas_call(kern, name=name, grid=(n_rows // tm, n // tn), in_specs=in_specs,
                          out_specs=pl.BlockSpec((tm, tn), lambda i, j: (i, j)),
                          out_shape=jax.ShapeDtypeStruct((n_rows, n), out_dtype),
                          compiler_params=_params(2))(*args)


def _mm_tn(name, a, g, *, a_col=None, g_col=None, tk=None, tn=None, tt=512):
    n_rows = a.shape[0]
    k = a.shape[1] if a_col is None else a_col[0]
    a_j = 0 if a_col is None else a_col[1]
    n = g.shape[1] if g_col is None else g_col[0]
    g_j = 0 if g_col is None else g_col[1]
    tk = k if tk is None else tk
    tn = n if tn is None else tn
    assert k % tk == 0 and n % tn == 0 and n_rows % tt == 0
    kb = k // tk
    nbk = n // tn

    def kern(a_ref, g_ref, o_ref):
        t = pl.program_id(2)

        @pl.when(t == 0)
        def _():
            o_ref[...] = jnp.zeros(o_ref.shape, F32)

        o_ref[...] += _dot_tn(a_ref[...].astype(BF16), g_ref[...].astype(BF16))

    return pl.pallas_call(
        kern, name=name, grid=(kb, n // tn, n_rows // tt),
        in_specs=[pl.BlockSpec((tt, tk), lambda i, j, t: (t, a_j * kb + i)),
                  pl.BlockSpec((tt, tn), lambda i, j, t: (t, g_j * nbk + j))],
        out_specs=pl.BlockSpec((tk, tn), lambda i, j, t: (i, j)),
        out_shape=jax.ShapeDtypeStruct((k, n), F32), compiler_params=_params(3))(a, g)


def _rmsnorm_fwd(name, x, g):
    def body(i, nt, ins, outs, scr):
        xv = ins[0][...]
        rstd = lax.rsqrt(jnp.mean(xv * xv, axis=-1, keepdims=True) + NORM_EPS)
        outs[0][...] = (xv * rstd * ins[1][...]).astype(BF16)

    return _rowwise(name, body, x.shape[0], 512, [('row', x), ('full', g)], [('row', x.shape[1], BF16)])[0]


def _rmsnorm_bwd(name, dh, x, g, dres):
    d = x.shape[1]

    def body(i, nt, ins, outs, scr):
        dy, xv, gv, dr = ins[0][...], ins[1][...], ins[2][...], ins[3][...]
        rstd = lax.rsqrt(jnp.mean(xv * xv, axis=-1, keepdims=True) + NORM_EPS)
        xhat = xv * rstd
        outs[1][0:1, :] += jnp.sum(dy * xhat, axis=0, keepdims=True)
        dxh = dy * gv
        outs[0][...] = dr + rstd * (dxh - xhat * jnp.mean(dxh * xhat, axis=-1, keepdims=True))

    return _rowwise(name, body, x.shape[0], 512, [('row', dh), ('row', x), ('full', g), ('row', dres)],
                    [('row', d, F32), ('acc', (8, d), F32)])


def _final_loss(x, g, target):
    d = x.shape[1]

    def body(i, nt, ins, outs, scr):
        xv, gv, tv = ins[0][...], ins[1][...], ins[2][...]
        rstd = lax.rsqrt(jnp.mean(xv * xv, axis=-1, keepdims=True) + NORM_EPS)
        xhat = xv * rstd
        err = xhat * gv - tv
        row_loss = 0.5 * jnp.mean(err * err, axis=-1, keepdims=True)
        outs[1][...] += jnp.sum(row_loss, axis=0, keepdims=True)
        dy = err * (1.0 / d)
        outs[2][0:1, :] += jnp.sum(dy * xhat, axis=0, keepdims=True)
        dxh = dy * gv
        outs[0][...] = rstd * (dxh - xhat * jnp.mean(dxh * xhat, axis=-1, keepdims=True))

    return _rowwise("final_loss", body, x.shape[0], 512, [('row', x), ('full', g), ('row', target)],
                    [('row', d, F32), ('acc', (8, LANES), F32), ('acc', (8, d), F32)])


def _conv_fwd(name, src, width, idx, w, b):
    def body(i, nt, ins, outs, scr):
        xv = ins[0][...]
        prev = jnp.where(i > 0, ins[1][...], 0.0)
        wv = ins[2][...]
        y = ins[3][...] + wv[3:4, :] * xv
        for s in (1, 2, 3):
            y = y + wv[3 - s:4 - s, :] * _shift_down(xv, s, prev)
        outs[0][...] = y

    return _rowwise(name, body, src.shape[0], 512,
                    [('col', src, width, idx), ('prev8', src, width, idx), ('full', w), ('full', b)],
                    [('row', width, F32)])[0]


def _conv_bwd(name, dpre, src, width, idx, w):
    def body(i, nt, ins, outs, scr):
        dy = ins[0][...]
        nxt = jnp.where(i < nt - 1, ins[1][...], 0.0)
        xv = ins[2][...]
        prev = jnp.where(i > 0, ins[3][...], 0.0)
        wv = ins[4][...]
        dx = wv[3:4, :] * dy
        outs[1][3:4, :] += jnp.sum(dy * xv, axis=0, keepdims=True)
        outs[1][4:5, :] += jnp.sum(dy, axis=0, keepdims=True)
        for s in (1, 2, 3):
            dx = dx + wv[3 - s:4 - s, :] * _shift_up(dy, s, nxt)
            outs[1][3 - s:4 - s, :] += jnp.sum(dy * _shift_down(xv, s, prev), axis=0, keepdims=True)
        outs[0][...] = dx

    return _rowwise(name, body, src.shape[0], 512,
                    [('row', dpre), ('next8', dpre, width, 0), ('col', src, width, idx),
                     ('prev8', src, width, idx), ('full', w)],
                    [('row', width, F32), ('acc', (8, width), F32)])


ATT_STEP_BLOCKS = 4


def _att_bias(not_first, dil, head):
    qi = _iota((ATT_BLOCK, 2 * ATT_BLOCK), 0)
    ki = _iota((ATT_BLOCK, 2 * ATT_BLOCK), 1)
    dist = ATT_BLOCK + qi - ki
    valid = (dist >= 0) & (dist <= ATT_BLOCK) & (not_first | (ki >= ATT_BLOCK))
    slope = 2.0 ** (-(head + 1))
    return jnp.where(valid, (-slope * dil) * dist.astype(F32), NEG)


def _head_mask():
    lane = _iota((ATT_BLOCK, LANES), 1)
    return lane < HEAD_DIM


def _att_q_specs(dil, nb, bq):
    big = (bq * ATT_BLOCK, ATT_W)
    one = (ATT_BLOCK, ATT_W)
    specs = [pl.BlockSpec(big, lambda r, n: (n, 3 * r)),
             pl.BlockSpec(big, lambda r, n: (n, 3 * r + 1)),
             pl.BlockSpec(one, lambda r, n: (jnp.maximum(n * bq - 1, 0), 3 * r + 1)),
             pl.BlockSpec(big, lambda r, n: (n, 3 * r + 2)),
             pl.BlockSpec(one, lambda r, n: (jnp.maximum(n * bq - 1, 0), 3 * r + 2))]
    wide = pl.BlockSpec(big, lambda r, n: (n, r))
    stat = pl.BlockSpec((bq * ATT_BLOCK, LANES), lambda r, n: (n, r))
    return specs, wide, stat


def _att_fwd(dil, qkv_v, stats):
    n_l = qkv_v.shape[0]
    nb = n_l // ATT_BLOCK
    bq = min(ATT_STEP_BLOCKS, nb)
    first = stats is None
    scale = HEAD_DIM ** -0.5

    def kern(*refs):
        n = pl.program_id(1)
        q_ref, kc_ref, kp_ref, vc_ref, vp_ref = refs[:5]
        if first:
            m_out, l_out, a_out = refs[5:]
        else:
            m_in, l_in, a_in, m_out, l_out, a_out = refs[5:]
        low = _head_mask()
        lane = _iota((ATT_BLOCK, LANES), 1)
        for b in range(bq):
            rows = slice(ATT_BLOCK * b, ATT_BLOCK * (b + 1))
            prev = slice(ATT_BLOCK * (b - 1), ATT_BLOCK * b)
            not_first = (n * bq + b) > 0
            m_acc = jnp.zeros((ATT_BLOCK, LANES), F32)
            l_acc = jnp.zeros((ATT_BLOCK, LANES), F32)
            for p in range(N_HEADS // 2):
                sl = slice(LANES * p, LANES * (p + 1))
                q2 = q_ref[rows, sl].astype(F32)
                k_prev = kp_ref[:, sl] if b == 0 else kc_ref[prev, sl]
                v_prev = vp_ref[:, sl] if b == 0 else vc_ref[prev, sl]
                k2 = jnp.concatenate([k_prev, kc_ref[rows, sl]], axis=0).astype(BF16)
                v2 = jnp.concatenate([v_prev, vc_ref[rows, sl]], axis=0).astype(BF16)
                res = []
                for e in range(2):
                    h = 2 * p + e
                    keep = low if e == 0 else jnp.logical_not(low)
                    qe = jnp.where(keep, q2, 0.0).astype(BF16)
                    s = _dot_nt(qe, k2) * scale + _att_bias(not_first, dil, h)
                    m_blk = jnp.max(s, axis=-1, keepdims=True)
                    if first:
                        m_new = m_blk
                        pe = jnp.exp(s - m_new)
                        l_new = jnp.sum(pe, axis=-1, keepdims=True)
                        a_new = _dot(pe.astype(BF16), v2)
                    else:
                        m_old = m_in[rows, h:h + 1]
                        l_old = l_in[rows, h:h + 1]
                        m_new = jnp.maximum(m_old, m_blk)
                        pe = jnp.exp(s - m_new)
                        alpha = jnp.exp(m_old - m_new)
                        l_new = alpha * l_old + jnp.sum(pe, axis=-1, keepdims=True)
                        a_new = alpha * a_in[rows, sl] + _dot(pe.astype(BF16), v2)
                    m_acc = jnp.where(lane == h, m_new, m_acc)
                    l_acc = jnp.where(lane == h, l_new, l_acc)
                    res.append(a_new)
                a_out[rows, sl] = jnp.where(low, res[0], res[1])
            m_out[rows, :] = m_acc
            l_out[rows, :] = l_acc

    specs, wide, stat = _att_q_specs(dil, nb, bq)
    args = [qkv_v] * 5
    if not first:
        specs = specs + [stat, stat, wide]
        args += list(stats)
    shp_s = jax.ShapeDtypeStruct((n_l, dil * LANES), F32)
    shp_a = jax.ShapeDtypeStruct((n_l, dil * ATT_W), F32)
    return pl.pallas_call(kern, name="att_fwd_d%d" % dil, grid=(dil, nb // bq), in_specs=specs,
                          out_specs=[stat, stat, wide], out_shape=[shp_s, shp_s, shp_a],
                          compiler_params=_params(2))(*args)


def _att_finish(m, l, acc):
    def body(i, nt, ins, outs, scr):
        lv = ins[1][...]
        real = _iota(lv.shape, 1) < N_HEADS
        outs[0][...] = ins[2][...] / _dot_hi(lv, _expand_mat())
        outs[1][...] = jnp.where(real, ins[0][...] + jnp.log(jnp.where(real, lv, 1.0)), 0.0)

    return _rowwise("att_finish", body, m.shape[0], 512, [('row', m), ('row', l), ('row', acc)],
                    [('row', ATT_W, F32), ('row', LANES, F32)])


def _att_delta(d_att, out):
    def body(i, nt, ins, outs, scr):
        outs[0][...] = _dot_hi(ins[0][...] * ins[1][...], _reduce_mat())

    return _rowwise("att_delta", body, out.shape[0], 512, [('row', d_att), ('row', out)],
                    [('row', LANES, F32)])[0]


def _att_bwd_dq(dil, qkv_v, do_v, lse_v, delta_v, dq_in):
    n_l = qkv_v.shape[0]
    nb = n_l // ATT_BLOCK
    bq = min(ATT_STEP_BLOCKS, nb)
    first = dq_in is None
    scale = HEAD_DIM ** -0.5

    def kern(*refs):
        n = pl.program_id(1)
        q_ref, kc_ref, kp_ref, vc_ref, vp_ref, do_ref, lse_ref, dl_ref = refs[:8]
        dq_out = refs[-1]
        low = _head_mask()
        for b in range(bq):
            rows = slice(ATT_BLOCK * b, ATT_BLOCK * (b + 1))
            prev = slice(ATT_BLOCK * (b - 1), ATT_BLOCK * b)
            not_first = (n * bq + b) > 0
            for p in range(N_HEADS // 2):
                sl = slice(LANES * p, LANES * (p + 1))
                q2 = q_ref[rows, sl].astype(F32)
                do2 = do_ref[rows, sl]
                k_prev = kp_ref[:, sl] if b == 0 else kc_ref[prev, sl]
                v_prev = vp_ref[:, sl] if b == 0 else vc_ref[prev, sl]
                k2 = jnp.concatenate([k_prev, kc_ref[rows, sl]], axis=0).astype(BF16)
                v2 = jnp.concatenate([v_prev, vc_ref[rows, sl]], axis=0).astype(BF16)
                res = []
                for e in range(2):
                    h = 2 * p + e
                    keep = low if e == 0 else jnp.logical_not(low)
                    qe = jnp.where(keep, q2, 0.0).astype(BF16)
                    doe = jnp.where(keep, do2, 0.0).astype(BF16)
                    s = _dot_nt(qe, k2) * scale + _att_bias(not_first, dil, h)
                    pe = jnp.exp(s - lse_ref[rows, h:h + 1])
                    dp = _dot_nt(doe, v2)
                    ds = pe * (dp - dl_ref[rows, h:h + 1])
                    res.append(_dot(ds.astype(BF16), k2) * scale)
                dq = jnp.where(low, res[0], res[1])
                if not first:
                    dq = dq + refs[8][rows, sl]
                dq_out[rows, sl] = dq

    specs, wide, stat = _att_q_specs(dil, nb, bq)
    specs = specs + [wide, stat, stat]
    args = [qkv_v] * 5 + [do_v, lse_v, delta_v]
    if not first:
        specs.append(wide)
        args.append(dq_in)
    return pl.pallas_call(kern, name="att_bwd_dq_d%d" % dil, grid=(dil, nb // bq), in_specs=specs,
                          out_specs=wide, out_shape=jax.ShapeDtypeStruct((n_l, dil * ATT_W), F32),
                          compiler_params=_params(2))(*args)


def _att_bwd_dkv(dil, qkv_v, do_v, lse_v, delta_v, dkv_in):
    n_l = qkv_v.shape[0]
    nb = n_l // ATT_BLOCK
    bq = min(ATT_STEP_BLOCKS, nb)
    steps = nb // bq
    first = dkv_in is None
    scale = HEAD_DIM ** -0.5

    def kern(*refs):
        j = pl.program_id(1)
        k_ref, v_ref, qc_ref, qn_ref, doc_ref, don_ref, lc_ref, ln_ref, dc_ref, dn_ref = refs[:10]
        dk_out, dv_out = refs[-2:]
        low = _head_mask()
        row = _iota((2 * ATT_BLOCK, ATT_BLOCK), 0)
        key = _iota((2 * ATT_BLOCK, ATT_BLOCK), 1)
        dist = row - key
        low2 = _iota((2 * ATT_BLOCK, LANES), 1) < HEAD_DIM
        for b in range(bq):
            rows = slice(ATT_BLOCK * b, ATT_BLOCK * (b + 1))
            nrows = slice(ATT_BLOCK * (b + 1), ATT_BLOCK * (b + 2))
            inner = b < bq - 1
            has_next = True if inner else (j < steps - 1)
            valid = (dist >= 0) & (dist <= ATT_BLOCK) & ((row < ATT_BLOCK) | has_next)
            lse2 = jnp.concatenate([lc_ref[rows, :], lc_ref[nrows, :] if inner else ln_ref[...]], axis=0)
            dl2 = jnp.concatenate([dc_ref[rows, :], dc_ref[nrows, :] if inner else dn_ref[...]], axis=0)
            for p in range(N_HEADS // 2):
                sl = slice(LANES * p, LANES * (p + 1))
                k2 = k_ref[rows, sl].astype(F32)
                v2 = v_ref[rows, sl].astype(F32)
                q2 = jnp.concatenate([qc_ref[rows, sl], qc_ref[nrows, sl] if inner else qn_ref[:, sl]],
                                     axis=0).astype(F32)
                do2 = jnp.concatenate([doc_ref[rows, sl], doc_ref[nrows, sl] if inner else don_ref[:, sl]], axis=0)
                q2b = q2.astype(BF16)
                do2b = do2.astype(BF16)
                dks, dvs = [], []
                for e in range(2):
                    h = 2 * p + e
                    keep = low if e == 0 else jnp.logical_not(low)
                    keep2 = low2 if e == 0 else jnp.logical_not(low2)
                    slope = 2.0 ** (-(h + 1))
                    bias = jnp.where(valid, (-slope * dil) * dist.astype(F32), NEG)
                    ke = jnp.where(keep, k2, 0.0).astype(BF16)
                    ve = jnp.where(keep, v2, 0.0).astype(BF16)
                    s = _dot_nt(q2b, ke) * scale + bias
                    pe = jnp.exp(s - lse2[:, h:h + 1])
                    dp = _dot_nt(do2b, ve)
                    ds = pe * (dp - dl2[:, h:h + 1])
                    dvs.append(_dot_tn(pe.astype(BF16), jnp.where(keep2, do2, 0.0).astype(BF16)))
                    dks.append(_dot_tn(ds.astype(BF16), jnp.where(keep2, q2, 0.0).astype(BF16)) * scale)
                dk = jnp.where(low, dks[0], dks[1])
                dv = jnp.where(low, dvs[0], dvs[1])
                if not first:
                    dk = dk + refs[10][rows, sl]
                    dv = dv + refs[11][rows, sl]
                dk_out[rows, sl] = dk
                dv_out[rows, sl] = dv

    big = (bq * ATT_BLOCK, ATT_W)
    one = (ATT_BLOCK, ATT_W)

    def nxt_idx(j):
        return jnp.minimum((j + 1) * bq, nb - 1)

    cur = pl.BlockSpec(big, lambda r, j: (j, r))
    nxt = pl.BlockSpec(one, lambda r, j: (nxt_idx(j), r))
    cur_s = pl.BlockSpec((bq * ATT_BLOCK, LANES), lambda r, j: (j, r))
    nxt_s = pl.BlockSpec((ATT_BLOCK, LANES), lambda r, j: (nxt_idx(j), r))
    in_specs = [pl.BlockSpec(big, lambda r, j: (j, 3 * r + 1)),
                pl.BlockSpec(big, lambda r, j: (j, 3 * r + 2)),
                pl.BlockSpec(big, lambda r, j: (j, 3 * r)),
                pl.BlockSpec(one, lambda r, j: (nxt_idx(j), 3 * r)),
                cur, nxt, cur_s, nxt_s, cur_s, nxt_s]
    args = [qkv_v] * 4 + [do_v, do_v, lse_v, lse_v, delta_v, delta_v]
    if not first:
        in_specs += [cur, cur]
        args += list(dkv_in)
    shp = jax.ShapeDtypeStruct((n_l, dil * ATT_W), F32)
    return pl.pallas_call(kern, name="att_bwd_dkv_d%d" % dil, grid=(dil, steps), in_specs=in_specs,
                          out_specs=[cur, cur], out_shape=[shp, shp], compiler_params=_params(2))(*args)


def _attention_fwd(qkv):
    t = qkv.shape[0]
    stats = None
    for dil in ATT_DILATIONS:
        if stats is not None:
            stats = [s.reshape(t // dil, -1) for s in stats]
        stats = _att_fwd(dil, qkv.reshape(t // dil, dil * D_MIX), stats)
    m, l, acc = [s.reshape(t, -1) for s in stats]
    return _att_finish(m, l, acc)


def _attention_bwd(qkv, d_att, out, lse):
    t = qkv.shape[0]
    delta = _att_delta(d_att, out)
    dq, dkv = None, None
    for dil in ATT_DILATIONS:
        view = lambda a: a.reshape(t // dil, -1)
        qkv_v = view(qkv)
        do_v, lse_v, dl_v = view(d_att), view(lse), view(delta)
        dq = _att_bwd_dq(dil, qkv_v, do_v, lse_v, dl_v, None if dq is None else view(dq))
        dkv = _att_bwd_dkv(dil, qkv_v, do_v, lse_v, dl_v, None if dkv is None else [view(a) for a in dkv])
    return dq.reshape(t, ATT_W), dkv[0].reshape(t, ATT_W), dkv[1].reshape(t, ATT_W)


def _ssd_chunk_common(pre, dtraw, bias_row, alog_row):
    q = SSD_CHUNK
    act = _silu(pre)
    lane = _iota((q, LANES), 1)
    dt = jnp.where(lane < N_HEADS, _softplus(dtraw + bias_row), 0.0)
    a_row = -jnp.exp(alog_row)
    tril = jnp.where(_iota((q, q), 0) >= _iota((q, q), 1), 1.0, 0.0).astype(F32)
    cs = _dot_hi(tril, dt * a_row)
    cs_last = cs[q - 1:q, :]
    return act, dt, a_row, tril, cs, cs_last


def _ssd_lmat(cs, cs_t, h):
    q = SSD_CHUNK
    seg = cs[:, h:h + 1] - cs_t[h:h + 1, :]
    causal = _iota((q, q), 0) >= _iota((q, q), 1)
    return jnp.exp(jnp.where(causal, seg, NEG))


def _ssd_gate_norm(y, z, norm_w):
    sz = _silu(z)
    yg = y * sz
    half = SSD_W // 2
    outs, rss = [], []
    for g in range(2):
        part = yg[:, half * g:half * (g + 1)]
        rs = lax.rsqrt(jnp.mean(part * part, axis=-1, keepdims=True) + SSD_NORM_EPS)
        outs.append(part * rs)
        rss.append(rs)
    yn = jnp.concatenate(outs, axis=1)
    return sz, yn, rss, yn * norm_w


def _ssd_fwd(pre, rest, dt_bias, a_log, d_skip, norm_w):
    t = pre.shape[0]
    q = SSD_CHUNK

    def body(c, nc, ins, outs, scr):
        pre_ref, z_ref, dtr_ref, bias_ref, alog_ref, dsk_ref, nw_ref = ins
        out_ref, y_ref, sp_ref = outs
        s_ref = scr[0]

        @pl.when(c == 0)
        def _():
            s_ref[...] = jnp.zeros(s_ref.shape, F32)

        act, dt, a_row, tril, cs, cs_last = _ssd_chunk_common(pre_ref[...], dtr_ref[...], bias_ref[...],
                                                               alog_ref[...])
        x = act[:, :SSD_W]
        cs_t = cs.T
        e_col = jnp.exp(cs)
        w = jnp.exp(cs_last - cs) * dt
        expand = _expand_mat()
        w_x = _dot_hi(w, expand)
        dt_x = _dot_hi(dt, expand)
        e_x = _dot_hi(e_col, expand)
        d_x = _dot_hi(dsk_ref[...], expand)
        cd_x = _dot_hi(jnp.exp(cs_last), expand)
        s_prev = s_ref[...]
        sp_ref[...] = s_prev
        xw = (x * w_x).astype(BF16)
        xd = (x * dt_x).astype(BF16)
        low = _head_mask()
        y_parts, s_parts = [], []
        for g in range(2):
            bg = act[:, SSD_W + SSD_STATE * g:SSD_W + SSD_STATE * (g + 1)].astype(BF16)
            cg = act[:, SSD_W + 2 * SSD_STATE + SSD_STATE * g:SSD_W + 2 * SSD_STATE + SSD_STATE * (g + 1)].astype(BF16)
            gsl = slice(256 * g, 256 * (g + 1))
            gmat = _dot_nt(cg, bg)
            s_parts.append(_dot_tn(bg, xw[:, gsl]))
            y0 = _dot(cg, s_prev[:, gsl].astype(BF16))
            for pp in range(2):
                pair = 2 * g + pp
                psl = slice(LANES * pair, LANES * (pair + 1))
                yd = []
                for e in range(2):
                    h = 2 * pair + e
                    mh = (gmat * _ssd_lmat(cs, cs_t, h)).astype(BF16)
                    yd.append(_dot(mh, xd[:, psl]))
                y_parts.append(jnp.where(low, yd[0], yd[1]) + e_x[:, psl] * y0[:, LANES * pp:LANES * (pp + 1)])
        y = jnp.concatenate(y_parts, axis=1) + d_x * x
        s_ref[...] = cd_x * s_prev + jnp.concatenate(s_parts, axis=1)
        y_ref[...] = y
        out_ref[...] = _ssd_gate_norm(y, z_ref[...], nw_ref[...])[3]

    return _rowwise("ssd_fwd", body, t, q,
                    [('row', pre), ('col', rest, SSD_W, 2), ('col', rest, LANES, 20), ('full', dt_bias),
                     ('full', a_log), ('full', d_skip), ('full', norm_w)],
                    [('row', SSD_W, F32), ('row', SSD_W, F32), ('row', SSD_W, F32)],
                    scratch=[pltpu.VMEM((SSD_STATE, SSD_W), F32)])


def _ssd_bwd(pre, rest, y, s_prev_all, d_mix, dt_bias, a_log, d_skip, norm_w):
    t = pre.shape[0]
    q = SSD_CHUNK

    def body(c, nc, ins, outs, scr):
        pre_ref, z_ref, dtr_ref, y_ref, sp_ref, do_ref, bias_ref, alog_ref, dsk_ref, nw_ref = ins
        dpre_ref, dz_ref, ddt_ref, a128_ref, a512_ref = outs
        ds_ref = scr[0]

        @pl.when(c == nc - 1)
        def _():
            ds_ref[...] = jnp.zeros(ds_ref.shape, F32)

        pre_v = pre_ref[...]
        dtr = dtr_ref[...]
        act, dt, a_row, tril, cs, cs_last = _ssd_chunk_common(pre_v, dtr, bias_ref[...], alog_ref[...])
        x = act[:, :SSD_W]
        cs_t = cs.T
        e_col = jnp.exp(cs)
        decay_end = jnp.exp(cs_last - cs)
        w = decay_end * dt
        cd = jnp.exp(cs_last)
        expand = _expand_mat()
        reduce = _reduce_mat()
        w_x = _dot_hi(w, expand)
        dt_x = _dot_hi(dt, expand)
        e_x = _dot_hi(e_col, expand)
        d_x = _dot_hi(dsk_ref[...], expand)
        cd_x = _dot_hi(cd, expand)
        s_prev = sp_ref[...]
        d_s = ds_ref[...]
        xw = (x * w_x).astype(BF16)
        xd = (x * dt_x).astype(BF16)
        low = _head_mask()
        lane = _iota((q, LANES), 1)
        sub = _iota((q, LANES), 0)

        yv, zv, nw = y_ref[...], z_ref[...], nw_ref[...]
        d_out = do_ref[...]
        sz, yn, rss, _ = _ssd_gate_norm(yv, zv, nw)
        a512_ref[0:1, :] += jnp.sum(d_out * yn, axis=0, keepdims=True)
        dyn = d_out * nw
        half = SSD_W // 2
        dyg_parts = []
        for g in range(2):
            hs = slice(half * g, half * (g + 1))
            dyg_parts.append(rss[g] * (dyn[:, hs] - yn[:, hs] * jnp.mean(dyn[:, hs] * yn[:, hs], axis=-1,
                                                                          keepdims=True)))
        dyg = jnp.concatenate(dyg_parts, axis=1)
        dy = dyg * sz
        dz_ref[...] = dyg * yv * _dsilu(zv)

        a128_ref[2:3, :] += _dot_hi(jnp.sum(dy * x, axis=0, keepdims=True), reduce)
        dx = d_x * dy

        dy0 = e_x * dy
        dyb = dy.astype(BF16)
        dcs = jnp.zeros((q, LANES), F32)
        dcs_rows = jnp.zeros((q, LANES), F32)
        ddt = jnp.zeros((q, LANES), F32)
        ds_prev_parts, z_parts, db_parts, dc_parts, dxd_parts, y0_parts = [], [], [], [], [], []
        for g in range(2):
            bg = act[:, SSD_W + SSD_STATE * g:SSD_W + SSD_STATE * (g + 1)].astype(BF16)
            cg = act[:, SSD_W + 2 * SSD_STATE + SSD_STATE * g:SSD_W + 2 * SSD_STATE + SSD_STATE * (g + 1)].astype(BF16)
            gsl = slice(256 * g, 256 * (g + 1))
            spg = s_prev[:, gsl].astype(BF16)
            dsg = d_s[:, gsl].astype(BF16)
            dy0g = dy0[:, gsl].astype(BF16)
            gmat = _dot_nt(cg, bg)
            y0_parts.append(_dot(cg, spg))
            dc_g = _dot_nt(dy0g, spg)
            ds_prev_parts.append(_dot_tn(cg, dy0g))
            z_parts.append(_dot(bg, dsg))
            db_g = _dot_nt(xw[:, gsl], dsg)
            dg_acc = jnp.zeros((q, q), F32)
            for pp in range(2):
                pair = 2 * g + pp
                psl = slice(LANES * pair, LANES * (pair + 1))
                dxd_e = []
                for e in range(2):
                    h = 2 * pair + e
                    keep = low if e == 0 else jnp.logical_not(low)
                    lm = _ssd_lmat(cs, cs_t, h)
                    mh = gmat * lm
                    dm = _dot_nt(jnp.where(keep, dy[:, psl], 0.0).astype(BF16), xd[:, psl])
                    dxd_e.append(_dot_tn(mh.astype(BF16), dyb[:, psl]))
                    wm = dm * mh
                    dcs = dcs + jnp.where(lane == h, jnp.sum(wm, axis=1, keepdims=True), 0.0)
                    dcs_rows = dcs_rows - jnp.where(sub == h, jnp.sum(wm, axis=0, keepdims=True), 0.0)
                    dg_acc = dg_acc + dm * lm
                dxd_parts.append(jnp.where(low, dxd_e[0], dxd_e[1]))
            dgb = dg_acc.astype(BF16)
            dc_parts.append(dc_g + _dot(dgb, bg))
            db_parts.append(db_g + _dot_tn(dgb, cg))
        y0 = jnp.concatenate(y0_parts, axis=1)
        zmat = jnp.concatenate(z_parts, axis=1)
        dxd = jnp.concatenate(dxd_parts, axis=1)
        ds_prev = jnp.concatenate(ds_prev_parts, axis=1) + cd_x * d_s
        ds_ref[...] = ds_prev

        dcs = dcs + _dot_hi(dy * y0, reduce) * e_col
        dcd = _dot_hi(jnp.sum(d_s * s_prev, axis=0, keepdims=True), reduce)
        dlast = dcd * cd
        dx = dx + w_x * zmat + dxd * dt_x
        dw = _dot_hi(zmat * x, reduce)
        ddt = ddt + dw * decay_end + _dot_hi(dxd * x, reduce)
        dwl = dw * w
        dcs = dcs - dwl
        dlast = dlast + jnp.sum(dwl, axis=0, keepdims=True)
        dcs = dcs + dcs_rows.T + jnp.where(sub == q - 1, dlast, 0.0)
        dda = _dot_hi(tril.T, dcs)
        ddt = ddt + dda * a_row
        a128_ref[1:2, :] += jnp.sum(dda * dt, axis=0, keepdims=True) * a_row
        draw = jnp.where(lane < N_HEADS, ddt * _sigmoid(dtr + bias_ref[...]), 0.0)
        a128_ref[0:1, :] += jnp.sum(draw, axis=0, keepdims=True)
        ddt_ref[...] = draw
        dact = jnp.concatenate([dx] + db_parts + dc_parts, axis=1)
        dpre_ref[...] = dact * _dsilu(pre_v)

    return _rowwise("ssd_bwd", body, t, q,
                    [('row', pre), ('col', rest, SSD_W, 2), ('col', rest, LANES, 20), ('row', y),
                     ('row', s_prev_all), ('col', d_mix, SSD_W, 0), ('full', dt_bias), ('full', a_log),
                     ('full', d_skip), ('full', norm_w)],
                    [('row', SSD_CONV, F32), ('row', SSD_W, F32), ('row', LANES, F32),
                     ('acc', (8, LANES), F32), ('acc', (8, SSD_W), F32)],
                    scratch=[pltpu.VMEM((SSD_STATE, SSD_W), F32)], reverse=True)


LRU_TM = 256


def _lru_gates(xc, wa, ba, wx, bx, lam):
    xb = xc.astype(BF16)
    r = _sigmoid(_dot(xb, wa) + ba)
    i = _sigmoid(_dot(xb, wx) + bx)
    sp = _softplus(-lam)
    a = jnp.exp(-LRU_C * r * sp)
    mult = jnp.sqrt(1.0 - a * a)
    return r, i, sp, a, mult


def _lru_fwd(xc, rest, wa, ba, wx, bx, lam):
    def body(i, nt, ins, outs, scr):
        xc_ref, g_ref, wa_ref, ba_ref, wx_ref, bx_ref, lam_ref = ins
        carry = scr[0]

        @pl.when(i == 0)
        def _():
            carry[...] = jnp.zeros(carry.shape, F32)

        xv = xc_ref[...]
        r, ig, sp, a, mult = _lru_gates(xv, wa_ref[...], ba_ref[...], wx_ref[...], bx_ref[...], lam_ref[...])
        u = mult * (ig * xv)
        row = _iota(a.shape, 0)
        s = 1
        while s < LRU_TM:
            a_sh = jnp.where(row >= s, pltpu.roll(a, s, 0), 1.0)
            u_sh = jnp.where(row >= s, pltpu.roll(u, s, 0), 0.0)
            u = a * u_sh + u
            a = a * a_sh
            s *= 2
        h = u + a * carry[0:1, :]
        carry[0:1, :] = h[LRU_TM - 1:LRU_TM, :]
        outs[1][...] = h
        outs[0][...] = h * _gelu(g_ref[...])

    return _rowwise("lru_fwd", body, xc.shape[0], LRU_TM,
                    [('row', xc), ('col', rest, LRU_W, 3), ('full', wa), ('full', ba), ('full', wx),
                     ('full', bx), ('full', lam)],
                    [('row', LRU_W, F32), ('row', LRU_W, F32)], scratch=[pltpu.VMEM((8, LRU_W), F32)])


def _lru_bwd(xc, rest, h, d_mix, wa, ba, wx, bx, lam, wa_t, wx_t):
    def body(i, nt, ins, outs, scr):
        xc_ref, g_ref, h_ref, hp_ref, do_ref, wa_ref, ba_ref, wx_ref, bx_ref, lam_ref, wat_ref, wxt_ref = ins
        dxc_ref, dg_ref, dza_ref, dzi_ref, acc_ref = outs
        carry = scr[0]

        @pl.when(i == nt - 1)
        def _():
            carry[...] = jnp.zeros(carry.shape, F32)

        xv, gv, hv, d_out = xc_ref[...], g_ref[...], h_ref[...], do_ref[...]
        r, ig, sp, a, mult = _lru_gates(xv, wa_ref[...], ba_ref[...], wx_ref[...], bx_ref[...], lam_ref[...])
        dg_ref[...] = d_out * hv * _dgelu(gv)
        gsum = d_out * _gelu(gv)
        row = _iota(a.shape, 0)
        b = jnp.where(row < LRU_TM - 1, pltpu.roll(a, LRU_TM - 1, 0), 1.0)
        s = 1
        while s < LRU_TM:
            keep = row < LRU_TM - s
            b_sh = jnp.where(keep, pltpu.roll(b, LRU_TM - s, 0), 1.0)
            g_sh = jnp.where(keep, pltpu.roll(gsum, LRU_TM - s, 0), 0.0)
            gsum = gsum + b * g_sh
            b = b * b_sh
            s *= 2
        dh = gsum + b * carry[0:1, :]
        carry[0:1, :] = a[0:1, :] * dh[0:1, :]
        h_prev = _shift_down(hv, 1, jnp.where(i > 0, hp_ref[...], 0.0))
        du = dh
        dmult = du * ig * xv
        di = du * mult * xv
        dxc = du * mult * ig
        da = dh * h_prev - dmult * a / mult
        dlog = da * a
        dr = dlog * (-LRU_C) * sp
        acc_ref[2:3, :] += jnp.sum(dlog * (-LRU_C) * r, axis=0, keepdims=True)
        dza = dr * r * (1.0 - r)
        dzi = di * ig * (1.0 - ig)
        acc_ref[0:1, :] += jnp.sum(dza, axis=0, keepdims=True)
        acc_ref[1:2, :] += jnp.sum(dzi, axis=0, keepdims=True)
        dzab, dzib = dza.astype(BF16), dzi.astype(BF16)
        dza_ref[...] = dzab
        dzi_ref[...] = dzib
        dxc_ref[...] = dxc + _dot(dzab, wat_ref[...]) + _dot(dzib, wxt_ref[...])

    return _rowwise("lru_bwd", body, xc.shape[0], LRU_TM,
                    [('row', xc), ('col', rest, LRU_W, 3), ('row', h), ('prev8', h, LRU_W, 0),
                     ('col', d_mix, LRU_W, 1), ('full', wa), ('full', ba), ('full', wx), ('full', bx),
                     ('full', lam), ('full', wa_t), ('full', wx_t)],
                    [('row', LRU_W, F32), ('row', LRU_W, F32), ('row', LRU_W, BF16), ('row', LRU_W, BF16),
                     ('acc', (8, LRU_W), F32)],
                    scratch=[pltpu.VMEM((8, LRU_W), F32)], reverse=True)


FFN_TM = 512
FFN_TN = 1536


def _ffn_up(name, h2, w_gu):
    t, k = h2.shape
    nh = D_FFP // FFN_TN

    def kern(a_ref, wg_ref, wu_ref, g_ref, u_ref, act_ref):
        a = a_ref[...].astype(BF16)
        gv = _dot(a, wg_ref[...])
        uv = _dot(a, wu_ref[...])
        g_ref[...] = gv
        u_ref[...] = uv
        act_ref[...] = (_silu(gv) * uv).astype(BF16)

    tile = pl.BlockSpec((FFN_TM, FFN_TN), lambda i, j: (i, j))
    return pl.pallas_call(
        kern, name=name, grid=(t // FFN_TM, nh),
        in_specs=[pl.BlockSpec((FFN_TM, k), lambda i, j: (i, 0)),
                  pl.BlockSpec((k, FFN_TN), lambda i, j: (0, j)),
                  pl.BlockSpec((k, FFN_TN), lambda i, j: (0, nh + j))],
        out_specs=[tile, tile, tile],
        out_shape=[jax.ShapeDtypeStruct((t, D_FFP), F32), jax.ShapeDtypeStruct((t, D_FFP), F32),
                   jax.ShapeDtypeStruct((t, D_FFP), BF16)],
        compiler_params=_params(2))(h2, w_gu, w_gu)


def _ffn_down_bwd(name, dx, w_down_t, gate, up):
    t, k = dx.shape

    def kern(dx_ref, w_ref, g_ref, u_ref, dg_ref, du_ref):
        da = _dot(dx_ref[...].astype(BF16), w_ref[...])
        gv, uv = g_ref[...], u_ref[...]
        dg_ref[...] = (da * uv * _dsilu(gv)).astype(BF16)
        du_ref[...] = (da * _silu(gv)).astype(BF16)

    tile = pl.BlockSpec((FFN_TM, FFN_TN), lambda i, j: (i, j))
    shp = jax.ShapeDtypeStruct((t, D_FFP), BF16)
    return pl.pallas_call(
        kern, name=name, grid=(t // FFN_TM, D_FFP // FFN_TN),
        in_specs=[pl.BlockSpec((FFN_TM, k), lambda i, j: (i, 0)),
                  pl.BlockSpec((k, FFN_TN), lambda i, j: (0, j)), tile, tile],
        out_specs=[tile, tile], out_shape=[shp, shp], compiler_params=_params(2))(dx, w_down_t, gate, up)


def _layer_fwd(x, w, l):
    tag = "_l%d" % l
    h = _rmsnorm_fwd("norm_mix" + tag, x, w['norm_mix'][l])
    qkv = _mm("proj_qkv" + tag, [h], w['w_qkv'][l], tn=768, out_dtype=BF16)
    rest = _mm("proj_rest" + tag, [h], w['w_rest'][l], tn=896)
    att, lse = _attention_fwd(qkv)
    pre = _conv_fwd("ssd_conv" + tag, rest, SSD_CONV, 0, w['ssd_conv_w'][l], w['ssd_conv_b'][l])
    ssd, y, s_prev = _ssd_fwd(pre, rest, w['ssd_dt_bias'][l], w['ssd_a_log'][l], w['ssd_d'][l], w['ssd_norm'][l])
    xc = _conv_fwd("lru_conv" + tag, rest, LRU_W, 4, w['lru_conv_w'][l], w['lru_conv_b'][l])
    lru, hl = _lru_fwd(xc, rest, w['lru_wa'][l], w['lru_ba'][l], w['lru_wx'][l], w['lru_bx'][l], w['lru_lambda'][l])
    x_mid = _mm("proj_out" + tag, [att, ssd, lru], w['w_out'][l], res=x, tn=512)
    h2 = _rmsnorm_fwd("norm_ffn" + tag, x_mid, w['norm_ffn'][l])
    gate, up, act = _ffn_up("proj_gu" + tag, h2, w['w_gu'][l])
    x_next = _mm("proj_down" + tag, [act], w['w_down'][l], res=x_mid, tn=512)
    saved = dict(x=x, h=h, qkv=qkv, rest=rest, att=att, lse=lse, pre=pre, ssd=ssd, y=y, s_prev=s_prev, xc=xc,
                 lru=lru, hl=hl, x_mid=x_mid, h2=h2, gate=gate, up=up, act=act)
    return x_next, saved


def _layer_bwd(dx_next, sv, w, l):
    tag = "_l%d_b" % l
    t = dx_next.shape[0]
    g = {}
    dgate, dup = _ffn_down_bwd("d_act" + tag, dx_next, w['w_down_t'][l], sv['gate'], sv['up'])
    g['w_down'] = _mm_tn("dw_down" + tag, sv['act'], dx_next, tk=1536)
    dh2 = _mm("d_h2" + tag, [dgate, dup], w['w_gu_t'][l], tn=512, tm=256)
    g['w_gate'] = _mm_tn("dw_gate" + tag, sv['h2'], dgate, tn=1536)
    g['w_up'] = _mm_tn("dw_up" + tag, sv['h2'], dup, tn=1536)
    dx_mid, acc = _rmsnorm_bwd("norm_ffn" + tag, dh2, sv['x_mid'], w['norm_ffn'][l], dx_next)
    g['norm_ffn'] = acc[0]
    d_att = _mm("d_att" + tag, [dx_mid], w['w_out_t'][l][:, :ATT_W], tn=512)
    d_mix = _mm("d_mix" + tag, [dx_mid], w['w_out_t'][l][:, ATT_W:], tn=512)
    g['w_out'] = jnp.concatenate([_mm_tn("dw_out%d" % k + tag, a, dx_mid)
                                  for k, a in enumerate((sv['att'], sv['ssd'], sv['lru']))], axis=0)
    dxc, dgl, dza, dzi, acc = _lru_bwd(sv['xc'], sv['rest'], sv['hl'], d_mix, w['lru_wa'][l], w['lru_ba'][l],
                                       w['lru_wx'][l], w['lru_bx'][l], w['lru_lambda'][l],
                                       w['lru_wa_t'][l], w['lru_wx_t'][l])
    g['lru_ba'], g['lru_bx'] = acc[0], acc[1]
    g['lru_lambda'] = acc[2] * (-_sigmoid(-w['lru_lambda'][l][0]))
    g['lru_wa'] = _diag_blocks(_mm_tn("dw_lru_a" + tag, sv['xc'], dza))
    g['lru_wx'] = _diag_blocks(_mm_tn("dw_lru_x" + tag, sv['xc'], dzi))
    dxl, acc = _conv_bwd("lru_conv" + tag, dxc, sv['rest'], LRU_W, 4, w['lru_conv_w'][l])
    g['lru_conv_w'], g['lru_conv_b'] = acc[:4], acc[4]
    dpre, dz, ddt, a128, a512 = _ssd_bwd(sv['pre'], sv['rest'], sv['y'], sv['s_prev'], d_mix, w['ssd_dt_bias'][l],
                                         w['ssd_a_log'][l], w['ssd_d'][l], w['ssd_norm'][l])
    g['ssd_dt_bias'], g['ssd_a_log'], g['ssd_d'] = a128[0, :N_HEADS], a128[1, :N_HEADS], a128[2, :N_HEADS]
    g['ssd_norm'] = a512[0]
    dxbc, acc = _conv_bwd("ssd_conv" + tag, dpre, sv['rest'], SSD_CONV, 0, w['ssd_conv_w'][l])
    g['ssd_conv_w'], g['ssd_conv_b'] = acc[:4], acc[4]
    dq, dk, dv = _attention_bwd(sv['qkv'], d_att, sv['att'], sv['lse'])
    pieces = [dq, dk, dv, dxbc, dz, dgl, dxl, ddt]
    dh = _mm("d_h" + tag, pieces, w['w_in_t'][l], tn=512)
    dws = [_mm_tn("dw_in%d" % k + tag, sv['h'], p) for k, p in enumerate(pieces)]
    g['w_in'] = jnp.concatenate([dws[0], dws[1], dws[2], dws[4], dws[3], _dt_tile_place(dws[7]), dws[5], dws[6]],
                                axis=1)
    dx, acc = _rmsnorm_bwd("norm_mix" + tag, dh, sv['x'], w['norm_mix'][l], dx_mid)
    g['norm_mix'] = acc[0]
    return dx, g


def _diag_blocks(m):
    return jnp.stack([m[64 * n:64 * (n + 1), 64 * n:64 * (n + 1)] for n in range(8)])


def _block_diag(w):
    eye = jnp.eye(8, dtype=w.dtype)
    return (w[:, :, None, :] * eye[:, None, :, None]).reshape(512, 512)


_ANY = pl.BlockSpec(memory_space=pl.ANY)
_MESH = pl.DeviceIdType.MESH


def _all_gather(name, xs):
    n = len(xs)

    def body(*refs):
        x_refs, out_refs = refs[:n], refs[n:2 * n]
        send_sems, recv_sems, local_sems = refs[2 * n:]
        x_, y_, c_ = lax.axis_index("x"), lax.axis_index("y"), lax.axis_index("c")
        me, sibling = (x_, y_, c_), (x_, y_, 1 - c_)
        chips = [(1 - x_, y_), (x_, 1 - y_), (1 - x_, 1 - y_)]

        def slot(a, px, py, pc):
            return out_refs[a].at[4 * px + 2 * py + pc]

        def copy(a, k, block, to, src=None):
            return pltpu.make_async_remote_copy(
                src_ref=slot(a, *block) if src is None else src, dst_ref=slot(a, *block),
                send_sem=send_sems.at[a, k], recv_sem=recv_sems.at[a, k], device_id=to, device_id_type=_MESH)

        mine = [pltpu.make_async_copy(x_refs[a], slot(a, *me), local_sems.at[a]) for a in range(n)]
        for cp in mine:
            cp.start()
        first = []
        for a in range(n):
            first.append(copy(a, 0, me, sibling, src=x_refs[a]))
            first += [copy(a, 1 + j, me, (*chip, c_), src=x_refs[a]) for j, chip in enumerate(chips)]
        for cp in first:
            cp.start()
        passed = []
        for j, chip in enumerate(chips):
            for a in range(n):
                copy(a, 1 + j, (*chip, c_), me).wait_recv()
                fwd = copy(a, 4 + j, (*chip, c_), sibling)
                fwd.start()
                passed.append(fwd)
        for a in range(n):
            copy(a, 0, sibling, me).wait_recv()
            for j, chip in enumerate(chips):
                copy(a, 4 + j, (*chip, 1 - c_), me).wait_recv()
        for cp in first + passed:
            cp.wait_send()
        for cp in mine:
            cp.wait()

    return pl.pallas_call(
        body, name=name, out_shape=[jax.ShapeDtypeStruct((N_DEV,) + x.shape, x.dtype) for x in xs],
        in_specs=[_ANY] * n, out_specs=[_ANY] * n,
        scratch_shapes=[pltpu.SemaphoreType.DMA((n, 7)), pltpu.SemaphoreType.DMA((n, 7)),
                        pltpu.SemaphoreType.DMA((n,))],
    )(*xs)


def _all_to_all(name, xs):
    n = len(xs)

    def body(*refs):
        x_refs, out_refs = refs[:n], refs[n:2 * n]
        send_sems, recv_sems, local_sems = refs[2 * n:]
        x_, y_, c_ = lax.axis_index("x"), lax.axis_index("y"), lax.axis_index("c")
        me = 4 * x_ + 2 * y_ + c_

        def peer(k):
            return ((1 - x_) if k & 4 else x_, (1 - y_) if k & 2 else y_, (1 - c_) if k & 1 else c_)

        def copy(a, k):
            px, py, pc = peer(k)
            return pltpu.make_async_remote_copy(
                src_ref=x_refs[a].at[4 * px + 2 * py + pc], dst_ref=out_refs[a].at[me],
                send_sem=send_sems.at[a, k - 1], recv_sem=recv_sems.at[a, k - 1],
                device_id=(px, py, pc), device_id_type=_MESH)

        def arrival(a, k):
            px, py, pc = peer(k)
            return pltpu.make_async_remote_copy(
                src_ref=x_refs[a].at[me], dst_ref=out_refs[a].at[4 * px + 2 * py + pc],
                send_sem=send_sems.at[a, k - 1], recv_sem=recv_sems.at[a, k - 1],
                device_id=(px, py, pc), device_id_type=_MESH)

        mine = [pltpu.make_async_copy(x_refs[a].at[me], out_refs[a].at[me], local_sems.at[a]) for a in range(n)]
        for cp in mine:
            cp.start()
        copies = [copy(a, k) for a in range(n) for k in range(1, N_DEV)]
        for cp in copies:
            cp.start()
        for a in range(n):
            for k in range(1, N_DEV):
                arrival(a, k).wait_recv()
        for cp in copies:
            cp.wait_send()
        for cp in mine:
            cp.wait()

    return pl.pallas_call(
        body, name=name, out_shape=[jax.ShapeDtypeStruct(x.shape, x.dtype) for x in xs],
        in_specs=[_ANY] * n, out_specs=[_ANY] * n,
        scratch_shapes=[pltpu.SemaphoreType.DMA((n, 7)), pltpu.SemaphoreType.DMA((n, 7)),
                        pltpu.SemaphoreType.DMA((n,))],
    )(*xs)


def _window_offset():
    me = 4 * lax.axis_index("x") + 2 * lax.axis_index("y") + lax.axis_index("c")
    return jnp.where(me < 6, me, me + 120)


def _place_w_in(shard):
    def body(i, nt, ins, outs, scr):
        outs[0][...] = pltpu.roll(ins[0][...], _window_offset(), 1).astype(BF16)

    return _rowwise("place_w_in", body, shard.shape[0], 256, [('row', shard)], [('row', IN_WIN, BF16)])[0]


def _adamw(name, w, m, v, g, tr, from_window=False):
    s_parts, r, c = g.shape
    assert r % tr == 0

    def kern(w_ref, m_ref, v_ref, g_ref, go_ref, d_ref, mo_ref, vo_ref):
        gs = g_ref[0].astype(F32)
        for s in range(1, s_parts):
            gs = gs + g_ref[s].astype(F32)
        if from_window:
            gs = pltpu.roll(gs, c - _window_offset(), 1)
        wv = w_ref[...]
        m2 = ADAM_B1 * m_ref[...] + (1.0 - ADAM_B1) * gs
        v2 = ADAM_B2 * v_ref[...] + (1.0 - ADAM_B2) * (gs * gs)
        m_hat = m2 / (1.0 - ADAM_B1 ** ADAM_STEP)
        v_hat = v2 / (1.0 - ADAM_B2 ** ADAM_STEP)
        go_ref[...] = gs
        d_ref[...] = -ADAM_LR * (m_hat / (jnp.sqrt(v_hat) + ADAM_EPS) + ADAM_WD * wv)
        mo_ref[...] = m2
        vo_ref[...] = v2

    spec = pl.BlockSpec((tr, c), lambda i: (i, 0))
    shp = jax.ShapeDtypeStruct((r, c), F32)
    return pl.pallas_call(kern, name=name, grid=(r // tr,),
                          in_specs=[spec, spec, spec, pl.BlockSpec((s_parts, tr, c), lambda i: (0, i, 0))],
                          out_specs=[spec] * 4, out_shape=[shp] * 4, compiler_params=_params(1))(w, m, v, g)


def _pack(arrs, rows, lead=0):
    parts = []
    for a in arrs:
        flat = a.reshape(a.shape[:lead] + (-1,))
        pad = (-flat.shape[-1]) % LANES
        if pad:
            flat = jnp.pad(flat, [(0, 0)] * lead + [(0, pad)])
        parts.append(flat)
    flat = jnp.concatenate(parts, axis=-1)
    pad = rows * LANES - flat.shape[-1]
    assert pad >= 0
    if pad:
        flat = jnp.pad(flat, [(0, 0)] * lead + [(0, pad)])
    return flat.reshape(flat.shape[:lead] + (rows, LANES))


def _unpack(buf, shapes, lead=0):
    flat = buf.reshape(buf.shape[:lead] + (-1,))
    out, off = [], 0
    for shp in shapes:
        n = math.prod(shp)
        out.append(flat[..., off:off + n].reshape(buf.shape[:lead] + tuple(shp)))
        off += n + ((-n) % LANES)
    return out


BIG = ('w_in', 'w_out', 'w_gate', 'w_up', 'w_down')
CONV =('ssd_conv_w', 'lru_conv_w')
CONV_SHARD_SHAPES = ((DEPTH, 4, SSD_CONV // N_DEV), (DEPTH, 4, LRU_W // N_DEV))
CONV_ROWS = 16
SMALL = ('norm_mix', 'ssd_conv_b', 'ssd_dt_bias', 'ssd_a_log', 'ssd_d', 'ssd_norm', 'lru_conv_b', 'lru_wa',
         'lru_ba', 'lru_wx', 'lru_bx', 'lru_lambda', 'norm_ffn', 'norm_final')
SMALL_ROWS = 1280
SMALL_TILE = 256
WEIGHTS = ('norm_mix', 'w_in', 'ssd_conv_w', 'ssd_conv_b', 'ssd_dt_bias', 'ssd_a_log', 'ssd_d', 'ssd_norm',
           'lru_conv_w', 'lru_conv_b', 'lru_wa', 'lru_ba', 'lru_wx', 'lru_bx', 'lru_lambda', 'w_out', 'norm_ffn',
           'w_gate', 'w_up', 'w_down', 'norm_final')


def _join_cols(a):
    return jnp.transpose(a, (1, 2, 0, 3)).reshape(a.shape[1], a.shape[2], -1)


def _join_rows(a):
    return jnp.transpose(a, (1, 0, 2, 3)).reshape(a.shape[1], -1, a.shape[3])


def _split_cols(a):
    l, r, c = a.shape
    return jnp.transpose(a.reshape(l, r, N_DEV, c // N_DEV), (2, 0, 1, 3)).reshape(N_DEV, l * r, c // N_DEV)


def _split_rows(a):
    l, r, c = a.shape
    return jnp.transpose(a.reshape(l, N_DEV, r // N_DEV, c), (1, 0, 2, 3)).reshape(N_DEV, l * r // N_DEV, c)


def _shard_form(k, a):
    if k == 'w_in':
        return jnp.pad(a.reshape(-1, IN_SHARD), ((0, 0), (0, IN_WIN - IN_SHARD)))
    if k in ('w_gate', 'w_up'):
        return jnp.pad(a.reshape(-1, FF_SHARD), ((0, 0), (0, FF_SHARD_P - FF_SHARD)))
    if k == 'w_down':
        return jnp.pad(a, ((0, 0), (0, FF_SHARD_P - FF_SHARD), (0, 0))).reshape(-1, D_MODEL)
    return a.reshape(-1, D_MODEL)


def _shard_back(k, a):
    if k == 'w_in':
        return a[:, :IN_SHARD].reshape(DEPTH, D_MODEL, IN_SHARD)
    if k in ('w_gate', 'w_up'):
        return a[:, :FF_SHARD].reshape(DEPTH, D_MODEL, FF_SHARD)
    if k == 'w_down':
        return a.reshape(DEPTH, FF_SHARD_P, D_MODEL)[:, :FF_SHARD]
    return a.reshape(DEPTH, D_MIX // N_DEV, D_MODEL)


def _dt_tile_place(a):
    zeros = jnp.zeros(a.shape[:-1] + (LANES - N_HEADS,), a.dtype)
    return jnp.concatenate([a[..., :6], zeros, a[..., 6:8]], axis=-1)


def _dt_tile_heads(tile):
    zeros = jnp.zeros(tile.shape[:-1] + (LANES - N_HEADS,), tile.dtype)
    return jnp.concatenate([tile[..., :6], tile[..., 126:128], zeros], axis=-1)


def _layout_from_windows(win):
    r = win.shape[1]
    main = jnp.concatenate([win[j][:, :512] for j in range(N_DEV)] + [jnp.zeros((r, LANES), win.dtype)], axis=1)
    gap = jnp.zeros((r, 384), win.dtype)
    tails = [jnp.zeros((r, 512), win.dtype)]
    for j in range(N_DEV - 1):
        tails += [win[j][:, 512:], gap]
    tails.append(win[N_DEV - 1][:, 512:])
    return main + jnp.concatenate(tails, axis=1)


def _prepare_weights(p, full):
    w = {}
    w_in = full['w_in']
    w_qkv = w_in[:, :, :D_MIX]
    dt_cols = _dt_tile_heads(w_in[:, :, 3072:3200])
    w_rest = jnp.concatenate([w_in[:, :, 2048:3072], w_in[:, :, 1536:2048], w_in[:, :, 3200:3712],
                              w_in[:, :, 3712:4224], dt_cols], axis=2)
    w['w_qkv'], w['w_rest'] = w_qkv, w_rest
    w['w_in_t'] = jnp.transpose(jnp.concatenate([w_qkv, w_rest], axis=2), (0, 2, 1))
    w['w_out'] = full['w_out']
    w['w_out_t'] = jnp.transpose(full['w_out'], (0, 2, 1))
    w['w_gu'] = jnp.concatenate([full['w_gate'], full['w_up']], axis=2)
    w['w_gu_t'] = jnp.transpose(w['w_gu'], (0, 2, 1))
    w['w_down'] = full['w_down']
    w['w_down_t'] = jnp.transpose(full['w_down'], (0, 2, 1))
    for k in ('norm_mix', 'ssd_conv_b', 'ssd_norm', 'lru_conv_b', 'lru_ba', 'lru_bx', 'lru_lambda', 'norm_ffn'):
        w[k] = p[k][:, None, :]
    for k in ('ssd_dt_bias', 'ssd_a_log', 'ssd_d'):
        w[k] = jnp.pad(p[k], ((0, 0), (0, LANES - N_HEADS)))[:, None, :]
    for k in CONV:
        w[k] = jnp.pad(full[k], ((0, 0), (0, 4), (0, 0)))
    for k in ('lru_wa', 'lru_wx'):
        bd = jnp.stack([_block_diag(p[k][l]) for l in range(DEPTH)]).astype(BF16)
        w[k] = bd
        w[k + '_t'] = jnp.transpose(bd, (0, 2, 1))
    return w


def _local_step(x, target, w, norm_final):
    saved = []
    for l in range(DEPTH):
        x, sv = _layer_fwd(x, w, l)
        saved.append(sv)
    dx, loss_acc, dgf = _final_loss(x, norm_final[None, :], target)
    grads = [None] * DEPTH
    for l in reversed(range(DEPTH)):
        dx, grads[l] = _layer_bwd(dx, saved[l], w, l)
    g = {k: jnp.stack([grads[l][k] for l in range(DEPTH)]) for k in grads[0]}
    g['norm_final'] = dgf[0]
    return loss_acc[0, 0], dx, g


def kernel(x, norm_mix, w_in, ssd_conv_w, ssd_conv_b, ssd_dt_bias, ssd_a_log, ssd_d, ssd_norm, lru_conv_w, lru_conv_b, lru_wa, lru_ba, lru_wx, lru_bx, lru_lambda, w_out, norm_ffn, w_gate, w_up, w_down, norm_final, loss_target, m_norm_mix, m_w_in, m_ssd_conv_w, m_ssd_conv_b, m_ssd_dt_bias, m_ssd_a_log, m_ssd_d, m_ssd_norm, m_lru_conv_w, m_lru_conv_b, m_lru_wa, m_lru_ba, m_lru_wx, m_lru_bx, m_lru_lambda, m_w_out, m_norm_ffn, m_w_gate, m_w_up, m_w_down, m_norm_final, v_norm_mix, v_w_in, v_ssd_conv_w, v_ssd_conv_b, v_ssd_dt_bias, v_ssd_a_log, v_ssd_d, v_ssd_norm, v_lru_conv_w, v_lru_conv_b, v_lru_wa, v_lru_ba, v_lru_wx, v_lru_bx, v_lru_lambda, v_w_out, v_norm_ffn, v_w_gate, v_w_up, v_w_down, v_norm_final):
    args = (norm_mix, w_in, ssd_conv_w, ssd_conv_b, ssd_dt_bias, ssd_a_log, ssd_d, ssd_norm, lru_conv_w, lru_conv_b, lru_wa, lru_ba, lru_wx, lru_bx, lru_lambda, w_out, norm_ffn, w_gate, w_up, w_down, norm_final)
    margs = (m_norm_mix, m_w_in, m_ssd_conv_w, m_ssd_conv_b, m_ssd_dt_bias, m_ssd_a_log, m_ssd_d, m_ssd_norm, m_lru_conv_w, m_lru_conv_b, m_lru_wa, m_lru_ba, m_lru_wx, m_lru_bx, m_lru_lambda, m_w_out, m_norm_ffn, m_w_gate, m_w_up, m_w_down, m_norm_final)
    vargs = (v_norm_mix, v_w_in, v_ssd_conv_w, v_ssd_conv_b, v_ssd_dt_bias, v_ssd_a_log, v_ssd_d, v_ssd_norm, v_lru_conv_w, v_lru_conv_b, v_lru_wa, v_lru_ba, v_lru_wx, v_lru_bx, v_lru_lambda, v_w_out, v_norm_ffn, v_w_gate, v_w_up, v_w_down, v_norm_final)
    p = dict(zip(WEIGHTS, args))
    pm = dict(zip(WEIGHTS, margs))
    pv = dict(zip(WEIGHTS, vargs))

    forms = {k: _shard_form(k, p[k]) for k in BIG}
    send = [_place_w_in(forms['w_in'])] + [forms[k].astype(BF16) for k in BIG[1:]]
    got = _all_gather("gather_weights", send + [_pack([p[k] for k in CONV], CONV_ROWS)])
    full = {'w_in': _layout_from_windows(got[0]).reshape(DEPTH, D_MODEL, IN_COLS_P),
            'w_out': _join_rows(got[1].reshape(N_DEV, DEPTH, D_MIX // N_DEV, D_MODEL)),
            'w_gate': _join_cols(got[2].reshape(N_DEV, DEPTH, D_MODEL, FF_SHARD_P)),
            'w_up': _join_cols(got[3].reshape(N_DEV, DEPTH, D_MODEL, FF_SHARD_P)),
            'w_down': _join_rows(got[4].reshape(N_DEV, DEPTH, FF_SHARD_P, D_MODEL))}
    for k, a in zip(CONV, _unpack(got[5], CONV_SHARD_SHAPES, lead=1)):
        full[k] = _join_cols(a)
    w = _prepare_weights(p, full)

    loss_local, dx, g = _local_step(x[0], loss_target[0], w, norm_final)
    loss = lax.psum(loss_local, ("x", "y", "c"))

    small_g = _all_gather("gather_small_grads", [_pack([g[k] for k in SMALL + CONV], SMALL_ROWS)])[0]
    zeros = [jnp.zeros_like(g[k]) for k in CONV]
    res_small = _adamw("adamw_small", _pack([p[k] for k in SMALL] + zeros, SMALL_ROWS),
                       _pack([pm[k] for k in SMALL] + zeros, SMALL_ROWS),
                       _pack([pv[k] for k in SMALL] + zeros, SMALL_ROWS), small_g, SMALL_TILE)
    small_shapes = [g[k].shape for k in SMALL + CONV]
    out = {kind: {} for kind in range(4)}
    for kind in range(4):
        for k, a in zip(SMALL + CONV, _unpack(res_small[kind], small_shapes)):
            out[kind][k] = a
    me = 4 * lax.axis_index("x") + 2 * lax.axis_index("y") + lax.axis_index("c")
    conv_g = []
    for k, shp in zip(CONV, CONV_SHARD_SHAPES):
        conv_g.append(lax.dynamic_slice_in_dim(out[0][k], me * shp[2], shp[2], axis=2))
    res_conv = _adamw("adamw_conv", _pack([p[k] for k in CONV], CONV_ROWS), _pack([pm[k] for k in CONV], CONV_ROWS),
                      _pack([pv[k] for k in CONV], CONV_ROWS), _pack(conv_g, CONV_ROWS)[None], CONV_ROWS)
    for kind in range(4):
        for k, a in zip(CONV, _unpack(res_conv[kind], CONV_SHARD_SHAPES)):
            out[kind][k] = a

    g_in = g['w_in'].reshape(DEPTH * D_MODEL, IN_COLS_P)
    dest = [jnp.stack([g_in[:, 512 * j:512 * j + IN_WIN] for j in range(N_DEV)]), _split_rows(g['w_out']),
            _split_cols(g['w_gate']), _split_cols(g['w_up']), _split_rows(g['w_down'])]
    parts = _all_to_all("exchange_big_grads", [a.astype(BF16) for a in dest])
    tiles = {'w_in': 256, 'w_out': 128, 'w_gate': 512, 'w_up': 512, 'w_down': 256}
    for k, part in zip(BIG, parts):
        res = _adamw("adamw_" + k, forms[k], _shard_form(k, pm[k]), _shard_form(k, pv[k]), part, tiles[k],
                     from_window=(k == 'w_in'))
        for kind in range(4):
            out[kind][k] = _shard_back(k, res[kind])

    outs = [loss, dx[None]]
    for kind in range(4):
        outs += [out[kind][k] for k in WEIGHTS]
    return tuple(outs)
```

```python
import functools
import math

import jax
import jax.numpy as jnp
from jax import lax
from jax.experimental import pallas as pl
from jax.experimental.pallas import tpu as pltpu

F32 = jnp.float32
BF16 = jnp.bfloat16

N_DEV = 8
DEPTH = 2
D_MODEL = 1024
ATT_W = 512
HEAD_DIM = 64
N_HEADS = 8
ATT_BLOCK = 128
ATT_DILATIONS = (16, 4, 1)
SSD_W = 512
SSD_STATE = 128
SSD_CONV = 1024
SSD_CHUNK = 128
LRU_W = 512
LRU_C = 8.0
D_MIX = 1536
D_FF = 2816
FF_SHARD = D_FF // N_DEV
FF_SHARD_P = 384
D_FFP = N_DEV * FF_SHARD_P
IN_COLS = 4104
IN_SHARD = IN_COLS // N_DEV
IN_WIN = 640
IN_COLS_P = 4224
REST_COLS = 2688
NORM_EPS = 1e-6
SSD_NORM_EPS = 1e-5
NEG = -1e30

ADAM_LR = 0.001
ADAM_B1 = 0.9
ADAM_B2 = 0.999
ADAM_EPS = 1e-08
ADAM_WD = 0.01
ADAM_STEP = 10

LANES = 128
VMEM_LIMIT = 52 * 1024 * 1024
HI = lax.Precision.HIGHEST


def _sigmoid(x):
    return 1.0 / (1.0 + jnp.exp(-x))


def _silu(x):
    return x * _sigmoid(x)


def _dsilu(x):
    s = _sigmoid(x)
    return s * (1.0 + x * (1.0 - s))


def _softplus(x):
    return jnp.maximum(x, 0.0) + jnp.log(1.0 + jnp.exp(-jnp.abs(x)))


_GELU_C = math.sqrt(2.0 / math.pi)


def _gelu(x):
    return 0.5 * x * (1.0 + jnp.tanh(_GELU_C * (x + 0.044715 * x * x * x)))


def _dgelu(x):
    t = jnp.tanh(_GELU_C * (x + 0.044715 * x * x * x))
    return 0.5 * (1.0 + t) + 0.5 * x * (1.0 - t * t) * _GELU_C * (1.0 + 3.0 * 0.044715 * x * x)


def _dot(a, b):
    return jnp.dot(a, b, preferred_element_type=F32)


def _dot_nt(a, b):
    return lax.dot_general(a, b, (((1,), (1,)), ((), ())), preferred_element_type=F32)


def _dot_tn(a, b):
    return lax.dot_general(a, b, (((0,), (0,)), ((), ())), preferred_element_type=F32)


def _dot_hi(a, b):
    return jnp.dot(a, b, preferred_element_type=F32, precision=HI)


def _iota(shape, axis):
    return lax.broadcasted_iota(jnp.int32, shape, axis)


def _shift_down(x, s, prev8):
    xs = pltpu.roll(x, s, 0)
    ps = pltpu.roll(prev8, s, 0)
    top = jnp.concatenate([ps, x[8:]], axis=0)
    return jnp.where(_iota(x.shape, 0) < s, top, xs)


def _shift_up(x, s, next8):
    tm = x.shape[0]
    xs = pltpu.roll(x, tm - s, 0)
    ns = pltpu.roll(next8, 8 - s, 0)
    bottom = jnp.concatenate([x[:tm - 8], ns], axis=0)
    return jnp.where(_iota(x.shape, 0) >= tm - s, bottom, xs)


def _expand_mat():
    return jnp.where(_iota((LANES, SSD_W), 1) // HEAD_DIM == _iota((LANES, SSD_W), 0), 1.0, 0.0).astype(F32)


def _reduce_mat():
    return jnp.where(_iota((SSD_W, LANES), 0) // HEAD_DIM == _iota((SSD_W, LANES), 1), 1.0, 0.0).astype(F32)


def _params(n_grid):
    return pltpu.CompilerParams(dimension_semantics=("arbitrary",) * n_grid, vmem_limit_bytes=VMEM_LIMIT)


def _rowwise(name, body, n_rows, tm, ins, outs, scratch=(), reverse=False):
    nt = n_rows // tm
    assert nt * tm == n_rows and tm % 8 == 0
    r8 = tm // 8
    last8 = n_rows // 8 - 1

    def pos(s):
        return (nt - 1 - s) if reverse else s

    in_specs, args = [], []
    for spec in ins:
        kind, arr = spec[0], spec[1]
        args.append(arr)
        if kind == 'row':
            in_specs.append(pl.BlockSpec((tm, arr.shape[1]), lambda s: (pos(s), 0)))
        elif kind == 'col':
            in_specs.append(pl.BlockSpec((tm, spec[2]), functools.partial(lambda s, j: (pos(s), j), j=spec[3])))
        elif kind == 'full':
            in_specs.append(pl.BlockSpec(arr.shape, functools.partial(lambda s, n: (0,) * n, n=arr.ndim)))
        elif kind == 'prev8':
            in_specs.append(pl.BlockSpec((8, spec[2]), functools.partial(
                lambda s, j: (jnp.maximum(pos(s) * r8 - 1, 0), j), j=spec[3])))
        elif kind == 'next8':
            in_specs.append(pl.BlockSpec((8, spec[2]), functools.partial(
                lambda s, j: (jnp.minimum((pos(s) + 1) * r8, last8), j), j=spec[3])))
        else:
            raise ValueError(kind)
    out_specs, out_shape, acc_idx = [], [], []
    for k, spec in enumerate(outs):
        if spec[0] == 'row':
            out_specs.append(pl.BlockSpec((tm, spec[1]), lambda s: (pos(s), 0)))
            out_shape.append(jax.ShapeDtypeStruct((n_rows, spec[1]), spec[2]))
        else:
            out_specs.append(pl.BlockSpec(spec[1], lambda s: (0, 0)))
            out_shape.append(jax.ShapeDtypeStruct(spec[1], spec[2]))
            acc_idx.append(k)
    n_in, n_out = len(ins), len(outs)

    def kern(*refs):
        s = pl.program_id(0)
        in_refs, out_refs, scr = refs[:n_in], refs[n_in:n_in + n_out], refs[n_in + n_out:]

        @pl.when(s == 0)
        def _():
            for k in acc_idx:
                out_refs[k][...] = jnp.zeros(out_refs[k].shape, out_refs[k].dtype)

        body(pos(s), nt, in_refs, out_refs, scr)

    res = pl.pallas_call(kern, name=name, grid=(nt,), in_specs=in_specs, out_specs=out_specs,
                         out_shape=out_shape, scratch_shapes=list(scratch), compiler_params=_params(1))(*args)
    return res


def _mm(name, a_list, b, *, res=None, out_dtype=F32, tm=512, tn=None):
    n_rows = a_list[0].shape[0]
    k_total, n = b.shape
    ks = [a.shape[1] for a in a_list]
    assert sum(ks) == k_total
    tn = n if tn is None else tn
    assert n_rows % tm == 0 and n % tn == 0
    na = len(a_list)

    def kern(*refs):
        a_refs, b_ref, o_ref = refs[:na], refs[na], refs[-1]
        acc, off = None, 0
        for a_ref, kp in zip(a_refs, ks):
            part = _dot(a_ref[...].astype(BF16), b_ref[off:off + kp, :])
            acc = part if acc is None else acc + part
            off += kp
        if res is not None:
            acc = acc + refs[na + 1][...]
        o_ref[...] = acc.astype(out_dtype)

    in_specs = [pl.BlockSpec((tm, kp), lambda i, j: (i, 0)) for kp in ks]
    in_specs.append(pl.BlockSpec((k_total, tn), lambda i, j: (0, j)))
    args = list(a_list) + [b]
    if res is not None:
        in_specs.append(pl.BlockSpec((tm, tn), lambda i, j: (i, j)))
        args.append(res)
    return pl.pallas_call(kern, name=name, grid=(n_rows // tm, n // tn), in_specs=in_specs,
                          out_specs=pl.BlockSpec((tm, tn), lambda i, j: (i, j)),
                          out_shape=jax.ShapeDtypeStruct((n_rows, n), out_dtype),
                          compiler_params=_params(2))(*args)


def _mm_tn(name, a, g, *, a_col=None, g_col=None, tk=None, tn=None, tt=512):
    n_rows = a.shape[0]
    k = a.shape[1] if a_col is None else a_col[0]
    a_j = 0 if a_col is None else a_col[1]
    n = g.shape[1] if g_col is None else g_col[0]
    g_j = 0 if g_col is None else g_col[1]
    tk = k if tk is None else tk
    tn = n if tn is None else tn
    assert k % tk == 0 and n % tn == 0 and n_rows % tt == 0
    kb = k // tk
    nbk = n // tn

    def kern(a_ref, g_ref, o_ref):
        t = pl.program_id(2)

        @pl.when(t == 0)
        def _():
            o_ref[...] = jnp.zeros(o_ref.shape, F32)

        o_ref[...] += _dot_tn(a_ref[...].astype(BF16), g_ref[...].astype(BF16))

    return pl.pallas_call(
        kern, name=name, grid=(kb, n // tn, n_rows // tt),
        in_specs=[pl.BlockSpec((tt, tk), lambda i, j, t: (t, a_j * kb + i)),
                  pl.BlockSpec((tt, tn), lambda i, j, t: (t, g_j * nbk + j))],
        out_specs=pl.BlockSpec((tk, tn), lambda i, j, t: (i, j)),
        out_shape=jax.ShapeDtypeStruct((k, n), F32), compiler_params=_params(3))(a, g)


def _rmsnorm_fwd(name, x, g):
    def body(i, nt, ins, outs, scr):
        xv = ins[0][...]
        rstd = lax.rsqrt(jnp.mean(xv * xv, axis=-1, keepdims=True) + NORM_EPS)
        outs[0][...] = (xv * rstd * ins[1][...]).astype(BF16)

    return _rowwise(name, body, x.shape[0], 512, [('row', x), ('full', g)], [('row', x.shape[1], BF16)])[0]


def _rmsnorm_bwd(name, dh, x, g, dres):
    d = x.shape[1]

    def body(i, nt, ins, outs, scr):
        dy, xv, gv, dr = ins[0][...], ins[1][...], ins[2][...], ins[3][...]
        rstd = lax.rsqrt(jnp.mean(xv * xv, axis=-1, keepdims=True) + NORM_EPS)
        xhat = xv * rstd
        outs[1][0:1, :] += jnp.sum(dy * xhat, axis=0, keepdims=True)
        dxh = dy * gv
        outs[0][...] = dr + rstd * (dxh - xhat * jnp.mean(dxh * xhat, axis=-1, keepdims=True))

    return _rowwise(name, body, x.shape[0], 512, [('row', dh), ('row', x), ('full', g), ('row', dres)],
                    [('row', d, F32), ('acc', (8, d), F32)])


def _final_loss(x, g, target):
    d = x.shape[1]

    def body(i, nt, ins, outs, scr):
        xv, gv, tv = ins[0][...], ins[1][...], ins[2][...]
        rstd = lax.rsqrt(jnp.mean(xv * xv, axis=-1, keepdims=True) + NORM_EPS)
        xhat = xv * rstd
        err = xhat * gv - tv
        row_loss = 0.5 * jnp.mean(err * err, axis=-1, keepdims=True)
        outs[1][...] += jnp.sum(row_loss, axis=0, keepdims=True)
        dy = err * (1.0 / d)
        outs[2][0:1, :] += jnp.sum(dy * xhat, axis=0, keepdims=True)
        dxh = dy * gv
        outs[0][...] = rstd * (dxh - xhat * jnp.mean(dxh * xhat, axis=-1, keepdims=True))

    return _rowwise("final_loss", body, x.shape[0], 512, [('row', x), ('full', g), ('row', target)],
                    [('row', d, F32), ('acc', (8, LANES), F32), ('acc', (8, d), F32)])


def _conv_fwd(name, src, width, idx, w, b):
    def body(i, nt, ins, outs, scr):
        xv = ins[0][...]
        prev = jnp.where(i > 0, ins[1][...], 0.0)
        wv = ins[2][...]
        y = ins[3][...] + wv[3:4, :] * xv
        for s in (1, 2, 3):
            y = y + wv[3 - s:4 - s, :] * _shift_down(xv, s, prev)
        outs[0][...] = y

    return _rowwise(name, body, src.shape[0], 512,
                    [('col', src, width, idx), ('prev8', src, width, idx), ('full', w), ('full', b)],
                    [('row', width, F32)])[0]


def _conv_bwd(name, dpre, src, width, idx, w):
    def body(i, nt, ins, outs, scr):
        dy = ins[0][...]
        nxt = jnp.where(i < nt - 1, ins[1][...], 0.0)
        xv = ins[2][...]
        prev = jnp.where(i > 0, ins[3][...], 0.0)
        wv = ins[4][...]
        dx = wv[3:4, :] * dy
        outs[1][3:4, :] += jnp.sum(dy * xv, axis=0, keepdims=True)
        outs[1][4:5, :] += jnp.sum(dy, axis=0, keepdims=True)
        for s in (1, 2, 3):
            dx = dx + wv[3 - s:4 - s, :] * _shift_up(dy, s, nxt)
            outs[1][3 - s:4 - s, :] += jnp.sum(dy * _shift_down(xv, s, prev), axis=0, keepdims=True)
        outs[0][...] = dx

    return _rowwise(name, body, src.shape[0], 512,
                    [('row', dpre), ('next8', dpre, width, 0), ('col', src, width, idx),
                     ('prev8', src, width, idx), ('full', w)],
                    [('row', width, F32), ('acc', (8, width), F32)])


ATT_STEP_BLOCKS = 4


def _att_bias(not_first, dil, head):
    qi = _iota((ATT_BLOCK, 2 * ATT_BLOCK), 0)
    ki = _iota((ATT_BLOCK, 2 * ATT_BLOCK), 1)
    dist = ATT_BLOCK + qi - ki
    valid = (dist >= 0) & (dist <= ATT_BLOCK) & (not_first | (ki >= ATT_BLOCK))
    slope = 2.0 ** (-(head + 1))
    return jnp.where(valid, (-slope * dil) * dist.astype(F32), NEG)


def _head_mask():
    lane = _iota((ATT_BLOCK, LANES), 1)
    return lane < HEAD_DIM


def _att_q_specs(dil, nb, bq):
    big = (bq * ATT_BLOCK, ATT_W)
    one = (ATT_BLOCK, ATT_W)
    specs = [pl.BlockSpec(big, lambda r, n: (n, 3 * r)),
             pl.BlockSpec(big, lambda r, n: (n, 3 * r + 1)),
             pl.BlockSpec(one, lambda r, n: (jnp.maximum(n * bq - 1, 0), 3 * r + 1)),
             pl.BlockSpec(big, lambda r, n: (n, 3 * r + 2)),
             pl.BlockSpec(one, lambda r, n: (jnp.maximum(n * bq - 1, 0), 3 * r + 2))]
    wide = pl.BlockSpec(big, lambda r, n: (n, r))
    stat = pl.BlockSpec((bq * ATT_BLOCK, LANES), lambda r, n: (n, r))
    return specs, wide, stat


def _att_fwd(dil, qkv_v, stats):
    n_l = qkv_v.shape[0]
    nb = n_l // ATT_BLOCK
    bq = min(ATT_STEP_BLOCKS, nb)
    first = stats is None
    scale = HEAD_DIM ** -0.5

    def kern(*refs):
        n = pl.program_id(1)
        q_ref, kc_ref, kp_ref, vc_ref, vp_ref = refs[:5]
        if first:
            m_out, l_out, a_out = refs[5:]
        else:
            m_in, l_in, a_in, m_out, l_out, a_out = refs[5:]
        low = _head_mask()
        lane = _iota((ATT_BLOCK, LANES), 1)
        for b in range(bq):
            rows = slice(ATT_BLOCK * b, ATT_BLOCK * (b + 1))
            prev = slice(ATT_BLOCK * (b - 1), ATT_BLOCK * b)
            not_first = (n * bq + b) > 0
            m_acc = jnp.zeros((ATT_BLOCK, LANES), F32)
            l_acc = jnp.zeros((ATT_BLOCK, LANES), F32)
            for p in range(N_HEADS // 2):
                sl = slice(LANES * p, LANES * (p + 1))
                q2 = q_ref[rows, sl].astype(F32)
                k_prev = kp_ref[:, sl] if b == 0 else kc_ref[prev, sl]
                v_prev = vp_ref[:, sl] if b == 0 else vc_ref[prev, sl]
                k2 = jnp.concatenate([k_prev, kc_ref[rows, sl]], axis=0).astype(BF16)
                v2 = jnp.concatenate([v_prev, vc_ref[rows, sl]], axis=0).astype(BF16)
                res = []
                for e in range(2):
                    h = 2 * p + e
                    keep = low if e == 0 else jnp.logical_not(low)
                    qe = jnp.where(keep, q2, 0.0).astype(BF16)
                    s = _dot_nt(qe, k2) * scale + _att_bias(not_first, dil, h)
                    m_blk = jnp.max(s, axis=-1, keepdims=True)
                    if first:
                        m_new = m_blk
                        pe = jnp.exp(s - m_new)
                        l_new = jnp.sum(pe, axis=-1, keepdims=True)
                        a_new = _dot(pe.astype(BF16), v2)
                    else:
                        m_old = m_in[rows, h:h + 1]
                        l_old = l_in[rows, h:h + 1]
                        m_new = jnp.maximum(m_old, m_blk)
                        pe = jnp.exp(s - m_new)
                        alpha = jnp.exp(m_old - m_new)
                        l_new = alpha * l_old + jnp.sum(pe, axis=-1, keepdims=True)
                        a_new = alpha * a_in[rows, sl] + _dot(pe.astype(BF16), v2)
                    m_acc = jnp.where(lane == h, m_new, m_acc)
                    l_acc = jnp.where(lane == h, l_new, l_acc)
                    res.append(a_new)
                a_out[rows, sl] = jnp.where(low, res[0], res[1])
            m_out[rows, :] = m_acc
            l_out[rows, :] = l_acc

    specs, wide, stat = _att_q_specs(dil, nb, bq)
    args = [qkv_v] * 5
    if not first:
        specs = specs + [stat, stat, wide]
        args += list(stats)
    shp_s = jax.ShapeDtypeStruct((n_l, dil * LANES), F32)
    shp_a = jax.ShapeDtypeStruct((n_l, dil * ATT_W), F32)
    return pl.pallas_call(kern, name="att_fwd_d%d" % dil, grid=(dil, nb // bq), in_specs=specs,
                          out_specs=[stat, stat, wide], out_shape=[shp_s, shp_s, shp_a],
                          compiler_params=_params(2))(*args)


def _att_finish(m, l, acc):
    def body(i, nt, ins, outs, scr):
        lv = ins[1][...]
        real = _iota(lv.shape, 1) < N_HEADS
        outs[0][...] = ins[2][...] / _dot_hi(lv, _expand_mat())
        outs[1][...] = jnp.where(real, ins[0][...] + jnp.log(jnp.where(real, lv, 1.0)), 0.0)

    return _rowwise("att_finish", body, m.shape[0], 512, [('row', m), ('row', l), ('row', acc)],
                    [('row', ATT_W, F32), ('row', LANES, F32)])


def _att_delta(d_att, out):
    def body(i, nt, ins, outs, scr):
        outs[0][...] = _dot_hi(ins[0][...] * ins[1][...], _reduce_mat())

    return _rowwise("att_delta", body, out.shape[0], 512, [('row', d_att), ('row', out)],
                    [('row', LANES, F32)])[0]


def _att_bwd(dil, qkv_v, do_v, lse_v, delta_v, dkv_in):
    n_l = qkv_v.shape[0]
    nb = n_l // ATT_BLOCK
    bq = min(ATT_STEP_BLOCKS, nb)
    steps = nb // bq
    first = dkv_in is None
    scale = HEAD_DIM ** -0.5

    def kern(*refs):
        j = pl.program_id(1)
        k_ref, v_ref, qc_ref, qn_ref, doc_ref, don_ref, lc_ref, ln_ref, dc_ref, dn_ref = refs[:10]
        dk_out, dv_out, dq_out, carry = refs[-4:]
        low = _head_mask()
        row = _iota((2 * ATT_BLOCK, ATT_BLOCK), 0)
        key = _iota((2 * ATT_BLOCK, ATT_BLOCK), 1)
        dist = row - key
        low2 = _iota((2 * ATT_BLOCK, LANES), 1) < HEAD_DIM

        @pl.when(j == 0)
        def _():
            carry[...] = jnp.zeros(carry.shape, F32)

        dq_prev = [carry[:, LANES * p:LANES * (p + 1)] for p in range(N_HEADS // 2)]
        for b in range(bq):
            rows = slice(ATT_BLOCK * b, ATT_BLOCK * (b + 1))
            nrows = slice(ATT_BLOCK * (b + 1), ATT_BLOCK * (b + 2))
            inner = b < bq - 1
            has_next = True if inner else (j < steps - 1)
            valid = (dist >= 0) & (dist <= ATT_BLOCK) & ((row < ATT_BLOCK) | has_next)
            lse2 = jnp.concatenate([lc_ref[rows, :], lc_ref[nrows, :] if inner else ln_ref[...]], axis=0)
            dl2 = jnp.concatenate([dc_ref[rows, :], dc_ref[nrows, :] if inner else dn_ref[...]], axis=0)
            for p in range(N_HEADS // 2):
                sl = slice(LANES * p, LANES * (p + 1))
                k2 = k_ref[rows, sl].astype(F32)
                v2 = v_ref[rows, sl].astype(F32)
                q2 = jnp.concatenate([qc_ref[rows, sl], qc_ref[nrows, sl] if inner else qn_ref[:, sl]],
                                     axis=0).astype(F32)
                do2 = jnp.concatenate([doc_ref[rows, sl], doc_ref[nrows, sl] if inner else don_ref[:, sl]], axis=0)
                q2b = q2.astype(BF16)
                do2b = do2.astype(BF16)
                dks, dvs, dq2 = [], [], None
                for e in range(2):
                    h = 2 * p + e
                    keep = low if e == 0 else jnp.logical_not(low)
                    keep2 = low2 if e == 0 else jnp.logical_not(low2)
                    slope = 2.0 ** (-(h + 1))
                    bias = jnp.where(valid, (-slope * dil) * dist.astype(F32), NEG)
                    ke = jnp.where(keep, k2, 0.0).astype(BF16)
                    ve = jnp.where(keep, v2, 0.0).astype(BF16)
                    s = _dot_nt(q2b, ke) * scale + bias
                    pe = jnp.exp(s - lse2[:, h:h + 1])
                    dp = _dot_nt(do2b, ve)
                    dsb = (pe * (dp - dl2[:, h:h + 1])).astype(BF16)
                    dvs.append(_dot_tn(pe.astype(BF16), jnp.where(keep2, do2, 0.0).astype(BF16)))
                    dks.append(_dot_tn(dsb, jnp.where(keep2, q2, 0.0).astype(BF16)) * scale)
                    dqe = _dot(dsb, ke)
                    dq2 = dqe if dq2 is None else dq2 + dqe
                dk = jnp.where(low, dks[0], dks[1])
                dv = jnp.where(low, dvs[0], dvs[1])
                dq = dq_prev[p] + dq2[:ATT_BLOCK] * scale
                dq_prev[p] = dq2[ATT_BLOCK:] * scale
                if not first:
                    dk = dk + refs[10][rows, sl]
                    dv = dv + refs[11][rows, sl]
                    dq = dq + refs[12][rows, sl]
                dk_out[rows, sl] = dk
                dv_out[rows, sl] = dv
                dq_out[rows, sl] = dq
        for p in range(N_HEADS // 2):
            carry[:, LANES * p:LANES * (p + 1)] = dq_prev[p]

    big = (bq * ATT_BLOCK, ATT_W)
    one = (ATT_BLOCK, ATT_W)

    def nxt_idx(j):
        return jnp.minimum((j + 1) * bq, nb - 1)

    cur = pl.BlockSpec(big, lambda r, j: (j, r))
    nxt = pl.BlockSpec(one, lambda r, j: (nxt_idx(j), r))
    cur_s = pl.BlockSpec((bq * ATT_BLOCK, LANES), lambda r, j: (j, r))
    nxt_s = pl.BlockSpec((ATT_BLOCK, LANES), lambda r, j: (nxt_idx(j), r))
    in_specs = [pl.BlockSpec(big, lambda r, j: (j, 3 * r + 1)),
                pl.BlockSpec(big, lambda r, j: (j, 3 * r + 2)),
                pl.BlockSpec(big, lambda r, j: (j, 3 * r)),
                pl.BlockSpec(one, lambda r, j: (nxt_idx(j), 3 * r)),
                cur, nxt, cur_s, nxt_s, cur_s, nxt_s]
    args = [qkv_v] * 4 + [do_v, do_v, lse_v, lse_v, delta_v, delta_v]
    if not first:
        in_specs += [cur, cur, cur]
        args += list(dkv_in)
    shp = jax.ShapeDtypeStruct((n_l, dil * ATT_W), F32)
    return pl.pallas_call(kern, name="att_bwd_d%d" % dil, grid=(dil, steps), in_specs=in_specs,
                          out_specs=[cur, cur, cur], out_shape=[shp, shp, shp],
                          scratch_shapes=[pltpu.VMEM((ATT_BLOCK, ATT_W), F32)], compiler_params=_params(2))(*args)


def _attention_fwd(qkv):
    t = qkv.shape[0]
    stats = None
    for dil in ATT_DILATIONS:
        if stats is not None:
            stats = [s.reshape(t // dil, -1) for s in stats]
        stats = _att_fwd(dil, qkv.reshape(t // dil, dil * D_MIX), stats)
    m, l, acc = [s.reshape(t, -1) for s in stats]
    return _att_finish(m, l, acc)


def _attention_bwd(qkv, d_att, out, lse):
    t = qkv.shape[0]
    delta = _att_delta(d_att, out)
    grads = None
    for dil in ATT_DILATIONS:
        view = lambda a: a.reshape(t // dil, -1)
        grads = _att_bwd(dil, view(qkv), view(d_att), view(lse), view(delta),
                         None if grads is None else [view(a) for a in grads])
    dk, dv, dq = [a.reshape(t, ATT_W) for a in grads]
    return dq, dk, dv


def _ssd_chunk_common(pre, dtraw, bias_row, alog_row):
    q = SSD_CHUNK
    act = _silu(pre)
    lane = _iota((q, LANES), 1)
    dt = jnp.where(lane < N_HEADS, _softplus(dtraw + bias_row), 0.0)
    a_row = -jnp.exp(alog_row)
    tril = jnp.where(_iota((q, q), 0) >= _iota((q, q), 1), 1.0, 0.0).astype(F32)
    cs = _dot_hi(tril, dt * a_row)
    cs_last = cs[q - 1:q, :]
    return act, dt, a_row, tril, cs, cs_last


def _ssd_lmat(cs, cs_t, h):
    q = SSD_CHUNK
    seg = cs[:, h:h + 1] - cs_t[h:h + 1, :]
    causal = _iota((q, q), 0) >= _iota((q, q), 1)
    return jnp.exp(jnp.where(causal, seg, NEG))


def _ssd_gate_norm(y, z, norm_w):
    sz = _silu(z)
    yg = y * sz
    half = SSD_W // 2
    outs, rss = [], []
    for g in range(2):
        part = yg[:, half * g:half * (g + 1)]
        rs = lax.rsqrt(jnp.mean(part * part, axis=-1, keepdims=True) + SSD_NORM_EPS)
        outs.append(part * rs)
        rss.append(rs)
    yn = jnp.concatenate(outs, axis=1)
    return sz, yn, rss, yn * norm_w


def _ssd_fwd(pre, rest, dt_bias, a_log, d_skip, norm_w):
    t = pre.shape[0]
    q = SSD_CHUNK

    def body(c, nc, ins, outs, scr):
        pre_ref, z_ref, dtr_ref, bias_ref, alog_ref, dsk_ref, nw_ref = ins
        out_ref, y_ref, sp_ref = outs
        s_ref = scr[0]

        @pl.when(c == 0)
        def _():
            s_ref[...] = jnp.zeros(s_ref.shape, F32)

        act, dt, a_row, tril, cs, cs_last = _ssd_chunk_common(pre_ref[...], dtr_ref[...], bias_ref[...],
                                                               alog_ref[...])
        x = act[:, :SSD_W]
        cs_t = cs.T
        e_col = jnp.exp(cs)
        w = jnp.exp(cs_last - cs) * dt
        expand = _expand_mat()
        w_x = _dot_hi(w, expand)
        dt_x = _dot_hi(dt, expand)
        e_x = _dot_hi(e_col, expand)
        d_x = _dot_hi(dsk_ref[...], expand)
        cd_x = _dot_hi(jnp.exp(cs_last), expand)
        s_prev = s_ref[...]
        sp_ref[...] = s_prev
        xw = (x * w_x).astype(BF16)
        xd = (x * dt_x).astype(BF16)
        low = _head_mask()
        y_parts, s_parts = [], []
        for g in range(2):
            bg = act[:, SSD_W + SSD_STATE * g:SSD_W + SSD_STATE * (g + 1)].astype(BF16)
            cg = act[:, SSD_W + 2 * SSD_STATE + SSD_STATE * g:SSD_W + 2 * SSD_STATE + SSD_STATE * (g + 1)].astype(BF16)
            gsl = slice(256 * g, 256 * (g + 1))
            gmat = _dot_nt(cg, bg)
            s_parts.append(_dot_tn(bg, xw[:, gsl]))
            y0 = _dot(cg, s_prev[:, gsl].astype(BF16))
            for pp in range(2):
                pair = 2 * g + pp
                psl = slice(LANES * pair, LANES * (pair + 1))
                yd = []
                for e in range(2):
                    h = 2 * pair + e
                    mh = (gmat * _ssd_lmat(cs, cs_t, h)).astype(BF16)
                    yd.append(_dot(mh, xd[:, psl]))
                y_parts.append(jnp.where(low, yd[0], yd[1]) + e_x[:, psl] * y0[:, LANES * pp:LANES * (pp + 1)])
        y = jnp.concatenate(y_parts, axis=1) + d_x * x
        s_ref[...] = cd_x * s_prev + jnp.concatenate(s_parts, axis=1)
        y_ref[...] = y
        out_ref[...] = _ssd_gate_norm(y, z_ref[...], nw_ref[...])[3]

    return _rowwise("ssd_fwd", body, t, q,
                    [('row', pre), ('col', rest, SSD_W, 2), ('col', rest, LANES, 20), ('full', dt_bias),
                     ('full', a_log), ('full', d_skip), ('full', norm_w)],
                    [('row', SSD_W, F32), ('row', SSD_W, F32), ('row', SSD_W, F32)],
                    scratch=[pltpu.VMEM((SSD_STATE, SSD_W), F32)])


def _ssd_bwd(pre, rest, y, s_prev_all, d_mix, dt_bias, a_log, d_skip, norm_w):
    t = pre.shape[0]
    q = SSD_CHUNK

    def body(c, nc, ins, outs, scr):
        pre_ref, z_ref, dtr_ref, y_ref, sp_ref, do_ref, bias_ref, alog_ref, dsk_ref, nw_ref = ins
        dpre_ref, dz_ref, ddt_ref, a128_ref, a512_ref = outs
        ds_ref = scr[0]

        @pl.when(c == nc - 1)
        def _():
            ds_ref[...] = jnp.zeros(ds_ref.shape, F32)

        pre_v = pre_ref[...]
        dtr = dtr_ref[...]
        act, dt, a_row, tril, cs, cs_last = _ssd_chunk_common(pre_v, dtr, bias_ref[...], alog_ref[...])
        x = act[:, :SSD_W]
        cs_t = cs.T
        e_col = jnp.exp(cs)
        decay_end = jnp.exp(cs_last - cs)
        w = decay_end * dt
        cd = jnp.exp(cs_last)
        expand = _expand_mat()
        reduce = _reduce_mat()
        w_x = _dot_hi(w, expand)
        dt_x = _dot_hi(dt, expand)
        e_x = _dot_hi(e_col, expand)
        d_x = _dot_hi(dsk_ref[...], expand)
        cd_x = _dot_hi(cd, expand)
        s_prev = sp_ref[...]
        d_s = ds_ref[...]
        xw = (x * w_x).astype(BF16)
        xd = (x * dt_x).astype(BF16)
        low = _head_mask()
        lane = _iota((q, LANES), 1)
        sub = _iota((q, LANES), 0)

        yv, zv, nw = y_ref[...], z_ref[...], nw_ref[...]
        d_out = do_ref[...]
        sz, yn, rss, _ = _ssd_gate_norm(yv, zv, nw)
        a512_ref[0:1, :] += jnp.sum(d_out * yn, axis=0, keepdims=True)
        dyn = d_out * nw
        half = SSD_W // 2
        dyg_parts = []
        for g in range(2):
            hs = slice(half * g, half * (g + 1))
            dyg_parts.append(rss[g] * (dyn[:, hs] - yn[:, hs] * jnp.mean(dyn[:, hs] * yn[:, hs], axis=-1,
                                                                          keepdims=True)))
        dyg = jnp.concatenate(dyg_parts, axis=1)
        dy = dyg * sz
        dz_ref[...] = dyg * yv * _dsilu(zv)

        a128_ref[2:3, :] += _dot_hi(jnp.sum(dy * x, axis=0, keepdims=True), reduce)
        dx = d_x * dy

        dy0 = e_x * dy
        dyb = dy.astype(BF16)
        dcs = jnp.zeros((q, LANES), F32)
        dcs_rows = jnp.zeros((q, LANES), F32)
        ddt = jnp.zeros((q, LANES), F32)
        ds_prev_parts, z_parts, db_parts, dc_parts, dxd_parts, y0_parts = [], [], [], [], [], []
        for g in range(2):
            bg = act[:, SSD_W + SSD_STATE * g:SSD_W + SSD_STATE * (g + 1)].astype(BF16)
            cg = act[:, SSD_W + 2 * SSD_STATE + SSD_STATE * g:SSD_W + 2 * SSD_STATE + SSD_STATE * (g + 1)].astype(BF16)
            gsl = slice(256 * g, 256 * (g + 1))
            spg = s_prev[:, gsl].astype(BF16)
            dsg = d_s[:, gsl].astype(BF16)
            dy0g = dy0[:, gsl].astype(BF16)
            gmat = _dot_nt(cg, bg)
            y0_parts.append(_dot(cg, spg))
            dc_g = _dot_nt(dy0g, spg)
            ds_prev_parts.append(_dot_tn(cg, dy0g))
            z_parts.append(_dot(bg, dsg))
            db_g = _dot_nt(xw[:, gsl], dsg)
            dg_acc = jnp.zeros((q, q), F32)
            for pp in range(2):
                pair = 2 * g + pp
                psl = slice(LANES * pair, LANES * (pair + 1))
                dxd_e = []
                for e in range(2):
                    h = 2 * pair + e
                    keep = low if e == 0 else jnp.logical_not(low)
                    lm = _ssd_lmat(cs, cs_t, h)
                    mh = gmat * lm
                    dm = _dot_nt(jnp.where(keep, dy[:, psl], 0.0).astype(BF16), xd[:, psl])
                    dxd_e.append(_dot_tn(mh.astype(BF16), dyb[:, psl]))
                    wm = dm * mh
                    dcs = dcs + jnp.where(lane == h, jnp.sum(wm, axis=1, keepdims=True), 0.0)
                    dcs_rows = dcs_rows - jnp.where(sub == h, jnp.sum(wm, axis=0, keepdims=True), 0.0)
                    dg_acc = dg_acc + dm * lm
                dxd_parts.append(jnp.where(low, dxd_e[0], dxd_e[1]))
            dgb = dg_acc.astype(BF16)
            dc_parts.append(dc_g + _dot(dgb, bg))
            db_parts.append(db_g + _dot_tn(dgb, cg))
        y0 = jnp.concatenate(y0_parts, axis=1)
        zmat = jnp.concatenate(z_parts, axis=1)
        dxd = jnp.concatenate(dxd_parts, axis=1)
        ds_prev = jnp.concatenate(ds_prev_parts, axis=1) + cd_x * d_s
        ds_ref[...] = ds_prev

        dcs = dcs + _dot_hi(dy * y0, reduce) * e_col
        dcd = _dot_hi(jnp.sum(d_s * s_prev, axis=0, keepdims=True), reduce)
        dlast = dcd * cd
        dx = dx + w_x * zmat + dxd * dt_x
        dw = _dot_hi(zmat * x, reduce)
        ddt = ddt + dw * decay_end + _dot_hi(dxd * x, reduce)
        dwl = dw * w
        dcs = dcs - dwl
        dlast = dlast + jnp.sum(dwl, axis=0, keepdims=True)
        dcs = dcs + dcs_rows.T + jnp.where(sub == q - 1, dlast, 0.0)
        dda = _dot_hi(tril.T, dcs)
        ddt = ddt + dda * a_row
        a128_ref[1:2, :] += jnp.sum(dda * dt, axis=0, keepdims=True) * a_row
        draw = jnp.where(lane < N_HEADS, ddt * _sigmoid(dtr + bias_ref[...]), 0.0)
        a128_ref[0:1, :] += jnp.sum(draw, axis=0, keepdims=True)
        ddt_ref[...] = draw
        dact = jnp.concatenate([dx] + db_parts + dc_parts, axis=1)
        dpre_ref[...] = dact * _dsilu(pre_v)

    return _rowwise("ssd_bwd", body, t, q,
                    [('row', pre), ('col', rest, SSD_W, 2), ('col', rest, LANES, 20), ('row', y),
                     ('row', s_prev_all), ('col', d_mix, SSD_W, 0), ('full', dt_bias), ('full', a_log),
                     ('full', d_skip), ('full', norm_w)],
                    [('row', SSD_CONV, F32), ('row', SSD_W, F32), ('row', LANES, F32),
                     ('acc', (8, LANES), F32), ('acc', (8, SSD_W), F32)],
                    scratch=[pltpu.VMEM((SSD_STATE, SSD_W), F32)], reverse=True)


LRU_TM = 256


def _lru_gates(xc, wa, ba, wx, bx, lam):
    xb = xc.astype(BF16)
    r = _sigmoid(_dot(xb, wa) + ba)
    i = _sigmoid(_dot(xb, wx) + bx)
    sp = _softplus(-lam)
    a = jnp.exp(-LRU_C * r * sp)
    mult = jnp.sqrt(1.0 - a * a)
    return r, i, sp, a, mult


def _lru_fwd(xc, rest, wa, ba, wx, bx, lam):
    def body(i, nt, ins, outs, scr):
        xc_ref, g_ref, wa_ref, ba_ref, wx_ref, bx_ref, lam_ref = ins
        carry = scr[0]

        @pl.when(i == 0)
        def _():
            carry[...] = jnp.zeros(carry.shape, F32)

        xv = xc_ref[...]
        r, ig, sp, a, mult = _lru_gates(xv, wa_ref[...], ba_ref[...], wx_ref[...], bx_ref[...], lam_ref[...])
        u = mult * (ig * xv)
        row = _iota(a.shape, 0)
        s = 1
        while s < LRU_TM:
            a_sh = jnp.where(row >= s, pltpu.roll(a, s, 0), 1.0)
            u_sh = jnp.where(row >= s, pltpu.roll(u, s, 0), 0.0)
            u = a * u_sh + u
            a = a * a_sh
            s *= 2
        h = u + a * carry[0:1, :]
        carry[0:1, :] = h[LRU_TM - 1:LRU_TM, :]
        outs[1][...] = h
        outs[0][...] = h * _gelu(g_ref[...])

    return _rowwise("lru_fwd", body, xc.shape[0], LRU_TM,
                    [('row', xc), ('col', rest, LRU_W, 3), ('full', wa), ('full', ba), ('full', wx),
                     ('full', bx), ('full', lam)],
                    [('row', LRU_W, F32), ('row', LRU_W, F32)], scratch=[pltpu.VMEM((8, LRU_W), F32)])


def _lru_bwd(xc, rest, h, d_mix, wa, ba, wx, bx, lam, wa_t, wx_t):
    def body(i, nt, ins, outs, scr):
        xc_ref, g_ref, h_ref, hp_ref, do_ref, wa_ref, ba_ref, wx_ref, bx_ref, lam_ref, wat_ref, wxt_ref = ins
        dxc_ref, dg_ref, dza_ref, dzi_ref, acc_ref = outs
        carry = scr[0]

        @pl.when(i == nt - 1)
        def _():
            carry[...] = jnp.zeros(carry.shape, F32)

        xv, gv, hv, d_out = xc_ref[...], g_ref[...], h_ref[...], do_ref[...]
        r, ig, sp, a, mult = _lru_gates(xv, wa_ref[...], ba_ref[...], wx_ref[...], bx_ref[...], lam_ref[...])
        dg_ref[...] = d_out * hv * _dgelu(gv)
        gsum = d_out * _gelu(gv)
        row = _iota(a.shape, 0)
        b = jnp.where(row < LRU_TM - 1, pltpu.roll(a, LRU_TM - 1, 0), 1.0)
        s = 1
        while s < LRU_TM:
            keep = row < LRU_TM - s
            b_sh = jnp.where(keep, pltpu.roll(b, LRU_TM - s, 0), 1.0)
            g_sh = jnp.where(keep, pltpu.roll(gsum, LRU_TM - s, 0), 0.0)
            gsum = gsum + b * g_sh
            b = b * b_sh
            s *= 2
        dh = gsum + b * carry[0:1, :]
        carry[0:1, :] = a[0:1, :] * dh[0:1, :]
        h_prev = _shift_down(hv, 1, jnp.where(i > 0, hp_ref[...], 0.0))
        du = dh
        dmult = du * ig * xv
        di = du * mult * xv
        dxc = du * mult * ig
        da = dh * h_prev - dmult * a / mult
        dlog = da * a
        dr = dlog * (-LRU_C) * sp
        acc_ref[2:3, :] += jnp.sum(dlog * (-LRU_C) * r, axis=0, keepdims=True)
        dza = dr * r * (1.0 - r)
        dzi = di * ig * (1.0 - ig)
        acc_ref[0:1, :] += jnp.sum(dza, axis=0, keepdims=True)
        acc_ref[1:2, :] += jnp.sum(dzi, axis=0, keepdims=True)
        dzab, dzib = dza.astype(BF16), dzi.astype(BF16)
        dza_ref[...] = dzab
        dzi_ref[...] = dzib
        dxc_ref[...] = dxc + _dot(dzab, wat_ref[...]) + _dot(dzib, wxt_ref[...])

    return _rowwise("lru_bwd", body, xc.shape[0], LRU_TM,
                    [('row', xc), ('col', rest, LRU_W, 3), ('row', h), ('prev8', h, LRU_W, 0),
                     ('col', d_mix, LRU_W, 1), ('full', wa), ('full', ba), ('full', wx), ('full', bx),
                     ('full', lam), ('full', wa_t), ('full', wx_t)],
                    [('row', LRU_W, F32), ('row', LRU_W, F32), ('row', LRU_W, BF16), ('row', LRU_W, BF16),
                     ('acc', (8, LRU_W), F32)],
                    scratch=[pltpu.VMEM((8, LRU_W), F32)], reverse=True)


FFN_TM = 512
FFN_TN = 1536


def _ffn_up(name, h2, w_gu):
    t, k = h2.shape
    nh = D_FFP // FFN_TN

    def kern(a_ref, wg_ref, wu_ref, g_ref, u_ref, act_ref):
        a = a_ref[...].astype(BF16)
        gv = _dot(a, wg_ref[...])
        uv = _dot(a, wu_ref[...])
        g_ref[...] = gv
        u_ref[...] = uv
        act_ref[...] = (_silu(gv) * uv).astype(BF16)

    tile = pl.BlockSpec((FFN_TM, FFN_TN), lambda i, j: (i, j))
    return pl.pallas_call(
        kern, name=name, grid=(t // FFN_TM, nh),
        in_specs=[pl.BlockSpec((FFN_TM, k), lambda i, j: (i, 0)),
                  pl.BlockSpec((k, FFN_TN), lambda i, j: (0, j)),
                  pl.BlockSpec((k, FFN_TN), lambda i, j: (0, nh + j))],
        out_specs=[tile, tile, tile],
        out_shape=[jax.ShapeDtypeStruct((t, D_FFP), F32), jax.ShapeDtypeStruct((t, D_FFP), F32),
                   jax.ShapeDtypeStruct((t, D_FFP), BF16)],
        compiler_params=_params(2))(h2, w_gu, w_gu)


def _ffn_down_bwd(name, dx, w_down_t, gate, up):
    t, k = dx.shape

    def kern(dx_ref, w_ref, g_ref, u_ref, dg_ref, du_ref):
        da = _dot(dx_ref[...].astype(BF16), w_ref[...])
        gv, uv = g_ref[...], u_ref[...]
        dg_ref[...] = (da * uv * _dsilu(gv)).astype(BF16)
        du_ref[...] = (da * _silu(gv)).astype(BF16)

    tile = pl.BlockSpec((FFN_TM, FFN_TN), lambda i, j: (i, j))
    shp = jax.ShapeDtypeStruct((t, D_FFP), BF16)
    return pl.pallas_call(
        kern, name=name, grid=(t // FFN_TM, D_FFP // FFN_TN),
        in_specs=[pl.BlockSpec((FFN_TM, k), lambda i, j: (i, 0)),
                  pl.BlockSpec((k, FFN_TN), lambda i, j: (0, j)), tile, tile],
        out_specs=[tile, tile], out_shape=[shp, shp], compiler_params=_params(2))(dx, w_down_t, gate, up)


def _layer_fwd(x, w, l):
    tag = "_l%d" % l
    h = _rmsnorm_fwd("norm_mix" + tag, x, w['norm_mix'][l])
    qkv = _mm("proj_qkv" + tag, [h], w['w_qkv'][l], tn=768, out_dtype=BF16)
    rest = _mm("proj_rest" + tag, [h], w['w_rest'][l], tn=896)
    att, lse = _attention_fwd(qkv)
    pre = _conv_fwd("ssd_conv" + tag, rest, SSD_CONV, 0, w['ssd_conv_w'][l], w['ssd_conv_b'][l])
    ssd, y, s_prev = _ssd_fwd(pre, rest, w['ssd_dt_bias'][l], w['ssd_a_log'][l], w['ssd_d'][l], w['ssd_norm'][l])
    xc = _conv_fwd("lru_conv" + tag, rest, LRU_W, 4, w['lru_conv_w'][l], w['lru_conv_b'][l])
    lru, hl = _lru_fwd(xc, rest, w['lru_wa'][l], w['lru_ba'][l], w['lru_wx'][l], w['lru_bx'][l], w['lru_lambda'][l])
    x_mid = _mm("proj_out" + tag, [att, ssd, lru], w['w_out'][l], res=x, tn=512)
    h2 = _rmsnorm_fwd("norm_ffn" + tag, x_mid, w['norm_ffn'][l])
    gate, up, act = _ffn_up("proj_gu" + tag, h2, w['w_gu'][l])
    x_next = _mm("proj_down" + tag, [act], w['w_down'][l], res=x_mid, tn=512)
    saved = dict(x=x, h=h, qkv=qkv, rest=rest, att=att, lse=lse, pre=pre, ssd=ssd, y=y, s_prev=s_prev, xc=xc,
                 lru=lru, hl=hl, x_mid=x_mid, h2=h2, gate=gate, up=up, act=act)
    return x_next, saved


def _layer_bwd(dx_next, sv, w, l):
    tag = "_l%d_b" % l
    t = dx_next.shape[0]
    g = {}
    dgate, dup = _ffn_down_bwd("d_act" + tag, dx_next, w['w_down_t'][l], sv['gate'], sv['up'])
    g['w_down'] = _mm_tn("dw_down" + tag, sv['act'], dx_next, tk=1536)
    dh2 = _mm("d_h2" + tag, [dgate, dup], w['w_gu_t'][l], tn=512)
    g['w_gate'] = _mm_tn("dw_gate" + tag, sv['h2'], dgate, tn=1536)
    g['w_up'] = _mm_tn("dw_up" + tag, sv['h2'], dup, tn=1536)
    dx_mid, acc = _rmsnorm_bwd("norm_ffn" + tag, dh2, sv['x_mid'], w['norm_ffn'][l], dx_next)
    g['norm_ffn'] = acc[0]
    d_att = _mm("d_att" + tag, [dx_mid], w['w_out_t'][l][:, :ATT_W], tn=512)
    d_mix = _mm("d_mix" + tag, [dx_mid], w['w_out_t'][l][:, ATT_W:], tn=512)
    g['w_out'] = jnp.concatenate([_mm_tn("dw_out%d" % k + tag, a, dx_mid)
                                  for k, a in enumerate((sv['att'], sv['ssd'], sv['lru']))], axis=0)
    dxc, dgl, dza, dzi, acc = _lru_bwd(sv['xc'], sv['rest'], sv['hl'], d_mix, w['lru_wa'][l], w['lru_ba'][l],
                                       w['lru_wx'][l], w['lru_bx'][l], w['lru_lambda'][l],
                                       w['lru_wa_t'][l], w['lru_wx_t'][l])
    g['lru_ba'], g['lru_bx'] = acc[0], acc[1]
    g['lru_lambda'] = acc[2] * (-_sigmoid(-w['lru_lambda'][l][0]))
    g['lru_wa'] = _diag_blocks(_mm_tn("dw_lru_a" + tag, sv['xc'], dza))
    g['lru_wx'] = _diag_blocks(_mm_tn("dw_lru_x" + tag, sv['xc'], dzi))
    dxl, acc = _conv_bwd("lru_conv" + tag, dxc, sv['rest'], LRU_W, 4, w['lru_conv_w'][l])
    g['lru_conv_w'], g['lru_conv_b'] = acc[:4], acc[4]
    dpre, dz, ddt, a128, a512 = _ssd_bwd(sv['pre'], sv['rest'], sv['y'], sv['s_prev'], d_mix, w['ssd_dt_bias'][l],
                                         w['ssd_a_log'][l], w['ssd_d'][l], w['ssd_norm'][l])
    g['ssd_dt_bias'], g['ssd_a_log'], g['ssd_d'] = a128[0, :N_HEADS], a128[1, :N_HEADS], a128[2, :N_HEADS]
    g['ssd_norm'] = a512[0]
    dxbc, acc = _conv_bwd("ssd_conv" + tag, dpre, sv['rest'], SSD_CONV, 0, w['ssd_conv_w'][l])
    g['ssd_conv_w'], g['ssd_conv_b'] = acc[:4], acc[4]
    dq, dk, dv = _attention_bwd(sv['qkv'], d_att, sv['att'], sv['lse'])
    pieces = [dq, dk, dv, dxbc, dz, dgl, dxl, ddt]
    dh = _mm("d_h" + tag, pieces, w['w_in_t'][l], tn=512)
    dws = [_mm_tn("dw_in%d" % k + tag, sv['h'], p) for k, p in enumerate(pieces)]
    g['w_in'] = jnp.concatenate([dws[0], dws[1], dws[2], dws[4], dws[3], _dt_tile_place(dws[7]), dws[5], dws[6]],
                                axis=1)
    dx, acc = _rmsnorm_bwd("norm_mix" + tag, dh, sv['x'], w['norm_mix'][l], dx_mid)
    g['norm_mix'] = acc[0]
    return dx, g


def _diag_blocks(m):
    return jnp.stack([m[64 * n:64 * (n + 1), 64 * n:64 * (n + 1)] for n in range(8)])


def _block_diag(w):
    eye = jnp.eye(8, dtype=w.dtype)
    return (w[:, :, None, :] * eye[:, None, :, None]).reshape(512, 512)


_ANY = pl.BlockSpec(memory_space=pl.ANY)
_MESH = pl.DeviceIdType.MESH


def _all_gather(name, xs):
    n = len(xs)

    def body(*refs):
        x_refs, out_refs = refs[:n], refs[n:2 * n]
        send_sems, recv_sems, local_sems = refs[2 * n:]
        x_, y_, c_ = lax.axis_index("x"), lax.axis_index("y"), lax.axis_index("c")
        me, sibling = (x_, y_, c_), (x_, y_, 1 - c_)
        chips = [(1 - x_, y_), (x_, 1 - y_), (1 - x_, 1 - y_)]

        def slot(a, px, py, pc):
            return out_refs[a].at[4 * px + 2 * py + pc]

        def copy(a, k, block, to, src=None):
            return pltpu.make_async_remote_copy(
                src_ref=slot(a, *block) if src is None else src, dst_ref=slot(a, *block),
                send_sem=send_sems.at[a, k], recv_sem=recv_sems.at[a, k], device_id=to, device_id_type=_MESH)

        mine = [pltpu.make_async_copy(x_refs[a], slot(a, *me), local_sems.at[a]) for a in range(n)]
        for cp in mine:
            cp.start()
        first = []
        for a in range(n):
            first.append(copy(a, 0, me, sibling, src=x_refs[a]))
            first += [copy(a, 1 + j, me, (*chip, c_), src=x_refs[a]) for j, chip in enumerate(chips)]
        for cp in first:
            cp.start()
        passed = []
        for j, chip in enumerate(chips):
            for a in range(n):
                copy(a, 1 + j, (*chip, c_), me).wait_recv()
                fwd = copy(a, 4 + j, (*chip, c_), sibling)
                fwd.start()
                passed.append(fwd)
        for a in range(n):
            copy(a, 0, sibling, me).wait_recv()
            for j, chip in enumerate(chips):
                copy(a, 4 + j, (*chip, 1 - c_), me).wait_recv()
        for cp in first + passed:
            cp.wait_send()
        for cp in mine:
            cp.wait()

    return pl.pallas_call(
        body, name=name, out_shape=[jax.ShapeDtypeStruct((N_DEV,) + x.shape, x.dtype) for x in xs],
        in_specs=[_ANY] * n, out_specs=[_ANY] * n,
        scratch_shapes=[pltpu.SemaphoreType.DMA((n, 7)), pltpu.SemaphoreType.DMA((n, 7)),
                        pltpu.SemaphoreType.DMA((n,))],
    )(*xs)


def _all_to_all(name, xs):
    n = len(xs)

    def body(*refs):
        x_refs, out_refs = refs[:n], refs[n:2 * n]
        send_sems, recv_sems, local_sems = refs[2 * n:]
        x_, y_, c_ = lax.axis_index("x"), lax.axis_index("y"), lax.axis_index("c")
        me = 4 * x_ + 2 * y_ + c_

        def peer(k):
            return ((1 - x_) if k & 4 else x_, (1 - y_) if k & 2 else y_, (1 - c_) if k & 1 else c_)

        def copy(a, k):
            px, py, pc = peer(k)
            return pltpu.make_async_remote_copy(
                src_ref=x_refs[a].at[4 * px + 2 * py + pc], dst_ref=out_refs[a].at[me],
                send_sem=send_sems.at[a, k - 1], recv_sem=recv_sems.at[a, k - 1],
                device_id=(px, py, pc), device_id_type=_MESH)

        def arrival(a, k):
            px, py, pc = peer(k)
            return pltpu.make_async_remote_copy(
                src_ref=x_refs[a].at[me], dst_ref=out_refs[a].at[4 * px + 2 * py + pc],
                send_sem=send_sems.at[a, k - 1], recv_sem=recv_sems.at[a, k - 1],
                device_id=(px, py, pc), device_id_type=_MESH)

        mine = [pltpu.make_async_copy(x_refs[a].at[me], out_refs[a].at[me], local_sems.at[a]) for a in range(n)]
        for cp in mine:
            cp.start()
        copies = [copy(a, k) for a in range(n) for k in range(1, N_DEV)]
        for cp in copies:
            cp.start()
        for a in range(n):
            for k in range(1, N_DEV):
                arrival(a, k).wait_recv()
        for cp in copies:
            cp.wait_send()
        for cp in mine:
            cp.wait()

    return pl.pallas_call(
        body, name=name, out_shape=[jax.ShapeDtypeStruct(x.shape, x.dtype) for x in xs],
        in_specs=[_ANY] * n, out_specs=[_ANY] * n,
        scratch_shapes=[pltpu.SemaphoreType.DMA((n, 7)), pltpu.SemaphoreType.DMA((n, 7)),
                        pltpu.SemaphoreType.DMA((n,))],
    )(*xs)


def _window_offset():
    me = 4 * lax.axis_index("x") + 2 * lax.axis_index("y") + lax.axis_index("c")
    return jnp.where(me < 6, me, me + 120)


def _place_w_in(shard):
    def body(i, nt, ins, outs, scr):
        outs[0][...] = pltpu.roll(ins[0][...], _window_offset(), 1).astype(BF16)

    return _rowwise("place_w_in", body, shard.shape[0], 256, [('row', shard)], [('row', IN_WIN, BF16)])[0]


def _adamw(name, w, m, v, g, tr, from_window=False):
    s_parts, r, c = g.shape
    assert r % tr == 0

    def kern(w_ref, m_ref, v_ref, g_ref, go_ref, d_ref, mo_ref, vo_ref):
        gs = g_ref[0].astype(F32)
        for s in range(1, s_parts):
            gs = gs + g_ref[s].astype(F32)
        if from_window:
            gs = pltpu.roll(gs, c - _window_offset(), 1)
        wv = w_ref[...]
        m2 = ADAM_B1 * m_ref[...] + (1.0 - ADAM_B1) * gs
        v2 = ADAM_B2 * v_ref[...] + (1.0 - ADAM_B2) * (gs * gs)
        m_hat = m2 / (1.0 - ADAM_B1 ** ADAM_STEP)
        v_hat = v2 / (1.0 - ADAM_B2 ** ADAM_STEP)
        go_ref[...] = gs
        d_ref[...] = -ADAM_LR * (m_hat / (jnp.sqrt(v_hat) + ADAM_EPS) + ADAM_WD * wv)
        mo_ref[...] = m2
        vo_ref[...] = v2

    spec = pl.BlockSpec((tr, c), lambda i: (i, 0))
    shp = jax.ShapeDtypeStruct((r, c), F32)
    return pl.pallas_call(kern, name=name, grid=(r // tr,),
                          in_specs=[spec, spec, spec, pl.BlockSpec((s_parts, tr, c), lambda i: (0, i, 0))],
                          out_specs=[spec] * 4, out_shape=[shp] * 4, compiler_params=_params(1))(w, m, v, g)


def _pack(arrs, rows, lead=0):
    parts = []
    for a in arrs:
        flat = a.reshape(a.shape[:lead] + (-1,))
        pad = (-flat.shape[-1]) % LANES
        if pad:
            flat = jnp.pad(flat, [(0, 0)] * lead + [(0, pad)])
        parts.append(flat)
    flat = jnp.concatenate(parts, axis=-1)
    pad = rows * LANES - flat.shape[-1]
    assert pad >= 0
    if pad:
        flat = jnp.pad(flat, [(0, 0)] * lead + [(0, pad)])
    return flat.reshape(flat.shape[:lead] + (rows, LANES))


def _unpack(buf, shapes, lead=0):
    flat = buf.reshape(buf.shape[:lead] + (-1,))
    out, off = [], 0
    for shp in shapes:
        n = math.prod(shp)
        out.append(flat[..., off:off + n].reshape(buf.shape[:lead] + tuple(shp)))
        off += n + ((-n) % LANES)
    return out


BIG = ('w_in', 'w_out', 'w_gate', 'w_up', 'w_down')
CONV =('ssd_conv_w', 'lru_conv_w')
CONV_SHARD_SHAPES = ((DEPTH, 4, SSD_CONV // N_DEV), (DEPTH, 4, LRU_W // N_DEV))
CONV_ROWS = 16
SMALL = ('norm_mix', 'ssd_conv_b', 'ssd_dt_bias', 'ssd_a_log', 'ssd_d', 'ssd_norm', 'lru_conv_b', 'lru_wa',
         'lru_ba', 'lru_wx', 'lru_bx', 'lru_lambda', 'norm_ffn', 'norm_final')
SMALL_ROWS = 1280
SMALL_TILE = 256
WEIGHTS = ('norm_mix', 'w_in', 'ssd_conv_w', 'ssd_conv_b', 'ssd_dt_bias', 'ssd_a_log', 'ssd_d', 'ssd_norm',
           'lru_conv_w', 'lru_conv_b', 'lru_wa', 'lru_ba', 'lru_wx', 'lru_bx', 'lru_lambda', 'w_out', 'norm_ffn',
           'w_gate', 'w_up', 'w_down', 'norm_final')


def _join_cols(a):
    return jnp.transpose(a, (1, 2, 0, 3)).reshape(a.shape[1], a.shape[2], -1)


def _join_rows(a):
    return jnp.transpose(a, (1, 0, 2, 3)).reshape(a.shape[1], -1, a.shape[3])


def _split_cols(a):
    l, r, c = a.shape
    return jnp.transpose(a.reshape(l, r, N_DEV, c // N_DEV), (2, 0, 1, 3)).reshape(N_DEV, l * r, c // N_DEV)


def _split_rows(a):
    l, r, c = a.shape
    return jnp.transpose(a.reshape(l, N_DEV, r // N_DEV, c), (1, 0, 2, 3)).reshape(N_DEV, l * r // N_DEV, c)


def _shard_form(k, a):
    if k == 'w_in':
        return jnp.pad(a.reshape(-1, IN_SHARD), ((0, 0), (0, IN_WIN - IN_SHARD)))
    if k in ('w_gate', 'w_up'):
        return jnp.pad(a.reshape(-1, FF_SHARD), ((0, 0), (0, FF_SHARD_P - FF_SHARD)))
    if k == 'w_down':
        return jnp.pad(a, ((0, 0), (0, FF_SHARD_P - FF_SHARD), (0, 0))).reshape(-1, D_MODEL)
    return a.reshape(-1, D_MODEL)


def _shard_back(k, a):
    if k == 'w_in':
        return a[:, :IN_SHARD].reshape(DEPTH, D_MODEL, IN_SHARD)
    if k in ('w_gate', 'w_up'):
        return a[:, :FF_SHARD].reshape(DEPTH, D_MODEL, FF_SHARD)
    if k == 'w_down':
        return a.reshape(DEPTH, FF_SHARD_P, D_MODEL)[:, :FF_SHARD]
    return a.reshape(DEPTH, D_MIX // N_DEV, D_MODEL)


def _dt_tile_place(a):
    zeros = jnp.zeros(a.shape[:-1] + (LANES - N_HEADS,), a.dtype)
    return jnp.concatenate([a[..., :6], zeros, a[..., 6:8]], axis=-1)


def _dt_tile_heads(tile):
    zeros = jnp.zeros(tile.shape[:-1] + (LANES - N_HEADS,), tile.dtype)
    return jnp.concatenate([tile[..., :6], tile[..., 126:128], zeros], axis=-1)


def _layout_from_windows(win):
    r = win.shape[1]
    main = jnp.concatenate([win[j][:, :512] for j in range(N_DEV)] + [jnp.zeros((r, LANES), win.dtype)], axis=1)
    gap = jnp.zeros((r, 384), win.dtype)
    tails = [jnp.zeros((r, 512), win.dtype)]
    for j in range(N_DEV - 1):
        tails += [win[j][:, 512:], gap]
    tails.append(win[N_DEV - 1][:, 512:])
    return main + jnp.concatenate(tails, axis=1)


def _prepare_weights(p, full):
    w = {}
    w_in = full['w_in']
    w_qkv = w_in[:, :, :D_MIX]
    dt_cols = _dt_tile_heads(w_in[:, :, 3072:3200])
    w_rest = jnp.concatenate([w_in[:, :, 2048:3072], w_in[:, :, 1536:2048], w_in[:, :, 3200:3712],
                              w_in[:, :, 3712:4224], dt_cols], axis=2)
    w['w_qkv'], w['w_rest'] = w_qkv, w_rest
    w['w_in_t'] = jnp.transpose(jnp.concatenate([w_qkv, w_rest], axis=2), (0, 2, 1))
    w['w_out'] = full['w_out']
    w['w_out_t'] = jnp.transpose(full['w_out'], (0, 2, 1))
    w['w_gu'] = jnp.concatenate([full['w_gate'], full['w_up']], axis=2)
    w['w_gu_t'] = jnp.transpose(w['w_gu'], (0, 2, 1))
    w['w_down'] = full['w_down']
    w['w_down_t'] = jnp.transpose(full['w_down'], (0, 2, 1))
    for k in ('norm_mix', 'ssd_conv_b', 'ssd_norm', 'lru_conv_b', 'lru_ba', 'lru_bx', 'lru_lambda', 'norm_ffn'):
        w[k] = p[k][:, None, :]
    for k in ('ssd_dt_bias', 'ssd_a_log', 'ssd_d'):
        w[k] = jnp.pad(p[k], ((0, 0), (0, LANES - N_HEADS)))[:, None, :]
    for k in CONV:
        w[k] = jnp.pad(full[k], ((0, 0), (0, 4), (0, 0)))
    for k in ('lru_wa', 'lru_wx'):
        bd = jnp.stack([_block_diag(p[k][l]) for l in range(DEPTH)]).astype(BF16)
        w[k] = bd
        w[k + '_t'] = jnp.transpose(bd, (0, 2, 1))
    return w


def _local_step(x, target, w, norm_final):
    saved = []
    for l in range(DEPTH):
        x, sv = _layer_fwd(x, w, l)
        saved.append(sv)
    dx, loss_acc, dgf = _final_loss(x, norm_final[None, :], target)
    grads = [None] * DEPTH
    for l in reversed(range(DEPTH)):
        dx, grads[l] = _layer_bwd(dx, saved[l], w, l)
    g = {k: jnp.stack([grads[l][k] for l in range(DEPTH)]) for k in grads[0]}
    g['norm_final'] = dgf[0]
    return loss_acc[0, 0], dx, g


def kernel(x, norm_mix, w_in, ssd_conv_w, ssd_conv_b, ssd_dt_bias, ssd_a_log, ssd_d, ssd_norm, lru_conv_w, lru_conv_b, lru_wa, lru_ba, lru_wx, lru_bx, lru_lambda, w_out, norm_ffn, w_gate, w_up, w_down, norm_final, loss_target, m_norm_mix, m_w_in, m_ssd_conv_w, m_ssd_conv_b, m_ssd_dt_bias, m_ssd_a_log, m_ssd_d, m_ssd_norm, m_lru_conv_w, m_lru_conv_b, m_lru_wa, m_lru_ba, m_lru_wx, m_lru_bx, m_lru_lambda, m_w_out, m_norm_ffn, m_w_gate, m_w_up, m_w_down, m_norm_final, v_norm_mix, v_w_in, v_ssd_conv_w, v_ssd_conv_b, v_ssd_dt_bias, v_ssd_a_log, v_ssd_d, v_ssd_norm, v_lru_conv_w, v_lru_conv_b, v_lru_wa, v_lru_ba, v_lru_wx, v_lru_bx, v_lru_lambda, v_w_out, v_norm_ffn, v_w_gate, v_w_up, v_w_down, v_norm_final):
    args = (norm_mix, w_in, ssd_conv_w, ssd_conv_b, ssd_dt_bias, ssd_a_log, ssd_d, ssd_norm, lru_conv_w, lru_conv_b, lru_wa, lru_ba, lru_wx, lru_bx, lru_lambda, w_out, norm_ffn, w_gate, w_up, w_down, norm_final)
    margs = (m_norm_mix, m_w_in, m_ssd_conv_w, m_ssd_conv_b, m_ssd_dt_bias, m_ssd_a_log, m_ssd_d, m_ssd_norm, m_lru_conv_w, m_lru_conv_b, m_lru_wa, m_lru_ba, m_lru_wx, m_lru_bx, m_lru_lambda, m_w_out, m_norm_ffn, m_w_gate, m_w_up, m_w_down, m_norm_final)
    vargs = (v_norm_mix, v_w_in, v_ssd_conv_w, v_ssd_conv_b, v_ssd_dt_bias, v_ssd_a_log, v_ssd_d, v_ssd_norm, v_lru_conv_w, v_lru_conv_b, v_lru_wa, v_lru_ba, v_lru_wx, v_lru_bx, v_lru_lambda, v_w_out, v_norm_ffn, v_w_gate, v_w_up, v_w_down, v_norm_final)
    p = dict(zip(WEIGHTS, args))
    pm = dict(zip(WEIGHTS, margs))
    pv = dict(zip(WEIGHTS, vargs))

    forms = {k: _shard_form(k, p[k]) for k in BIG}
    send = [_place_w_in(forms['w_in'])] + [forms[k].astype(BF16) for k in BIG[1:]]
    got = _all_gather("gather_weights", send + [_pack([p[k] for k in CONV], CONV_ROWS)])
    full = {'w_in': _layout_from_windows(got[0]).reshape(DEPTH, D_MODEL, IN_COLS_P),
            'w_out': _join_rows(got[1].reshape(N_DEV, DEPTH, D_MIX // N_DEV, D_MODEL)),
            'w_gate': _join_cols(got[2].reshape(N_DEV, DEPTH, D_MODEL, FF_SHARD_P)),
            'w_up': _join_cols(got[3].reshape(N_DEV, DEPTH, D_MODEL, FF_SHARD_P)),
            'w_down': _join_rows(got[4].reshape(N_DEV, DEPTH, FF_SHARD_P, D_MODEL))}
    for k, a in zip(CONV, _unpack(got[5], CONV_SHARD_SHAPES, lead=1)):
        full[k] = _join_cols(a)
    w = _prepare_weights(p, full)

    loss_local, dx, g = _local_step(x[0], loss_target[0], w, norm_final)
    loss = lax.psum(loss_local, ("x", "y", "c"))

    small_g = _all_gather("gather_small_grads", [_pack([g[k] for k in SMALL + CONV], SMALL_ROWS)])[0]
    zeros = [jnp.zeros_like(g[k]) for k in CONV]
    res_small = _adamw("adamw_small", _pack([p[k] for k in SMALL] + zeros, SMALL_ROWS),
                       _pack([pm[k] for k in SMALL] + zeros, SMALL_ROWS),
                       _pack([pv[k] for k in SMALL] + zeros, SMALL_ROWS), small_g, SMALL_TILE)
    small_shapes = [g[k].shape for k in SMALL + CONV]
    out = {kind: {} for kind in range(4)}
    for kind in range(4):
        for k, a in zip(SMALL + CONV, _unpack(res_small[kind], small_shapes)):
            out[kind][k] = a
    me = 4 * lax.axis_index("x") + 2 * lax.axis_index("y") + lax.axis_index("c")
    conv_g = []
    for k, shp in zip(CONV, CONV_SHARD_SHAPES):
        conv_g.append(lax.dynamic_slice_in_dim(out[0][k], me * shp[2], shp[2], axis=2))
    res_conv = _adamw("adamw_conv", _pack([p[k] for k in CONV], CONV_ROWS), _pack([pm[k] for k in CONV], CONV_ROWS),
                      _pack([pv[k] for k in CONV], CONV_ROWS), _pack(conv_g, CONV_ROWS)[None], CONV_ROWS)
    for kind in range(4):
        for k, a in zip(CONV, _unpack(res_conv[kind], CONV_SHARD_SHAPES)):
            out[kind][k] = a

    g_in = g['w_in'].reshape(DEPTH * D_MODEL, IN_COLS_P)
    dest = [jnp.stack([g_in[:, 512 * j:512 * j + IN_WIN] for j in range(N_DEV)]), _split_rows(g['w_out']),
            _split_cols(g['w_gate']), _split_cols(g['w_up']), _split_rows(g['w_down'])]
    parts = _all_to_all("exchange_big_grads", [a.astype(BF16) for a in dest])
    tiles = {'w_in': 256, 'w_out': 128, 'w_gate': 512, 'w_up': 512, 'w_down': 256}
    for k, part in zip(BIG, parts):
        res = _adamw("adamw_" + k, forms[k], _shard_form(k, pm[k]), _shard_form(k, pv[k]), part, tiles[k],
                     from_window=(k == 'w_in'))
        for kind in range(4):
            out[kind][k] = _shard_back(k, res[kind])

    outs = [loss, dx[None]]
    for kind in range(4):
        outs += [out[kind][k] for k in WEIGHTS]
    return tuple(outs)
```

```python
import functools
import math

import jax
import jax.numpy as jnp
from jax import lax
from jax.experimental import pallas as pl
from jax.experimental.pallas import tpu as pltpu

F32 = jnp.float32
BF16 = jnp.bfloat16

N_DEV = 8
DEPTH = 2
D_MODEL = 1024
ATT_W = 512
HEAD_DIM = 64
N_HEADS = 8
ATT_BLOCK = 128
ATT_DILATIONS = (16, 4, 1)
SSD_W = 512
SSD_STATE = 128
SSD_CONV = 1024
SSD_CHUNK = 128
LRU_W = 512
LRU_C = 8.0
D_MIX = 1536
D_FF = 2816
FF_SHARD = D_FF // N_DEV
FF_SHARD_P = 384
D_FFP = N_DEV * FF_SHARD_P
IN_COLS = 4104
IN_SHARD = IN_COLS // N_DEV
IN_WIN = 640
IN_COLS_P = 4224
REST_COLS = 2688
NORM_EPS = 1e-6
SSD_NORM_EPS = 1e-5
NEG = -1e30

ADAM_LR = 0.001
ADAM_B1 = 0.9
ADAM_B2 = 0.999
ADAM_EPS = 1e-08
ADAM_WD = 0.01
ADAM_STEP = 10

LANES = 128
VMEM_LIMIT = 52 * 1024 * 1024
HI = lax.Precision.HIGHEST


def _sigmoid(x):
    return 1.0 / (1.0 + jnp.exp(-x))


def _silu(x):
    return x * _sigmoid(x)


def _dsilu(x):
    s = _sigmoid(x)
    return s * (1.0 + x * (1.0 - s))


def _softplus(x):
    return jnp.maximum(x, 0.0) + jnp.log(1.0 + jnp.exp(-jnp.abs(x)))


_GELU_C = math.sqrt(2.0 / math.pi)


def _gelu(x):
    return 0.5 * x * (1.0 + jnp.tanh(_GELU_C * (x + 0.044715 * x * x * x)))


def _dgelu(x):
    t = jnp.tanh(_GELU_C * (x + 0.044715 * x * x * x))
    return 0.5 * (1.0 + t) + 0.5 * x * (1.0 - t * t) * _GELU_C * (1.0 + 3.0 * 0.044715 * x * x)


def _dot(a, b):
    return jnp.dot(a, b, preferred_element_type=F32)


def _dot_nt(a, b):
    return lax.dot_general(a, b, (((1,), (1,)), ((), ())), preferred_element_type=F32)


def _dot_tn(a, b):
    return lax.dot_general(a, b, (((0,), (0,)), ((), ())), preferred_element_type=F32)


def _dot_hi(a, b):
    return jnp.dot(a, b, preferred_element_type=F32, precision=HI)


def _iota(shape, axis):
    return lax.broadcasted_iota(jnp.int32, shape, axis)


def _shift_down(x, s, prev8):
    xs = pltpu.roll(x, s, 0)
    ps = pltpu.roll(prev8, s, 0)
    top = jnp.concatenate([ps, x[8:]], axis=0)
    return jnp.where(_iota(x.shape, 0) < s, top, xs)


def _shift_up(x, s, next8):
    tm = x.shape[0]
    xs = pltpu.roll(x, tm - s, 0)
    ns = pltpu.roll(next8, 8 - s, 0)
    bottom = jnp.concatenate([x[:tm - 8], ns], axis=0)
    return jnp.where(_iota(x.shape, 0) >= tm - s, bottom, xs)


def _expand_mat():
    return jnp.where(_iota((LANES, SSD_W), 1) // HEAD_DIM == _iota((LANES, SSD_W), 0), 1.0, 0.0).astype(F32)


def _reduce_mat():
    return jnp.where(_iota((SSD_W, LANES), 0) // HEAD_DIM == _iota((SSD_W, LANES), 1), 1.0, 0.0).astype(F32)


def _params(n_grid):
    return pltpu.CompilerParams(dimension_semantics=("arbitrary",) * n_grid, vmem_limit_bytes=VMEM_LIMIT)


def _rowwise(name, body, n_rows, tm, ins, outs, scratch=(), reverse=False):
    nt = n_rows // tm
    assert nt * tm == n_rows and tm % 8 == 0
    r8 = tm // 8
    last8 = n_rows // 8 - 1

    def pos(s):
        return (nt - 1 - s) if reverse else s

    in_specs, args = [], []
    for spec in ins:
        kind, arr = spec[0], spec[1]
        args.append(arr)
        if kind == 'row':
            in_specs.append(pl.BlockSpec((tm, arr.shape[1]), lambda s: (pos(s), 0)))
        elif kind == 'col':
            in_specs.append(pl.BlockSpec((tm, spec[2]), functools.partial(lambda s, j: (pos(s), j), j=spec[3])))
        elif kind == 'full':
            in_specs.append(pl.BlockSpec(arr.shape, functools.partial(lambda s, n: (0,) * n, n=arr.ndim)))
        elif kind == 'prev8':
            in_specs.append(pl.BlockSpec((8, spec[2]), functools.partial(
                lambda s, j: (jnp.maximum(pos(s) * r8 - 1, 0), j), j=spec[3])))
        elif kind == 'next8':
            in_specs.append(pl.BlockSpec((8, spec[2]), functools.partial(
                lambda s, j: (jnp.minimum((pos(s) + 1) * r8, last8), j), j=spec[3])))
        else:
            raise ValueError(kind)
    out_specs, out_shape, acc_idx = [], [], []
    for k, spec in enumerate(outs):
        if spec[0] == 'row':
            out_specs.append(pl.BlockSpec((tm, spec[1]), lambda s: (pos(s), 0)))
            out_shape.append(jax.ShapeDtypeStruct((n_rows, spec[1]), spec[2]))
        else:
            out_specs.append(pl.BlockSpec(spec[1], lambda s: (0, 0)))
            out_shape.append(jax.ShapeDtypeStruct(spec[1], spec[2]))
            acc_idx.append(k)
    n_in, n_out = len(ins), len(outs)

    def kern(*refs):
        s = pl.program_id(0)
        in_refs, out_refs, scr = refs[:n_in], refs[n_in:n_in + n_out], refs[n_in + n_out:]

        @pl.when(s == 0)
        def _():
            for k in acc_idx:
                out_refs[k][...] = jnp.zeros(out_refs[k].shape, out_refs[k].dtype)

        body(pos(s), nt, in_refs, out_refs, scr)

    res = pl.pallas_call(kern, name=name, grid=(nt,), in_specs=in_specs, out_specs=out_specs,
                         out_shape=out_shape, scratch_shapes=list(scratch), compiler_params=_params(1))(*args)
    return res


def _mm(name, a_list, b, *, res=None, out_dtype=F32, tm=512, tn=None):
    n_rows = a_list[0].shape[0]
    k_total, n = b.shape
    ks = [a.shape[1] for a in a_list]
    assert sum(ks) == k_total
    tn = n if tn is None else tn
    assert n_rows % tm == 0 and n % tn == 0
    na = len(a_list)

    def kern(*refs):
        a_refs, b_ref, o_ref = refs[:na], refs[na], refs[-1]
        acc, off = None, 0
        for a_ref, kp in zip(a_refs, ks):
            part = _dot(a_ref[...].astype(BF16), b_ref[off:off + kp, :])
            acc = part if acc is None else acc + part
            off += kp
        if res is not None:
            acc = acc + refs[na + 1][...]
        o_ref[...] = acc.astype(out_dtype)

    in_specs = [pl.BlockSpec((tm, kp), lambda i, j: (i, 0)) for kp in ks]
    in_specs.append(pl.BlockSpec((k_total, tn), lambda i, j: (0, j)))
    args = list(a_list) + [b]
    if res is not None:
        in_specs.append(pl.BlockSpec((tm, tn), lambda i, j: (i, j)))
        args.append(res)
    return pl.pallas_call(kern, name=name, grid=(n_rows // tm, n // tn), in_specs=in_specs,
                          out_specs=pl.BlockSpec((tm, tn), lambda i, j: (i, j)),
                          out_shape=jax.ShapeDtypeStruct((n_rows, n), out_dtype),
                          compiler_params=_params(2))(*args)


def _mm_tn(name, a, g, *, a_col=None, g_col=None, tk=None, tn=None, tt=512):
    n_rows = a.shape[0]
    k = a.shape[1] if a_col is None else a_col[0]
    a_j = 0 if a_col is None else a_col[1]
    n = g.shape[1] if g_col is None else g_col[0]
    g_j = 0 if g_col is None else g_col[1]
    tk = k if tk is None else tk
    tn = n if tn is None else tn
    assert k % tk == 0 and n % tn == 0 and n_rows % tt == 0
    kb = k // tk
    nbk = n // tn

    def kern(a_ref, g_ref, o_ref):
        t = pl.program_id(2)

        @pl.when(t == 0)
        def _():
            o_ref[...] = jnp.zeros(o_ref.shape, F32)

        o_ref[...] += _dot_tn(a_ref[...].astype(BF16), g_ref[...].astype(BF16))

    return pl.pallas_call(
        kern, name=name, grid=(kb, n // tn, n_rows // tt),
        in_specs=[pl.BlockSpec((tt, tk), lambda i, j, t: (t, a_j * kb + i)),
                  pl.BlockSpec((tt, tn), lambda i, j, t: (t, g_j * nbk + j))],
        out_specs=pl.BlockSpec((tk, tn), lambda i, j, t: (i, j)),
        out_shape=jax.ShapeDtypeStruct((k, n), F32), compiler_params=_params(3))(a, g)


def _rmsnorm_fwd(name, x, g):
    def body(i, nt, ins, outs, scr):
        xv = ins[0][...]
        rstd = lax.rsqrt(jnp.mean(xv * xv, axis=-1, keepdims=True) + NORM_EPS)
        outs[0][...] = (xv * rstd * ins[1][...]).astype(BF16)

    return _rowwise(name, body, x.shape[0], 512, [('row', x), ('full', g)], [('row', x.shape[1], BF16)])[0]


def _rmsnorm_bwd(name, dh, x, g, dres):
    d = x.shape[1]

    def body(i, nt, ins, outs, scr):
        dy, xv, gv, dr = ins[0][...], ins[1][...], ins[2][...], ins[3][...]
        rstd = lax.rsqrt(jnp.mean(xv * xv, axis=-1, keepdims=True) + NORM_EPS)
        xhat = xv * rstd
        outs[1][0:1, :] += jnp.sum(dy * xhat, axis=0, keepdims=True)
        dxh = dy * gv
        outs[0][...] = dr + rstd * (dxh - xhat * jnp.mean(dxh * xhat, axis=-1, keepdims=True))

    return _rowwise(name, body, x.shape[0], 512, [('row', dh), ('row', x), ('full', g), ('row', dres)],
                    [('row', d, F32), ('acc', (8, d), F32)])


def _final_loss(x, g, target):
    d = x.shape[1]

    def body(i, nt, ins, outs, scr):
        xv, gv, tv = ins[0][...], ins[1][...], ins[2][...]
        rstd = lax.rsqrt(jnp.mean(xv * xv, axis=-1, keepdims=True) + NORM_EPS)
        xhat = xv * rstd
        err = xhat * gv - tv
        row_loss = 0.5 * jnp.mean(err * err, axis=-1, keepdims=True)
        outs[1][...] += jnp.sum(row_loss, axis=0, keepdims=True)
        dy = err * (1.0 / d)
        outs[2][0:1, :] += jnp.sum(dy * xhat, axis=0, keepdims=True)
        dxh = dy * gv
        outs[0][...] = rstd * (dxh - xhat * jnp.mean(dxh * xhat, axis=-1, keepdims=True))

    return _rowwise("final_loss", body, x.shape[0], 512, [('row', x), ('full', g), ('row', target)],
                    [('row', d, F32), ('acc', (8, LANES), F32), ('acc', (8, d), F32)])


def _conv_fwd(name, src, width, idx, w, b):
    def body(i, nt, ins, outs, scr):
        xv = ins[0][...]
        prev = jnp.where(i > 0, ins[1][...], 0.0)
        wv = ins[2][...]
        y = ins[3][...] + wv[3:4, :] * xv
        for s in (1, 2, 3):
            y = y + wv[3 - s:4 - s, :] * _shift_down(xv, s, prev)
        outs[0][...] = y

    return _rowwise(name, body, src.shape[0], 512,
                    [('col', src, width, idx), ('prev8', src, width, idx), ('full', w), ('full', b)],
                    [('row', width, F32)])[0]


def _conv_bwd(name, dpre, src, width, idx, w):
    def body(i, nt, ins, outs, scr):
        dy = ins[0][...]
        nxt = jnp.where(i < nt - 1, ins[1][...], 0.0)
        xv = ins[2][...]
        prev = jnp.where(i > 0, ins[3][...], 0.0)
        wv = ins[4][...]
        dx = wv[3:4, :] * dy
        outs[1][3:4, :] += jnp.sum(dy * xv, axis=0, keepdims=True)
        outs[1][4:5, :] += jnp.sum(dy, axis=0, keepdims=True)
        for s in (1, 2, 3):
            dx = dx + wv[3 - s:4 - s, :] * _shift_up(dy, s, nxt)
            outs[1][3 - s:4 - s, :] += jnp.sum(dy * _shift_down(xv, s, prev), axis=0, keepdims=True)
        outs[0][...] = dx

    return _rowwise(name, body, src.shape[0], 512,
                    [('row', dpre), ('next8', dpre, width, 0), ('col', src, width, idx),
                     ('prev8', src, width, idx), ('full', w)],
                    [('row', width, F32), ('acc', (8, width), F32)])


ATT_STEP_BLOCKS = 4


def _att_bias(not_first, dil, head):
    qi = _iota((ATT_BLOCK, 2 * ATT_BLOCK), 0)
    ki = _iota((ATT_BLOCK, 2 * ATT_BLOCK), 1)
    dist = ATT_BLOCK + qi - ki
    valid = (dist >= 0) & (dist <= ATT_BLOCK) & (not_first | (ki >= ATT_BLOCK))
    slope = 2.0 ** (-(head + 1))
    return jnp.where(valid, (-slope * dil) * dist.astype(F32), NEG)


def _head_mask():
    lane = _iota((ATT_BLOCK, LANES), 1)
    return lane < HEAD_DIM


def _att_q_specs(dil, nb, bq):
    big = (bq * ATT_BLOCK, ATT_W)
    one = (ATT_BLOCK, ATT_W)
    specs = [pl.BlockSpec(big, lambda r, n: (n, 3 * r)),
             pl.BlockSpec(big, lambda r, n: (n, 3 * r + 1)),
             pl.BlockSpec(one, lambda r, n: (jnp.maximum(n * bq - 1, 0), 3 * r + 1)),
             pl.BlockSpec(big, lambda r, n: (n, 3 * r + 2)),
             pl.BlockSpec(one, lambda r, n: (jnp.maximum(n * bq - 1, 0), 3 * r + 2))]
    wide = pl.BlockSpec(big, lambda r, n: (n, r))
    stat = pl.BlockSpec((bq * ATT_BLOCK, LANES), lambda r, n: (n, r))
    return specs, wide, stat


def _att_fwd(dil, qkv_v):
    n_l = qkv_v.shape[0]
    nb = n_l // ATT_BLOCK
    bq = min(ATT_STEP_BLOCKS, nb)
    scale = HEAD_DIM ** -0.5

    def kern(q_ref, kc_ref, kp_ref, vc_ref, vp_ref, m_out, l_out, a_out):
        n = pl.program_id(1)
        low = _head_mask()
        lane = _iota((ATT_BLOCK, LANES), 1)
        for b in range(bq):
            rows = slice(ATT_BLOCK * b, ATT_BLOCK * (b + 1))
            prev = slice(ATT_BLOCK * (b - 1), ATT_BLOCK * b)
            not_first = (n * bq + b) > 0
            m_acc = jnp.zeros((ATT_BLOCK, LANES), F32)
            l_acc = jnp.zeros((ATT_BLOCK, LANES), F32)
            for p in range(N_HEADS // 2):
                sl = slice(LANES * p, LANES * (p + 1))
                q2 = q_ref[rows, sl].astype(F32)
                k_prev = kp_ref[:, sl] if b == 0 else kc_ref[prev, sl]
                v_prev = vp_ref[:, sl] if b == 0 else vc_ref[prev, sl]
                k2 = jnp.concatenate([k_prev, kc_ref[rows, sl]], axis=0).astype(BF16)
                v2 = jnp.concatenate([v_prev, vc_ref[rows, sl]], axis=0).astype(BF16)
                res = []
                for e in range(2):
                    h = 2 * p + e
                    keep = low if e == 0 else jnp.logical_not(low)
                    qe = jnp.where(keep, q2, 0.0).astype(BF16)
                    s = _dot_nt(qe, k2) * scale + _att_bias(not_first, dil, h)
                    m_new = jnp.max(s, axis=-1, keepdims=True)
                    pe = jnp.exp(s - m_new)
                    l_new = jnp.sum(pe, axis=-1, keepdims=True)
                    m_acc = jnp.where(lane == h, m_new, m_acc)
                    l_acc = jnp.where(lane == h, l_new, l_acc)
                    res.append(_dot(pe.astype(BF16), v2))
                a_out[rows, sl] = jnp.where(low, res[0], res[1])
            m_out[rows, :] = m_acc
            l_out[rows, :] = l_acc

    specs, wide, stat = _att_q_specs(dil, nb, bq)
    shp_s = jax.ShapeDtypeStruct((n_l, dil * LANES), F32)
    shp_a = jax.ShapeDtypeStruct((n_l, dil * ATT_W), F32)
    return pl.pallas_call(kern, name="att_fwd_d%d" % dil, grid=(dil, nb // bq), in_specs=specs,
                          out_specs=[stat, stat, wide], out_shape=[shp_s, shp_s, shp_a],
                          compiler_params=_params(2))(*([qkv_v] * 5))


def _att_merge(parts):
    n_pat = len(parts)

    def body(i, nt, ins, outs, scr):
        ms = [ins[3 * g][...] for g in range(n_pat)]
        m_all = functools.reduce(jnp.maximum, ms)
        expand = _expand_mat()
        num, den = None, None
        for g in range(n_pat):
            e = jnp.exp(ms[g] - m_all)
            d_g = ins[3 * g + 1][...] * e
            n_g = ins[3 * g + 2][...] * _dot_hi(e, expand)
            num = n_g if num is None else num + n_g
            den = d_g if den is None else den + d_g
        real = _iota(den.shape, 1) < N_HEADS
        outs[0][...] = num / _dot_hi(den, expand)
        outs[1][...] = jnp.where(real, m_all + jnp.log(jnp.where(real, den, 1.0)), 0.0)

    ins = []
    for m, l, acc in parts:
        ins += [('row', m), ('row', l), ('row', acc)]
    return _rowwise("att_merge", body, parts[0][0].shape[0], 512, ins,
                    [('row', ATT_W, F32), ('row', LANES, F32)])


def _att_delta(d_att, out):
    def body(i, nt, ins, outs, scr):
        outs[0][...] = _dot_hi(ins[0][...] * ins[1][...], _reduce_mat())

    return _rowwise("att_delta", body, out.shape[0], 512, [('row', d_att), ('row', out)],
                    [('row', LANES, F32)])[0]


def _att_bwd(dil, qkv_v, do_v, lse_v, delta_v, dkv_in):
    n_l = qkv_v.shape[0]
    nb = n_l // ATT_BLOCK
    bq = min(ATT_STEP_BLOCKS, nb)
    steps = nb // bq
    first = dkv_in is None
    scale = HEAD_DIM ** -0.5

    def kern(*refs):
        j = pl.program_id(1)
        k_ref, v_ref, qc_ref, qn_ref, doc_ref, don_ref, lc_ref, ln_ref, dc_ref, dn_ref = refs[:10]
        dk_out, dv_out, dq_out, carry = refs[-4:]
        low = _head_mask()
        row = _iota((2 * ATT_BLOCK, ATT_BLOCK), 0)
        key = _iota((2 * ATT_BLOCK, ATT_BLOCK), 1)
        dist = row - key
        low2 = _iota((2 * ATT_BLOCK, LANES), 1) < HEAD_DIM

        @pl.when(j == 0)
        def _():
            carry[...] = jnp.zeros(carry.shape, F32)

        dq_prev = [carry[:, LANES * p:LANES * (p + 1)] for p in range(N_HEADS // 2)]
        for b in range(bq):
            rows = slice(ATT_BLOCK * b, ATT_BLOCK * (b + 1))
            nrows = slice(ATT_BLOCK * (b + 1), ATT_BLOCK * (b + 2))
            inner = b < bq - 1
            has_next = True if inner else (j < steps - 1)
            valid = (dist >= 0) & (dist <= ATT_BLOCK) & ((row < ATT_BLOCK) | has_next)
            lse2 = jnp.concatenate([lc_ref[rows, :], lc_ref[nrows, :] if inner else ln_ref[...]], axis=0)
            dl2 = jnp.concatenate([dc_ref[rows, :], dc_ref[nrows, :] if inner else dn_ref[...]], axis=0)
            for p in range(N_HEADS // 2):
                sl = slice(LANES * p, LANES * (p + 1))
                k2 = k_ref[rows, sl].astype(F32)
                v2 = v_ref[rows, sl].astype(F32)
                q2 = jnp.concatenate([qc_ref[rows, sl], qc_ref[nrows, sl] if inner else qn_ref[:, sl]],
                                     axis=0).astype(F32)
                do2 = jnp.concatenate([doc_ref[rows, sl], doc_ref[nrows, sl] if inner else don_ref[:, sl]], axis=0)
                q2b = q2.astype(BF16)
                do2b = do2.astype(BF16)
                dks, dvs, dq2 = [], [], None
                for e in range(2):
                    h = 2 * p + e
                    keep = low if e == 0 else jnp.logical_not(low)
                    keep2 = low2 if e == 0 else jnp.logical_not(low2)
                    slope = 2.0 ** (-(h + 1))
                    bias = jnp.where(valid, (-slope * dil) * dist.astype(F32), NEG)
                    ke = jnp.where(keep, k2, 0.0).astype(BF16)
                    ve = jnp.where(keep, v2, 0.0).astype(BF16)
                    s = _dot_nt(q2b, ke) * scale + bias
                    pe = jnp.exp(s - lse2[:, h:h + 1])
                    dp = _dot_nt(do2b, ve)
                    dsb = (pe * (dp - dl2[:, h:h + 1])).astype(BF16)
                    dvs.append(_dot_tn(pe.astype(BF16), jnp.where(keep2, do2, 0.0).astype(BF16)))
                    dks.append(_dot_tn(dsb, jnp.where(keep2, q2, 0.0).astype(BF16)) * scale)
                    dqe = _dot(dsb, ke)
                    dq2 = dqe if dq2 is None else dq2 + dqe
                dk = jnp.where(low, dks[0], dks[1])
                dv = jnp.where(low, dvs[0], dvs[1])
                dq = dq_prev[p] + dq2[:ATT_BLOCK] * scale
                dq_prev[p] = dq2[ATT_BLOCK:] * scale
                if not first:
                    dk = dk + refs[10][rows, sl]
                    dv = dv + refs[11][rows, sl]
                    dq = dq + refs[12][rows, sl]
                dk_out[rows, sl] = dk
                dv_out[rows, sl] = dv
                dq_out[rows, sl] = dq
        for p in range(N_HEADS // 2):
            carry[:, LANES * p:LANES * (p + 1)] = dq_prev[p]

    big = (bq * ATT_BLOCK, ATT_W)
    one = (ATT_BLOCK, ATT_W)

    def nxt_idx(j):
        return jnp.minimum((j + 1) * bq, nb - 1)

    cur = pl.BlockSpec(big, lambda r, j: (j, r))
    nxt = pl.BlockSpec(one, lambda r, j: (nxt_idx(j), r))
    cur_s = pl.BlockSpec((bq * ATT_BLOCK, LANES), lambda r, j: (j, r))
    nxt_s = pl.BlockSpec((ATT_BLOCK, LANES), lambda r, j: (nxt_idx(j), r))
    in_specs = [pl.BlockSpec(big, lambda r, j: (j, 3 * r + 1)),
                pl.BlockSpec(big, lambda r, j: (j, 3 * r + 2)),
                pl.BlockSpec(big, lambda r, j: (j, 3 * r)),
                pl.BlockSpec(one, lambda r, j: (nxt_idx(j), 3 * r)),
                cur, nxt, cur_s, nxt_s, cur_s, nxt_s]
    args = [qkv_v] * 4 + [do_v, do_v, lse_v, lse_v, delta_v, delta_v]
    if not first:
        in_specs += [cur, cur, cur]
        args += list(dkv_in)
    shp = jax.ShapeDtypeStruct((n_l, dil * ATT_W), F32)
    return pl.pallas_call(kern, name="att_bwd_d%d" % dil, grid=(dil, steps), in_specs=in_specs,
                          out_specs=[cur, cur, cur], out_shape=[shp, shp, shp],
                          scratch_shapes=[pltpu.VMEM((ATT_BLOCK, ATT_W), F32)], compiler_params=_params(2))(*args)


def _attention_fwd(qkv):
    t = qkv.shape[0]
    parts = [[s.reshape(t, -1) for s in _att_fwd(dil, qkv.reshape(t // dil, dil * D_MIX))]
             for dil in ATT_DILATIONS]
    return _att_merge(parts)


def _attention_bwd(qkv, d_att, out, lse):
    t = qkv.shape[0]
    delta = _att_delta(d_att, out)
    grads = None
    for dil in ATT_DILATIONS:
        view = lambda a: a.reshape(t // dil, -1)
        grads = _att_bwd(dil, view(qkv), view(d_att), view(lse), view(delta),
                         None if grads is None else [view(a) for a in grads])
    dk, dv, dq = [a.reshape(t, ATT_W) for a in grads]
    return dq, dk, dv


def _ssd_chunk_common(pre, dtraw, bias_row, alog_row):
    q = SSD_CHUNK
    act = _silu(pre)
    lane = _iota((q, LANES), 1)
    dt = jnp.where(lane < N_HEADS, _softplus(dtraw + bias_row), 0.0)
    a_row = -jnp.exp(alog_row)
    tril = jnp.where(_iota((q, q), 0) >= _iota((q, q), 1), 1.0, 0.0).astype(F32)
    cs = _dot_hi(tril, dt * a_row)
    cs_last = cs[q - 1:q, :]
    return act, dt, a_row, tril, cs, cs_last


def _ssd_lmat(cs, cs_t, h):
    q = SSD_CHUNK
    seg = cs[:, h:h + 1] - cs_t[h:h + 1, :]
    causal = _iota((q, q), 0) >= _iota((q, q), 1)
    return jnp.exp(jnp.where(causal, seg, NEG))


def _ssd_gate_norm(y, z, norm_w):
    sz = _silu(z)
    yg = y * sz
    half = SSD_W // 2
    outs, rss = [], []
    for g in range(2):
        part = yg[:, half * g:half * (g + 1)]
        rs = lax.rsqrt(jnp.mean(part * part, axis=-1, keepdims=True) + SSD_NORM_EPS)
        outs.append(part * rs)
        rss.append(rs)
    yn = jnp.concatenate(outs, axis=1)
    return sz, yn, rss, yn * norm_w


def _ssd_fwd(pre, rest, dt_bias, a_log, d_skip, norm_w):
    t = pre.shape[0]
    q = SSD_CHUNK

    def body(c, nc, ins, outs, scr):
        pre_ref, z_ref, dtr_ref, bias_ref, alog_ref, dsk_ref, nw_ref = ins
        out_ref, y_ref, sp_ref = outs
        s_ref = scr[0]

        @pl.when(c == 0)
        def _():
            s_ref[...] = jnp.zeros(s_ref.shape, F32)

        act, dt, a_row, tril, cs, cs_last = _ssd_chunk_common(pre_ref[...], dtr_ref[...], bias_ref[...],
                                                               alog_ref[...])
        x = act[:, :SSD_W]
        cs_t = cs.T
        e_col = jnp.exp(cs)
        w = jnp.exp(cs_last - cs) * dt
        expand = _expand_mat()
        w_x = _dot_hi(w, expand)
        dt_x = _dot_hi(dt, expand)
        e_x = _dot_hi(e_col, expand)
        d_x = _dot_hi(dsk_ref[...], expand)
        cd_x = _dot_hi(jnp.exp(cs_last), expand)
        s_prev = s_ref[...]
        sp_ref[...] = s_prev
        xw = (x * w_x).astype(BF16)
        xd = (x * dt_x).astype(BF16)
        low = _head_mask()
        y_parts, s_parts = [], []
        for g in range(2):
            bg = act[:, SSD_W + SSD_STATE * g:SSD_W + SSD_STATE * (g + 1)].astype(BF16)
            cg = act[:, SSD_W + 2 * SSD_STATE + SSD_STATE * g:SSD_W + 2 * SSD_STATE + SSD_STATE * (g + 1)].astype(BF16)
            gsl = slice(256 * g, 256 * (g + 1))
            gmat = _dot_nt(cg, bg)
            s_parts.append(_dot_tn(bg, xw[:, gsl]))
            y0 = _dot(cg, s_prev[:, gsl].astype(BF16))
            for pp in range(2):
                pair = 2 * g + pp
                psl = slice(LANES * pair, LANES * (pair + 1))
                yd = []
                for e in range(2):
                    h = 2 * pair + e
                    mh = (gmat * _ssd_lmat(cs, cs_t, h)).astype(BF16)
                    yd.append(_dot(mh, xd[:, psl]))
                y_parts.append(jnp.where(low, yd[0], yd[1]) + e_x[:, psl] * y0[:, LANES * pp:LANES * (pp + 1)])
        y = jnp.concatenate(y_parts, axis=1) + d_x * x
        s_ref[...] = cd_x * s_prev + jnp.concatenate(s_parts, axis=1)
        y_ref[...] = y
        out_ref[...] = _ssd_gate_norm(y, z_ref[...], nw_ref[...])[3]

    return _rowwise("ssd_fwd", body, t, q,
                    [('row', pre), ('col', rest, SSD_W, 2), ('col', rest, LANES, 20), ('full', dt_bias),
                     ('full', a_log), ('full', d_skip), ('full', norm_w)],
                    [('row', SSD_W, F32), ('row', SSD_W, F32), ('row', SSD_W, F32)],
                    scratch=[pltpu.VMEM((SSD_STATE, SSD_W), F32)])


def _ssd_bwd(pre, rest, y, s_prev_all, d_mix, dt_bias, a_log, d_skip, norm_w):
    t = pre.shape[0]
    q = SSD_CHUNK

    def body(c, nc, ins, outs, scr):
        pre_ref, z_ref, dtr_ref, y_ref, sp_ref, do_ref, bias_ref, alog_ref, dsk_ref, nw_ref = ins
        dpre_ref, dz_ref, ddt_ref, a128_ref, a512_ref = outs
        ds_ref = scr[0]

        @pl.when(c == nc - 1)
        def _():
            ds_ref[...] = jnp.zeros(ds_ref.shape, F32)

        pre_v = pre_ref[...]
        dtr = dtr_ref[...]
        act, dt, a_row, tril, cs, cs_last = _ssd_chunk_common(pre_v, dtr, bias_ref[...], alog_ref[...])
        x = act[:, :SSD_W]
        cs_t = cs.T
        e_col = jnp.exp(cs)
        decay_end = jnp.exp(cs_last - cs)
        w = decay_end * dt
        cd = jnp.exp(cs_last)
        expand = _expand_mat()
        reduce = _reduce_mat()
        w_x = _dot_hi(w, expand)
        dt_x = _dot_hi(dt, expand)
        e_x = _dot_hi(e_col, expand)
        d_x = _dot_hi(dsk_ref[...], expand)
        cd_x = _dot_hi(cd, expand)
        s_prev = sp_ref[...]
        d_s = ds_ref[...]
        xw = (x * w_x).astype(BF16)
        xd = (x * dt_x).astype(BF16)
        low = _head_mask()
        lane = _iota((q, LANES), 1)
        sub = _iota((q, LANES), 0)

        yv, zv, nw = y_ref[...], z_ref[...], nw_ref[...]
        d_out = do_ref[...]
        sz, yn, rss, _ = _ssd_gate_norm(yv, zv, nw)
        a512_ref[0:1, :] += jnp.sum(d_out * yn, axis=0, keepdims=True)
        dyn = d_out * nw
        half = SSD_W // 2
        dyg_parts = []
        for g in range(2):
            hs = slice(half * g, half * (g + 1))
            dyg_parts.append(rss[g] * (dyn[:, hs] - yn[:, hs] * jnp.mean(dyn[:, hs] * yn[:, hs], axis=-1,
                                                                          keepdims=True)))
        dyg = jnp.concatenate(dyg_parts, axis=1)
        dy = dyg * sz
        dz_ref[...] = dyg * yv * _dsilu(zv)

        a128_ref[2:3, :] += _dot_hi(jnp.sum(dy * x, axis=0, keepdims=True), reduce)
        dx = d_x * dy

        dy0 = e_x * dy
        dyb = dy.astype(BF16)
        dcs = jnp.zeros((q, LANES), F32)
        dcs_rows = jnp.zeros((q, LANES), F32)
        ddt = jnp.zeros((q, LANES), F32)
        ds_prev_parts, z_parts, db_parts, dc_parts, dxd_parts, y0_parts = [], [], [], [], [], []
        for g in range(2):
            bg = act[:, SSD_W + SSD_STATE * g:SSD_W + SSD_STATE * (g + 1)].astype(BF16)
            cg = act[:, SSD_W + 2 * SSD_STATE + SSD_STATE * g:SSD_W + 2 * SSD_STATE + SSD_STATE * (g + 1)].astype(BF16)
            gsl = slice(256 * g, 256 * (g + 1))
            spg = s_prev[:, gsl].astype(BF16)
            dsg = d_s[:, gsl].astype(BF16)
            dy0g = dy0[:, gsl].astype(BF16)
            gmat = _dot_nt(cg, bg)
            y0_parts.append(_dot(cg, spg))
            dc_g = _dot_nt(dy0g, spg)
            ds_prev_parts.append(_dot_tn(cg, dy0g))
            z_parts.append(_dot(bg, dsg))
            db_g = _dot_nt(xw[:, gsl], dsg)
            dg_acc = jnp.zeros((q, q), F32)
            for pp in range(2):
                pair = 2 * g + pp
                psl = slice(LANES * pair, LANES * (pair + 1))
                dxd_e = []
                for e in range(2):
                    h = 2 * pair + e
                    keep = low if e == 0 else jnp.logical_not(low)
                    lm = _ssd_lmat(cs, cs_t, h)
                    mh = gmat * lm
                    dm = _dot_nt(jnp.where(keep, dy[:, psl], 0.0).astype(BF16), xd[:, psl])
                    dxd_e.append(_dot_tn(mh.astype(BF16), dyb[:, psl]))
                    wm = dm * mh
                    dcs = dcs + jnp.where(lane == h, jnp.sum(wm, axis=1, keepdims=True), 0.0)
                    dcs_rows = dcs_rows - jnp.where(sub == h, jnp.sum(wm, axis=0, keepdims=True), 0.0)
                    dg_acc = dg_acc + dm * lm
                dxd_parts.append(jnp.where(low, dxd_e[0], dxd_e[1]))
            dgb = dg_acc.astype(BF16)
            dc_parts.append(dc_g + _dot(dgb, bg))
            db_parts.append(db_g + _dot_tn(dgb, cg))
        y0 = jnp.concatenate(y0_parts, axis=1)
        zmat = jnp.concatenate(z_parts, axis=1)
        dxd = jnp.concatenate(dxd_parts, axis=1)
        ds_prev = jnp.concatenate(ds_prev_parts, axis=1) + cd_x * d_s
        ds_ref[...] = ds_prev

        dcs = dcs + _dot_hi(dy * y0, reduce) * e_col
        dcd = _dot_hi(jnp.sum(d_s * s_prev, axis=0, keepdims=True), reduce)
        dlast = dcd * cd
        dx = dx + w_x * zmat + dxd * dt_x
        dw = _dot_hi(zmat * x, reduce)
        ddt = ddt + dw * decay_end + _dot_hi(dxd * x, reduce)
        dwl = dw * w
        dcs = dcs - dwl
        dlast = dlast + jnp.sum(dwl, axis=0, keepdims=True)
        dcs = dcs + dcs_rows.T + jnp.where(sub == q - 1, dlast, 0.0)
        dda = _dot_hi(tril.T, dcs)
        ddt = ddt + dda * a_row
        a128_ref[1:2, :] += jnp.sum(dda * dt, axis=0, keepdims=True) * a_row
        draw = jnp.where(lane < N_HEADS, ddt * _sigmoid(dtr + bias_ref[...]), 0.0)
        a128_ref[0:1, :] += jnp.sum(draw, axis=0, keepdims=True)
        ddt_ref[...] = draw
        dact = jnp.concatenate([dx] + db_parts + dc_parts, axis=1)
        dpre_ref[...] = dact * _dsilu(pre_v)

    return _rowwise("ssd_bwd", body, t, q,
                    [('row', pre), ('col', rest, SSD_W, 2), ('col', rest, LANES, 20), ('row', y),
                     ('row', s_prev_all), ('col', d_mix, SSD_W, 0), ('full', dt_bias), ('full', a_log),
                     ('full', d_skip), ('full', norm_w)],
                    [('row', SSD_CONV, F32), ('row', SSD_W, F32), ('row', LANES, F32),
                     ('acc', (8, LANES), F32), ('acc', (8, SSD_W), F32)],
                    scratch=[pltpu.VMEM((SSD_STATE, SSD_W), F32)], reverse=True)


LRU_TM = 256


def _lru_gates(xc, wa, ba, wx, bx, lam):
    xb = xc.astype(BF16)
    r = _sigmoid(_dot(xb, wa) + ba)
    i = _sigmoid(_dot(xb, wx) + bx)
    sp = _softplus(-lam)
    a = jnp.exp(-LRU_C * r * sp)
    mult = jnp.sqrt(1.0 - a * a)
    return r, i, sp, a, mult


def _lru_fwd(xc, rest, wa, ba, wx, bx, lam):
    def body(i, nt, ins, outs, scr):
        xc_ref, g_ref, wa_ref, ba_ref, wx_ref, bx_ref, lam_ref = ins
        carry = scr[0]

        @pl.when(i == 0)
        def _():
            carry[...] = jnp.zeros(carry.shape, F32)

        xv = xc_ref[...]
        r, ig, sp, a, mult = _lru_gates(xv, wa_ref[...], ba_ref[...], wx_ref[...], bx_ref[...], lam_ref[...])
        u = mult * (ig * xv)
        row = _iota(a.shape, 0)
        s = 1
        while s < LRU_TM:
            a_sh = jnp.where(row >= s, pltpu.roll(a, s, 0), 1.0)
            u_sh = jnp.where(row >= s, pltpu.roll(u, s, 0), 0.0)
            u = a * u_sh + u
            a = a * a_sh
            s *= 2
        h = u + a * carry[0:1, :]
        carry[0:1, :] = h[LRU_TM - 1:LRU_TM, :]
        outs[1][...] = h
        outs[0][...] = h * _gelu(g_ref[...])

    return _rowwise("lru_fwd", body, xc.shape[0], LRU_TM,
                    [('row', xc), ('col', rest, LRU_W, 3), ('full', wa), ('full', ba), ('full', wx),
                     ('full', bx), ('full', lam)],
                    [('row', LRU_W, F32), ('row', LRU_W, F32)], scratch=[pltpu.VMEM((8, LRU_W), F32)])


def _lru_bwd(xc, rest, h, d_mix, wa, ba, wx, bx, lam, wa_t, wx_t):
    def body(i, nt, ins, outs, scr):
        xc_ref, g_ref, h_ref, hp_ref, do_ref, wa_ref, ba_ref, wx_ref, bx_ref, lam_ref, wat_ref, wxt_ref = ins
        dxc_ref, dg_ref, dza_ref, dzi_ref, acc_ref = outs
        carry = scr[0]

        @pl.when(i == nt - 1)
        def _():
            carry[...] = jnp.zeros(carry.shape, F32)

        xv, gv, hv, d_out = xc_ref[...], g_ref[...], h_ref[...], do_ref[...]
        r, ig, sp, a, mult = _lru_gates(xv, wa_ref[...], ba_ref[...], wx_ref[...], bx_ref[...], lam_ref[...])
        dg_ref[...] = d_out * hv * _dgelu(gv)
        gsum = d_out * _gelu(gv)
        row = _iota(a.shape, 0)
        b = jnp.where(row < LRU_TM - 1, pltpu.roll(a, LRU_TM - 1, 0), 1.0)
        s = 1
        while s < LRU_TM:
            keep = row < LRU_TM - s
            b_sh = jnp.where(keep, pltpu.roll(b, LRU_TM - s, 0), 1.0)
            g_sh = jnp.where(keep, pltpu.roll(gsum, LRU_TM - s, 0), 0.0)
            gsum = gsum + b * g_sh
            b = b * b_sh
            s *= 2
        dh = gsum + b * carry[0:1, :]
        carry[0:1, :] = a[0:1, :] * dh[0:1, :]
        h_prev = _shift_down(hv, 1, jnp.where(i > 0, hp_ref[...], 0.0))
        du = dh
        dmult = du * ig * xv
        di = du * mult * xv
        dxc = du * mult * ig
        da = dh * h_prev - dmult * a / mult
        dlog = da * a
        dr = dlog * (-LRU_C) * sp
        acc_ref[2:3, :] += jnp.sum(dlog * (-LRU_C) * r, axis=0, keepdims=True)
        dza = dr * r * (1.0 - r)
        dzi = di * ig * (1.0 - ig)
        acc_ref[0:1, :] += jnp.sum(dza, axis=0, keepdims=True)
        acc_ref[1:2, :] += jnp.sum(dzi, axis=0, keepdims=True)
        dzab, dzib = dza.astype(BF16), dzi.astype(BF16)
        dza_ref[...] = dzab
        dzi_ref[...] = dzib
        dxc_ref[...] = dxc + _dot(dzab, wat_ref[...]) + _dot(dzib, wxt_ref[...])

    return _rowwise("lru_bwd", body, xc.shape[0], LRU_TM,
                    [('row', xc), ('col', rest, LRU_W, 3), ('row', h), ('prev8', h, LRU_W, 0),
                     ('col', d_mix, LRU_W, 1), ('full', wa), ('full', ba), ('full', wx), ('full', bx),
                     ('full', lam), ('full', wa_t), ('full', wx_t)],
                    [('row', LRU_W, F32), ('row', LRU_W, F32), ('row', LRU_W, BF16), ('row', LRU_W, BF16),
                     ('acc', (8, LRU_W), F32)],
                    scratch=[pltpu.VMEM((8, LRU_W), F32)], reverse=True)


FFN_TM = 512
FFN_TN = 1536


def _ffn_up(name, h2, w_gu):
    t, k = h2.shape
    nh = D_FFP // FFN_TN

    def kern(a_ref, wg_ref, wu_ref, g_ref, u_ref, act_ref):
        a = a_ref[...].astype(BF16)
        gv = _dot(a, wg_ref[...])
        uv = _dot(a, wu_ref[...])
        g_ref[...] = gv
        u_ref[...] = uv
        act_ref[...] = (_silu(gv) * uv).astype(BF16)

    tile = pl.BlockSpec((FFN_TM, FFN_TN), lambda i, j: (i, j))
    return pl.pallas_call(
        kern, name=name, grid=(t // FFN_TM, nh),
        in_specs=[pl.BlockSpec((FFN_TM, k), lambda i, j: (i, 0)),
                  pl.BlockSpec((k, FFN_TN), lambda i, j: (0, j)),
                  pl.BlockSpec((k, FFN_TN), lambda i, j: (0, nh + j))],
        out_specs=[tile, tile, tile],
        out_shape=[jax.ShapeDtypeStruct((t, D_FFP), F32), jax.ShapeDtypeStruct((t, D_FFP), F32),
                   jax.ShapeDtypeStruct((t, D_FFP), BF16)],
        compiler_params=_params(2))(h2, w_gu, w_gu)


def _ffn_down_bwd(name, dx, w_down_t, gate, up):
    t, k = dx.shape

    def kern(dx_ref, w_ref, g_ref, u_ref, dg_ref, du_ref):
        da = _dot(dx_ref[...].astype(BF16), w_ref[...])
        gv, uv = g_ref[...], u_ref[...]
        dg_ref[...] = (da * uv * _dsilu(gv)).astype(BF16)
        du_ref[...] = (da * _silu(gv)).astype(BF16)

    tile = pl.BlockSpec((FFN_TM, FFN_TN), lambda i, j: (i, j))
    shp = jax.ShapeDtypeStruct((t, D_FFP), BF16)
    return pl.pallas_call(
        kern, name=name, grid=(t // FFN_TM, D_FFP // FFN_TN),
        in_specs=[pl.BlockSpec((FFN_TM, k), lambda i, j: (i, 0)),
                  pl.BlockSpec((k, FFN_TN), lambda i, j: (0, j)), tile, tile],
        out_specs=[tile, tile], out_shape=[shp, shp], compiler_params=_params(2))(dx, w_down_t, gate, up)


def _layer_fwd(x, w, l):
    tag = "_l%d" % l
    h = _rmsnorm_fwd("norm_mix" + tag, x, w['norm_mix'][l])
    qkv = _mm("proj_qkv" + tag, [h], w['w_qkv'][l], tn=768, out_dtype=BF16)
    rest = _mm("proj_rest" + tag, [h], w['w_rest'][l], tn=896)
    att, lse = _attention_fwd(qkv)
    pre = _conv_fwd("ssd_conv" + tag, rest, SSD_CONV, 0, w['ssd_conv_w'][l], w['ssd_conv_b'][l])
    ssd, y, s_prev = _ssd_fwd(pre, rest, w['ssd_dt_bias'][l], w['ssd_a_log'][l], w['ssd_d'][l], w['ssd_norm'][l])
    xc = _conv_fwd("lru_conv" + tag, rest, LRU_W, 4, w['lru_conv_w'][l], w['lru_conv_b'][l])
    lru, hl = _lru_fwd(xc, rest, w['lru_wa'][l], w['lru_ba'][l], w['lru_wx'][l], w['lru_bx'][l], w['lru_lambda'][l])
    x_mid = _mm("proj_out" + tag, [att, ssd, lru], w['w_out'][l], res=x, tn=512)
    h2 = _rmsnorm_fwd("norm_ffn" + tag, x_mid, w['norm_ffn'][l])
    gate, up, act = _ffn_up("proj_gu" + tag, h2, w['w_gu'][l])
    x_next = _mm("proj_down" + tag, [act], w['w_down'][l], res=x_mid, tn=512)
    saved = dict(x=x, h=h, qkv=qkv, rest=rest, att=att, lse=lse, pre=pre, ssd=ssd, y=y, s_prev=s_prev, xc=xc,
                 lru=lru, hl=hl, x_mid=x_mid, h2=h2, gate=gate, up=up, act=act)
    return x_next, saved


def _layer_bwd(dx_next, sv, w, l):
    tag = "_l%d_b" % l
    t = dx_next.shape[0]
    g = {}
    dgate, dup = _ffn_down_bwd("d_act" + tag, dx_next, w['w_down_t'][l], sv['gate'], sv['up'])
    g['w_down'] = _mm_tn("dw_down" + tag, sv['act'], dx_next, tk=1536)
    dh2 = _mm("d_h2" + tag, [dgate, dup], w['w_gu_t'][l], tn=512)
    g['w_gate'] = _mm_tn("dw_gate" + tag, sv['h2'], dgate, tn=1536)
    g['w_up'] = _mm_tn("dw_up" + tag, sv['h2'], dup, tn=1536)
    dx_mid, acc = _rmsnorm_bwd("norm_ffn" + tag, dh2, sv['x_mid'], w['norm_ffn'][l], dx_next)
    g['norm_ffn'] = acc[0]
    d_att = _mm("d_att" + tag, [dx_mid], w['w_out_t'][l][:, :ATT_W], tn=512)
    d_mix = _mm("d_mix" + tag, [dx_mid], w['w_out_t'][l][:, ATT_W:], tn=512)
    g['w_out'] = jnp.concatenate([_mm_tn("dw_out%d" % k + tag, a, dx_mid)
                                  for k, a in enumerate((sv['att'], sv['ssd'], sv['lru']))], axis=0)
    dxc, dgl, dza, dzi, acc = _lru_bwd(sv['xc'], sv['rest'], sv['hl'], d_mix, w['lru_wa'][l], w['lru_ba'][l],
                                       w['lru_wx'][l], w['lru_bx'][l], w['lru_lambda'][l],
                                       w['lru_wa_t'][l], w['lru_wx_t'][l])
    g['lru_ba'], g['lru_bx'] = acc[0], acc[1]
    g['lru_lambda'] = acc[2] * (-_sigmoid(-w['lru_lambda'][l][0]))
    g['lru_wa'] = _diag_blocks(_mm_tn("dw_lru_a" + tag, sv['xc'], dza))
    g['lru_wx'] = _diag_blocks(_mm_tn("dw_lru_x" + tag, sv['xc'], dzi))
    dxl, acc = _conv_bwd("lru_conv" + tag, dxc, sv['rest'], LRU_W, 4, w['lru_conv_w'][l])
    g['lru_conv_w'], g['lru_conv_b'] = acc[:4], acc[4]
    dpre, dz, ddt, a128, a512 = _ssd_bwd(sv['pre'], sv['rest'], sv['y'], sv['s_prev'], d_mix, w['ssd_dt_bias'][l],
                                         w['ssd_a_log'][l], w['ssd_d'][l], w['ssd_norm'][l])
    g['ssd_dt_bias'], g['ssd_a_log'], g['ssd_d'] = a128[0, :N_HEADS], a128[1, :N_HEADS], a128[2, :N_HEADS]
    g['ssd_norm'] = a512[0]
    dxbc, acc = _conv_bwd("ssd_conv" + tag, dpre, sv['rest'], SSD_CONV, 0, w['ssd_conv_w'][l])
    g['ssd_conv_w'], g['ssd_conv_b'] = acc[:4], acc[4]
    dq, dk, dv = _attention_bwd(sv['qkv'], d_att, sv['att'], sv['lse'])
    pieces = [dq, dk, dv, dxbc, dz, dgl, dxl, ddt]
    dh = _mm("d_h" + tag, pieces, w['w_in_t'][l], tn=512)
    dws = [_mm_tn("dw_in%d" % k + tag, sv['h'], p) for k, p in enumerate(pieces)]
    g['w_in'] = jnp.concatenate([dws[0], dws[1], dws[2], dws[4], dws[3], _dt_tile_place(dws[7]), dws[5], dws[6]],
                                axis=1)
    dx, acc = _rmsnorm_bwd("norm_mix" + tag, dh, sv['x'], w['norm_mix'][l], dx_mid)
    g['norm_mix'] = acc[0]
    return dx, g


def _diag_blocks(m):
    return jnp.stack([m[64 * n:64 * (n + 1), 64 * n:64 * (n + 1)] for n in range(8)])


def _block_diag(w):
    eye = jnp.eye(8, dtype=w.dtype)
    return (w[:, :, None, :] * eye[:, None, :, None]).reshape(512, 512)


_ANY = pl.BlockSpec(memory_space=pl.ANY)
_MESH = pl.DeviceIdType.MESH


def _all_gather(name, xs):
    n = len(xs)

    def body(*refs):
        x_refs, out_refs = refs[:n], refs[n:2 * n]
        send_sems, recv_sems, local_sems = refs[2 * n:]
        x_, y_, c_ = lax.axis_index("x"), lax.axis_index("y"), lax.axis_index("c")
        me, sibling = (x_, y_, c_), (x_, y_, 1 - c_)
        chips = [(1 - x_, y_), (x_, 1 - y_), (1 - x_, 1 - y_)]

        def slot(a, px, py, pc):
            return out_refs[a].at[4 * px + 2 * py + pc]

        def copy(a, k, block, to, src=None):
            return pltpu.make_async_remote_copy(
                src_ref=slot(a, *block) if src is None else src, dst_ref=slot(a, *block),
                send_sem=send_sems.at[a, k], recv_sem=recv_sems.at[a, k], device_id=to, device_id_type=_MESH)

        mine = [pltpu.make_async_copy(x_refs[a], slot(a, *me), local_sems.at[a]) for a in range(n)]
        for cp in mine:
            cp.start()
        first = []
        for a in range(n):
            first.append(copy(a, 0, me, sibling, src=x_refs[a]))
            first += [copy(a, 1 + j, me, (*chip, c_), src=x_refs[a]) for j, chip in enumerate(chips)]
        for cp in first:
            cp.start()
        passed = []
        for j, chip in enumerate(chips):
            for a in range(n):
                copy(a, 1 + j, (*chip, c_), me).wait_recv()
                fwd = copy(a, 4 + j, (*chip, c_), sibling)
                fwd.start()
                passed.append(fwd)
        for a in range(n):
            copy(a, 0, sibling, me).wait_recv()
            for j, chip in enumerate(chips):
                copy(a, 4 + j, (*chip, 1 - c_), me).wait_recv()
        for cp in first + passed:
            cp.wait_send()
        for cp in mine:
            cp.wait()

    return pl.pallas_call(
        body, name=name, out_shape=[jax.ShapeDtypeStruct((N_DEV,) + x.shape, x.dtype) for x in xs],
        in_specs=[_ANY] * n, out_specs=[_ANY] * n,
        scratch_shapes=[pltpu.SemaphoreType.DMA((n, 7)), pltpu.SemaphoreType.DMA((n, 7)),
                        pltpu.SemaphoreType.DMA((n,))],
    )(*xs)


def _all_to_all(name, xs):
    n = len(xs)

    def body(*refs):
        x_refs, out_refs = refs[:n], refs[n:2 * n]
        send_sems, recv_sems, local_sems = refs[2 * n:]
        x_, y_, c_ = lax.axis_index("x"), lax.axis_index("y"), lax.axis_index("c")
        me = 4 * x_ + 2 * y_ + c_

        def peer(k):
            return ((1 - x_) if k & 4 else x_, (1 - y_) if k & 2 else y_, (1 - c_) if k & 1 else c_)

        def copy(a, k):
            px, py, pc = peer(k)
            return pltpu.make_async_remote_copy(
                src_ref=x_refs[a].at[4 * px + 2 * py + pc], dst_ref=out_refs[a].at[me],
                send_sem=send_sems.at[a, k - 1], recv_sem=recv_sems.at[a, k - 1],
                device_id=(px, py, pc), device_id_type=_MESH)

        def arrival(a, k):
            px, py, pc = peer(k)
            return pltpu.make_async_remote_copy(
                src_ref=x_refs[a].at[me], dst_ref=out_refs[a].at[4 * px + 2 * py + pc],
                send_sem=send_sems.at[a, k - 1], recv_sem=recv_sems.at[a, k - 1],
                device_id=(px, py, pc), device_id_type=_MESH)

        mine = [pltpu.make_async_copy(x_refs[a].at[me], out_refs[a].at[me], local_sems.at[a]) for a in range(n)]
        for cp in mine:
            cp.start()
        copies = [copy(a, k) for a in range(n) for k in range(1, N_DEV)]
        for cp in copies:
            cp.start()
        for a in range(n):
            for k in range(1, N_DEV):
                arrival(a, k).wait_recv()
        for cp in copies:
            cp.wait_send()
        for cp in mine:
            cp.wait()

    return pl.pallas_call(
        body, name=name, out_shape=[jax.ShapeDtypeStruct(x.shape, x.dtype) for x in xs],
        in_specs=[_ANY] * n, out_specs=[_ANY] * n,
        scratch_shapes=[pltpu.SemaphoreType.DMA((n, 7)), pltpu.SemaphoreType.DMA((n, 7)),
                        pltpu.SemaphoreType.DMA((n,))],
    )(*xs)


def _window_offset():
    me = 4 * lax.axis_index("x") + 2 * lax.axis_index("y") + lax.axis_index("c")
    return jnp.where(me < 6, me, me + 120)


def _place_w_in(shard):
    def body(i, nt, ins, outs, scr):
        outs[0][...] = pltpu.roll(ins[0][...], _window_offset(), 1).astype(BF16)

    return _rowwise("place_w_in", body, shard.shape[0], 256, [('row', shard)], [('row', IN_WIN, BF16)])[0]


def _adamw(name, w, m, v, g, tr, from_window=False):
    s_parts, r, c = g.shape
    assert r % tr == 0

    def kern(w_ref, m_ref, v_ref, g_ref, go_ref, d_ref, mo_ref, vo_ref):
        gs = g_ref[0].astype(F32)
        for s in range(1, s_parts):
            gs = gs + g_ref[s].astype(F32)
        if from_window:
            gs = pltpu.roll(gs, c - _window_offset(), 1)
        wv = w_ref[...]
        m2 = ADAM_B1 * m_ref[...] + (1.0 - ADAM_B1) * gs
        v2 = ADAM_B2 * v_ref[...] + (1.0 - ADAM_B2) * (gs * gs)
        m_hat = m2 / (1.0 - ADAM_B1 ** ADAM_STEP)
        v_hat = v2 / (1.0 - ADAM_B2 ** ADAM_STEP)
        go_ref[...] = gs
        d_ref[...] = -ADAM_LR * (m_hat / (jnp.sqrt(v_hat) + ADAM_EPS) + ADAM_WD * wv)
        mo_ref[...] = m2
        vo_ref[...] = v2

    spec = pl.BlockSpec((tr, c), lambda i: (i, 0))
    shp = jax.ShapeDtypeStruct((r, c), F32)
    return pl.pallas_call(kern, name=name, grid=(r // tr,),
                          in_specs=[spec, spec, spec, pl.BlockSpec((s_parts, tr, c), lambda i: (0, i, 0))],
                          out_specs=[spec] * 4, out_shape=[shp] * 4, compiler_params=_params(1))(w, m, v, g)


def _pack(arrs, rows, lead=0):
    parts = []
    for a in arrs:
        flat = a.reshape(a.shape[:lead] + (-1,))
        pad = (-flat.shape[-1]) % LANES
        if pad:
            flat = jnp.pad(flat, [(0, 0)] * lead + [(0, pad)])
        parts.append(flat)
    flat = jnp.concatenate(parts, axis=-1)
    pad = rows * LANES - flat.shape[-1]
    assert pad >= 0
    if pad:
        flat = jnp.pad(flat, [(0, 0)] * lead + [(0, pad)])
    return flat.reshape(flat.shape[:lead] + (rows, LANES))


def _unpack(buf, shapes, lead=0):
    flat = buf.reshape(buf.shape[:lead] + (-1,))
    out, off = [], 0
    for shp in shapes:
        n = math.prod(shp)
        out.append(flat[..., off:off + n].reshape(buf.shape[:lead] + tuple(shp)))
        off += n + ((-n) % LANES)
    return out


BIG = ('w_in', 'w_out', 'w_gate', 'w_up', 'w_down')
CONV =('ssd_conv_w', 'lru_conv_w')
CONV_SHARD_SHAPES = ((DEPTH, 4, SSD_CONV // N_DEV), (DEPTH, 4, LRU_W // N_DEV))
CONV_ROWS = 16
SMALL = ('norm_mix', 'ssd_conv_b', 'ssd_dt_bias', 'ssd_a_log', 'ssd_d', 'ssd_norm', 'lru_conv_b', 'lru_wa',
         'lru_ba', 'lru_wx', 'lru_bx', 'lru_lambda', 'norm_ffn', 'norm_final')
SMALL_ROWS = 1280
SMALL_TILE = 256
WEIGHTS = ('norm_mix', 'w_in', 'ssd_conv_w', 'ssd_conv_b', 'ssd_dt_bias', 'ssd_a_log', 'ssd_d', 'ssd_norm',
           'lru_conv_w', 'lru_conv_b', 'lru_wa', 'lru_ba', 'lru_wx', 'lru_bx', 'lru_lambda', 'w_out', 'norm_ffn',
           'w_gate', 'w_up', 'w_down', 'norm_final')


def _join_cols(a):
    return jnp.transpose(a, (1, 2, 0, 3)).reshape(a.shape[1], a.shape[2], -1)


def _join_rows(a):
    return jnp.transpose(a, (1, 0, 2, 3)).reshape(a.shape[1], -1, a.shape[3])


def _split_cols(a):
    l, r, c = a.shape
    return jnp.transpose(a.reshape(l, r, N_DEV, c // N_DEV), (2, 0, 1, 3)).reshape(N_DEV, l * r, c // N_DEV)


def _split_rows(a):
    l, r, c = a.shape
    return jnp.transpose(a.reshape(l, N_DEV, r // N_DEV, c), (1, 0, 2, 3)).reshape(N_DEV, l * r // N_DEV, c)


def _shard_form(k, a):
    if k == 'w_in':
        return jnp.pad(a.reshape(-1, IN_SHARD), ((0, 0), (0, IN_WIN - IN_SHARD)))
    if k in ('w_gate', 'w_up'):
        return jnp.pad(a.reshape(-1, FF_SHARD), ((0, 0), (0, FF_SHARD_P - FF_SHARD)))
    if k == 'w_down':
        return jnp.pad(a, ((0, 0), (0, FF_SHARD_P - FF_SHARD), (0, 0))).reshape(-1, D_MODEL)
    return a.reshape(-1, D_MODEL)


def _shard_back(k, a):
    if k == 'w_in':
        return a[:, :IN_SHARD].reshape(DEPTH, D_MODEL, IN_SHARD)
    if k in ('w_gate', 'w_up'):
        return a[:, :FF_SHARD].reshape(DEPTH, D_MODEL, FF_SHARD)
    if k == 'w_down':
        return a.reshape(DEPTH, FF_SHARD_P, D_MODEL)[:, :FF_SHARD]
    return a.reshape(DEPTH, D_MIX // N_DEV, D_MODEL)


def _dt_tile_place(a):
    zeros = jnp.zeros(a.shape[:-1] + (LANES - N_HEADS,), a.dtype)
    return jnp.concatenate([a[..., :6], zeros, a[..., 6:8]], axis=-1)


def _dt_tile_heads(tile):
    zeros = jnp.zeros(tile.shape[:-1] + (LANES - N_HEADS,), tile.dtype)
    return jnp.concatenate([tile[..., :6], tile[..., 126:128], zeros], axis=-1)


def _layout_from_windows(win):
    r = win.shape[1]
    main = jnp.concatenate([win[j][:, :512] for j in range(N_DEV)] + [jnp.zeros((r, LANES), win.dtype)], axis=1)
    gap = jnp.zeros((r, 384), win.dtype)
    tails = [jnp.zeros((r, 512), win.dtype)]
    for j in range(N_DEV - 1):
        tails += [win[j][:, 512:], gap]
    tails.append(win[N_DEV - 1][:, 512:])
    return main + jnp.concatenate(tails, axis=1)


def _prepare_weights(p, full):
    w = {}
    w_in = full['w_in']
    w_qkv = w_in[:, :, :D_MIX]
    dt_cols = _dt_tile_heads(w_in[:, :, 3072:3200])
    w_rest = jnp.concatenate([w_in[:, :, 2048:3072], w_in[:, :, 1536:2048], w_in[:, :, 3200:3712],
                              w_in[:, :, 3712:4224], dt_cols], axis=2)
    w['w_qkv'], w['w_rest'] = w_qkv, w_rest
    w['w_in_t'] = jnp.transpose(jnp.concatenate([w_qkv, w_rest], axis=2), (0, 2, 1))
    w['w_out'] = full['w_out']
    w['w_out_t'] = jnp.transpose(full['w_out'], (0, 2, 1))
    w['w_gu'] = jnp.concatenate([full['w_gate'], full['w_up']], axis=2)
    w['w_gu_t'] = jnp.transpose(w['w_gu'], (0, 2, 1))
    w['w_down'] = full['w_down']
    w['w_down_t'] = jnp.transpose(full['w_down'], (0, 2, 1))
    for k in ('norm_mix', 'ssd_conv_b', 'ssd_norm', 'lru_conv_b', 'lru_ba', 'lru_bx', 'lru_lambda', 'norm_ffn'):
        w[k] = p[k][:, None, :]
    for k in ('ssd_dt_bias', 'ssd_a_log', 'ssd_d'):
        w[k] = jnp.pad(p[k], ((0, 0), (0, LANES - N_HEADS)))[:, None, :]
    for k in CONV:
        w[k] = jnp.pad(full[k], ((0, 0), (0, 4), (0, 0)))
    for k in ('lru_wa', 'lru_wx'):
        bd = jnp.stack([_block_diag(p[k][l]) for l in range(DEPTH)]).astype(BF16)
        w[k] = bd
        w[k + '_t'] = jnp.transpose(bd, (0, 2, 1))
    return w


def _local_step(x, target, w, norm_final):
    saved = []
    for l in range(DEPTH):
        x, sv = _layer_fwd(x, w, l)
        saved.append(sv)
    dx, loss_acc, dgf = _final_loss(x, norm_final[None, :], target)
    grads = [None] * DEPTH
    for l in reversed(range(DEPTH)):
        dx, grads[l] = _layer_bwd(dx, saved[l], w, l)
    g = {k: jnp.stack([grads[l][k] for l in range(DEPTH)]) for k in grads[0]}
    g['norm_final'] = dgf[0]
    return loss_acc[0, 0], dx, g


def kernel(x, norm_mix, w_in, ssd_conv_w, ssd_conv_b, ssd_dt_bias, ssd_a_log, ssd_d, ssd_norm, lru_conv_w, lru_conv_b, lru_wa, lru_ba, lru_wx, lru_bx, lru_lambda, w_out, norm_ffn, w_gate, w_up, w_down, norm_final, loss_target, m_norm_mix, m_w_in, m_ssd_conv_w, m_ssd_conv_b, m_ssd_dt_bias, m_ssd_a_log, m_ssd_d, m_ssd_norm, m_lru_conv_w, m_lru_conv_b, m_lru_wa, m_lru_ba, m_lru_wx, m_lru_bx, m_lru_lambda, m_w_out, m_norm_ffn, m_w_gate, m_w_up, m_w_down, m_norm_final, v_norm_mix, v_w_in, v_ssd_conv_w, v_ssd_conv_b, v_ssd_dt_bias, v_ssd_a_log, v_ssd_d, v_ssd_norm, v_lru_conv_w, v_lru_conv_b, v_lru_wa, v_lru_ba, v_lru_wx, v_lru_bx, v_lru_lambda, v_w_out, v_norm_ffn, v_w_gate, v_w_up, v_w_down, v_norm_final):
    args = (norm_mix, w_in, ssd_conv_w, ssd_conv_b, ssd_dt_bias, ssd_a_log, ssd_d, ssd_norm, lru_conv_w, lru_conv_b, lru_wa, lru_ba, lru_wx, lru_bx, lru_lambda, w_out, norm_ffn, w_gate, w_up, w_down, norm_final)
    margs = (m_norm_mix, m_w_in, m_ssd_conv_w, m_ssd_conv_b, m_ssd_dt_bias, m_ssd_a_log, m_ssd_d, m_ssd_norm, m_lru_conv_w, m_lru_conv_b, m_lru_wa, m_lru_ba, m_lru_wx, m_lru_bx, m_lru_lambda, m_w_out, m_norm_ffn, m_w_gate, m_w_up, m_w_down, m_norm_final)
    vargs = (v_norm_mix, v_w_in, v_ssd_conv_w, v_ssd_conv_b, v_ssd_dt_bias, v_ssd_a_log, v_ssd_d, v_ssd_norm, v_lru_conv_w, v_lru_conv_b, v_lru_wa, v_lru_ba, v_lru_wx, v_lru_bx, v_lru_lambda, v_w_out, v_norm_ffn, v_w_gate, v_w_up, v_w_down, v_norm_final)
    p = dict(zip(WEIGHTS, args))
    pm = dict(zip(WEIGHTS, margs))
    pv = dict(zip(WEIGHTS, vargs))

    forms = {k: _shard_form(k, p[k]) for k in BIG}
    send = [_place_w_in(forms['w_in'])] + [forms[k].astype(BF16) for k in BIG[1:]]
    got = _all_gather("gather_weights", send + [_pack([p[k] for k in CONV], CONV_ROWS)])
    full = {'w_in': _layout_from_windows(got[0]).reshape(DEPTH, D_MODEL, IN_COLS_P),
            'w_out': _join_rows(got[1].reshape(N_DEV, DEPTH, D_MIX // N_DEV, D_MODEL)),
            'w_gate': _join_cols(got[2].reshape(N_DEV, DEPTH, D_MODEL, FF_SHARD_P)),
            'w_up': _join_cols(got[3].reshape(N_DEV, DEPTH, D_MODEL, FF_SHARD_P)),
            'w_down': _join_rows(got[4].reshape(N_DEV, DEPTH, FF_SHARD_P, D_MODEL))}
    for k, a in zip(CONV, _unpack(got[5], CONV_SHARD_SHAPES, lead=1)):
        full[k] = _join_cols(a)
    w = _prepare_weights(p, full)

    loss_local, dx, g = _local_step(x[0], loss_target[0], w, norm_final)
    loss = lax.psum(loss_local, ("x", "y", "c"))

    small_g = _all_gather("gather_small_grads", [_pack([g[k] for k in SMALL + CONV], SMALL_ROWS)])[0]
    zeros = [jnp.zeros_like(g[k]) for k in CONV]
    res_small = _adamw("adamw_small", _pack([p[k] for k in SMALL] + zeros, SMALL_ROWS),
                       _pack([pm[k] for k in SMALL] + zeros, SMALL_ROWS),
                       _pack([pv[k] for k in SMALL] + zeros, SMALL_ROWS), small_g, SMALL_TILE)
    small_shapes = [g[k].shape for k in SMALL + CONV]
    out = {kind: {} for kind in range(4)}
    for kind in range(4):
        for k, a in zip(SMALL + CONV, _unpack(res_small[kind], small_shapes)):
            out[kind][k] = a
    me = 4 * lax.axis_index("x") + 2 * lax.axis_index("y") + lax.axis_index("c")
    conv_g = []
    for k, shp in zip(CONV, CONV_SHARD_SHAPES):
        conv_g.append(lax.dynamic_slice_in_dim(out[0][k], me * shp[2], shp[2], axis=2))
    res_conv = _adamw("adamw_conv", _pack([p[k] for k in CONV], CONV_ROWS), _pack([pm[k] for k in CONV], CONV_ROWS),
                      _pack([pv[k] for k in CONV], CONV_ROWS), _pack(conv_g, CONV_ROWS)[None], CONV_ROWS)
    for kind in range(4):
        for k, a in zip(CONV, _unpack(res_conv[kind], CONV_SHARD_SHAPES)):
            out[kind][k] = a

    g_in = g['w_in'].reshape(DEPTH * D_MODEL, IN_COLS_P)
    dest = [jnp.stack([g_in[:, 512 * j:512 * j + IN_WIN] for j in range(N_DEV)]), _split_rows(g['w_out']),
            _split_cols(g['w_gate']), _split_cols(g['w_up']), _split_rows(g['w_down'])]
    parts = _all_to_all("exchange_big_grads", [a.astype(BF16) for a in dest])
    tiles = {'w_in': 256, 'w_out': 128, 'w_gate': 512, 'w_up': 512, 'w_down': 256}
    for k, part in zip(BIG, parts):
        res = _adamw("adamw_" + k, forms[k], _shard_form(k, pm[k]), _shard_form(k, pv[k]), part, tiles[k],
                     from_window=(k == 'w_in'))
        for kind in range(4):
            out[kind][k] = _shard_back(k, res[kind])

    outs = [loss, dx[None]]
    for kind in range(4):
        outs += [out[kind][k] for k in WEIGHTS]
    return tuple(outs)
```

```python
import functools
import math

import jax
import jax.numpy as jnp
from jax import lax
from jax.experimental import pallas as pl
from jax.experimental.pallas import tpu as pltpu

F32 = jnp.float32
BF16 = jnp.bfloat16

N_DEV = 8
DEPTH = 2
D_MODEL = 1024
ATT_W = 512
HEAD_DIM = 64
N_HEADS = 8
ATT_BLOCK = 128
ATT_DILATIONS = (16, 4, 1)
SSD_W = 512
SSD_STATE = 128
SSD_CONV = 1024
SSD_CHUNK = 128
LRU_W = 512
LRU_C = 8.0
D_MIX = 1536
D_FF = 2816
FF_SHARD = D_FF // N_DEV
FF_SHARD_P = 384
D_FFP = N_DEV * FF_SHARD_P
IN_COLS = 4104
IN_SHARD = IN_COLS // N_DEV
IN_WIN = 640
IN_COLS_P = 4224
REST_COLS = 2688
NORM_EPS = 1e-6
SSD_NORM_EPS = 1e-5
NEG = -1e30

ADAM_LR = 0.001
ADAM_B1 = 0.9
ADAM_B2 = 0.999
ADAM_EPS = 1e-08
ADAM_WD = 0.01
ADAM_STEP = 10

LANES = 128
VMEM_LIMIT = 52 * 1024 * 1024
HI = lax.Precision.HIGHEST


def _sigmoid(x):
    return 1.0 / (1.0 + jnp.exp(-x))


def _silu(x):
    return x * _sigmoid(x)


def _dsilu(x):
    s = _sigmoid(x)
    return s * (1.0 + x * (1.0 - s))


def _softplus(x):
    return jnp.maximum(x, 0.0) + jnp.log(1.0 + jnp.exp(-jnp.abs(x)))


_GELU_C = math.sqrt(2.0 / math.pi)


def _gelu(x):
    return 0.5 * x * (1.0 + jnp.tanh(_GELU_C * (x + 0.044715 * x * x * x)))


def _dgelu(x):
    t = jnp.tanh(_GELU_C * (x + 0.044715 * x * x * x))
    return 0.5 * (1.0 + t) + 0.5 * x * (1.0 - t * t) * _GELU_C * (1.0 + 3.0 * 0.044715 * x * x)


def _dot(a, b):
    return jnp.dot(a, b, preferred_element_type=F32)


def _dot_nt(a, b):
    return lax.dot_general(a, b, (((1,), (1,)), ((), ())), preferred_element_type=F32)


def _dot_tn(a, b):
    return lax.dot_general(a, b, (((0,), (0,)), ((), ())), preferred_element_type=F32)


def _dot_hi(a, b):
    return jnp.dot(a, b, preferred_element_type=F32, precision=HI)


def _iota(shape, axis):
    return lax.broadcasted_iota(jnp.int32, shape, axis)


def _shift_down(x, s, prev8):
    xs = pltpu.roll(x, s, 0)
    ps = pltpu.roll(prev8, s, 0)
    top = jnp.concatenate([ps, x[8:]], axis=0)
    return jnp.where(_iota(x.shape, 0) < s, top, xs)


def _shift_up(x, s, next8):
    tm = x.shape[0]
    xs = pltpu.roll(x, tm - s, 0)
    ns = pltpu.roll(next8, 8 - s, 0)
    bottom = jnp.concatenate([x[:tm - 8], ns], axis=0)
    return jnp.where(_iota(x.shape, 0) >= tm - s, bottom, xs)


def _expand_mat():
    return jnp.where(_iota((LANES, SSD_W), 1) // HEAD_DIM == _iota((LANES, SSD_W), 0), 1.0, 0.0).astype(F32)


def _reduce_mat():
    return jnp.where(_iota((SSD_W, LANES), 0) // HEAD_DIM == _iota((SSD_W, LANES), 1), 1.0, 0.0).astype(F32)


def _params(n_grid):
    return pltpu.CompilerParams(dimension_semantics=("arbitrary",) * n_grid, vmem_limit_bytes=VMEM_LIMIT)


def _rowwise(name, body, n_rows, tm, ins, outs, scratch=(), reverse=False):
    nt = n_rows // tm
    assert nt * tm == n_rows and tm % 8 == 0
    r8 = tm // 8
    last8 = n_rows // 8 - 1

    def pos(s):
        return (nt - 1 - s) if reverse else s

    in_specs, args = [], []
    for spec in ins:
        kind, arr = spec[0], spec[1]
        args.append(arr)
        if kind == 'row':
            in_specs.append(pl.BlockSpec((tm, arr.shape[1]), lambda s: (pos(s), 0)))
        elif kind == 'col':
            in_specs.append(pl.BlockSpec((tm, spec[2]), functools.partial(lambda s, j: (pos(s), j), j=spec[3])))
        elif kind == 'full':
            in_specs.append(pl.BlockSpec(arr.shape, functools.partial(lambda s, n: (0,) * n, n=arr.ndim)))
        elif kind == 'prev8':
            in_specs.append(pl.BlockSpec((8, spec[2]), functools.partial(
                lambda s, j: (jnp.maximum(pos(s) * r8 - 1, 0), j), j=spec[3])))
        elif kind == 'next8':
            in_specs.append(pl.BlockSpec((8, spec[2]), functools.partial(
                lambda s, j: (jnp.minimum((pos(s) + 1) * r8, last8), j), j=spec[3])))
        else:
            raise ValueError(kind)
    out_specs, out_shape, acc_idx = [], [], []
    for k, spec in enumerate(outs):
        if spec[0] == 'row':
            out_specs.append(pl.BlockSpec((tm, spec[1]), lambda s: (pos(s), 0)))
            out_shape.append(jax.ShapeDtypeStruct((n_rows, spec[1]), spec[2]))
        else:
            out_specs.append(pl.BlockSpec(spec[1], lambda s: (0, 0)))
            out_shape.append(jax.ShapeDtypeStruct(spec[1], spec[2]))
            acc_idx.append(k)
    n_in, n_out = len(ins), len(outs)

    def kern(*refs):
        s = pl.program_id(0)
        in_refs, out_refs, scr = refs[:n_in], refs[n_in:n_in + n_out], refs[n_in + n_out:]

        @pl.when(s == 0)
        def _():
            for k in acc_idx:
                out_refs[k][...] = jnp.zeros(out_refs[k].shape, out_refs[k].dtype)

        body(pos(s), nt, in_refs, out_refs, scr)

    res = pl.pallas_call(kern, name=name, grid=(nt,), in_specs=in_specs, out_specs=out_specs,
                         out_shape=out_shape, scratch_shapes=list(scratch), compiler_params=_params(1))(*args)
    return res


def _mm(name, a_list, b, *, res=None, out_dtype=F32, tm=512, tn=None):
    n_rows = a_list[0].shape[0]
    k_total, n = b.shape
    ks = [a.shape[1] for a in a_list]
    assert sum(ks) == k_total
    tn = n if tn is None else tn
    assert n_rows % tm == 0 and n % tn == 0
    na = len(a_list)

    def kern(*refs):
        a_refs, b_ref, o_ref = refs[:na], refs[na], refs[-1]
        acc, off = None, 0
        for a_ref, kp in zip(a_refs, ks):
            part = _dot(a_ref[...].astype(BF16), b_ref[off:off + kp, :])
            acc = part if acc is None else acc + part
            off += kp
        if res is not None:
            acc = acc + refs[na + 1][...]
        o_ref[...] = acc.astype(out_dtype)

    in_specs = [pl.BlockSpec((tm, kp), lambda i, j: (i, 0)) for kp in ks]
    in_specs.append(pl.BlockSpec((k_total, tn), lambda i, j: (0, j)))
    args = list(a_list) + [b]
    if res is not None:
        in_specs.append(pl.BlockSpec((tm, tn), lambda i, j: (i, j)))
        args.append(res)
    return pl.pallas_call(kern, name=name, grid=(n_rows // tm, n // tn), in_specs=in_specs,
                          out_specs=pl.BlockSpec((tm, tn), lambda i, j: (i, j)),
                          out_shape=jax.ShapeDtypeStruct((n_rows, n), out_dtype),
                          compiler_params=_params(2))(*args)


def _mm_tn(name, a, g, *, a_col=None, g_col=None, tk=None, tn=None, tt=512):
    n_rows = a.shape[0]
    k = a.shape[1] if a_col is None else a_col[0]
    a_j = 0 if a_col is None else a_col[1]
    n = g.shape[1] if g_col is None else g_col[0]
    g_j = 0 if g_col is None else g_col[1]
    tk = k if tk is None else tk
    tn = n if tn is None else tn
    assert k % tk == 0 and n % tn == 0 and n_rows % tt == 0
    kb = k // tk
    nbk = n // tn

    def kern(a_ref, g_ref, o_ref):
        t = pl.program_id(2)

        @pl.when(t == 0)
        def _():
            o_ref[...] = jnp.zeros(o_ref.shape, F32)

        o_ref[...] += _dot_tn(a_ref[...].astype(BF16), g_ref[...].astype(BF16))

    return pl.pallas_call(
        kern, name=name, grid=(kb, n // tn, n_rows // tt),
        in_specs=[pl.BlockSpec((tt, tk), lambda i, j, t: (t, a_j * kb + i)),
                  pl.BlockSpec((tt, tn), lambda i, j, t: (t, g_j * nbk + j))],
        out_specs=pl.BlockSpec((tk, tn), lambda i, j, t: (i, j)),
        out_shape=jax.ShapeDtypeStruct((k, n), F32), compiler_params=_params(3))(a, g)


def _rmsnorm_fwd(name, x, g):
    def body(i, nt, ins, outs, scr):
        xv = ins[0][...]
        rstd = lax.rsqrt(jnp.mean(xv * xv, axis=-1, keepdims=True) + NORM_EPS)
        outs[0][...] = (xv * rstd * ins[1][...]).astype(BF16)

    return _rowwise(name, body, x.shape[0], 512, [('row', x), ('full', g)], [('row', x.shape[1], BF16)])[0]


def _rmsnorm_bwd(name, dh, x, g, dres):
    d = x.shape[1]

    def body(i, nt, ins, outs, scr):
        dy, xv, gv, dr = ins[0][...], ins[1][...], ins[2][...], ins[3][...]
        rstd = lax.rsqrt(jnp.mean(xv * xv, axis=-1, keepdims=True) + NORM_EPS)
        xhat = xv * rstd
        outs[1][0:1, :] += jnp.sum(dy * xhat, axis=0, keepdims=True)
        dxh = dy * gv
        outs[0][...] = dr + rstd * (dxh - xhat * jnp.mean(dxh * xhat, axis=-1, keepdims=True))

    return _rowwise(name, body, x.shape[0], 512, [('row', dh), ('row', x), ('full', g), ('row', dres)],
                    [('row', d, F32), ('acc', (8, d), F32)])


def _final_loss(x, g, target):
    d = x.shape[1]

    def body(i, nt, ins, outs, scr):
        xv, gv, tv = ins[0][...], ins[1][...], ins[2][...]
        rstd = lax.rsqrt(jnp.mean(xv * xv, axis=-1, keepdims=True) + NORM_EPS)
        xhat = xv * rstd
        err = xhat * gv - tv
        row_loss = 0.5 * jnp.mean(err * err, axis=-1, keepdims=True)
        outs[1][...] += jnp.sum(row_loss, axis=0, keepdims=True)
        dy = err * (1.0 / d)
        outs[2][0:1, :] += jnp.sum(dy * xhat, axis=0, keepdims=True)
        dxh = dy * gv
        outs[0][...] = rstd * (dxh - xhat * jnp.mean(dxh * xhat, axis=-1, keepdims=True))

    return _rowwise("final_loss", body, x.shape[0], 512, [('row', x), ('full', g), ('row', target)],
                    [('row', d, F32), ('acc', (8, LANES), F32), ('acc', (8, d), F32)])


def _conv_fwd(name, src, width, idx, w, b):
    def body(i, nt, ins, outs, scr):
        xv = ins[0][...]
        prev = jnp.where(i > 0, ins[1][...], 0.0)
        wv = ins[2][...]
        y = ins[3][...] + wv[3:4, :] * xv
        for s in (1, 2, 3):
            y = y + wv[3 - s:4 - s, :] * _shift_down(xv, s, prev)
        outs[0][...] = y

    return _rowwise(name, body, src.shape[0], 512,
                    [('col', src, width, idx), ('prev8', src, width, idx), ('full', w), ('full', b)],
                    [('row', width, F32)])[0]


def _conv_bwd(name, dpre, src, width, idx, w):
    def body(i, nt, ins, outs, scr):
        dy = ins[0][...]
        nxt = jnp.where(i < nt - 1, ins[1][...], 0.0)
        xv = ins[2][...]
        prev = jnp.where(i > 0, ins[3][...], 0.0)
        wv = ins[4][...]
        dx = wv[3:4, :] * dy
        outs[1][3:4, :] += jnp.sum(dy * xv, axis=0, keepdims=True)
        outs[1][4:5, :] += jnp.sum(dy, axis=0, keepdims=True)
        for s in (1, 2, 3):
            dx = dx + wv[3 - s:4 - s, :] * _shift_up(dy, s, nxt)
            outs[1][3 - s:4 - s, :] += jnp.sum(dy * _shift_down(xv, s, prev), axis=0, keepdims=True)
        outs[0][...] = dx

    return _rowwise(name, body, src.shape[0], 512,
                    [('row', dpre), ('next8', dpre, width, 0), ('col', src, width, idx),
                     ('prev8', src, width, idx), ('full', w)],
                    [('row', width, F32), ('acc', (8, width), F32)])


ATT_STEP_BLOCKS = 4


def _att_bias(not_first, dil, head):
    qi = _iota((ATT_BLOCK, 2 * ATT_BLOCK), 0)
    ki = _iota((ATT_BLOCK, 2 * ATT_BLOCK), 1)
    dist = ATT_BLOCK + qi - ki
    valid = (dist >= 0) & (dist <= ATT_BLOCK) & (not_first | (ki >= ATT_BLOCK))
    slope = 2.0 ** (-(head + 1))
    return jnp.where(valid, (-slope * dil) * dist.astype(F32), NEG)


def _head_mask():
    lane = _iota((ATT_BLOCK, LANES), 1)
    return lane < HEAD_DIM


def _att_q_specs(dil, nb, bq):
    big = (bq * ATT_BLOCK, ATT_W)
    one = (ATT_BLOCK, ATT_W)
    specs = [pl.BlockSpec(big, lambda r, n: (n, 3 * r)),
             pl.BlockSpec(big, lambda r, n: (n, 3 * r + 1)),
             pl.BlockSpec(one, lambda r, n: (jnp.maximum(n * bq - 1, 0), 3 * r + 1)),
             pl.BlockSpec(big, lambda r, n: (n, 3 * r + 2)),
             pl.BlockSpec(one, lambda r, n: (jnp.maximum(n * bq - 1, 0), 3 * r + 2))]
    wide = pl.BlockSpec(big, lambda r, n: (n, r))
    stat = pl.BlockSpec((bq * ATT_BLOCK, LANES), lambda r, n: (n, r))
    return specs, wide, stat


def _att_fwd(dil, qkv_v):
    n_l = qkv_v.shape[0]
    nb = n_l // ATT_BLOCK
    bq = min(ATT_STEP_BLOCKS, nb)
    scale = HEAD_DIM ** -0.5

    def kern(q_ref, kc_ref, kp_ref, vc_ref, vp_ref, m_out, l_out, a_out):
        n = pl.program_id(1)
        low = _head_mask()
        lane = _iota((ATT_BLOCK, LANES), 1)
        for b in range(bq):
            rows = slice(ATT_BLOCK * b, ATT_BLOCK * (b + 1))
            prev = slice(ATT_BLOCK * (b - 1), ATT_BLOCK * b)
            not_first = (n * bq + b) > 0
            m_acc = jnp.zeros((ATT_BLOCK, LANES), F32)
            l_acc = jnp.zeros((ATT_BLOCK, LANES), F32)
            for p in range(N_HEADS // 2):
                sl = slice(LANES * p, LANES * (p + 1))
                q2 = q_ref[rows, sl].astype(F32)
                k_prev = kp_ref[:, sl] if b == 0 else kc_ref[prev, sl]
                v_prev = vp_ref[:, sl] if b == 0 else vc_ref[prev, sl]
                k2 = jnp.concatenate([k_prev, kc_ref[rows, sl]], axis=0).astype(BF16)
                v2 = jnp.concatenate([v_prev, vc_ref[rows, sl]], axis=0).astype(BF16)
                res = []
                for e in range(2):
                    h = 2 * p + e
                    keep = low if e == 0 else jnp.logical_not(low)
                    qe = jnp.where(keep, q2, 0.0).astype(BF16)
                    s = _dot_nt(qe, k2) * scale + _att_bias(not_first, dil, h)
                    m_new = jnp.max(s, axis=-1, keepdims=True)
                    pe = jnp.exp(s - m_new)
                    l_new = jnp.sum(pe, axis=-1, keepdims=True)
                    m_acc = jnp.where(lane == h, m_new, m_acc)
                    l_acc = jnp.where(lane == h, l_new, l_acc)
                    res.append(_dot(pe.astype(BF16), v2))
                a_out[rows, sl] = jnp.where(low, res[0], res[1])
            m_out[rows, :] = m_acc
            l_out[rows, :] = l_acc

    specs, wide, stat = _att_q_specs(dil, nb, bq)
    shp_s = jax.ShapeDtypeStruct((n_l, dil * LANES), F32)
    shp_a = jax.ShapeDtypeStruct((n_l, dil * ATT_W), F32)
    return pl.pallas_call(kern, name="att_fwd_d%d" % dil, grid=(dil, nb // bq), in_specs=specs,
                          out_specs=[stat, stat, wide], out_shape=[shp_s, shp_s, shp_a],
                          compiler_params=_params(2))(*([qkv_v] * 5))


def _att_merge(parts):
    n_pat = len(parts)

    def body(i, nt, ins, outs, scr):
        ms = [ins[3 * g][...] for g in range(n_pat)]
        m_all = functools.reduce(jnp.maximum, ms)
        expand = _expand_mat()
        num, den = None, None
        for g in range(n_pat):
            e = jnp.exp(ms[g] - m_all)
            d_g = ins[3 * g + 1][...] * e
            n_g = ins[3 * g + 2][...] * _dot_hi(e, expand)
            num = n_g if num is None else num + n_g
            den = d_g if den is None else den + d_g
        real = _iota(den.shape, 1) < N_HEADS
        outs[0][...] = num / _dot_hi(den, expand)
        outs[1][...] = jnp.where(real, m_all + jnp.log(jnp.where(real, den, 1.0)), 0.0)

    ins = []
    for m, l, acc in parts:
        ins += [('row', m), ('row', l), ('row', acc)]
    return _rowwise("att_merge", body, parts[0][0].shape[0], 512, ins,
                    [('row', ATT_W, F32), ('row', LANES, F32)])


def _att_delta(d_att, out):
    def body(i, nt, ins, outs, scr):
        outs[0][...] = _dot_hi(ins[0][...] * ins[1][...], _reduce_mat())

    return _rowwise("att_delta", body, out.shape[0], 512, [('row', d_att), ('row', out)],
                    [('row', LANES, F32)])[0]


def _att_bwd(dil, qkv_v, do_v, lse_v, delta_v, dkv_in):
    n_l = qkv_v.shape[0]
    nb = n_l // ATT_BLOCK
    bq = min(ATT_STEP_BLOCKS, nb)
    steps = nb // bq
    first = dkv_in is None
    scale = HEAD_DIM ** -0.5

    def kern(*refs):
        j = pl.program_id(1)
        k_ref, v_ref, qc_ref, qn_ref, doc_ref, don_ref, lc_ref, ln_ref, dc_ref, dn_ref = refs[:10]
        dk_out, dv_out, dq_out, carry = refs[-4:]
        low = _head_mask()
        row = _iota((2 * ATT_BLOCK, ATT_BLOCK), 0)
        key = _iota((2 * ATT_BLOCK, ATT_BLOCK), 1)
        dist = row - key
        low2 = _iota((2 * ATT_BLOCK, LANES), 1) < HEAD_DIM

        @pl.when(j == 0)
        def _():
            carry[...] = jnp.zeros(carry.shape, F32)

        dq_prev = [carry[:, LANES * p:LANES * (p + 1)] for p in range(N_HEADS // 2)]
        for b in range(bq):
            rows = slice(ATT_BLOCK * b, ATT_BLOCK * (b + 1))
            nrows = slice(ATT_BLOCK * (b + 1), ATT_BLOCK * (b + 2))
            inner = b < bq - 1
            has_next = True if inner else (j < steps - 1)
            valid = (dist >= 0) & (dist <= ATT_BLOCK) & ((row < ATT_BLOCK) | has_next)
            lse2 = jnp.concatenate([lc_ref[rows, :], lc_ref[nrows, :] if inner else ln_ref[...]], axis=0)
            dl2 = jnp.concatenate([dc_ref[rows, :], dc_ref[nrows, :] if inner else dn_ref[...]], axis=0)
            for p in range(N_HEADS // 2):
                sl = slice(LANES * p, LANES * (p + 1))
                k2 = k_ref[rows, sl].astype(F32)
                v2 = v_ref[rows, sl].astype(F32)
                q2 = jnp.concatenate([qc_ref[rows, sl], qc_ref[nrows, sl] if inner else qn_ref[:, sl]],
                                     axis=0).astype(F32)
                do2 = jnp.concatenate([doc_ref[rows, sl], doc_ref[nrows, sl] if inner else don_ref[:, sl]], axis=0)
                q2b = q2.astype(BF16)
                do2b = do2.astype(BF16)
                dks, dvs, dq2 = [], [], None
                for e in range(2):
                    h = 2 * p + e
                    keep = low if e == 0 else jnp.logical_not(low)
                    keep2 = low2 if e == 0 else jnp.logical_not(low2)
                    slope = 2.0 ** (-(h + 1))
                    bias = jnp.where(valid, (-slope * dil) * dist.astype(F32), NEG)
                    ke = jnp.where(keep, k2, 0.0).astype(BF16)
                    ve = jnp.where(keep, v2, 0.0).astype(BF16)
                    s = _dot_nt(q2b, ke) * scale + bias
                    pe = jnp.exp(s - lse2[:, h:h + 1])
                    dp = _dot_nt(do2b, ve)
                    dsb = (pe * (dp - dl2[:, h:h + 1])).astype(BF16)
                    dvs.append(_dot_tn(pe.astype(BF16), jnp.where(keep2, do2, 0.0).astype(BF16)))
                    dks.append(_dot_tn(dsb, jnp.where(keep2, q2, 0.0).astype(BF16)) * scale)
                    dqe = _dot(dsb, ke)
                    dq2 = dqe if dq2 is None else dq2 + dqe
                dk = jnp.where(low, dks[0], dks[1])
                dv = jnp.where(low, dvs[0], dvs[1])
                dq = dq_prev[p] + dq2[:ATT_BLOCK] * scale
                dq_prev[p] = dq2[ATT_BLOCK:] * scale
                if not first:
                    dk = dk + refs[10][rows, sl].astype(F32)
                    dv = dv + refs[11][rows, sl].astype(F32)
                    dq = dq + refs[12][rows, sl].astype(F32)
                dk_out[rows, sl] = dk.astype(BF16)
                dv_out[rows, sl] = dv.astype(BF16)
                dq_out[rows, sl] = dq.astype(BF16)
        for p in range(N_HEADS // 2):
            carry[:, LANES * p:LANES * (p + 1)] = dq_prev[p]

    big = (bq * ATT_BLOCK, ATT_W)
    one = (ATT_BLOCK, ATT_W)

    def nxt_idx(j):
        return jnp.minimum((j + 1) * bq, nb - 1)

    cur = pl.BlockSpec(big, lambda r, j: (j, r))
    nxt = pl.BlockSpec(one, lambda r, j: (nxt_idx(j), r))
    cur_s = pl.BlockSpec((bq * ATT_BLOCK, LANES), lambda r, j: (j, r))
    nxt_s = pl.BlockSpec((ATT_BLOCK, LANES), lambda r, j: (nxt_idx(j), r))
    in_specs = [pl.BlockSpec(big, lambda r, j: (j, 3 * r + 1)),
                pl.BlockSpec(big, lambda r, j: (j, 3 * r + 2)),
                pl.BlockSpec(big, lambda r, j: (j, 3 * r)),
                pl.BlockSpec(one, lambda r, j: (nxt_idx(j), 3 * r)),
                cur, nxt, cur_s, nxt_s, cur_s, nxt_s]
    args = [qkv_v] * 4 + [do_v, do_v, lse_v, lse_v, delta_v, delta_v]
    if not first:
        in_specs += [cur, cur, cur]
        args += list(dkv_in)
    shp = jax.ShapeDtypeStruct((n_l, dil * ATT_W), BF16)
    return pl.pallas_call(kern, name="att_bwd_d%d" % dil, grid=(dil, steps), in_specs=in_specs,
                          out_specs=[cur, cur, cur], out_shape=[shp, shp, shp],
                          scratch_shapes=[pltpu.VMEM((ATT_BLOCK, ATT_W), F32)], compiler_params=_params(2))(*args)


def _attention_fwd(qkv):
    t = qkv.shape[0]
    parts = [[s.reshape(t, -1) for s in _att_fwd(dil, qkv.reshape(t // dil, dil * D_MIX))]
             for dil in ATT_DILATIONS]
    return _att_merge(parts)


def _attention_bwd(qkv, d_att, out, lse):
    t = qkv.shape[0]
    delta = _att_delta(d_att, out)
    grads = None
    for dil in ATT_DILATIONS:
        view = lambda a: a.reshape(t // dil, -1)
        grads = _att_bwd(dil, view(qkv), view(d_att), view(lse), view(delta),
                         None if grads is None else [view(a) for a in grads])
    dk, dv, dq = [a.reshape(t, ATT_W) for a in grads]
    return dq, dk, dv


def _ssd_chunk_common(pre, dtraw, bias_row, alog_row):
    q = SSD_CHUNK
    act = _silu(pre)
    lane = _iota((q, LANES), 1)
    dt = jnp.where(lane < N_HEADS, _softplus(dtraw + bias_row), 0.0)
    a_row = -jnp.exp(alog_row)
    tril = jnp.where(_iota((q, q), 0) >= _iota((q, q), 1), 1.0, 0.0).astype(F32)
    cs = _dot_hi(tril, dt * a_row)
    cs_last = cs[q - 1:q, :]
    return act, dt, a_row, tril, cs, cs_last


def _ssd_lmat(cs, cs_t, h):
    q = SSD_CHUNK
    seg = cs[:, h:h + 1] - cs_t[h:h + 1, :]
    causal = _iota((q, q), 0) >= _iota((q, q), 1)
    return jnp.exp(jnp.where(causal, seg, NEG))


def _ssd_gate_norm(y, z, norm_w):
    sz = _silu(z)
    yg = y * sz
    half = SSD_W // 2
    outs, rss = [], []
    for g in range(2):
        part = yg[:, half * g:half * (g + 1)]
        rs = lax.rsqrt(jnp.mean(part * part, axis=-1, keepdims=True) + SSD_NORM_EPS)
        outs.append(part * rs)
        rss.append(rs)
    yn = jnp.concatenate(outs, axis=1)
    return sz, yn, rss, yn * norm_w


def _ssd_fwd(pre, rest, dt_bias, a_log, d_skip, norm_w):
    t = pre.shape[0]
    q = SSD_CHUNK

    def body(c, nc, ins, outs, scr):
        pre_ref, z_ref, dtr_ref, bias_ref, alog_ref, dsk_ref, nw_ref = ins
        out_ref, y_ref, sp_ref = outs
        s_ref = scr[0]

        @pl.when(c == 0)
        def _():
            s_ref[...] = jnp.zeros(s_ref.shape, F32)

        act, dt, a_row, tril, cs, cs_last = _ssd_chunk_common(pre_ref[...], dtr_ref[...], bias_ref[...],
                                                               alog_ref[...])
        x = act[:, :SSD_W]
        cs_t = cs.T
        e_col = jnp.exp(cs)
        w = jnp.exp(cs_last - cs) * dt
        expand = _expand_mat()
        w_x = _dot_hi(w, expand)
        dt_x = _dot_hi(dt, expand)
        e_x = _dot_hi(e_col, expand)
        d_x = _dot_hi(dsk_ref[...], expand)
        cd_x = _dot_hi(jnp.exp(cs_last), expand)
        s_prev = s_ref[...]
        sp_ref[...] = s_prev
        xw = (x * w_x).astype(BF16)
        xd = (x * dt_x).astype(BF16)
        low = _head_mask()
        y_parts, s_parts = [], []
        for g in range(2):
            bg = act[:, SSD_W + SSD_STATE * g:SSD_W + SSD_STATE * (g + 1)].astype(BF16)
            cg = act[:, SSD_W + 2 * SSD_STATE + SSD_STATE * g:SSD_W + 2 * SSD_STATE + SSD_STATE * (g + 1)].astype(BF16)
            gsl = slice(256 * g, 256 * (g + 1))
            gmat = _dot_nt(cg, bg)
            s_parts.append(_dot_tn(bg, xw[:, gsl]))
            y0 = _dot(cg, s_prev[:, gsl].astype(BF16))
            for pp in range(2):
                pair = 2 * g + pp
                psl = slice(LANES * pair, LANES * (pair + 1))
                yd = []
                for e in range(2):
                    h = 2 * pair + e
                    mh = (gmat * _ssd_lmat(cs, cs_t, h)).astype(BF16)
                    yd.append(_dot(mh, xd[:, psl]))
                y_parts.append(jnp.where(low, yd[0], yd[1]) + e_x[:, psl] * y0[:, LANES * pp:LANES * (pp + 1)])
        y = jnp.concatenate(y_parts, axis=1) + d_x * x
        s_ref[...] = cd_x * s_prev + jnp.concatenate(s_parts, axis=1)
        y_ref[...] = y
        out_ref[...] = _ssd_gate_norm(y, z_ref[...], nw_ref[...])[3]

    return _rowwise("ssd_fwd", body, t, q,
                    [('row', pre), ('col', rest, SSD_W, 2), ('col', rest, LANES, 20), ('full', dt_bias),
                     ('full', a_log), ('full', d_skip), ('full', norm_w)],
                    [('row', SSD_W, F32), ('row', SSD_W, F32), ('row', SSD_W, F32)],
                    scratch=[pltpu.VMEM((SSD_STATE, SSD_W), F32)])


def _ssd_bwd(pre, rest, y, s_prev_all, d_mix, dt_bias, a_log, d_skip, norm_w):
    t = pre.shape[0]
    q = SSD_CHUNK

    def body(c, nc, ins, outs, scr):
        pre_ref, z_ref, dtr_ref, y_ref, sp_ref, do_ref, bias_ref, alog_ref, dsk_ref, nw_ref = ins
        dpre_ref, dz_ref, ddt_ref, a128_ref, a512_ref = outs
        ds_ref = scr[0]

        @pl.when(c == nc - 1)
        def _():
            ds_ref[...] = jnp.zeros(ds_ref.shape, F32)

        pre_v = pre_ref[...]
        dtr = dtr_ref[...]
        act, dt, a_row, tril, cs, cs_last = _ssd_chunk_common(pre_v, dtr, bias_ref[...], alog_ref[...])
        x = act[:, :SSD_W]
        cs_t = cs.T
        e_col = jnp.exp(cs)
        decay_end = jnp.exp(cs_last - cs)
        w = decay_end * dt
        cd = jnp.exp(cs_last)
        expand = _expand_mat()
        reduce = _reduce_mat()
        w_x = _dot_hi(w, expand)
        dt_x = _dot_hi(dt, expand)
        e_x = _dot_hi(e_col, expand)
        d_x = _dot_hi(dsk_ref[...], expand)
        cd_x = _dot_hi(cd, expand)
        s_prev = sp_ref[...]
        d_s = ds_ref[...]
        xw = (x * w_x).astype(BF16)
        xd = (x * dt_x).astype(BF16)
        low = _head_mask()
        lane = _iota((q, LANES), 1)
        sub = _iota((q, LANES), 0)

        yv, zv, nw = y_ref[...], z_ref[...], nw_ref[...]
        d_out = do_ref[...]
        sz, yn, rss, _ = _ssd_gate_norm(yv, zv, nw)
        a512_ref[0:1, :] += jnp.sum(d_out * yn, axis=0, keepdims=True)
        dyn = d_out * nw
        half = SSD_W // 2
        dyg_parts = []
        for g in range(2):
            hs = slice(half * g, half * (g + 1))
            dyg_parts.append(rss[g] * (dyn[:, hs] - yn[:, hs] * jnp.mean(dyn[:, hs] * yn[:, hs], axis=-1,
                                                                          keepdims=True)))
        dyg = jnp.concatenate(dyg_parts, axis=1)
        dy = dyg * sz
        dz_ref[...] = dyg * yv * _dsilu(zv)

        a128_ref[2:3, :] += _dot_hi(jnp.sum(dy * x, axis=0, keepdims=True), reduce)
        dx = d_x * dy

        dy0 = e_x * dy
        dyb = dy.astype(BF16)
        dcs = jnp.zeros((q, LANES), F32)
        dcs_rows = jnp.zeros((q, LANES), F32)
        ddt = jnp.zeros((q, LANES), F32)
        ds_prev_parts, z_parts, db_parts, dc_parts, dxd_parts, y0_parts = [], [], [], [], [], []
        for g in range(2):
            bg = act[:, SSD_W + SSD_STATE * g:SSD_W + SSD_STATE * (g + 1)].astype(BF16)
            cg = act[:, SSD_W + 2 * SSD_STATE + SSD_STATE * g:SSD_W + 2 * SSD_STATE + SSD_STATE * (g + 1)].astype(BF16)
            gsl = slice(256 * g, 256 * (g + 1))
            spg = s_prev[:, gsl].astype(BF16)
            dsg = d_s[:, gsl].astype(BF16)
            dy0g = dy0[:, gsl].astype(BF16)
            gmat = _dot_nt(cg, bg)
            y0_parts.append(_dot(cg, spg))
            dc_g = _dot_nt(dy0g, spg)
            ds_prev_parts.append(_dot_tn(cg, dy0g))
            z_parts.append(_dot(bg, dsg))
            db_g = _dot_nt(xw[:, gsl], dsg)
            dg_acc = jnp.zeros((q, q), F32)
            for pp in range(2):
                pair = 2 * g + pp
                psl = slice(LANES * pair, LANES * (pair + 1))
                dxd_e = []
                for e in range(2):
                    h = 2 * pair + e
                    keep = low if e == 0 else jnp.logical_not(low)
                    lm = _ssd_lmat(cs, cs_t, h)
                    mh = gmat * lm
                    dm = _dot_nt(jnp.where(keep, dy[:, psl], 0.0).astype(BF16), xd[:, psl])
                    dxd_e.append(_dot_tn(mh.astype(BF16), dyb[:, psl]))
                    wm = dm * mh
                    dcs = dcs + jnp.where(lane == h, jnp.sum(wm, axis=1, keepdims=True), 0.0)
                    dcs_rows = dcs_rows - jnp.where(sub == h, jnp.sum(wm, axis=0, keepdims=True), 0.0)
                    dg_acc = dg_acc + dm * lm
                dxd_parts.append(jnp.where(low, dxd_e[0], dxd_e[1]))
            dgb = dg_acc.astype(BF16)
            dc_parts.append(dc_g + _dot(dgb, bg))
            db_parts.append(db_g + _dot_tn(dgb, cg))
        y0 = jnp.concatenate(y0_parts, axis=1)
        zmat = jnp.concatenate(z_parts, axis=1)
        dxd = jnp.concatenate(dxd_parts, axis=1)
        ds_prev = jnp.concatenate(ds_prev_parts, axis=1) + cd_x * d_s
        ds_ref[...] = ds_prev

        dcs = dcs + _dot_hi(dy * y0, reduce) * e_col
        dcd = _dot_hi(jnp.sum(d_s * s_prev, axis=0, keepdims=True), reduce)
        dlast = dcd * cd
        dx = dx + w_x * zmat + dxd * dt_x
        dw = _dot_hi(zmat * x, reduce)
        ddt = ddt + dw * decay_end + _dot_hi(dxd * x, reduce)
        dwl = dw * w
        dcs = dcs - dwl
        dlast = dlast + jnp.sum(dwl, axis=0, keepdims=True)
        dcs = dcs + dcs_rows.T + jnp.where(sub == q - 1, dlast, 0.0)
        dda = _dot_hi(tril.T, dcs)
        ddt = ddt + dda * a_row
        a128_ref[1:2, :] += jnp.sum(dda * dt, axis=0, keepdims=True) * a_row
        draw = jnp.where(lane < N_HEADS, ddt * _sigmoid(dtr + bias_ref[...]), 0.0)
        a128_ref[0:1, :] += jnp.sum(draw, axis=0, keepdims=True)
        ddt_ref[...] = draw
        dact = jnp.concatenate([dx] + db_parts + dc_parts, axis=1)
        dpre_ref[...] = dact * _dsilu(pre_v)

    return _rowwise("ssd_bwd", body, t, q,
                    [('row', pre), ('col', rest, SSD_W, 2), ('col', rest, LANES, 20), ('row', y),
                     ('row', s_prev_all), ('col', d_mix, SSD_W, 0), ('full', dt_bias), ('full', a_log),
                     ('full', d_skip), ('full', norm_w)],
                    [('row', SSD_CONV, F32), ('row', SSD_W, F32), ('row', LANES, F32),
                     ('acc', (8, LANES), F32), ('acc', (8, SSD_W), F32)],
                    scratch=[pltpu.VMEM((SSD_STATE, SSD_W), F32)], reverse=True)


LRU_TM = 256


def _lru_gates(xc, wa, ba, wx, bx, lam):
    xb = xc.astype(BF16)
    r = _sigmoid(_dot(xb, wa) + ba)
    i = _sigmoid(_dot(xb, wx) + bx)
    sp = _softplus(-lam)
    a = jnp.exp(-LRU_C * r * sp)
    mult = jnp.sqrt(1.0 - a * a)
    return r, i, sp, a, mult


def _lru_fwd(xc, rest, wa, ba, wx, bx, lam):
    def body(i, nt, ins, outs, scr):
        xc_ref, g_ref, wa_ref, ba_ref, wx_ref, bx_ref, lam_ref = ins
        carry = scr[0]

        @pl.when(i == 0)
        def _():
            carry[...] = jnp.zeros(carry.shape, F32)

        xv = xc_ref[...]
        r, ig, sp, a, mult = _lru_gates(xv, wa_ref[...], ba_ref[...], wx_ref[...], bx_ref[...], lam_ref[...])
        u = mult * (ig * xv)
        row = _iota(a.shape, 0)
        s = 1
        while s < LRU_TM:
            a_sh = jnp.where(row >= s, pltpu.roll(a, s, 0), 1.0)
            u_sh = jnp.where(row >= s, pltpu.roll(u, s, 0), 0.0)
            u = a * u_sh + u
            a = a * a_sh
            s *= 2
        h = u + a * carry[0:1, :]
        carry[0:1, :] = h[LRU_TM - 1:LRU_TM, :]
        outs[1][...] = h
        outs[0][...] = h * _gelu(g_ref[...])

    return _rowwise("lru_fwd", body, xc.shape[0], LRU_TM,
                    [('row', xc), ('col', rest, LRU_W, 3), ('full', wa), ('full', ba), ('full', wx),
                     ('full', bx), ('full', lam)],
                    [('row', LRU_W, F32), ('row', LRU_W, F32)], scratch=[pltpu.VMEM((8, LRU_W), F32)])


def _lru_bwd(xc, rest, h, d_mix, wa, ba, wx, bx, lam, wa_t, wx_t):
    def body(i, nt, ins, outs, scr):
        xc_ref, g_ref, h_ref, hp_ref, do_ref, wa_ref, ba_ref, wx_ref, bx_ref, lam_ref, wat_ref, wxt_ref = ins
        dxc_ref, dg_ref, dza_ref, dzi_ref, acc_ref = outs
        carry = scr[0]

        @pl.when(i == nt - 1)
        def _():
            carry[...] = jnp.zeros(carry.shape, F32)

        xv, gv, hv, d_out = xc_ref[...], g_ref[...], h_ref[...], do_ref[...]
        r, ig, sp, a, mult = _lru_gates(xv, wa_ref[...], ba_ref[...], wx_ref[...], bx_ref[...], lam_ref[...])
        dg_ref[...] = d_out * hv * _dgelu(gv)
        gsum = d_out * _gelu(gv)
        row = _iota(a.shape, 0)
        b = jnp.where(row < LRU_TM - 1, pltpu.roll(a, LRU_TM - 1, 0), 1.0)
        s = 1
        while s < LRU_TM:
            keep = row < LRU_TM - s
            b_sh = jnp.where(keep, pltpu.roll(b, LRU_TM - s, 0), 1.0)
            g_sh = jnp.where(keep, pltpu.roll(gsum, LRU_TM - s, 0), 0.0)
            gsum = gsum + b * g_sh
            b = b * b_sh
            s *= 2
        dh = gsum + b * carry[0:1, :]
        carry[0:1, :] = a[0:1, :] * dh[0:1, :]
        h_prev = _shift_down(hv, 1, jnp.where(i > 0, hp_ref[...], 0.0))
        du = dh
        dmult = du * ig * xv
        di = du * mult * xv
        dxc = du * mult * ig
        da = dh * h_prev - dmult * a / mult
        dlog = da * a
        dr = dlog * (-LRU_C) * sp
        acc_ref[2:3, :] += jnp.sum(dlog * (-LRU_C) * r, axis=0, keepdims=True)
        dza = dr * r * (1.0 - r)
        dzi = di * ig * (1.0 - ig)
        acc_ref[0:1, :] += jnp.sum(dza, axis=0, keepdims=True)
        acc_ref[1:2, :] += jnp.sum(dzi, axis=0, keepdims=True)
        dzab, dzib = dza.astype(BF16), dzi.astype(BF16)
        dza_ref[...] = dzab
        dzi_ref[...] = dzib
        dxc_ref[...] = dxc + _dot(dzab, wat_ref[...]) + _dot(dzib, wxt_ref[...])

    return _rowwise("lru_bwd", body, xc.shape[0], LRU_TM,
                    [('row', xc), ('col', rest, LRU_W, 3), ('row', h), ('prev8', h, LRU_W, 0),
                     ('col', d_mix, LRU_W, 1), ('full', wa), ('full', ba), ('full', wx), ('full', bx),
                     ('full', lam), ('full', wa_t), ('full', wx_t)],
                    [('row', LRU_W, F32), ('row', LRU_W, F32), ('row', LRU_W, BF16), ('row', LRU_W, BF16),
                     ('acc', (8, LRU_W), F32)],
                    scratch=[pltpu.VMEM((8, LRU_W), F32)], reverse=True)


FFN_TM = 512
FFN_TN = 1536


def _ffn_up(name, h2, w_gu):
    t, k = h2.shape
    nh = D_FFP // FFN_TN

    def kern(a_ref, wg_ref, wu_ref, g_ref, u_ref, act_ref):
        a = a_ref[...].astype(BF16)
        gv = _dot(a, wg_ref[...])
        uv = _dot(a, wu_ref[...])
        g_ref[...] = gv
        u_ref[...] = uv
        act_ref[...] = (_silu(gv) * uv).astype(BF16)

    tile = pl.BlockSpec((FFN_TM, FFN_TN), lambda i, j: (i, j))
    return pl.pallas_call(
        kern, name=name, grid=(t // FFN_TM, nh),
        in_specs=[pl.BlockSpec((FFN_TM, k), lambda i, j: (i, 0)),
                  pl.BlockSpec((k, FFN_TN), lambda i, j: (0, j)),
                  pl.BlockSpec((k, FFN_TN), lambda i, j: (0, nh + j))],
        out_specs=[tile, tile, tile],
        out_shape=[jax.ShapeDtypeStruct((t, D_FFP), F32), jax.ShapeDtypeStruct((t, D_FFP), F32),
                   jax.ShapeDtypeStruct((t, D_FFP), BF16)],
        compiler_params=_params(2))(h2, w_gu, w_gu)


def _ffn_down_bwd(name, dx, w_down_t, gate, up):
    t, k = dx.shape

    def kern(dx_ref, w_ref, g_ref, u_ref, dg_ref, du_ref):
        da = _dot(dx_ref[...].astype(BF16), w_ref[...])
        gv, uv = g_ref[...], u_ref[...]
        dg_ref[...] = (da * uv * _dsilu(gv)).astype(BF16)
        du_ref[...] = (da * _silu(gv)).astype(BF16)

    tile = pl.BlockSpec((FFN_TM, FFN_TN), lambda i, j: (i, j))
    shp = jax.ShapeDtypeStruct((t, D_FFP), BF16)
    return pl.pallas_call(
        kern, name=name, grid=(t // FFN_TM, D_FFP // FFN_TN),
        in_specs=[pl.BlockSpec((FFN_TM, k), lambda i, j: (i, 0)),
                  pl.BlockSpec((k, FFN_TN), lambda i, j: (0, j)), tile, tile],
        out_specs=[tile, tile], out_shape=[shp, shp], compiler_params=_params(2))(dx, w_down_t, gate, up)


def _layer_fwd(x, w, l):
    tag = "_l%d" % l
    h = _rmsnorm_fwd("norm_mix" + tag, x, w['norm_mix'][l])
    qkv = _mm("proj_qkv" + tag, [h], w['w_qkv'][l], tn=768, out_dtype=BF16)
    rest = _mm("proj_rest" + tag, [h], w['w_rest'][l], tn=896)
    att, lse = _attention_fwd(qkv)
    pre = _conv_fwd("ssd_conv" + tag, rest, SSD_CONV, 0, w['ssd_conv_w'][l], w['ssd_conv_b'][l])
    ssd, y, s_prev = _ssd_fwd(pre, rest, w['ssd_dt_bias'][l], w['ssd_a_log'][l], w['ssd_d'][l], w['ssd_norm'][l])
    xc = _conv_fwd("lru_conv" + tag, rest, LRU_W, 4, w['lru_conv_w'][l], w['lru_conv_b'][l])
    lru, hl = _lru_fwd(xc, rest, w['lru_wa'][l], w['lru_ba'][l], w['lru_wx'][l], w['lru_bx'][l], w['lru_lambda'][l])
    x_mid = _mm("proj_out" + tag, [att, ssd, lru], w['w_out'][l], res=x, tn=512)
    h2 = _rmsnorm_fwd("norm_ffn" + tag, x_mid, w['norm_ffn'][l])
    gate, up, act = _ffn_up("proj_gu" + tag, h2, w['w_gu'][l])
    x_next = _mm("proj_down" + tag, [act], w['w_down'][l], res=x_mid, tn=512)
    saved = dict(x=x, h=h, qkv=qkv, rest=rest, att=att, lse=lse, pre=pre, ssd=ssd, y=y, s_prev=s_prev, xc=xc,
                 lru=lru, hl=hl, x_mid=x_mid, h2=h2, gate=gate, up=up, act=act)
    return x_next, saved


def _layer_bwd(dx_next, sv, w, l):
    tag = "_l%d_b" % l
    t = dx_next.shape[0]
    g = {}
    dgate, dup = _ffn_down_bwd("d_act" + tag, dx_next, w['w_down_t'][l], sv['gate'], sv['up'])
    g['w_down'] = _mm_tn("dw_down" + tag, sv['act'], dx_next, tk=1536)
    dh2 = _mm("d_h2" + tag, [dgate, dup], w['w_gu_t'][l], tn=512)
    g['w_gate'] = _mm_tn("dw_gate" + tag, sv['h2'], dgate, tn=1536)
    g['w_up'] = _mm_tn("dw_up" + tag, sv['h2'], dup, tn=1536)
    dx_mid, acc = _rmsnorm_bwd("norm_ffn" + tag, dh2, sv['x_mid'], w['norm_ffn'][l], dx_next)
    g['norm_ffn'] = acc[0]
    d_att = _mm("d_att" + tag, [dx_mid], w['w_out_t'][l][:, :ATT_W], tn=512)
    d_mix = _mm("d_mix" + tag, [dx_mid], w['w_out_t'][l][:, ATT_W:], tn=512)
    g['w_out'] = jnp.concatenate([_mm_tn("dw_out%d" % k + tag, a, dx_mid)
                                  for k, a in enumerate((sv['att'], sv['ssd'], sv['lru']))], axis=0)
    dxc, dgl, dza, dzi, acc = _lru_bwd(sv['xc'], sv['rest'], sv['hl'], d_mix, w['lru_wa'][l], w['lru_ba'][l],
                                       w['lru_wx'][l], w['lru_bx'][l], w['lru_lambda'][l],
                                       w['lru_wa_t'][l], w['lru_wx_t'][l])
    g['lru_ba'], g['lru_bx'] = acc[0], acc[1]
    g['lru_lambda'] = acc[2] * (-_sigmoid(-w['lru_lambda'][l][0]))
    g['lru_wa'] = _diag_blocks(_mm_tn("dw_lru_a" + tag, sv['xc'], dza))
    g['lru_wx'] = _diag_blocks(_mm_tn("dw_lru_x" + tag, sv['xc'], dzi))
    dxl, acc = _conv_bwd("lru_conv" + tag, dxc, sv['rest'], LRU_W, 4, w['lru_conv_w'][l])
    g['lru_conv_w'], g['lru_conv_b'] = acc[:4], acc[4]
    dpre, dz, ddt, a128, a512 = _ssd_bwd(sv['pre'], sv['rest'], sv['y'], sv['s_prev'], d_mix, w['ssd_dt_bias'][l],
                                         w['ssd_a_log'][l], w['ssd_d'][l], w['ssd_norm'][l])
    g['ssd_dt_bias'], g['ssd_a_log'], g['ssd_d'] = a128[0, :N_HEADS], a128[1, :N_HEADS], a128[2, :N_HEADS]
    g['ssd_norm'] = a512[0]
    dxbc, acc = _conv_bwd("ssd_conv" + tag, dpre, sv['rest'], SSD_CONV, 0, w['ssd_conv_w'][l])
    g['ssd_conv_w'], g['ssd_conv_b'] = acc[:4], acc[4]
    dq, dk, dv = _attention_bwd(sv['qkv'], d_att, sv['att'], sv['lse'])
    pieces = [dq, dk, dv, dxbc, dz, dgl, dxl, ddt]
    dh = _mm("d_h" + tag, pieces, w['w_in_t'][l], tn=512)
    dws = [_mm_tn("dw_in%d" % k + tag, sv['h'], p) for k, p in enumerate(pieces)]
    g['w_in'] = jnp.concatenate([dws[0], dws[1], dws[2], dws[4], dws[3], _dt_tile_place(dws[7]), dws[5], dws[6]],
                                axis=1)
    dx, acc = _rmsnorm_bwd("norm_mix" + tag, dh, sv['x'], w['norm_mix'][l], dx_mid)
    g['norm_mix'] = acc[0]
    return dx, g


def _diag_blocks(m):
    return jnp.stack([m[64 * n:64 * (n + 1), 64 * n:64 * (n + 1)] for n in range(8)])


def _block_diag(w):
    eye = jnp.eye(8, dtype=w.dtype)
    return (w[:, :, None, :] * eye[:, None, :, None]).reshape(512, 512)


_ANY = pl.BlockSpec(memory_space=pl.ANY)
_MESH = pl.DeviceIdType.MESH


def _all_gather(name, xs):
    n = len(xs)

    def body(*refs):
        x_refs, out_refs = refs[:n], refs[n:2 * n]
        send_sems, recv_sems, local_sems = refs[2 * n:]
        x_, y_, c_ = lax.axis_index("x"), lax.axis_index("y"), lax.axis_index("c")
        me, sibling = (x_, y_, c_), (x_, y_, 1 - c_)
        chips = [(1 - x_, y_), (x_, 1 - y_), (1 - x_, 1 - y_)]

        def slot(a, px, py, pc):
            return out_refs[a].at[4 * px + 2 * py + pc]

        def copy(a, k, block, to, src=None):
            return pltpu.make_async_remote_copy(
                src_ref=slot(a, *block) if src is None else src, dst_ref=slot(a, *block),
                send_sem=send_sems.at[a, k], recv_sem=recv_sems.at[a, k], device_id=to, device_id_type=_MESH)

        mine = [pltpu.make_async_copy(x_refs[a], slot(a, *me), local_sems.at[a]) for a in range(n)]
        for cp in mine:
            cp.start()
        first = []
        for a in range(n):
            first.append(copy(a, 0, me, sibling, src=x_refs[a]))
            first += [copy(a, 1 + j, me, (*chip, c_), src=x_refs[a]) for j, chip in enumerate(chips)]
        for cp in first:
            cp.start()
        passed = []
        for j, chip in enumerate(chips):
            for a in range(n):
                copy(a, 1 + j, (*chip, c_), me).wait_recv()
                fwd = copy(a, 4 + j, (*chip, c_), sibling)
                fwd.start()
                passed.append(fwd)
        for a in range(n):
            copy(a, 0, sibling, me).wait_recv()
            for j, chip in enumerate(chips):
                copy(a, 4 + j, (*chip, 1 - c_), me).wait_recv()
        for cp in first + passed:
            cp.wait_send()
        for cp in mine:
            cp.wait()

    return pl.pallas_call(
        body, name=name, out_shape=[jax.ShapeDtypeStruct((N_DEV,) + x.shape, x.dtype) for x in xs],
        in_specs=[_ANY] * n, out_specs=[_ANY] * n,
        scratch_shapes=[pltpu.SemaphoreType.DMA((n, 7)), pltpu.SemaphoreType.DMA((n, 7)),
                        pltpu.SemaphoreType.DMA((n,))],
    )(*xs)


def _all_to_all(name, xs):
    n = len(xs)

    def body(*refs):
        x_refs, out_refs = refs[:n], refs[n:2 * n]
        send_sems, recv_sems, local_sems = refs[2 * n:]
        x_, y_, c_ = lax.axis_index("x"), lax.axis_index("y"), lax.axis_index("c")
        me = 4 * x_ + 2 * y_ + c_

        def peer(k):
            return ((1 - x_) if k & 4 else x_, (1 - y_) if k & 2 else y_, (1 - c_) if k & 1 else c_)

        def copy(a, k):
            px, py, pc = peer(k)
            return pltpu.make_async_remote_copy(
                src_ref=x_refs[a].at[4 * px + 2 * py + pc], dst_ref=out_refs[a].at[me],
                send_sem=send_sems.at[a, k - 1], recv_sem=recv_sems.at[a, k - 1],
                device_id=(px, py, pc), device_id_type=_MESH)

        def arrival(a, k):
            px, py, pc = peer(k)
            return pltpu.make_async_remote_copy(
                src_ref=x_refs[a].at[me], dst_ref=out_refs[a].at[4 * px + 2 * py + pc],
                send_sem=send_sems.at[a, k - 1], recv_sem=recv_sems.at[a, k - 1],
                device_id=(px, py, pc), device_id_type=_MESH)

        mine = [pltpu.make_async_copy(x_refs[a].at[me], out_refs[a].at[me], local_sems.at[a]) for a in range(n)]
        for cp in mine:
            cp.start()
        copies = [copy(a, k) for a in range(n) for k in range(1, N_DEV)]
        for cp in copies:
            cp.start()
        for a in range(n):
            for k in range(1, N_DEV):
                arrival(a, k).wait_recv()
        for cp in copies:
            cp.wait_send()
        for cp in mine:
            cp.wait()

    return pl.pallas_call(
        body, name=name, out_shape=[jax.ShapeDtypeStruct(x.shape, x.dtype) for x in xs],
        in_specs=[_ANY] * n, out_specs=[_ANY] * n,
        scratch_shapes=[pltpu.SemaphoreType.DMA((n, 7)), pltpu.SemaphoreType.DMA((n, 7)),
                        pltpu.SemaphoreType.DMA((n,))],
    )(*xs)


def _window_offset():
    me = 4 * lax.axis_index("x") + 2 * lax.axis_index("y") + lax.axis_index("c")
    return jnp.where(me < 6, me, me + 120)


def _place_w_in(shard):
    def body(i, nt, ins, outs, scr):
        outs[0][...] = pltpu.roll(ins[0][...], _window_offset(), 1).astype(BF16)

    return _rowwise("place_w_in", body, shard.shape[0], 256, [('row', shard)], [('row', IN_WIN, BF16)])[0]


def _adamw(name, w, m, v, g, tr, from_window=False):
    s_parts, r, c = g.shape
    assert r % tr == 0

    def kern(w_ref, m_ref, v_ref, g_ref, go_ref, d_ref, mo_ref, vo_ref):
        gs = g_ref[0].astype(F32)
        for s in range(1, s_parts):
            gs = gs + g_ref[s].astype(F32)
        if from_window:
            gs = pltpu.roll(gs, c - _window_offset(), 1)
        wv = w_ref[...]
        m2 = ADAM_B1 * m_ref[...] + (1.0 - ADAM_B1) * gs
        v2 = ADAM_B2 * v_ref[...] + (1.0 - ADAM_B2) * (gs * gs)
        m_hat = m2 / (1.0 - ADAM_B1 ** ADAM_STEP)
        v_hat = v2 / (1.0 - ADAM_B2 ** ADAM_STEP)
        go_ref[...] = gs
        d_ref[...] = -ADAM_LR * (m_hat / (jnp.sqrt(v_hat) + ADAM_EPS) + ADAM_WD * wv)
        mo_ref[...] = m2
        vo_ref[...] = v2

    spec = pl.BlockSpec((tr, c), lambda i: (i, 0))
    shp = jax.ShapeDtypeStruct((r, c), F32)
    return pl.pallas_call(kern, name=name, grid=(r // tr,),
                          in_specs=[spec, spec, spec, pl.BlockSpec((s_parts, tr, c), lambda i: (0, i, 0))],
                          out_specs=[spec] * 4, out_shape=[shp] * 4, compiler_params=_params(1))(w, m, v, g)


def _pack(arrs, rows, lead=0):
    parts = []
    for a in arrs:
        flat = a.reshape(a.shape[:lead] + (-1,))
        pad = (-flat.shape[-1]) % LANES
        if pad:
            flat = jnp.pad(flat, [(0, 0)] * lead + [(0, pad)])
        parts.append(flat)
    flat = jnp.concatenate(parts, axis=-1)
    pad = rows * LANES - flat.shape[-1]
    assert pad >= 0
    if pad:
        flat = jnp.pad(flat, [(0, 0)] * lead + [(0, pad)])
    return flat.reshape(flat.shape[:lead] + (rows, LANES))


def _unpack(buf, shapes, lead=0):
    flat = buf.reshape(buf.shape[:lead] + (-1,))
    out, off = [], 0
    for shp in shapes:
        n = math.prod(shp)
        out.append(flat[..., off:off + n].reshape(buf.shape[:lead] + tuple(shp)))
        off += n + ((-n) % LANES)
    return out


BIG = ('w_in', 'w_out', 'w_gate', 'w_up', 'w_down')
CONV =('ssd_conv_w', 'lru_conv_w')
CONV_SHARD_SHAPES = ((DEPTH, 4, SSD_CONV // N_DEV), (DEPTH, 4, LRU_W // N_DEV))
CONV_ROWS = 16
SMALL = ('norm_mix', 'ssd_conv_b', 'ssd_dt_bias', 'ssd_a_log', 'ssd_d', 'ssd_norm', 'lru_conv_b', 'lru_wa',
         'lru_ba', 'lru_wx', 'lru_bx', 'lru_lambda', 'norm_ffn', 'norm_final')
SMALL_ROWS = 1280
SMALL_TILE = 256
WEIGHTS = ('norm_mix', 'w_in', 'ssd_conv_w', 'ssd_conv_b', 'ssd_dt_bias', 'ssd_a_log', 'ssd_d', 'ssd_norm',
           'lru_conv_w', 'lru_conv_b', 'lru_wa', 'lru_ba', 'lru_wx', 'lru_bx', 'lru_lambda', 'w_out', 'norm_ffn',
           'w_gate', 'w_up', 'w_down', 'norm_final')


def _join_cols(a):
    return jnp.transpose(a, (1, 2, 0, 3)).reshape(a.shape[1], a.shape[2], -1)


def _join_rows(a):
    return jnp.transpose(a, (1, 0, 2, 3)).reshape(a.shape[1], -1, a.shape[3])


def _split_cols(a):
    l, r, c = a.shape
    return jnp.transpose(a.reshape(l, r, N_DEV, c // N_DEV), (2, 0, 1, 3)).reshape(N_DEV, l * r, c // N_DEV)


def _split_rows(a):
    l, r, c = a.shape
    return jnp.transpose(a.reshape(l, N_DEV, r // N_DEV, c), (1, 0, 2, 3)).reshape(N_DEV, l * r // N_DEV, c)


def _shard_form(k, a):
    if k == 'w_in':
        return jnp.pad(a.reshape(-1, IN_SHARD), ((0, 0), (0, IN_WIN - IN_SHARD)))
    if k in ('w_gate', 'w_up'):
        return jnp.pad(a.reshape(-1, FF_SHARD), ((0, 0), (0, FF_SHARD_P - FF_SHARD)))
    if k == 'w_down':
        return jnp.pad(a, ((0, 0), (0, FF_SHARD_P - FF_SHARD), (0, 0))).reshape(-1, D_MODEL)
    return a.reshape(-1, D_MODEL)


def _shard_back(k, a):
    if k == 'w_in':
        return a[:, :IN_SHARD].reshape(DEPTH, D_MODEL, IN_SHARD)
    if k in ('w_gate', 'w_up'):
        return a[:, :FF_SHARD].reshape(DEPTH, D_MODEL, FF_SHARD)
    if k == 'w_down':
        return a.reshape(DEPTH, FF_SHARD_P, D_MODEL)[:, :FF_SHARD]
    return a.reshape(DEPTH, D_MIX // N_DEV, D_MODEL)


def _dt_tile_place(a):
    zeros = jnp.zeros(a.shape[:-1] + (LANES - N_HEADS,), a.dtype)
    return jnp.concatenate([a[..., :6], zeros, a[..., 6:8]], axis=-1)


def _dt_tile_heads(tile):
    zeros = jnp.zeros(tile.shape[:-1] + (LANES - N_HEADS,), tile.dtype)
    return jnp.concatenate([tile[..., :6], tile[..., 126:128], zeros], axis=-1)


def _layout_from_windows(win):
    r = win.shape[1]
    main = jnp.concatenate([win[j][:, :512] for j in range(N_DEV)] + [jnp.zeros((r, LANES), win.dtype)], axis=1)
    gap = jnp.zeros((r, 384), win.dtype)
    tails = [jnp.zeros((r, 512), win.dtype)]
    for j in range(N_DEV - 1):
        tails += [win[j][:, 512:], gap]
    tails.append(win[N_DEV - 1][:, 512:])
    return main + jnp.concatenate(tails, axis=1)


def _prepare_weights(p, full):
    w = {}
    w_in = full['w_in']
    w_qkv = w_in[:, :, :D_MIX]
    dt_cols = _dt_tile_heads(w_in[:, :, 3072:3200])
    w_rest = jnp.concatenate([w_in[:, :, 2048:3072], w_in[:, :, 1536:2048], w_in[:, :, 3200:3712],
                              w_in[:, :, 3712:4224], dt_cols], axis=2)
    w['w_qkv'], w['w_rest'] = w_qkv, w_rest
    w['w_in_t'] = jnp.transpose(jnp.concatenate([w_qkv, w_rest], axis=2), (0, 2, 1))
    w['w_out'] = full['w_out']
    w['w_out_t'] = jnp.transpose(full['w_out'], (0, 2, 1))
    w['w_gu'] = jnp.concatenate([full['w_gate'], full['w_up']], axis=2)
    w['w_gu_t'] = jnp.transpose(w['w_gu'], (0, 2, 1))
    w['w_down'] = full['w_down']
    w['w_down_t'] = jnp.transpose(full['w_down'], (0, 2, 1))
    for k in ('norm_mix', 'ssd_conv_b', 'ssd_norm', 'lru_conv_b', 'lru_ba', 'lru_bx', 'lru_lambda', 'norm_ffn'):
        w[k] = p[k][:, None, :]
    for k in ('ssd_dt_bias', 'ssd_a_log', 'ssd_d'):
        w[k] = jnp.pad(p[k], ((0, 0), (0, LANES - N_HEADS)))[:, None, :]
    for k in CONV:
        w[k] = jnp.pad(full[k], ((0, 0), (0, 4), (0, 0)))
    for k in ('lru_wa', 'lru_wx'):
        bd = jnp.stack([_block_diag(p[k][l]) for l in range(DEPTH)]).astype(BF16)
        w[k] = bd
        w[k + '_t'] = jnp.transpose(bd, (0, 2, 1))
    return w


def _local_step(x, target, w, norm_final):
    saved = []
    for l in range(DEPTH):
        x, sv = _layer_fwd(x, w, l)
        saved.append(sv)
    dx, loss_acc, dgf = _final_loss(x, norm_final[None, :], target)
    grads = [None] * DEPTH
    for l in reversed(range(DEPTH)):
        dx, grads[l] = _layer_bwd(dx, saved[l], w, l)
    g = {k: jnp.stack([grads[l][k] for l in range(DEPTH)]) for k in grads[0]}
    g['norm_final'] = dgf[0]
    return loss_acc[0, 0], dx, g


def kernel(x, norm_mix, w_in, ssd_conv_w, ssd_conv_b, ssd_dt_bias, ssd_a_log, ssd_d, ssd_norm, lru_conv_w, lru_conv_b, lru_wa, lru_ba, lru_wx, lru_bx, lru_lambda, w_out, norm_ffn, w_gate, w_up, w_down, norm_final, loss_target, m_norm_mix, m_w_in, m_ssd_conv_w, m_ssd_conv_b, m_ssd_dt_bias, m_ssd_a_log, m_ssd_d, m_ssd_norm, m_lru_conv_w, m_lru_conv_b, m_lru_wa, m_lru_ba, m_lru_wx, m_lru_bx, m_lru_lambda, m_w_out, m_norm_ffn, m_w_gate, m_w_up, m_w_down, m_norm_final, v_norm_mix, v_w_in, v_ssd_conv_w, v_ssd_conv_b, v_ssd_dt_bias, v_ssd_a_log, v_ssd_d, v_ssd_norm, v_lru_conv_w, v_lru_conv_b, v_lru_wa, v_lru_ba, v_lru_wx, v_lru_bx, v_lru_lambda, v_w_out, v_norm_ffn, v_w_gate, v_w_up, v_w_down, v_norm_final):
    args = (norm_mix, w_in, ssd_conv_w, ssd_conv_b, ssd_dt_bias, ssd_a_log, ssd_d, ssd_norm, lru_conv_w, lru_conv_b, lru_wa, lru_ba, lru_wx, lru_bx, lru_lambda, w_out, norm_ffn, w_gate, w_up, w_down, norm_final)
    margs = (m_norm_mix, m_w_in, m_ssd_conv_w, m_ssd_conv_b, m_ssd_dt_bias, m_ssd_a_log, m_ssd_d, m_ssd_norm, m_lru_conv_w, m_lru_conv_b, m_lru_wa, m_lru_ba, m_lru_wx, m_lru_bx, m_lru_lambda, m_w_out, m_norm_ffn, m_w_gate, m_w_up, m_w_down, m_norm_final)
    vargs = (v_norm_mix, v_w_in, v_ssd_conv_w, v_ssd_conv_b, v_ssd_dt_bias, v_ssd_a_log, v_ssd_d, v_ssd_norm, v_lru_conv_w, v_lru_conv_b, v_lru_wa, v_lru_ba, v_lru_wx, v_lru_bx, v_lru_lambda, v_w_out, v_norm_ffn, v_w_gate, v_w_up, v_w_down, v_norm_final)
    p = dict(zip(WEIGHTS, args))
    pm = dict(zip(WEIGHTS, margs))
    pv = dict(zip(WEIGHTS, vargs))

    forms = {k: _shard_form(k, p[k]) for k in BIG}
    send = [_place_w_in(forms['w_in'])] + [forms[k].astype(BF16) for k in BIG[1:]]
    got = _all_gather("gather_weights", send + [_pack([p[k] for k in CONV], CONV_ROWS)])
    full = {'w_in': _layout_from_windows(got[0]).reshape(DEPTH, D_MODEL, IN_COLS_P),
            'w_out': _join_rows(got[1].reshape(N_DEV, DEPTH, D_MIX // N_DEV, D_MODEL)),
            'w_gate': _join_cols(got[2].reshape(N_DEV, DEPTH, D_MODEL, FF_SHARD_P)),
            'w_up': _join_cols(got[3].reshape(N_DEV, DEPTH, D_MODEL, FF_SHARD_P)),
            'w_down': _join_rows(got[4].reshape(N_DEV, DEPTH, FF_SHARD_P, D_MODEL))}
    for k, a in zip(CONV, _unpack(got[5], CONV_SHARD_SHAPES, lead=1)):
        full[k] = _join_cols(a)
    w = _prepare_weights(p, full)

    loss_local, dx, g = _local_step(x[0], loss_target[0], w, norm_final)
    loss = lax.psum(loss_local, ("x", "y", "c"))

    small_g = _all_gather("gather_small_grads", [_pack([g[k] for k in SMALL + CONV], SMALL_ROWS)])[0]
    zeros = [jnp.zeros_like(g[k]) for k in CONV]
    res_small = _adamw("adamw_small", _pack([p[k] for k in SMALL] + zeros, SMALL_ROWS),
                       _pack([pm[k] for k in SMALL] + zeros, SMALL_ROWS),
                       _pack([pv[k] for k in SMALL] + zeros, SMALL_ROWS), small_g, SMALL_TILE)
    small_shapes = [g[k].shape for k in SMALL + CONV]
    out = {kind: {} for kind in range(4)}
    for kind in range(4):
        for k, a in zip(SMALL + CONV, _unpack(res_small[kind], small_shapes)):
            out[kind][k] = a
    me = 4 * lax.axis_index("x") + 2 * lax.axis_index("y") + lax.axis_index("c")
    conv_g = []
    for k, shp in zip(CONV, CONV_SHARD_SHAPES):
        conv_g.append(lax.dynamic_slice_in_dim(out[0][k], me * shp[2], shp[2], axis=2))
    res_conv = _adamw("adamw_conv", _pack([p[k] for k in CONV], CONV_ROWS), _pack([pm[k] for k in CONV], CONV_ROWS),
                      _pack([pv[k] for k in CONV], CONV_ROWS), _pack(conv_g, CONV_ROWS)[None], CONV_ROWS)
    for kind in range(4):
        for k, a in zip(CONV, _unpack(res_conv[kind], CONV_SHARD_SHAPES)):
            out[kind][k] = a

    g_in = g['w_in'].reshape(DEPTH * D_MODEL, IN_COLS_P)
    dest = [jnp.stack([g_in[:, 512 * j:512 * j + IN_WIN] for j in range(N_DEV)]), _split_rows(g['w_out']),
            _split_cols(g['w_gate']), _split_cols(g['w_up']), _split_rows(g['w_down'])]
    parts = _all_to_all("exchange_big_grads", [a.astype(BF16) for a in dest])
    tiles = {'w_in': 256, 'w_out': 128, 'w_gate': 512, 'w_up': 512, 'w_down': 256}
    for k, part in zip(BIG, parts):
        res = _adamw("adamw_" + k, forms[k], _shard_form(k, pm[k]), _shard_form(k, pv[k]), part, tiles[k],
                     from_window=(k == 'w_in'))
        for kind in range(4):
            out[kind][k] = _shard_back(k, res[kind])

    outs = [loss, dx[None]]
    for kind in range(4):
        outs += [out[kind][k] for k in WEIGHTS]
    return tuple(outs)
```

```python
import functools
import math

import jax
import jax.numpy as jnp
from jax import lax
from jax.experimental import pallas as pl
from jax.experimental.pallas import tpu as pltpu

F32 = jnp.float32
BF16 = jnp.bfloat16

N_DEV = 8
DEPTH = 2
D_MODEL = 1024
ATT_W = 512
HEAD_DIM = 64
N_HEADS = 8
ATT_BLOCK = 128
ATT_DILATIONS = (16, 4, 1)
SSD_W = 512
SSD_STATE = 128
SSD_CONV = 1024
SSD_CHUNK = 128
LRU_W = 512
LRU_C = 8.0
D_MIX = 1536
D_FF = 2816
FF_SHARD = D_FF // N_DEV
FF_SHARD_P = 384
D_FFP = N_DEV * FF_SHARD_P
IN_COLS = 4104
IN_SHARD = IN_COLS // N_DEV
IN_WIN = 640
IN_COLS_P = 4224
REST_COLS = 2688
NORM_EPS = 1e-6
SSD_NORM_EPS = 1e-5
NEG = -1e30

ADAM_LR = 0.001
ADAM_B1 = 0.9
ADAM_B2 = 0.999
ADAM_EPS = 1e-08
ADAM_WD = 0.01
ADAM_STEP = 10

LANES = 128
VMEM_LIMIT = 52 * 1024 * 1024
HI = lax.Precision.HIGHEST


def _sigmoid(x):
    return 1.0 / (1.0 + jnp.exp(-x))


def _silu(x):
    return x * _sigmoid(x)


def _dsilu(x):
    s = _sigmoid(x)
    return s * (1.0 + x * (1.0 - s))


def _softplus(x):
    return jnp.maximum(x, 0.0) + jnp.log(1.0 + jnp.exp(-jnp.abs(x)))


_GELU_C = math.sqrt(2.0 / math.pi)


def _gelu(x):
    return 0.5 * x * (1.0 + jnp.tanh(_GELU_C * (x + 0.044715 * x * x * x)))


def _dgelu(x):
    t = jnp.tanh(_GELU_C * (x + 0.044715 * x * x * x))
    return 0.5 * (1.0 + t) + 0.5 * x * (1.0 - t * t) * _GELU_C * (1.0 + 3.0 * 0.044715 * x * x)


def _dot(a, b):
    return jnp.dot(a, b, preferred_element_type=F32)


def _dot_nt(a, b):
    return lax.dot_general(a, b, (((1,), (1,)), ((), ())), preferred_element_type=F32)


def _dot_tn(a, b):
    return lax.dot_general(a, b, (((0,), (0,)), ((), ())), preferred_element_type=F32)


def _dot_hi(a, b):
    return jnp.dot(a, b, preferred_element_type=F32, precision=HI)


def _iota(shape, axis):
    return lax.broadcasted_iota(jnp.int32, shape, axis)


def _shift_down(x, s, prev8):
    xs = pltpu.roll(x, s, 0)
    ps = pltpu.roll(prev8, s, 0)
    top = jnp.concatenate([ps, x[8:]], axis=0)
    return jnp.where(_iota(x.shape, 0) < s, top, xs)


def _shift_up(x, s, next8):
    tm = x.shape[0]
    xs = pltpu.roll(x, tm - s, 0)
    ns = pltpu.roll(next8, 8 - s, 0)
    bottom = jnp.concatenate([x[:tm - 8], ns], axis=0)
    return jnp.where(_iota(x.shape, 0) >= tm - s, bottom, xs)


def _expand_mat():
    return jnp.where(_iota((LANES, SSD_W), 1) // HEAD_DIM == _iota((LANES, SSD_W), 0), 1.0, 0.0).astype(F32)


def _reduce_mat():
    return jnp.where(_iota((SSD_W, LANES), 0) // HEAD_DIM == _iota((SSD_W, LANES), 1), 1.0, 0.0).astype(F32)


def _params(n_grid):
    return pltpu.CompilerParams(dimension_semantics=("arbitrary",) * n_grid, vmem_limit_bytes=VMEM_LIMIT)


def _rowwise(name, body, n_rows, tm, ins, outs, scratch=(), reverse=False):
    nt = n_rows // tm
    assert nt * tm == n_rows and tm % 8 == 0
    r8 = tm // 8
    last8 = n_rows // 8 - 1

    def pos(s):
        return (nt - 1 - s) if reverse else s

    in_specs, args = [], []
    for spec in ins:
        kind, arr = spec[0], spec[1]
        args.append(arr)
        if kind == 'row':
            in_specs.append(pl.BlockSpec((tm, arr.shape[1]), lambda s: (pos(s), 0)))
        elif kind == 'col':
            in_specs.append(pl.BlockSpec((tm, spec[2]), functools.partial(lambda s, j: (pos(s), j), j=spec[3])))
        elif kind == 'full':
            in_specs.append(pl.BlockSpec(arr.shape, functools.partial(lambda s, n: (0,) * n, n=arr.ndim)))
        elif kind == 'prev8':
            in_specs.append(pl.BlockSpec((8, spec[2]), functools.partial(
                lambda s, j: (jnp.maximum(pos(s) * r8 - 1, 0), j), j=spec[3])))
        elif kind == 'next8':
            in_specs.append(pl.BlockSpec((8, spec[2]), functools.partial(
                lambda s, j: (jnp.minimum((pos(s) + 1) * r8, last8), j), j=spec[3])))
        else:
            raise ValueError(kind)
    out_specs, out_shape, acc_idx = [], [], []
    for k, spec in enumerate(outs):
        if spec[0] == 'row':
            out_specs.append(pl.BlockSpec((tm, spec[1]), lambda s: (pos(s), 0)))
            out_shape.append(jax.ShapeDtypeStruct((n_rows, spec[1]), spec[2]))
        else:
            out_specs.append(pl.BlockSpec(spec[1], lambda s: (0, 0)))
            out_shape.append(jax.ShapeDtypeStruct(spec[1], spec[2]))
            acc_idx.append(k)
    n_in, n_out = len(ins), len(outs)

    def kern(*refs):
        s = pl.program_id(0)
        in_refs, out_refs, scr = refs[:n_in], refs[n_in:n_in + n_out], refs[n_in + n_out:]

        @pl.when(s == 0)
        def _():
            for k in acc_idx:
                out_refs[k][...] = jnp.zeros(out_refs[k].shape, out_refs[k].dtype)

        body(pos(s), nt, in_refs, out_refs, scr)

    res = pl.pallas_call(kern, name=name, grid=(nt,), in_specs=in_specs, out_specs=out_specs,
                         out_shape=out_shape, scratch_shapes=list(scratch), compiler_params=_params(1))(*args)
    return res


def _mm(name, a_list, b, *, res=None, out_dtype=F32, tm=512, tn=None):
    n_rows = a_list[0].shape[0]
    k_total, n = b.shape
    ks = [a.shape[1] for a in a_list]
    assert sum(ks) == k_total
    tn = n if tn is None else tn
    assert n_rows % tm == 0 and n % tn == 0
    na = len(a_list)

    def kern(*refs):
        a_refs, b_ref, o_ref = refs[:na], refs[na], refs[-1]
        acc, off = None, 0
        for a_ref, kp in zip(a_refs, ks):
            part = _dot(a_ref[...].astype(BF16), b_ref[off:off + kp, :])
            acc = part if acc is None else acc + part
            off += kp
        if res is not None:
            acc = acc + refs[na + 1][...]
        o_ref[...] = acc.astype(out_dtype)

    in_specs = [pl.BlockSpec((tm, kp), lambda i, j: (i, 0)) for kp in ks]
    in_specs.append(pl.BlockSpec((k_total, tn), lambda i, j: (0, j)))
    args = list(a_list) + [b]
    if res is not None:
        in_specs.append(pl.BlockSpec((tm, tn), lambda i, j: (i, j)))
        args.append(res)
    return pl.pallas_call(kern, name=name, grid=(n_rows // tm, n // tn), in_specs=in_specs,
                          out_specs=pl.BlockSpec((tm, tn), lambda i, j: (i, j)),
                          out_shape=jax.ShapeDtypeStruct((n_rows, n), out_dtype),
                          compiler_params=_params(2))(*args)


def _mm_tn(name, a, g, *, a_col=None, g_col=None, tk=None, tn=None, tt=512):
    n_rows = a.shape[0]
    k = a.shape[1] if a_col is None else a_col[0]
    a_j = 0 if a_col is None else a_col[1]
    n = g.shape[1] if g_col is None else g_col[0]
    g_j = 0 if g_col is None else g_col[1]
    tk = k if tk is None else tk
    tn = n if tn is None else tn
    assert k % tk == 0 and n % tn == 0 and n_rows % tt == 0
    kb = k // tk
    nbk = n // tn

    def kern(a_ref, g_ref, o_ref):
        t = pl.program_id(2)

        @pl.when(t == 0)
        def _():
            o_ref[...] = jnp.zeros(o_ref.shape, F32)

        o_ref[...] += _dot_tn(a_ref[...].astype(BF16), g_ref[...].astype(BF16))

    return pl.pallas_call(
        kern, name=name, grid=(kb, n // tn, n_rows // tt),
        in_specs=[pl.BlockSpec((tt, tk), lambda i, j, t: (t, a_j * kb + i)),
                  pl.BlockSpec((tt, tn), lambda i, j, t: (t, g_j * nbk + j))],
        out_specs=pl.BlockSpec((tk, tn), lambda i, j, t: (i, j)),
        out_shape=jax.ShapeDtypeStruct((k, n), F32), compiler_params=_params(3))(a, g)


def _rmsnorm_fwd(name, x, g):
    def body(i, nt, ins, outs, scr):
        xv = ins[0][...]
        rstd = lax.rsqrt(jnp.mean(xv * xv, axis=-1, keepdims=True) + NORM_EPS)
        outs[0][...] = (xv * rstd * ins[1][...]).astype(BF16)

    return _rowwise(name, body, x.shape[0], 512, [('row', x), ('full', g)], [('row', x.shape[1], BF16)])[0]


def _rmsnorm_bwd(name, dh, x, g, dres):
    d = x.shape[1]

    def body(i, nt, ins, outs, scr):
        dy, xv, gv, dr = ins[0][...], ins[1][...], ins[2][...], ins[3][...]
        rstd = lax.rsqrt(jnp.mean(xv * xv, axis=-1, keepdims=True) + NORM_EPS)
        xhat = xv * rstd
        outs[1][0:1, :] += jnp.sum(dy * xhat, axis=0, keepdims=True)
        dxh = dy * gv
        outs[0][...] = dr + rstd * (dxh - xhat * jnp.mean(dxh * xhat, axis=-1, keepdims=True))

    return _rowwise(name, body, x.shape[0], 512, [('row', dh), ('row', x), ('full', g), ('row', dres)],
                    [('row', d, F32), ('acc', (8, d), F32)])


def _final_loss(x, g, target):
    d = x.shape[1]

    def body(i, nt, ins, outs, scr):
        xv, gv, tv = ins[0][...], ins[1][...], ins[2][...]
        rstd = lax.rsqrt(jnp.mean(xv * xv, axis=-1, keepdims=True) + NORM_EPS)
        xhat = xv * rstd
        err = xhat * gv - tv
        row_loss = 0.5 * jnp.mean(err * err, axis=-1, keepdims=True)
        outs[1][...] += jnp.sum(row_loss, axis=0, keepdims=True)
        dy = err * (1.0 / d)
        outs[2][0:1, :] += jnp.sum(dy * xhat, axis=0, keepdims=True)
        dxh = dy * gv
        outs[0][...] = rstd * (dxh - xhat * jnp.mean(dxh * xhat, axis=-1, keepdims=True))

    return _rowwise("final_loss", body, x.shape[0], 512, [('row', x), ('full', g), ('row', target)],
                    [('row', d, F32), ('acc', (8, LANES), F32), ('acc', (8, d), F32)])


def _conv_fwd(name, src, width, idx, w, b):
    def body(i, nt, ins, outs, scr):
        xv = ins[0][...]
        prev = jnp.where(i > 0, ins[1][...], 0.0)
        wv = ins[2][...]
        y = ins[3][...] + wv[3:4, :] * xv
        for s in (1, 2, 3):
            y = y + wv[3 - s:4 - s, :] * _shift_down(xv, s, prev)
        outs[0][...] = y

    return _rowwise(name, body, src.shape[0], 512,
                    [('col', src, width, idx), ('prev8', src, width, idx), ('full', w), ('full', b)],
                    [('row', width, F32)])[0]


def _conv_bwd(name, dpre, src, width, idx, w):
    def body(i, nt, ins, outs, scr):
        dy = ins[0][...]
        nxt = jnp.where(i < nt - 1, ins[1][...], 0.0)
        xv = ins[2][...]
        prev = jnp.where(i > 0, ins[3][...], 0.0)
        wv = ins[4][...]
        dx = wv[3:4, :] * dy
        outs[1][3:4, :] += jnp.sum(dy * xv, axis=0, keepdims=True)
        outs[1][4:5, :] += jnp.sum(dy, axis=0, keepdims=True)
        for s in (1, 2, 3):
            dx = dx + wv[3 - s:4 - s, :] * _shift_up(dy, s, nxt)
            outs[1][3 - s:4 - s, :] += jnp.sum(dy * _shift_down(xv, s, prev), axis=0, keepdims=True)
        outs[0][...] = dx

    return _rowwise(name, body, src.shape[0], 512,
                    [('row', dpre), ('next8', dpre, width, 0), ('col', src, width, idx),
                     ('prev8', src, width, idx), ('full', w)],
                    [('row', width, F32), ('acc', (8, width), F32)])


ATT_STEP_BLOCKS = 4


def _att_bias(not_first, dil, head):
    qi = _iota((ATT_BLOCK, 2 * ATT_BLOCK), 0)
    ki = _iota((ATT_BLOCK, 2 * ATT_BLOCK), 1)
    dist = ATT_BLOCK + qi - ki
    valid = (dist >= 0) & (dist <= ATT_BLOCK) & (not_first | (ki >= ATT_BLOCK))
    slope = 2.0 ** (-(head + 1))
    return jnp.where(valid, (-slope * dil) * dist.astype(F32), NEG)


def _head_mask():
    lane = _iota((ATT_BLOCK, LANES), 1)
    return lane < HEAD_DIM


def _att_q_specs(dil, nb, bq):
    big = (bq * ATT_BLOCK, ATT_W)
    one = (ATT_BLOCK, ATT_W)
    specs = [pl.BlockSpec(big, lambda r, n: (n, 3 * r)),
             pl.BlockSpec(big, lambda r, n: (n, 3 * r + 1)),
             pl.BlockSpec(one, lambda r, n: (jnp.maximum(n * bq - 1, 0), 3 * r + 1)),
             pl.BlockSpec(big, lambda r, n: (n, 3 * r + 2)),
             pl.BlockSpec(one, lambda r, n: (jnp.maximum(n * bq - 1, 0), 3 * r + 2))]
    wide = pl.BlockSpec(big, lambda r, n: (n, r))
    stat = pl.BlockSpec((bq * ATT_BLOCK, LANES), lambda r, n: (n, r))
    return specs, wide, stat


def _att_fwd(dil, qkv_v):
    n_l = qkv_v.shape[0]
    nb = n_l // ATT_BLOCK
    bq = min(ATT_STEP_BLOCKS, nb)
    scale = HEAD_DIM ** -0.5

    def kern(q_ref, kc_ref, kp_ref, vc_ref, vp_ref, m_out, l_out, a_out):
        n = pl.program_id(1)
        low = _head_mask()
        lane = _iota((ATT_BLOCK, LANES), 1)
        for b in range(bq):
            rows = slice(ATT_BLOCK * b, ATT_BLOCK * (b + 1))
            prev = slice(ATT_BLOCK * (b - 1), ATT_BLOCK * b)
            not_first = (n * bq + b) > 0
            m_acc = jnp.zeros((ATT_BLOCK, LANES), F32)
            l_acc = jnp.zeros((ATT_BLOCK, LANES), F32)
            for p in range(N_HEADS // 2):
                sl = slice(LANES * p, LANES * (p + 1))
                q2 = q_ref[rows, sl].astype(F32)
                k_prev = kp_ref[:, sl] if b == 0 else kc_ref[prev, sl]
                v_prev = vp_ref[:, sl] if b == 0 else vc_ref[prev, sl]
                k2 = jnp.concatenate([k_prev, kc_ref[rows, sl]], axis=0).astype(BF16)
                v2 = jnp.concatenate([v_prev, vc_ref[rows, sl]], axis=0).astype(BF16)
                res = []
                for e in range(2):
                    h = 2 * p + e
                    keep = low if e == 0 else jnp.logical_not(low)
                    qe = jnp.where(keep, q2, 0.0).astype(BF16)
                    s = _dot_nt(qe, k2) * scale + _att_bias(not_first, dil, h)
                    m_new = jnp.max(s, axis=-1, keepdims=True)
                    pe = jnp.exp(s - m_new)
                    l_new = jnp.sum(pe, axis=-1, keepdims=True)
                    m_acc = jnp.where(lane == h, m_new, m_acc)
                    l_acc = jnp.where(lane == h, l_new, l_acc)
                    res.append(_dot(pe.astype(BF16), v2))
                a_out[rows, sl] = jnp.where(low, res[0], res[1])
            m_out[rows, :] = m_acc
            l_out[rows, :] = l_acc

    specs, wide, stat = _att_q_specs(dil, nb, bq)
    shp_s = jax.ShapeDtypeStruct((n_l, dil * LANES), F32)
    shp_a = jax.ShapeDtypeStruct((n_l, dil * ATT_W), F32)
    return pl.pallas_call(kern, name="att_fwd_d%d" % dil, grid=(dil, nb // bq), in_specs=specs,
                          out_specs=[stat, stat, wide], out_shape=[shp_s, shp_s, shp_a],
                          compiler_params=_params(2))(*([qkv_v] * 5))


def _att_merge(parts):
    n_pat = len(parts)

    def body(i, nt, ins, outs, scr):
        ms = [ins[3 * g][...] for g in range(n_pat)]
        m_all = functools.reduce(jnp.maximum, ms)
        expand = _expand_mat()
        num, den = None, None
        for g in range(n_pat):
            e = jnp.exp(ms[g] - m_all)
            d_g = ins[3 * g + 1][...] * e
            n_g = ins[3 * g + 2][...] * _dot_hi(e, expand)
            num = n_g if num is None else num + n_g
            den = d_g if den is None else den + d_g
        real = _iota(den.shape, 1) < N_HEADS
        outs[0][...] = num / _dot_hi(den, expand)
        outs[1][...] = jnp.where(real, m_all + jnp.log(jnp.where(real, den, 1.0)), 0.0)

    ins = []
    for m, l, acc in parts:
        ins += [('row', m), ('row', l), ('row', acc)]
    return _rowwise("att_merge", body, parts[0][0].shape[0], 512, ins,
                    [('row', ATT_W, F32), ('row', LANES, F32)])


def _att_delta(d_att, out):
    def body(i, nt, ins, outs, scr):
        outs[0][...] = _dot_hi(ins[0][...] * ins[1][...], _reduce_mat())

    return _rowwise("att_delta", body, out.shape[0], 512, [('row', d_att), ('row', out)],
                    [('row', LANES, F32)])[0]


def _att_bwd(dil, qkv_v, do_v, lse_v, delta_v, dkv_in):
    n_l = qkv_v.shape[0]
    nb = n_l // ATT_BLOCK
    bq = min(ATT_STEP_BLOCKS, nb)
    steps = nb // bq
    first = dkv_in is None
    scale = HEAD_DIM ** -0.5

    def kern(*refs):
        j = pl.program_id(1)
        k_ref, v_ref, qc_ref, qn_ref, doc_ref, don_ref, lc_ref, ln_ref, dc_ref, dn_ref = refs[:10]
        dk_out, dv_out, dq_out, carry = refs[-4:]
        low = _head_mask()
        row = _iota((2 * ATT_BLOCK, ATT_BLOCK), 0)
        key = _iota((2 * ATT_BLOCK, ATT_BLOCK), 1)
        dist = row - key
        low2 = _iota((2 * ATT_BLOCK, LANES), 1) < HEAD_DIM

        @pl.when(j == 0)
        def _():
            carry[...] = jnp.zeros(carry.shape, F32)

        dq_prev = [carry[:, LANES * p:LANES * (p + 1)] for p in range(N_HEADS // 2)]
        for b in range(bq):
            rows = slice(ATT_BLOCK * b, ATT_BLOCK * (b + 1))
            nrows = slice(ATT_BLOCK * (b + 1), ATT_BLOCK * (b + 2))
            inner = b < bq - 1
            has_next = True if inner else (j < steps - 1)
            valid = (dist >= 0) & (dist <= ATT_BLOCK) & ((row < ATT_BLOCK) | has_next)
            lse2 = jnp.concatenate([lc_ref[rows, :], lc_ref[nrows, :] if inner else ln_ref[...]], axis=0)
            dl2 = jnp.concatenate([dc_ref[rows, :], dc_ref[nrows, :] if inner else dn_ref[...]], axis=0)
            for p in range(N_HEADS // 2):
                sl = slice(LANES * p, LANES * (p + 1))
                k2 = k_ref[rows, sl].astype(F32)
                v2 = v_ref[rows, sl].astype(F32)
                q2 = jnp.concatenate([qc_ref[rows, sl], qc_ref[nrows, sl] if inner else qn_ref[:, sl]],
                                     axis=0).astype(F32)
                do2 = jnp.concatenate([doc_ref[rows, sl], doc_ref[nrows, sl] if inner else don_ref[:, sl]], axis=0)
                q2b = q2.astype(BF16)
                do2b = do2.astype(BF16)
                dks, dvs, dq2 = [], [], None
                for e in range(2):
                    h = 2 * p + e
                    keep = low if e == 0 else jnp.logical_not(low)
                    keep2 = low2 if e == 0 else jnp.logical_not(low2)
                    slope = 2.0 ** (-(h + 1))
                    bias = jnp.where(valid, (-slope * dil) * dist.astype(F32), NEG)
                    ke = jnp.where(keep, k2, 0.0).astype(BF16)
                    ve = jnp.where(keep, v2, 0.0).astype(BF16)
                    s = _dot_nt(q2b, ke) * scale + bias
                    pe = jnp.exp(s - lse2[:, h:h + 1])
                    dp = _dot_nt(do2b, ve)
                    dsb = (pe * (dp - dl2[:, h:h + 1])).astype(BF16)
                    dvs.append(_dot_tn(pe.astype(BF16), jnp.where(keep2, do2, 0.0).astype(BF16)))
                    dks.append(_dot_tn(dsb, jnp.where(keep2, q2, 0.0).astype(BF16)) * scale)
                    dqe = _dot(dsb, ke)
                    dq2 = dqe if dq2 is None else dq2 + dqe
                dk = jnp.where(low, dks[0], dks[1])
                dv = jnp.where(low, dvs[0], dvs[1])
                dq = dq_prev[p] + dq2[:ATT_BLOCK] * scale
                dq_prev[p] = dq2[ATT_BLOCK:] * scale
                if not first:
                    dk = dk + refs[10][rows, sl].astype(F32)
                    dv = dv + refs[11][rows, sl].astype(F32)
                    dq = dq + refs[12][rows, sl].astype(F32)
                dk_out[rows, sl] = dk.astype(BF16)
                dv_out[rows, sl] = dv.astype(BF16)
                dq_out[rows, sl] = dq.astype(BF16)
        for p in range(N_HEADS // 2):
            carry[:, LANES * p:LANES * (p + 1)] = dq_prev[p]

    big = (bq * ATT_BLOCK, ATT_W)
    one = (ATT_BLOCK, ATT_W)

    def nxt_idx(j):
        return jnp.minimum((j + 1) * bq, nb - 1)

    cur = pl.BlockSpec(big, lambda r, j: (j, r))
    nxt = pl.BlockSpec(one, lambda r, j: (nxt_idx(j), r))
    cur_s = pl.BlockSpec((bq * ATT_BLOCK, LANES), lambda r, j: (j, r))
    nxt_s = pl.BlockSpec((ATT_BLOCK, LANES), lambda r, j: (nxt_idx(j), r))
    in_specs = [pl.BlockSpec(big, lambda r, j: (j, 3 * r + 1)),
                pl.BlockSpec(big, lambda r, j: (j, 3 * r + 2)),
                pl.BlockSpec(big, lambda r, j: (j, 3 * r)),
                pl.BlockSpec(one, lambda r, j: (nxt_idx(j), 3 * r)),
                cur, nxt, cur_s, nxt_s, cur_s, nxt_s]
    args = [qkv_v] * 4 + [do_v, do_v, lse_v, lse_v, delta_v, delta_v]
    if not first:
        in_specs += [cur, cur, cur]
        args += list(dkv_in)
    shp = jax.ShapeDtypeStruct((n_l, dil * ATT_W), BF16)
    return pl.pallas_call(kern, name="att_bwd_d%d" % dil, grid=(dil, steps), in_specs=in_specs,
                          out_specs=[cur, cur, cur], out_shape=[shp, shp, shp],
                          scratch_shapes=[pltpu.VMEM((ATT_BLOCK, ATT_W), F32)], compiler_params=_params(2))(*args)


def _attention_fwd(qkv):
    t = qkv.shape[0]
    parts = [[s.reshape(t, -1) for s in _att_fwd(dil, qkv.reshape(t // dil, dil * D_MIX))]
             for dil in ATT_DILATIONS]
    return _att_merge(parts)


def _attention_bwd(qkv, d_att, out, lse):
    t = qkv.shape[0]
    delta = _att_delta(d_att, out)
    grads = None
    for dil in ATT_DILATIONS:
        view = lambda a: a.reshape(t // dil, -1)
        grads = _att_bwd(dil, view(qkv), view(d_att), view(lse), view(delta),
                         None if grads is None else [view(a) for a in grads])
    dk, dv, dq = [a.reshape(t, ATT_W) for a in grads]
    return dq, dk, dv


def _ssd_chunk_common(pre, dtraw, bias_row, alog_row):
    q = SSD_CHUNK
    act = _silu(pre)
    lane = _iota((q, LANES), 1)
    dt = jnp.where(lane < N_HEADS, _softplus(dtraw + bias_row), 0.0)
    a_row = -jnp.exp(alog_row)
    tril = jnp.where(_iota((q, q), 0) >= _iota((q, q), 1), 1.0, 0.0).astype(F32)
    cs = _dot_hi(tril, dt * a_row)
    cs_last = cs[q - 1:q, :]
    return act, dt, a_row, tril, cs, cs_last


def _ssd_lmat(cs, cs_t, h):
    q = SSD_CHUNK
    seg = cs[:, h:h + 1] - cs_t[h:h + 1, :]
    causal = _iota((q, q), 0) >= _iota((q, q), 1)
    return jnp.exp(jnp.where(causal, seg, NEG))


def _ssd_gate_norm(y, z, norm_w):
    sz = _silu(z)
    yg = y * sz
    half = SSD_W // 2
    outs, rss = [], []
    for g in range(2):
        part = yg[:, half * g:half * (g + 1)]
        rs = lax.rsqrt(jnp.mean(part * part, axis=-1, keepdims=True) + SSD_NORM_EPS)
        outs.append(part * rs)
        rss.append(rs)
    yn = jnp.concatenate(outs, axis=1)
    return sz, yn, rss, yn * norm_w


def _ssd_fwd(pre, rest, dt_bias, a_log, d_skip, norm_w):
    t = pre.shape[0]
    q = SSD_CHUNK

    def body(c, nc, ins, outs, scr):
        pre_ref, z_ref, dtr_ref, bias_ref, alog_ref, dsk_ref, nw_ref = ins
        out_ref, y_ref, sp_ref = outs
        s_ref = scr[0]

        @pl.when(c == 0)
        def _():
            s_ref[...] = jnp.zeros(s_ref.shape, F32)

        act, dt, a_row, tril, cs, cs_last = _ssd_chunk_common(pre_ref[...], dtr_ref[...], bias_ref[...],
                                                               alog_ref[...])
        x = act[:, :SSD_W]
        cs_t = cs.T
        e_col = jnp.exp(cs)
        w = jnp.exp(cs_last - cs) * dt
        expand = _expand_mat()
        w_x = _dot_hi(w, expand)
        dt_x = _dot_hi(dt, expand)
        e_x = _dot_hi(e_col, expand)
        d_x = _dot_hi(dsk_ref[...], expand)
        cd_x = _dot_hi(jnp.exp(cs_last), expand)
        s_prev = s_ref[...]
        sp_ref[...] = s_prev
        xw = (x * w_x).astype(BF16)
        xd = (x * dt_x).astype(BF16)
        low = _head_mask()
        y_parts, s_parts = [], []
        for g in range(2):
            bg = act[:, SSD_W + SSD_STATE * g:SSD_W + SSD_STATE * (g + 1)].astype(BF16)
            cg = act[:, SSD_W + 2 * SSD_STATE + SSD_STATE * g:SSD_W + 2 * SSD_STATE + SSD_STATE * (g + 1)].astype(BF16)
            gsl = slice(256 * g, 256 * (g + 1))
            gmat = _dot_nt(cg, bg)
            s_parts.append(_dot_tn(bg, xw[:, gsl]))
            y0 = _dot(cg, s_prev[:, gsl].astype(BF16))
            for pp in range(2):
                pair = 2 * g + pp
                psl = slice(LANES * pair, LANES * (pair + 1))
                yd = []
                for e in range(2):
                    h = 2 * pair + e
                    mh = (gmat * _ssd_lmat(cs, cs_t, h)).astype(BF16)
                    yd.append(_dot(mh, xd[:, psl]))
                y_parts.append(jnp.where(low, yd[0], yd[1]) + e_x[:, psl] * y0[:, LANES * pp:LANES * (pp + 1)])
        y = jnp.concatenate(y_parts, axis=1) + d_x * x
        s_ref[...] = cd_x * s_prev + jnp.concatenate(s_parts, axis=1)
        y_ref[...] = y
        out_ref[...] = _ssd_gate_norm(y, z_ref[...], nw_ref[...])[3]

    return _rowwise("ssd_fwd", body, t, q,
                    [('row', pre), ('col', rest, SSD_W, 2), ('col', rest, LANES, 20), ('full', dt_bias),
                     ('full', a_log), ('full', d_skip), ('full', norm_w)],
                    [('row', SSD_W, F32), ('row', SSD_W, F32), ('row', SSD_W, F32)],
                    scratch=[pltpu.VMEM((SSD_STATE, SSD_W), F32)])


def _ssd_bwd(pre, rest, y, s_prev_all, d_mix, dt_bias, a_log, d_skip, norm_w):
    t = pre.shape[0]
    q = SSD_CHUNK

    def body(c, nc, ins, outs, scr):
        pre_ref, z_ref, dtr_ref, y_ref, sp_ref, do_ref, bias_ref, alog_ref, dsk_ref, nw_ref = ins
        dpre_ref, dz_ref, ddt_ref, a128_ref, a512_ref = outs
        ds_ref = scr[0]

        @pl.when(c == nc - 1)
        def _():
            ds_ref[...] = jnp.zeros(ds_ref.shape, F32)

        pre_v = pre_ref[...]
        dtr = dtr_ref[...]
        act, dt, a_row, tril, cs, cs_last = _ssd_chunk_common(pre_v, dtr, bias_ref[...], alog_ref[...])
        x = act[:, :SSD_W]
        cs_t = cs.T
        e_col = jnp.exp(cs)
        decay_end = jnp.exp(cs_last - cs)
        w = decay_end * dt
        cd = jnp.exp(cs_last)
        expand = _expand_mat()
        reduce = _reduce_mat()
        w_x = _dot_hi(w, expand)
        dt_x = _dot_hi(dt, expand)
        e_x = _dot_hi(e_col, expand)
        d_x = _dot_hi(dsk_ref[...], expand)
        cd_x = _dot_hi(cd, expand)
        s_prev = sp_ref[...]
        d_s = ds_ref[...]
        xw = (x * w_x).astype(BF16)
        xd = (x * dt_x).astype(BF16)
        low = _head_mask()
        lane = _iota((q, LANES), 1)
        sub = _iota((q, LANES), 0)

        yv, zv, nw = y_ref[...], z_ref[...], nw_ref[...]
        d_out = do_ref[...]
        sz, yn, rss, _ = _ssd_gate_norm(yv, zv, nw)
        a512_ref[0:1, :] += jnp.sum(d_out * yn, axis=0, keepdims=True)
        dyn = d_out * nw
        half = SSD_W // 2
        dyg_parts = []
        for g in range(2):
            hs = slice(half * g, half * (g + 1))
            dyg_parts.append(rss[g] * (dyn[:, hs] - yn[:, hs] * jnp.mean(dyn[:, hs] * yn[:, hs], axis=-1,
                                                                          keepdims=True)))
        dyg = jnp.concatenate(dyg_parts, axis=1)
        dy = dyg * sz
        dz_ref[...] = dyg * yv * _dsilu(zv)

        a128_ref[2:3, :] += _dot_hi(jnp.sum(dy * x, axis=0, keepdims=True), reduce)
        dx = d_x * dy

        dy0 = e_x * dy
        dyb = dy.astype(BF16)
        dcs = jnp.zeros((q, LANES), F32)
        dcs_rows = jnp.zeros((q, LANES), F32)
        ddt = jnp.zeros((q, LANES), F32)
        ds_prev_parts, z_parts, db_parts, dc_parts, dxd_parts, y0_parts = [], [], [], [], [], []
        for g in range(2):
            bg = act[:, SSD_W + SSD_STATE * g:SSD_W + SSD_STATE * (g + 1)].astype(BF16)
            cg = act[:, SSD_W + 2 * SSD_STATE + SSD_STATE * g:SSD_W + 2 * SSD_STATE + SSD_STATE * (g + 1)].astype(BF16)
            gsl = slice(256 * g, 256 * (g + 1))
            spg = s_prev[:, gsl].astype(BF16)
            dsg = d_s[:, gsl].astype(BF16)
            dy0g = dy0[:, gsl].astype(BF16)
            gmat = _dot_nt(cg, bg)
            y0_parts.append(_dot(cg, spg))
            dc_g = _dot_nt(dy0g, spg)
            ds_prev_parts.append(_dot_tn(cg, dy0g))
            z_parts.append(_dot(bg, dsg))
            db_g = _dot_nt(xw[:, gsl], dsg)
            dg_acc = jnp.zeros((q, q), F32)
            for pp in range(2):
                pair = 2 * g + pp
                psl = slice(LANES * pair, LANES * (pair + 1))
                dxd_e = []
                for e in range(2):
                    h = 2 * pair + e
                    keep = low if e == 0 else jnp.logical_not(low)
                    lm = _ssd_lmat(cs, cs_t, h)
                    mh = gmat * lm
                    dm = _dot_nt(jnp.where(keep, dy[:, psl], 0.0).astype(BF16), xd[:, psl])
                    dxd_e.append(_dot_tn(mh.astype(BF16), dyb[:, psl]))
                    wm = dm * mh
                    dcs = dcs + jnp.where(lane == h, jnp.sum(wm, axis=1, keepdims=True), 0.0)
                    dcs_rows = dcs_rows - jnp.where(sub == h, jnp.sum(wm, axis=0, keepdims=True), 0.0)
                    dg_acc = dg_acc + dm * lm
                dxd_parts.append(jnp.where(low, dxd_e[0], dxd_e[1]))
            dgb = dg_acc.astype(BF16)
            dc_parts.append(dc_g + _dot(dgb, bg))
            db_parts.append(db_g + _dot_tn(dgb, cg))
        y0 = jnp.concatenate(y0_parts, axis=1)
        zmat = jnp.concatenate(z_parts, axis=1)
        dxd = jnp.concatenate(dxd_parts, axis=1)
        ds_prev = jnp.concatenate(ds_prev_parts, axis=1) + cd_x * d_s
        ds_ref[...] = ds_prev

        dcs = dcs + _dot_hi(dy * y0, reduce) * e_col
        dcd = _dot_hi(jnp.sum(d_s * s_prev, axis=0, keepdims=True), reduce)
        dlast = dcd * cd
        dx = dx + w_x * zmat + dxd * dt_x
        dw = _dot_hi(zmat * x, reduce)
        ddt = ddt + dw * decay_end + _dot_hi(dxd * x, reduce)
        dwl = dw * w
        dcs = dcs - dwl
        dlast = dlast + jnp.sum(dwl, axis=0, keepdims=True)
        dcs = dcs + dcs_rows.T + jnp.where(sub == q - 1, dlast, 0.0)
        dda = _dot_hi(tril.T, dcs)
        ddt = ddt + dda * a_row
        a128_ref[1:2, :] += jnp.sum(dda * dt, axis=0, keepdims=True) * a_row
        draw = jnp.where(lane < N_HEADS, ddt * _sigmoid(dtr + bias_ref[...]), 0.0)
        a128_ref[0:1, :] += jnp.sum(draw, axis=0, keepdims=True)
        ddt_ref[...] = draw
        dact = jnp.concatenate([dx] + db_parts + dc_parts, axis=1)
        dpre_ref[...] = dact * _dsilu(pre_v)

    return _rowwise("ssd_bwd", body, t, q,
                    [('row', pre), ('col', rest, SSD_W, 2), ('col', rest, LANES, 20), ('row', y),
                     ('row', s_prev_all), ('col', d_mix, SSD_W, 0), ('full', dt_bias), ('full', a_log),
                     ('full', d_skip), ('full', norm_w)],
                    [('row', SSD_CONV, F32), ('row', SSD_W, F32), ('row', LANES, F32),
                     ('acc', (8, LANES), F32), ('acc', (8, SSD_W), F32)],
                    scratch=[pltpu.VMEM((SSD_STATE, SSD_W), F32)], reverse=True)


LRU_TM = 256


def _lru_gates(xc, wa, ba, wx, bx, lam):
    xb = xc.astype(BF16)
    r = _sigmoid(_dot(xb, wa) + ba)
    i = _sigmoid(_dot(xb, wx) + bx)
    sp = _softplus(-lam)
    a = jnp.exp(-LRU_C * r * sp)
    mult = jnp.sqrt(1.0 - a * a)
    return r, i, sp, a, mult


def _lru_fwd(xc, rest, wa, ba, wx, bx, lam):
    def body(i, nt, ins, outs, scr):
        xc_ref, g_ref, wa_ref, ba_ref, wx_ref, bx_ref, lam_ref = ins
        carry = scr[0]

        @pl.when(i == 0)
        def _():
            carry[...] = jnp.zeros(carry.shape, F32)

        xv = xc_ref[...]
        r, ig, sp, a, mult = _lru_gates(xv, wa_ref[...], ba_ref[...], wx_ref[...], bx_ref[...], lam_ref[...])
        u = mult * (ig * xv)
        row = _iota(a.shape, 0)
        s = 1
        while s < LRU_TM:
            a_sh = jnp.where(row >= s, pltpu.roll(a, s, 0), 1.0)
            u_sh = jnp.where(row >= s, pltpu.roll(u, s, 0), 0.0)
            u = a * u_sh + u
            a = a * a_sh
            s *= 2
        h = u + a * carry[0:1, :]
        carry[0:1, :] = h[LRU_TM - 1:LRU_TM, :]
        outs[1][...] = h
        outs[0][...] = h * _gelu(g_ref[...])

    return _rowwise("lru_fwd", body, xc.shape[0], LRU_TM,
                    [('row', xc), ('col', rest, LRU_W, 3), ('full', wa), ('full', ba), ('full', wx),
                     ('full', bx), ('full', lam)],
                    [('row', LRU_W, F32), ('row', LRU_W, F32)], scratch=[pltpu.VMEM((8, LRU_W), F32)])


def _lru_bwd(xc, rest, h, d_mix, wa, ba, wx, bx, lam, wa_t, wx_t):
    def body(i, nt, ins, outs, scr):
        xc_ref, g_ref, h_ref, hp_ref, do_ref, wa_ref, ba_ref, wx_ref, bx_ref, lam_ref, wat_ref, wxt_ref = ins
        dxc_ref, dg_ref, dza_ref, dzi_ref, acc_ref = outs
        carry = scr[0]

        @pl.when(i == nt - 1)
        def _():
            carry[...] = jnp.zeros(carry.shape, F32)

        xv, gv, hv, d_out = xc_ref[...], g_ref[...], h_ref[...], do_ref[...]
        r, ig, sp, a, mult = _lru_gates(xv, wa_ref[...], ba_ref[...], wx_ref[...], bx_ref[...], lam_ref[...])
        dg_ref[...] = d_out * hv * _dgelu(gv)
        gsum = d_out * _gelu(gv)
        row = _iota(a.shape, 0)
        b = jnp.where(row < LRU_TM - 1, pltpu.roll(a, LRU_TM - 1, 0), 1.0)
        s = 1
        while s < LRU_TM:
            keep = row < LRU_TM - s
            b_sh = jnp.where(keep, pltpu.roll(b, LRU_TM - s, 0), 1.0)
            g_sh = jnp.where(keep, pltpu.roll(gsum, LRU_TM - s, 0), 0.0)
            gsum = gsum + b * g_sh
            b = b * b_sh
            s *= 2
        dh = gsum + b * carry[0:1, :]
        carry[0:1, :] = a[0:1, :] * dh[0:1, :]
        h_prev = _shift_down(hv, 1, jnp.where(i > 0, hp_ref[...], 0.0))
        du = dh
        dmult = du * ig * xv
        di = du * mult * xv
        dxc = du * mult * ig
        da = dh * h_prev - dmult * a / mult
        dlog = da * a
        dr = dlog * (-LRU_C) * sp
        acc_ref[2:3, :] += jnp.sum(dlog * (-LRU_C) * r, axis=0, keepdims=True)
        dza = dr * r * (1.0 - r)
        dzi = di * ig * (1.0 - ig)
        acc_ref[0:1, :] += jnp.sum(dza, axis=0, keepdims=True)
        acc_ref[1:2, :] += jnp.sum(dzi, axis=0, keepdims=True)
        dzab, dzib = dza.astype(BF16), dzi.astype(BF16)
        dza_ref[...] = dzab
        dzi_ref[...] = dzib
        dxc_ref[...] = dxc + _dot(dzab, wat_ref[...]) + _dot(dzib, wxt_ref[...])

    return _rowwise("lru_bwd", body, xc.shape[0], LRU_TM,
                    [('row', xc), ('col', rest, LRU_W, 3), ('row', h), ('prev8', h, LRU_W, 0),
                     ('col', d_mix, LRU_W, 1), ('full', wa), ('full', ba), ('full', wx), ('full', bx),
                     ('full', lam), ('full', wa_t), ('full', wx_t)],
                    [('row', LRU_W, F32), ('row', LRU_W, F32), ('row', LRU_W, BF16), ('row', LRU_W, BF16),
                     ('acc', (8, LRU_W), F32)],
                    scratch=[pltpu.VMEM((8, LRU_W), F32)], reverse=True)


FFN_TM = 512
FFN_TN = 1536


def _ffn_up(name, h2, w_gu):
    t, k = h2.shape
    nh = D_FFP // FFN_TN

    def kern(a_ref, wg_ref, wu_ref, g_ref, u_ref, act_ref):
        a = a_ref[...].astype(BF16)
        gv = _dot(a, wg_ref[...])
        uv = _dot(a, wu_ref[...])
        g_ref[...] = gv.astype(BF16)
        u_ref[...] = uv.astype(BF16)
        act_ref[...] = (_silu(gv) * uv).astype(BF16)

    tile = pl.BlockSpec((FFN_TM, FFN_TN), lambda i, j: (i, j))
    return pl.pallas_call(
        kern, name=name, grid=(t // FFN_TM, nh),
        in_specs=[pl.BlockSpec((FFN_TM, k), lambda i, j: (i, 0)),
                  pl.BlockSpec((k, FFN_TN), lambda i, j: (0, j)),
                  pl.BlockSpec((k, FFN_TN), lambda i, j: (0, nh + j))],
        out_specs=[tile, tile, tile],
        out_shape=[jax.ShapeDtypeStruct((t, D_FFP), BF16)] * 3,
        compiler_params=_params(2))(h2, w_gu, w_gu)


def _ffn_down_bwd(name, dx, w_down_t, gate, up):
    t, k = dx.shape

    def kern(dx_ref, w_ref, g_ref, u_ref, dg_ref, du_ref):
        da = _dot(dx_ref[...].astype(BF16), w_ref[...])
        gv, uv = g_ref[...].astype(F32), u_ref[...].astype(F32)
        dg_ref[...] = (da * uv * _dsilu(gv)).astype(BF16)
        du_ref[...] = (da * _silu(gv)).astype(BF16)

    tile = pl.BlockSpec((FFN_TM, FFN_TN), lambda i, j: (i, j))
    shp = jax.ShapeDtypeStruct((t, D_FFP), BF16)
    return pl.pallas_call(
        kern, name=name, grid=(t // FFN_TM, D_FFP // FFN_TN),
        in_specs=[pl.BlockSpec((FFN_TM, k), lambda i, j: (i, 0)),
                  pl.BlockSpec((k, FFN_TN), lambda i, j: (0, j)), tile, tile],
        out_specs=[tile, tile], out_shape=[shp, shp], compiler_params=_params(2))(dx, w_down_t, gate, up)


def _layer_fwd(x, w, l):
    tag = "_l%d" % l
    h = _rmsnorm_fwd("norm_mix" + tag, x, w['norm_mix'][l])
    qkv = _mm("proj_qkv" + tag, [h], w['w_qkv'][l], tn=768, out_dtype=BF16)
    rest = _mm("proj_rest" + tag, [h], w['w_rest'][l], tn=896)
    att, lse = _attention_fwd(qkv)
    pre = _conv_fwd("ssd_conv" + tag, rest, SSD_CONV, 0, w['ssd_conv_w'][l], w['ssd_conv_b'][l])
    ssd, y, s_prev = _ssd_fwd(pre, rest, w['ssd_dt_bias'][l], w['ssd_a_log'][l], w['ssd_d'][l], w['ssd_norm'][l])
    xc = _conv_fwd("lru_conv" + tag, rest, LRU_W, 4, w['lru_conv_w'][l], w['lru_conv_b'][l])
    lru, hl = _lru_fwd(xc, rest, w['lru_wa'][l], w['lru_ba'][l], w['lru_wx'][l], w['lru_bx'][l], w['lru_lambda'][l])
    x_mid = _mm("proj_out" + tag, [att, ssd, lru], w['w_out'][l], res=x, tn=512)
    h2 = _rmsnorm_fwd("norm_ffn" + tag, x_mid, w['norm_ffn'][l])
    gate, up, act = _ffn_up("proj_gu" + tag, h2, w['w_gu'][l])
    x_next = _mm("proj_down" + tag, [act], w['w_down'][l], res=x_mid, tn=512)
    saved = dict(x=x, h=h, qkv=qkv, rest=rest, att=att, lse=lse, pre=pre, ssd=ssd, y=y, s_prev=s_prev, xc=xc,
                 lru=lru, hl=hl, x_mid=x_mid, h2=h2, gate=gate, up=up, act=act)
    return x_next, saved


def _layer_bwd(dx_next, sv, w, l):
    tag = "_l%d_b" % l
    t = dx_next.shape[0]
    g = {}
    dgate, dup = _ffn_down_bwd("d_act" + tag, dx_next, w['w_down_t'][l], sv['gate'], sv['up'])
    g['w_down'] = _mm_tn("dw_down" + tag, sv['act'], dx_next, tk=1536)
    dh2 = _mm("d_h2" + tag, [dgate, dup], w['w_gu_t'][l], tn=512)
    g['w_gate'] = _mm_tn("dw_gate" + tag, sv['h2'], dgate, tn=1536)
    g['w_up'] = _mm_tn("dw_up" + tag, sv['h2'], dup, tn=1536)
    dx_mid, acc = _rmsnorm_bwd("norm_ffn" + tag, dh2, sv['x_mid'], w['norm_ffn'][l], dx_next)
    g['norm_ffn'] = acc[0]
    d_att = _mm("d_att" + tag, [dx_mid], w['w_out_t'][l][:, :ATT_W], tn=512)
    d_mix = _mm("d_mix" + tag, [dx_mid], w['w_out_t'][l][:, ATT_W:], tn=512)
    g['w_out'] = jnp.concatenate([_mm_tn("dw_out%d" % k + tag, a, dx_mid)
                                  for k, a in enumerate((sv['att'], sv['ssd'], sv['lru']))], axis=0)
    dxc, dgl, dza, dzi, acc = _lru_bwd(sv['xc'], sv['rest'], sv['hl'], d_mix, w['lru_wa'][l], w['lru_ba'][l],
                                       w['lru_wx'][l], w['lru_bx'][l], w['lru_lambda'][l],
                                       w['lru_wa_t'][l], w['lru_wx_t'][l])
    g['lru_ba'], g['lru_bx'] = acc[0], acc[1]
    g['lru_lambda'] = acc[2] * (-_sigmoid(-w['lru_lambda'][l][0]))
    g['lru_wa'] = _diag_blocks(_mm_tn("dw_lru_a" + tag, sv['xc'], dza))
    g['lru_wx'] = _diag_blocks(_mm_tn("dw_lru_x" + tag, sv['xc'], dzi))
    dxl, acc = _conv_bwd("lru_conv" + tag, dxc, sv['rest'], LRU_W, 4, w['lru_conv_w'][l])
    g['lru_conv_w'], g['lru_conv_b'] = acc[:4], acc[4]
    dpre, dz, ddt, a128, a512 = _ssd_bwd(sv['pre'], sv['rest'], sv['y'], sv['s_prev'], d_mix, w['ssd_dt_bias'][l],
                                         w['ssd_a_log'][l], w['ssd_d'][l], w['ssd_norm'][l])
    g['ssd_dt_bias'], g['ssd_a_log'], g['ssd_d'] = a128[0, :N_HEADS], a128[1, :N_HEADS], a128[2, :N_HEADS]
    g['ssd_norm'] = a512[0]
    dxbc, acc = _conv_bwd("ssd_conv" + tag, dpre, sv['rest'], SSD_CONV, 0, w['ssd_conv_w'][l])
    g['ssd_conv_w'], g['ssd_conv_b'] = acc[:4], acc[4]
    dq, dk, dv = _attention_bwd(sv['qkv'], d_att, sv['att'], sv['lse'])
    pieces = [dq, dk, dv, dxbc, dz, dgl, dxl, ddt]
    dh = _mm("d_h" + tag, pieces, w['w_in_t'][l], tn=512)
    dws = [_mm_tn("dw_in%d" % k + tag, sv['h'], p) for k, p in enumerate(pieces)]
    g['w_in'] = jnp.concatenate([dws[0], dws[1], dws[2], dws[4], dws[3], _dt_tile_place(dws[7]), dws[5], dws[6]],
                                axis=1)
    dx, acc = _rmsnorm_bwd("norm_mix" + tag, dh, sv['x'], w['norm_mix'][l], dx_mid)
    g['norm_mix'] = acc[0]
    return dx, g


def _diag_blocks(m):
    return jnp.stack([m[64 * n:64 * (n + 1), 64 * n:64 * (n + 1)] for n in range(8)])


def _block_diag(w):
    eye = jnp.eye(8, dtype=w.dtype)
    return (w[:, :, None, :] * eye[:, None, :, None]).reshape(512, 512)


_ANY = pl.BlockSpec(memory_space=pl.ANY)
_MESH = pl.DeviceIdType.MESH


def _all_gather(name, xs):
    n = len(xs)

    def body(*refs):
        x_refs, out_refs = refs[:n], refs[n:2 * n]
        send_sems, recv_sems, local_sems = refs[2 * n:]
        x_, y_, c_ = lax.axis_index("x"), lax.axis_index("y"), lax.axis_index("c")
        me, sibling = (x_, y_, c_), (x_, y_, 1 - c_)
        chips = [(1 - x_, y_), (x_, 1 - y_), (1 - x_, 1 - y_)]

        def slot(a, px, py, pc):
            return out_refs[a].at[4 * px + 2 * py + pc]

        def copy(a, k, block, to, src=None):
            return pltpu.make_async_remote_copy(
                src_ref=slot(a, *block) if src is None else src, dst_ref=slot(a, *block),
                send_sem=send_sems.at[a, k], recv_sem=recv_sems.at[a, k], device_id=to, device_id_type=_MESH)

        mine = [pltpu.make_async_copy(x_refs[a], slot(a, *me), local_sems.at[a]) for a in range(n)]
        for cp in mine:
            cp.start()
        first = []
        for a in range(n):
            first.append(copy(a, 0, me, sibling, src=x_refs[a]))
            first += [copy(a, 1 + j, me, (*chip, c_), src=x_refs[a]) for j, chip in enumerate(chips)]
        for cp in first:
            cp.start()
        passed = []
        for j, chip in enumerate(chips):
            for a in range(n):
                copy(a, 1 + j, (*chip, c_), me).wait_recv()
                fwd = copy(a, 4 + j, (*chip, c_), sibling)
                fwd.start()
                passed.append(fwd)
        for a in range(n):
            copy(a, 0, sibling, me).wait_recv()
            for j, chip in enumerate(chips):
                copy(a, 4 + j, (*chip, 1 - c_), me).wait_recv()
        for cp in first + passed:
            cp.wait_send()
        for cp in mine:
            cp.wait()

    return pl.pallas_call(
        body, name=name, out_shape=[jax.ShapeDtypeStruct((N_DEV,) + x.shape, x.dtype) for x in xs],
        in_specs=[_ANY] * n, out_specs=[_ANY] * n,
        scratch_shapes=[pltpu.SemaphoreType.DMA((n, 7)), pltpu.SemaphoreType.DMA((n, 7)),
                        pltpu.SemaphoreType.DMA((n,))],
    )(*xs)


def _all_to_all(name, xs):
    n = len(xs)

    def body(*refs):
        x_refs, out_refs = refs[:n], refs[n:2 * n]
        send_sems, recv_sems, local_sems = refs[2 * n:]
        x_, y_, c_ = lax.axis_index("x"), lax.axis_index("y"), lax.axis_index("c")
        me = 4 * x_ + 2 * y_ + c_

        def peer(k):
            return ((1 - x_) if k & 4 else x_, (1 - y_) if k & 2 else y_, (1 - c_) if k & 1 else c_)

        def copy(a, k):
            px, py, pc = peer(k)
            return pltpu.make_async_remote_copy(
                src_ref=x_refs[a].at[4 * px + 2 * py + pc], dst_ref=out_refs[a].at[me],
                send_sem=send_sems.at[a, k - 1], recv_sem=recv_sems.at[a, k - 1],
                device_id=(px, py, pc), device_id_type=_MESH)

        def arrival(a, k):
            px, py, pc = peer(k)
            return pltpu.make_async_remote_copy(
                src_ref=x_refs[a].at[me], dst_ref=out_refs[a].at[4 * px + 2 * py + pc],
                send_sem=send_sems.at[a, k - 1], recv_sem=recv_sems.at[a, k - 1],
                device_id=(px, py, pc), device_id_type=_MESH)

        mine = [pltpu.make_async_copy(x_refs[a].at[me], out_refs[a].at[me], local_sems.at[a]) for a in range(n)]
        for cp in mine:
            cp.start()
        copies = [copy(a, k) for a in range(n) for k in range(1, N_DEV)]
        for cp in copies:
            cp.start()
        for a in range(n):
            for k in range(1, N_DEV):
                arrival(a, k).wait_recv()
        for cp in copies:
            cp.wait_send()
        for cp in mine:
            cp.wait()

    return pl.pallas_call(
        body, name=name, out_shape=[jax.ShapeDtypeStruct(x.shape, x.dtype) for x in xs],
        in_specs=[_ANY] * n, out_specs=[_ANY] * n,
        scratch_shapes=[pltpu.SemaphoreType.DMA((n, 7)), pltpu.SemaphoreType.DMA((n, 7)),
                        pltpu.SemaphoreType.DMA((n,))],
    )(*xs)


def _window_offset():
    me = 4 * lax.axis_index("x") + 2 * lax.axis_index("y") + lax.axis_index("c")
    return jnp.where(me < 6, me, me + 120)


def _place_w_in(shard):
    def body(i, nt, ins, outs, scr):
        outs[0][...] = pltpu.roll(ins[0][...], _window_offset(), 1).astype(BF16)

    return _rowwise("place_w_in", body, shard.shape[0], 256, [('row', shard)], [('row', IN_WIN, BF16)])[0]


def _adamw(name, w, m, v, g, tr, from_window=False):
    s_parts, r, c = g.shape
    assert r % tr == 0

    def kern(w_ref, m_ref, v_ref, g_ref, go_ref, d_ref, mo_ref, vo_ref):
        gs = g_ref[0].astype(F32)
        for s in range(1, s_parts):
            gs = gs + g_ref[s].astype(F32)
        if from_window:
            gs = pltpu.roll(gs, c - _window_offset(), 1)
        wv = w_ref[...]
        m2 = ADAM_B1 * m_ref[...] + (1.0 - ADAM_B1) * gs
        v2 = ADAM_B2 * v_ref[...] + (1.0 - ADAM_B2) * (gs * gs)
        m_hat = m2 / (1.0 - ADAM_B1 ** ADAM_STEP)
        v_hat = v2 / (1.0 - ADAM_B2 ** ADAM_STEP)
        go_ref[...] = gs
        d_ref[...] = -ADAM_LR * (m_hat / (jnp.sqrt(v_hat) + ADAM_EPS) + ADAM_WD * wv)
        mo_ref[...] = m2
        vo_ref[...] = v2

    spec = pl.BlockSpec((tr, c), lambda i: (i, 0))
    shp = jax.ShapeDtypeStruct((r, c), F32)
    return pl.pallas_call(kern, name=name, grid=(r // tr,),
                          in_specs=[spec, spec, spec, pl.BlockSpec((s_parts, tr, c), lambda i: (0, i, 0))],
                          out_specs=[spec] * 4, out_shape=[shp] * 4, compiler_params=_params(1))(w, m, v, g)


def _pack(arrs, rows, lead=0):
    parts = []
    for a in arrs:
        flat = a.reshape(a.shape[:lead] + (-1,))
        pad = (-flat.shape[-1]) % LANES
        if pad:
            flat = jnp.pad(flat, [(0, 0)] * lead + [(0, pad)])
        parts.append(flat)
    flat = jnp.concatenate(parts, axis=-1)
    pad = rows * LANES - flat.shape[-1]
    assert pad >= 0
    if pad:
        flat = jnp.pad(flat, [(0, 0)] * lead + [(0, pad)])
    return flat.reshape(flat.shape[:lead] + (rows, LANES))


def _unpack(buf, shapes, lead=0):
    flat = buf.reshape(buf.shape[:lead] + (-1,))
    out, off = [], 0
    for shp in shapes:
        n = math.prod(shp)
        out.append(flat[..., off:off + n].reshape(buf.shape[:lead] + tuple(shp)))
        off += n + ((-n) % LANES)
    return out


BIG = ('w_in', 'w_out', 'w_gate', 'w_up', 'w_down')
CONV =('ssd_conv_w', 'lru_conv_w')
CONV_SHARD_SHAPES = ((DEPTH, 4, SSD_CONV // N_DEV), (DEPTH, 4, LRU_W // N_DEV))
CONV_ROWS = 16
SMALL = ('norm_mix', 'ssd_conv_b', 'ssd_dt_bias', 'ssd_a_log', 'ssd_d', 'ssd_norm', 'lru_conv_b', 'lru_wa',
         'lru_ba', 'lru_wx', 'lru_bx', 'lru_lambda', 'norm_ffn', 'norm_final')
SMALL_ROWS = 1280
SMALL_TILE = 256
WEIGHTS = ('norm_mix', 'w_in', 'ssd_conv_w', 'ssd_conv_b', 'ssd_dt_bias', 'ssd_a_log', 'ssd_d', 'ssd_norm',
           'lru_conv_w', 'lru_conv_b', 'lru_wa', 'lru_ba', 'lru_wx', 'lru_bx', 'lru_lambda', 'w_out', 'norm_ffn',
           'w_gate', 'w_up', 'w_down', 'norm_final')


def _join_cols(a):
    return jnp.transpose(a, (1, 2, 0, 3)).reshape(a.shape[1], a.shape[2], -1)


def _join_rows(a):
    return jnp.transpose(a, (1, 0, 2, 3)).reshape(a.shape[1], -1, a.shape[3])


def _split_cols(a):
    l, r, c = a.shape
    return jnp.transpose(a.reshape(l, r, N_DEV, c // N_DEV), (2, 0, 1, 3)).reshape(N_DEV, l * r, c // N_DEV)


def _split_rows(a):
    l, r, c = a.shape
    return jnp.transpose(a.reshape(l, N_DEV, r // N_DEV, c), (1, 0, 2, 3)).reshape(N_DEV, l * r // N_DEV, c)


def _shard_form(k, a):
    if k == 'w_in':
        return jnp.pad(a.reshape(-1, IN_SHARD), ((0, 0), (0, IN_WIN - IN_SHARD)))
    if k in ('w_gate', 'w_up'):
        return jnp.pad(a.reshape(-1, FF_SHARD), ((0, 0), (0, FF_SHARD_P - FF_SHARD)))
    if k == 'w_down':
        return jnp.pad(a, ((0, 0), (0, FF_SHARD_P - FF_SHARD), (0, 0))).reshape(-1, D_MODEL)
    return a.reshape(-1, D_MODEL)


def _shard_back(k, a):
    if k == 'w_in':
        return a[:, :IN_SHARD].reshape(DEPTH, D_MODEL, IN_SHARD)
    if k in ('w_gate', 'w_up'):
        return a[:, :FF_SHARD].reshape(DEPTH, D_MODEL, FF_SHARD)
    if k == 'w_down':
        return a.reshape(DEPTH, FF_SHARD_P, D_MODEL)[:, :FF_SHARD]
    return a.reshape(DEPTH, D_MIX // N_DEV, D_MODEL)


def _dt_tile_place(a):
    zeros = jnp.zeros(a.shape[:-1] + (LANES - N_HEADS,), a.dtype)
    return jnp.concatenate([a[..., :6], zeros, a[..., 6:8]], axis=-1)


def _dt_tile_heads(tile):
    zeros = jnp.zeros(tile.shape[:-1] + (LANES - N_HEADS,), tile.dtype)
    return jnp.concatenate([tile[..., :6], tile[..., 126:128], zeros], axis=-1)


def _layout_from_windows(win):
    r = win.shape[1]
    main = jnp.concatenate([win[j][:, :512] for j in range(N_DEV)] + [jnp.zeros((r, LANES), win.dtype)], axis=1)
    gap = jnp.zeros((r, 384), win.dtype)
    tails = [jnp.zeros((r, 512), win.dtype)]
    for j in range(N_DEV - 1):
        tails += [win[j][:, 512:], gap]
    tails.append(win[N_DEV - 1][:, 512:])
    return main + jnp.concatenate(tails, axis=1)


def _prepare_weights(p, full):
    w = {}
    w_in = full['w_in']
    w_qkv = w_in[:, :, :D_MIX]
    dt_cols = _dt_tile_heads(w_in[:, :, 3072:3200])
    w_rest = jnp.concatenate([w_in[:, :, 2048:3072], w_in[:, :, 1536:2048], w_in[:, :, 3200:3712],
                              w_in[:, :, 3712:4224], dt_cols], axis=2)
    w['w_qkv'], w['w_rest'] = w_qkv, w_rest
    w['w_in_t'] = jnp.transpose(jnp.concatenate([w_qkv, w_rest], axis=2), (0, 2, 1))
    w['w_out'] = full['w_out']
    w['w_out_t'] = jnp.transpose(full['w_out'], (0, 2, 1))
    w['w_gu'] = jnp.concatenate([full['w_gate'], full['w_up']], axis=2)
    w['w_gu_t'] = jnp.transpose(w['w_gu'], (0, 2, 1))
    w['w_down'] = full['w_down']
    w['w_down_t'] = jnp.transpose(full['w_down'], (0, 2, 1))
    for k in ('norm_mix', 'ssd_conv_b', 'ssd_norm', 'lru_conv_b', 'lru_ba', 'lru_bx', 'lru_lambda', 'norm_ffn'):
        w[k] = p[k][:, None, :]
    for k in ('ssd_dt_bias', 'ssd_a_log', 'ssd_d'):
        w[k] = jnp.pad(p[k], ((0, 0), (0, LANES - N_HEADS)))[:, None, :]
    for k in CONV:
        w[k] = jnp.pad(full[k], ((0, 0), (0, 4), (0, 0)))
    for k in ('lru_wa', 'lru_wx'):
        bd = jnp.stack([_block_diag(p[k][l]) for l in range(DEPTH)]).astype(BF16)
        w[k] = bd
        w[k + '_t'] = jnp.transpose(bd, (0, 2, 1))
    return w


def _local_step(x, target, w, norm_final):
    saved = []
    for l in range(DEPTH):
        x, sv = _layer_fwd(x, w, l)
        saved.append(sv)
    dx, loss_acc, dgf = _final_loss(x, norm_final[None, :], target)
    grads = [None] * DEPTH
    for l in reversed(range(DEPTH)):
        dx, grads[l] = _layer_bwd(dx, saved[l], w, l)
    g = {k: jnp.stack([grads[l][k] for l in range(DEPTH)]) for k in grads[0]}
    g['norm_final'] = dgf[0]
    return loss_acc[0, 0], dx, g


def kernel(x, norm_mix, w_in, ssd_conv_w, ssd_conv_b, ssd_dt_bias, ssd_a_log, ssd_d, ssd_norm, lru_conv_w, lru_conv_b, lru_wa, lru_ba, lru_wx, lru_bx, lru_lambda, w_out, norm_ffn, w_gate, w_up, w_down, norm_final, loss_target, m_norm_mix, m_w_in, m_ssd_conv_w, m_ssd_conv_b, m_ssd_dt_bias, m_ssd_a_log, m_ssd_d, m_ssd_norm, m_lru_conv_w, m_lru_conv_b, m_lru_wa, m_lru_ba, m_lru_wx, m_lru_bx, m_lru_lambda, m_w_out, m_norm_ffn, m_w_gate, m_w_up, m_w_down, m_norm_final, v_norm_mix, v_w_in, v_ssd_conv_w, v_ssd_conv_b, v_ssd_dt_bias, v_ssd_a_log, v_ssd_d, v_ssd_norm, v_lru_conv_w, v_lru_conv_b, v_lru_wa, v_lru_ba, v_lru_wx, v_lru_bx, v_lru_lambda, v_w_out, v_norm_ffn, v_w_gate, v_w_up, v_w_down, v_norm_final):
    args = (norm_mix, w_in, ssd_conv_w, ssd_conv_b, ssd_dt_bias, ssd_a_log, ssd_d, ssd_norm, lru_conv_w, lru_conv_b, lru_wa, lru_ba, lru_wx, lru_bx, lru_lambda, w_out, norm_ffn, w_gate, w_up, w_down, norm_final)
    margs = (m_norm_mix, m_w_in, m_ssd_conv_w, m_ssd_conv_b, m_ssd_dt_bias, m_ssd_a_log, m_ssd_d, m_ssd_norm, m_lru_conv_w, m_lru_conv_b, m_lru_wa, m_lru_ba, m_lru_wx, m_lru_bx, m_lru_lambda, m_w_out, m_norm_ffn, m_w_gate, m_w_up, m_w_down, m_norm_final)
    vargs = (v_norm_mix, v_w_in, v_ssd_conv_w, v_ssd_conv_b, v_ssd_dt_bias, v_ssd_a_log, v_ssd_d, v_ssd_norm, v_lru_conv_w, v_lru_conv_b, v_lru_wa, v_lru_ba, v_lru_wx, v_lru_bx, v_lru_lambda, v_w_out, v_norm_ffn, v_w_gate, v_w_up, v_w_down, v_norm_final)
    p = dict(zip(WEIGHTS, args))
    pm = dict(zip(WEIGHTS, margs))
    pv = dict(zip(WEIGHTS, vargs))

    forms = {k: _shard_form(k, p[k]) for k in BIG}
    send = [_place_w_in(forms['w_in'])] + [forms[k].astype(BF16) for k in BIG[1:]]
    got = _all_gather("gather_weights", send + [_pack([p[k] for k in CONV], CONV_ROWS)])
    full = {'w_in': _layout_from_windows(got[0]).reshape(DEPTH, D_MODEL, IN_COLS_P),
            'w_out': _join_rows(got[1].reshape(N_DEV, DEPTH, D_MIX // N_DEV, D_MODEL)),
            'w_gate': _join_cols(got[2].reshape(N_DEV, DEPTH, D_MODEL, FF_SHARD_P)),
            'w_up': _join_cols(got[3].reshape(N_DEV, DEPTH, D_MODEL, FF_SHARD_P)),
            'w_down': _join_rows(got[4].reshape(N_DEV, DEPTH, FF_SHARD_P, D_MODEL))}
    for k, a in zip(CONV, _unpack(got[5], CONV_SHARD_SHAPES, lead=1)):
        full[k] = _join_cols(a)
    w = _prepare_weights(p, full)

    loss_local, dx, g = _local_step(x[0], loss_target[0], w, norm_final)
    loss = lax.psum(loss_local, ("x", "y", "c"))

    small_g = _all_gather("gather_small_grads", [_pack([g[k] for k in SMALL + CONV], SMALL_ROWS)])[0]
    zeros = [jnp.zeros_like(g[k]) for k in CONV]
    res_small = _adamw("adamw_small", _pack([p[k] for k in SMALL] + zeros, SMALL_ROWS),
                       _pack([pm[k] for k in SMALL] + zeros, SMALL_ROWS),
                       _pack([pv[k] for k in SMALL] + zeros, SMALL_ROWS), small_g, SMALL_TILE)
    small_shapes = [g[k].shape for k in SMALL + CONV]
    out = {kind: {} for kind in range(4)}
    for kind in range(4):
        for k, a in zip(SMALL + CONV, _unpack(res_small[kind], small_shapes)):
            out[kind][k] = a
    me = 4 * lax.axis_index("x") + 2 * lax.axis_index("y") + lax.axis_index("c")
    conv_g = []
    for k, shp in zip(CONV, CONV_SHARD_SHAPES):
        conv_g.append(lax.dynamic_slice_in_dim(out[0][k], me * shp[2], shp[2], axis=2))
    res_conv = _adamw("adamw_conv", _pack([p[k] for k in CONV], CONV_ROWS), _pack([pm[k] for k in CONV], CONV_ROWS),
                      _pack([pv[k] for k in CONV], CONV_ROWS), _pack(conv_g, CONV_ROWS)[None], CONV_ROWS)
    for kind in range(4):
        for k, a in zip(CONV, _unpack(res_conv[kind], CONV_SHARD_SHAPES)):
            out[kind][k] = a

    g_in = g['w_in'].reshape(DEPTH * D_MODEL, IN_COLS_P)
    dest = [jnp.stack([g_in[:, 512 * j:512 * j + IN_WIN] for j in range(N_DEV)]), _split_rows(g['w_out']),
            _split_cols(g['w_gate']), _split_cols(g['w_up']), _split_rows(g['w_down'])]
    parts = _all_to_all("exchange_big_grads", [a.astype(BF16) for a in dest])
    tiles = {'w_in': 256, 'w_out': 128, 'w_gate': 512, 'w_up': 512, 'w_down': 256}
    for k, part in zip(BIG, parts):
        res = _adamw("adamw_" + k, forms[k], _shard_form(k, pm[k]), _shard_form(k, pv[k]), part, tiles[k],
                     from_window=(k == 'w_in'))
        for kind in range(4):
            out[kind][k] = _shard_back(k, res[kind])

    outs = [loss, dx[None]]
    for kind in range(4):
        outs += [out[kind][k] for k in WEIGHTS]
    return tuple(outs)
```

```python
import functools
import math

import jax
import jax.numpy as jnp
from jax import lax
from jax.experimental import pallas as pl
from jax.experimental.pallas import tpu as pltpu

F32 = jnp.float32
BF16 = jnp.bfloat16

N_DEV = 8
DEPTH = 2
D_MODEL = 1024
ATT_W = 512
HEAD_DIM = 64
N_HEADS = 8
ATT_BLOCK = 128
ATT_DILATIONS = (16, 4, 1)
SSD_W = 512
SSD_STATE = 128
SSD_CONV = 1024
SSD_CHUNK = 128
LRU_W = 512
LRU_C = 8.0
D_MIX = 1536
D_FF = 2816
FF_SHARD = D_FF // N_DEV
FF_SHARD_P = 384
D_FFP = N_DEV * FF_SHARD_P
IN_COLS = 4104
IN_SHARD = IN_COLS // N_DEV
IN_WIN = 640
IN_COLS_P = 4224
REST_COLS = 2688
NORM_EPS = 1e-6
SSD_NORM_EPS = 1e-5
NEG = -1e30

ADAM_LR = 0.001
ADAM_B1 = 0.9
ADAM_B2 = 0.999
ADAM_EPS = 1e-08
ADAM_WD = 0.01
ADAM_STEP = 10

LANES = 128
VMEM_LIMIT = 52 * 1024 * 1024
HI = lax.Precision.HIGHEST


def _sigmoid(x):
    return 1.0 / (1.0 + jnp.exp(-x))


def _silu(x):
    return x * _sigmoid(x)


def _dsilu(x):
    s = _sigmoid(x)
    return s * (1.0 + x * (1.0 - s))


def _softplus(x):
    return jnp.maximum(x, 0.0) + jnp.log(1.0 + jnp.exp(-jnp.abs(x)))


_GELU_C = math.sqrt(2.0 / math.pi)


def _gelu(x):
    return 0.5 * x * (1.0 + jnp.tanh(_GELU_C * (x + 0.044715 * x * x * x)))


def _dgelu(x):
    t = jnp.tanh(_GELU_C * (x + 0.044715 * x * x * x))
    return 0.5 * (1.0 + t) + 0.5 * x * (1.0 - t * t) * _GELU_C * (1.0 + 3.0 * 0.044715 * x * x)


def _dot(a, b):
    return jnp.dot(a, b, preferred_element_type=F32)


def _dot_nt(a, b):
    return lax.dot_general(a, b, (((1,), (1,)), ((), ())), preferred_element_type=F32)


def _dot_tn(a, b):
    return lax.dot_general(a, b, (((0,), (0,)), ((), ())), preferred_element_type=F32)


def _dot_hi(a, b):
    return jnp.dot(a, b, preferred_element_type=F32, precision=HI)


def _iota(shape, axis):
    return lax.broadcasted_iota(jnp.int32, shape, axis)


def _shift_down(x, s, prev8):
    xs = pltpu.roll(x, s, 0)
    ps = pltpu.roll(prev8, s, 0)
    top = jnp.concatenate([ps, x[8:]], axis=0)
    return jnp.where(_iota(x.shape, 0) < s, top, xs)


def _shift_up(x, s, next8):
    tm = x.shape[0]
    xs = pltpu.roll(x, tm - s, 0)
    ns = pltpu.roll(next8, 8 - s, 0)
    bottom = jnp.concatenate([x[:tm - 8], ns], axis=0)
    return jnp.where(_iota(x.shape, 0) >= tm - s, bottom, xs)


def _expand_mat():
    return jnp.where(_iota((LANES, SSD_W), 1) // HEAD_DIM == _iota((LANES, SSD_W), 0), 1.0, 0.0).astype(F32)


def _reduce_mat():
    return jnp.where(_iota((SSD_W, LANES), 0) // HEAD_DIM == _iota((SSD_W, LANES), 1), 1.0, 0.0).astype(F32)


def _params(n_grid):
    return pltpu.CompilerParams(dimension_semantics=("arbitrary",) * n_grid, vmem_limit_bytes=VMEM_LIMIT)


def _rowwise(name, body, n_rows, tm, ins, outs, scratch=(), reverse=False):
    nt = n_rows // tm
    assert nt * tm == n_rows and tm % 8 == 0
    r8 = tm // 8
    last8 = n_rows // 8 - 1

    def pos(s):
        return (nt - 1 - s) if reverse else s

    in_specs, args = [], []
    for spec in ins:
        kind, arr = spec[0], spec[1]
        args.append(arr)
        if kind == 'row':
            in_specs.append(pl.BlockSpec((tm, arr.shape[1]), lambda s: (pos(s), 0)))
        elif kind == 'col':
            in_specs.append(pl.BlockSpec((tm, spec[2]), functools.partial(lambda s, j: (pos(s), j), j=spec[3])))
        elif kind == 'full':
            in_specs.append(pl.BlockSpec(arr.shape, functools.partial(lambda s, n: (0,) * n, n=arr.ndim)))
        elif kind == 'prev8':
            in_specs.append(pl.BlockSpec((8, spec[2]), functools.partial(
                lambda s, j: (jnp.maximum(pos(s) * r8 - 1, 0), j), j=spec[3])))
        elif kind == 'next8':
            in_specs.append(pl.BlockSpec((8, spec[2]), functools.partial(
                lambda s, j: (jnp.minimum((pos(s) + 1) * r8, last8), j), j=spec[3])))
        else:
            raise ValueError(kind)
    out_specs, out_shape, acc_idx = [], [], []
    for k, spec in enumerate(outs):
        if spec[0] == 'row':
            out_specs.append(pl.BlockSpec((tm, spec[1]), lambda s: (pos(s), 0)))
            out_shape.append(jax.ShapeDtypeStruct((n_rows, spec[1]), spec[2]))
        else:
            out_specs.append(pl.BlockSpec(spec[1], lambda s: (0, 0)))
            out_shape.append(jax.ShapeDtypeStruct(spec[1], spec[2]))
            acc_idx.append(k)
    n_in, n_out = len(ins), len(outs)

    def kern(*refs):
        s = pl.program_id(0)
        in_refs, out_refs, scr = refs[:n_in], refs[n_in:n_in + n_out], refs[n_in + n_out:]

        @pl.when(s == 0)
        def _():
            for k in acc_idx:
                out_refs[k][...] = jnp.zeros(out_refs[k].shape, out_refs[k].dtype)

        body(pos(s), nt, in_refs, out_refs, scr)

    res = pl.pallas_call(kern, name=name, grid=(nt,), in_specs=in_specs, out_specs=out_specs,
                         out_shape=out_shape, scratch_shapes=list(scratch), compiler_params=_params(1))(*args)
    return res


def _mm(name, a_list, b, *, res=None, out_dtype=F32, tm=512, tn=None):
    n_rows = a_list[0].shape[0]
    b_list = list(b) if isinstance(b, (list, tuple)) else [b]
    n = b_list[0].shape[1]
    ks = [a.shape[1] for a in a_list]
    assert sum(ks) == sum(m.shape[0] for m in b_list)
    assert len(b_list) == 1 or [m.shape[0] for m in b_list] == ks
    tn = n if tn is None else tn
    assert n_rows % tm == 0 and n % tn == 0
    na, nbm = len(a_list), len(b_list)

    def kern(*refs):
        a_refs, b_refs, o_ref = refs[:na], refs[na:na + nbm], refs[-1]
        acc, off = None, 0
        for idx, (a_ref, kp) in enumerate(zip(a_refs, ks)):
            rhs = b_refs[idx][...] if nbm > 1 else b_refs[0][off:off + kp, :]
            part = _dot(a_ref[...].astype(BF16), rhs)
            acc = part if acc is None else acc + part
            off += kp
        if res is not None:
            acc = acc + refs[na + nbm][...]
        o_ref[...] = acc.astype(out_dtype)

    in_specs = [pl.BlockSpec((tm, kp), lambda i, j: (i, 0)) for kp in ks]
    in_specs += [pl.BlockSpec((m.shape[0], tn), lambda i, j: (0, j)) for m in b_list]
    args = list(a_list) + b_list
    if res is not None:
        in_specs.append(pl.BlockSpec((tm, tn), lambda i, j: (i, j)))
        args.append(res)
    return pl.pallas_call(kern, name=name, grid=(n_rows // tm, n // tn), in_specs=in_specs,
                          out_specs=pl.BlockSpec((tm, tn), lambda i, j: (i, j)),
                          out_shape=jax.ShapeDtypeStruct((n_rows, n), out_dtype),
                          compiler_params=_params(2))(*args)


def _mm_tn(name, a, g, *, a_col=None, g_col=None, tk=None, tn=None, tt=512):
    n_rows = a.shape[0]
    k = a.shape[1] if a_col is None else a_col[0]
    a_j = 0 if a_col is None else a_col[1]
    n = g.shape[1] if g_col is None else g_col[0]
    g_j = 0 if g_col is None else g_col[1]
    tk = k if tk is None else tk
    tn = n if tn is None else tn
    assert k % tk == 0 and n % tn == 0 and n_rows % tt == 0
    kb = k // tk
    nbk = n // tn

    def kern(a_ref, g_ref, o_ref):
        t = pl.program_id(2)

        @pl.when(t == 0)
        def _():
            o_ref[...] = jnp.zeros(o_ref.shape, F32)

        o_ref[...] += _dot_tn(a_ref[...].astype(BF16), g_ref[...].astype(BF16))

    return pl.pallas_call(
        kern, name=name, grid=(kb, n // tn, n_rows // tt),
        in_specs=[pl.BlockSpec((tt, tk), lambda i, j, t: (t, a_j * kb + i)),
                  pl.BlockSpec((tt, tn), lambda i, j, t: (t, g_j * nbk + j))],
        out_specs=pl.BlockSpec((tk, tn), lambda i, j, t: (i, j)),
        out_shape=jax.ShapeDtypeStruct((k, n), F32), compiler_params=_params(3))(a, g)


def _rmsnorm_fwd(name, x, g):
    def body(i, nt, ins, outs, scr):
        xv = ins[0][...]
        rstd = lax.rsqrt(jnp.mean(xv * xv, axis=-1, keepdims=True) + NORM_EPS)
        outs[0][...] = (xv * rstd * ins[1][...]).astype(BF16)

    return _rowwise(name, body, x.shape[0], 512, [('row', x), ('full', g)], [('row', x.shape[1], BF16)])[0]


def _rmsnorm_bwd(name, dh, x, g, dres):
    d = x.shape[1]

    def body(i, nt, ins, outs, scr):
        dy, xv, gv, dr = ins[0][...], ins[1][...], ins[2][...], ins[3][...]
        rstd = lax.rsqrt(jnp.mean(xv * xv, axis=-1, keepdims=True) + NORM_EPS)
        xhat = xv * rstd
        outs[1][0:1, :] += jnp.sum(dy * xhat, axis=0, keepdims=True)
        dxh = dy * gv
        outs[0][...] = dr + rstd * (dxh - xhat * jnp.mean(dxh * xhat, axis=-1, keepdims=True))

    return _rowwise(name, body, x.shape[0], 512, [('row', dh), ('row', x), ('full', g), ('row', dres)],
                    [('row', d, F32), ('acc', (8, d), F32)])


def _final_loss(x, g, target):
    d = x.shape[1]

    def body(i, nt, ins, outs, scr):
        xv, gv, tv = ins[0][...], ins[1][...], ins[2][...]
        rstd = lax.rsqrt(jnp.mean(xv * xv, axis=-1, keepdims=True) + NORM_EPS)
        xhat = xv * rstd
        err = xhat * gv - tv
        row_loss = 0.5 * jnp.mean(err * err, axis=-1, keepdims=True)
        outs[1][...] += jnp.sum(row_loss, axis=0, keepdims=True)
        dy = err * (1.0 / d)
        outs[2][0:1, :] += jnp.sum(dy * xhat, axis=0, keepdims=True)
        dxh = dy * gv
        outs[0][...] = rstd * (dxh - xhat * jnp.mean(dxh * xhat, axis=-1, keepdims=True))

    return _rowwise("final_loss", body, x.shape[0], 512, [('row', x), ('full', g), ('row', target)],
                    [('row', d, F32), ('acc', (8, LANES), F32), ('acc', (8, d), F32)])


def _conv_fwd(name, src, width, idx, w, b):
    def body(i, nt, ins, outs, scr):
        xv = ins[0][...]
        prev = jnp.where(i > 0, ins[1][...], 0.0)
        wv = ins[2][...]
        y = ins[3][...] + wv[3:4, :] * xv
        for s in (1, 2, 3):
            y = y + wv[3 - s:4 - s, :] * _shift_down(xv, s, prev)
        outs[0][...] = y

    return _rowwise(name, body, src.shape[0], 512,
                    [('col', src, width, idx), ('prev8', src, width, idx), ('full', w), ('full', b)],
                    [('row', width, F32)])[0]


def _conv_bwd(name, dpre, src, width, idx, w):
    def body(i, nt, ins, outs, scr):
        dy = ins[0][...]
        nxt = jnp.where(i < nt - 1, ins[1][...], 0.0)
        xv = ins[2][...]
        prev = jnp.where(i > 0, ins[3][...], 0.0)
        wv = ins[4][...]
        dx = wv[3:4, :] * dy
        outs[1][3:4, :] += jnp.sum(dy * xv, axis=0, keepdims=True)
        outs[1][4:5, :] += jnp.sum(dy, axis=0, keepdims=True)
        for s in (1, 2, 3):
            dx = dx + wv[3 - s:4 - s, :] * _shift_up(dy, s, nxt)
            outs[1][3 - s:4 - s, :] += jnp.sum(dy * _shift_down(xv, s, prev), axis=0, keepdims=True)
        outs[0][...] = dx

    return _rowwise(name, body, src.shape[0], 512,
                    [('row', dpre), ('next8', dpre, width, 0), ('col', src, width, idx),
                     ('prev8', src, width, idx), ('full', w)],
                    [('row', width, F32), ('acc', (8, width), F32)])


ATT_STEP_BLOCKS = 4


def _att_bias(not_first, dil, head):
    qi = _iota((ATT_BLOCK, 2 * ATT_BLOCK), 0)
    ki = _iota((ATT_BLOCK, 2 * ATT_BLOCK), 1)
    dist = ATT_BLOCK + qi - ki
    valid = (dist >= 0) & (dist <= ATT_BLOCK) & (not_first | (ki >= ATT_BLOCK))
    slope = 2.0 ** (-(head + 1))
    return jnp.where(valid, (-slope * dil) * dist.astype(F32), NEG)


def _head_mask():
    lane = _iota((ATT_BLOCK, LANES), 1)
    return lane < HEAD_DIM


def _att_q_specs(dil, nb, bq):
    big = (bq * ATT_BLOCK, ATT_W)
    one = (ATT_BLOCK, ATT_W)
    specs = [pl.BlockSpec(big, lambda r, n: (n, 3 * r)),
             pl.BlockSpec(big, lambda r, n: (n, 3 * r + 1)),
             pl.BlockSpec(one, lambda r, n: (jnp.maximum(n * bq - 1, 0), 3 * r + 1)),
             pl.BlockSpec(big, lambda r, n: (n, 3 * r + 2)),
             pl.BlockSpec(one, lambda r, n: (jnp.maximum(n * bq - 1, 0), 3 * r + 2))]
    wide = pl.BlockSpec(big, lambda r, n: (n, r))
    stat = pl.BlockSpec((bq * ATT_BLOCK, LANES), lambda r, n: (n, r))
    return specs, wide, stat


def _att_fwd(dil, qkv_v):
    n_l = qkv_v.shape[0]
    nb = n_l // ATT_BLOCK
    bq = min(ATT_STEP_BLOCKS, nb)
    scale = HEAD_DIM ** -0.5

    def kern(q_ref, kc_ref, kp_ref, vc_ref, vp_ref, m_out, l_out, a_out):
        n = pl.program_id(1)
        low = _head_mask()
        lane = _iota((ATT_BLOCK, LANES), 1)
        for b in range(bq):
            rows = slice(ATT_BLOCK * b, ATT_BLOCK * (b + 1))
            prev = slice(ATT_BLOCK * (b - 1), ATT_BLOCK * b)
            not_first = (n * bq + b) > 0
            m_acc = jnp.zeros((ATT_BLOCK, LANES), F32)
            l_acc = jnp.zeros((ATT_BLOCK, LANES), F32)
            for p in range(N_HEADS // 2):
                sl = slice(LANES * p, LANES * (p + 1))
                q2 = q_ref[rows, sl].astype(F32)
                k_prev = kp_ref[:, sl] if b == 0 else kc_ref[prev, sl]
                v_prev = vp_ref[:, sl] if b == 0 else vc_ref[prev, sl]
                k2 = jnp.concatenate([k_prev, kc_ref[rows, sl]], axis=0).astype(BF16)
                v2 = jnp.concatenate([v_prev, vc_ref[rows, sl]], axis=0).astype(BF16)
                res = []
                for e in range(2):
                    h = 2 * p + e
                    keep = low if e == 0 else jnp.logical_not(low)
                    qe = jnp.where(keep, q2, 0.0).astype(BF16)
                    s = _dot_nt(qe, k2) * scale + _att_bias(not_first, dil, h)
                    m_new = jnp.max(s, axis=-1, keepdims=True)
                    pe = jnp.exp(s - m_new)
                    l_new = jnp.sum(pe, axis=-1, keepdims=True)
                    m_acc = jnp.where(lane == h, m_new, m_acc)
                    l_acc = jnp.where(lane == h, l_new, l_acc)
                    res.append(_dot(pe.astype(BF16), v2))
                a_out[rows, sl] = jnp.where(low, res[0], res[1])
            m_out[rows, :] = m_acc
            l_out[rows, :] = l_acc

    specs, wide, stat = _att_q_specs(dil, nb, bq)
    shp_s = jax.ShapeDtypeStruct((n_l, dil * LANES), F32)
    shp_a = jax.ShapeDtypeStruct((n_l, dil * ATT_W), F32)
    return pl.pallas_call(kern, name="att_fwd_d%d" % dil, grid=(dil, nb // bq), in_specs=specs,
                          out_specs=[stat, stat, wide], out_shape=[shp_s, shp_s, shp_a],
                          compiler_params=_params(2))(*([qkv_v] * 5))


def _att_merge(parts):
    n_pat = len(parts)

    def body(i, nt, ins, outs, scr):
        ms = [ins[3 * g][...] for g in range(n_pat)]
        m_all = functools.reduce(jnp.maximum, ms)
        expand = _expand_mat()
        num, den = None, None
        for g in range(n_pat):
            e = jnp.exp(ms[g] - m_all)
            d_g = ins[3 * g + 1][...] * e
            n_g = ins[3 * g + 2][...] * _dot_hi(e, expand)
            num = n_g if num is None else num + n_g
            den = d_g if den is None else den + d_g
        real = _iota(den.shape, 1) < N_HEADS
        outs[0][...] = num / _dot_hi(den, expand)
        outs[1][...] = jnp.where(real, m_all + jnp.log(jnp.where(real, den, 1.0)), 0.0)

    ins = []
    for m, l, acc in parts:
        ins += [('row', m), ('row', l), ('row', acc)]
    return _rowwise("att_merge", body, parts[0][0].shape[0], 512, ins,
                    [('row', ATT_W, F32), ('row', LANES, F32)])


def _att_delta(d_att, out):
    def body(i, nt, ins, outs, scr):
        outs[0][...] = _dot_hi(ins[0][...] * ins[1][...], _reduce_mat())

    return _rowwise("att_delta", body, out.shape[0], 512, [('row', d_att), ('row', out)],
                    [('row', LANES, F32)])[0]


def _att_bwd(dil, qkv_v, do_v, lse_v, delta_v, dkv_in):
    n_l = qkv_v.shape[0]
    nb = n_l // ATT_BLOCK
    bq = min(ATT_STEP_BLOCKS, nb)
    steps = nb // bq
    first = dkv_in is None
    scale = HEAD_DIM ** -0.5

    def kern(*refs):
        j = pl.program_id(1)
        k_ref, v_ref, qc_ref, qn_ref, doc_ref, don_ref, lc_ref, ln_ref, dc_ref, dn_ref = refs[:10]
        dk_out, dv_out, dq_out, carry = refs[-4:]
        low = _head_mask()
        row = _iota((2 * ATT_BLOCK, ATT_BLOCK), 0)
        key = _iota((2 * ATT_BLOCK, ATT_BLOCK), 1)
        dist = row - key
        low2 = _iota((2 * ATT_BLOCK, LANES), 1) < HEAD_DIM

        @pl.when(j == 0)
        def _():
            carry[...] = jnp.zeros(carry.shape, F32)

        dq_prev = [carry[:, LANES * p:LANES * (p + 1)] for p in range(N_HEADS // 2)]
        for b in range(bq):
            rows = slice(ATT_BLOCK * b, ATT_BLOCK * (b + 1))
            nrows = slice(ATT_BLOCK * (b + 1), ATT_BLOCK * (b + 2))
            inner = b < bq - 1
            has_next = True if inner else (j < steps - 1)
            valid = (dist >= 0) & (dist <= ATT_BLOCK) & ((row < ATT_BLOCK) | has_next)
            lse2 = jnp.concatenate([lc_ref[rows, :], lc_ref[nrows, :] if inner else ln_ref[...]], axis=0)
            dl2 = jnp.concatenate([dc_ref[rows, :], dc_ref[nrows, :] if inner else dn_ref[...]], axis=0)
            for p in range(N_HEADS // 2):
                sl = slice(LANES * p, LANES * (p + 1))
                k2 = k_ref[rows, sl].astype(F32)
                v2 = v_ref[rows, sl].astype(F32)
                q2 = jnp.concatenate([qc_ref[rows, sl], qc_ref[nrows, sl] if inner else qn_ref[:, sl]],
                                     axis=0).astype(F32)
                do2 = jnp.concatenate([doc_ref[rows, sl], doc_ref[nrows, sl] if inner else don_ref[:, sl]], axis=0)
                q2b = q2.astype(BF16)
                do2b = do2.astype(BF16)
                dks, dvs, dq2 = [], [], None
                for e in range(2):
                    h = 2 * p + e
                    keep = low if e == 0 else jnp.logical_not(low)
                    keep2 = low2 if e == 0 else jnp.logical_not(low2)
                    slope = 2.0 ** (-(h + 1))
                    bias = jnp.where(valid, (-slope * dil) * dist.astype(F32), NEG)
                    ke = jnp.where(keep, k2, 0.0).astype(BF16)
                    ve = jnp.where(keep, v2, 0.0).astype(BF16)
                    s = _dot_nt(q2b, ke) * scale + bias
                    pe = jnp.exp(s - lse2[:, h:h + 1])
                    dp = _dot_nt(do2b, ve)
                    dsb = (pe * (dp - dl2[:, h:h + 1])).astype(BF16)
                    dvs.append(_dot_tn(pe.astype(BF16), jnp.where(keep2, do2, 0.0).astype(BF16)))
                    dks.append(_dot_tn(dsb, jnp.where(keep2, q2, 0.0).astype(BF16)) * scale)
                    dqe = _dot(dsb, ke)
                    dq2 = dqe if dq2 is None else dq2 + dqe
                dk = jnp.where(low, dks[0], dks[1])
                dv = jnp.where(low, dvs[0], dvs[1])
                dq = dq_prev[p] + dq2[:ATT_BLOCK] * scale
                dq_prev[p] = dq2[ATT_BLOCK:] * scale
                if not first:
                    dk = dk + refs[10][rows, sl].astype(F32)
                    dv = dv + refs[11][rows, sl].astype(F32)
                    dq = dq + refs[12][rows, sl].astype(F32)
                dk_out[rows, sl] = dk.astype(BF16)
                dv_out[rows, sl] = dv.astype(BF16)
                dq_out[rows, sl] = dq.astype(BF16)
        for p in range(N_HEADS // 2):
            carry[:, LANES * p:LANES * (p + 1)] = dq_prev[p]

    big = (bq * ATT_BLOCK, ATT_W)
    one = (ATT_BLOCK, ATT_W)

    def nxt_idx(j):
        return jnp.minimum((j + 1) * bq, nb - 1)

    cur = pl.BlockSpec(big, lambda r, j: (j, r))
    nxt = pl.BlockSpec(one, lambda r, j: (nxt_idx(j), r))
    cur_s = pl.BlockSpec((bq * ATT_BLOCK, LANES), lambda r, j: (j, r))
    nxt_s = pl.BlockSpec((ATT_BLOCK, LANES), lambda r, j: (nxt_idx(j), r))
    in_specs = [pl.BlockSpec(big, lambda r, j: (j, 3 * r + 1)),
                pl.BlockSpec(big, lambda r, j: (j, 3 * r + 2)),
                pl.BlockSpec(big, lambda r, j: (j, 3 * r)),
                pl.BlockSpec(one, lambda r, j: (nxt_idx(j), 3 * r)),
                cur, nxt, cur_s, nxt_s, cur_s, nxt_s]
    args = [qkv_v] * 4 + [do_v, do_v, lse_v, lse_v, delta_v, delta_v]
    if not first:
        in_specs += [cur, cur, cur]
        args += list(dkv_in)
    shp = jax.ShapeDtypeStruct((n_l, dil * ATT_W), BF16)
    return pl.pallas_call(kern, name="att_bwd_d%d" % dil, grid=(dil, steps), in_specs=in_specs,
                          out_specs=[cur, cur, cur], out_shape=[shp, shp, shp],
                          scratch_shapes=[pltpu.VMEM((ATT_BLOCK, ATT_W), F32)], compiler_params=_params(2))(*args)


def _attention_fwd(qkv):
    t = qkv.shape[0]
    parts = [[s.reshape(t, -1) for s in _att_fwd(dil, qkv.reshape(t // dil, dil * D_MIX))]
             for dil in ATT_DILATIONS]
    return _att_merge(parts)


def _attention_bwd(qkv, d_att, out, lse):
    t = qkv.shape[0]
    delta = _att_delta(d_att, out)
    grads = None
    for dil in ATT_DILATIONS:
        view = lambda a: a.reshape(t // dil, -1)
        grads = _att_bwd(dil, view(qkv), view(d_att), view(lse), view(delta),
                         None if grads is None else [view(a) for a in grads])
    dk, dv, dq = [a.reshape(t, ATT_W) for a in grads]
    return dq, dk, dv


def _ssd_chunk_common(pre, dtraw, bias_row, alog_row):
    q = SSD_CHUNK
    act = _silu(pre)
    lane = _iota((q, LANES), 1)
    dt = jnp.where(lane < N_HEADS, _softplus(dtraw + bias_row), 0.0)
    a_row = -jnp.exp(alog_row)
    tril = jnp.where(_iota((q, q), 0) >= _iota((q, q), 1), 1.0, 0.0).astype(F32)
    cs = _dot_hi(tril, dt * a_row)
    cs_last = cs[q - 1:q, :]
    return act, dt, a_row, tril, cs, cs_last


def _ssd_lmat(cs, cs_t, h):
    q = SSD_CHUNK
    seg = cs[:, h:h + 1] - cs_t[h:h + 1, :]
    causal = _iota((q, q), 0) >= _iota((q, q), 1)
    return jnp.exp(jnp.where(causal, seg, NEG))


def _ssd_gate_norm(y, z, norm_w):
    sz = _silu(z)
    yg = y * sz
    half = SSD_W // 2
    outs, rss = [], []
    for g in range(2):
        part = yg[:, half * g:half * (g + 1)]
        rs = lax.rsqrt(jnp.mean(part * part, axis=-1, keepdims=True) + SSD_NORM_EPS)
        outs.append(part * rs)
        rss.append(rs)
    yn = jnp.concatenate(outs, axis=1)
    return sz, yn, rss, yn * norm_w


def _ssd_fwd(pre, rest, dt_bias, a_log, d_skip, norm_w):
    t = pre.shape[0]
    q = SSD_CHUNK

    def body(c, nc, ins, outs, scr):
        pre_ref, z_ref, dtr_ref, bias_ref, alog_ref, dsk_ref, nw_ref = ins
        out_ref, y_ref, sp_ref = outs
        s_ref = scr[0]

        @pl.when(c == 0)
        def _():
            s_ref[...] = jnp.zeros(s_ref.shape, F32)

        act, dt, a_row, tril, cs, cs_last = _ssd_chunk_common(pre_ref[...], dtr_ref[...], bias_ref[...],
                                                               alog_ref[...])
        x = act[:, :SSD_W]
        cs_t = cs.T
        e_col = jnp.exp(cs)
        w = jnp.exp(cs_last - cs) * dt
        expand = _expand_mat()
        w_x = _dot_hi(w, expand)
        dt_x = _dot_hi(dt, expand)
        e_x = _dot_hi(e_col, expand)
        d_x = _dot_hi(dsk_ref[...], expand)
        cd_x = _dot_hi(jnp.exp(cs_last), expand)
        s_prev = s_ref[...]
        sp_ref[...] = s_prev
        xw = (x * w_x).astype(BF16)
        xd = (x * dt_x).astype(BF16)
        low = _head_mask()
        y_parts, s_parts = [], []
        for g in range(2):
            bg = act[:, SSD_W + SSD_STATE * g:SSD_W + SSD_STATE * (g + 1)].astype(BF16)
            cg = act[:, SSD_W + 2 * SSD_STATE + SSD_STATE * g:SSD_W + 2 * SSD_STATE + SSD_STATE * (g + 1)].astype(BF16)
            gsl = slice(256 * g, 256 * (g + 1))
            gmat = _dot_nt(cg, bg)
            s_parts.append(_dot_tn(bg, xw[:, gsl]))
            y0 = _dot(cg, s_prev[:, gsl].astype(BF16))
            for pp in range(2):
                pair = 2 * g + pp
                psl = slice(LANES * pair, LANES * (pair + 1))
                yd = []
                for e in range(2):
                    h = 2 * pair + e
                    mh = (gmat * _ssd_lmat(cs, cs_t, h)).astype(BF16)
                    yd.append(_dot(mh, xd[:, psl]))
                y_parts.append(jnp.where(low, yd[0], yd[1]) + e_x[:, psl] * y0[:, LANES * pp:LANES * (pp + 1)])
        y = jnp.concatenate(y_parts, axis=1) + d_x * x
        s_ref[...] = cd_x * s_prev + jnp.concatenate(s_parts, axis=1)
        y_ref[...] = y
        out_ref[...] = _ssd_gate_norm(y, z_ref[...], nw_ref[...])[3]

    return _rowwise("ssd_fwd", body, t, q,
                    [('row', pre), ('col', rest, SSD_W, 2), ('col', rest, LANES, 20), ('full', dt_bias),
                     ('full', a_log), ('full', d_skip), ('full', norm_w)],
                    [('row', SSD_W, F32), ('row', SSD_W, F32), ('row', SSD_W, F32)],
                    scratch=[pltpu.VMEM((SSD_STATE, SSD_W), F32)])


def _ssd_bwd(pre, rest, y, s_prev_all, d_mix, dt_bias, a_log, d_skip, norm_w):
    t = pre.shape[0]
    q = SSD_CHUNK

    def body(c, nc, ins, outs, scr):
        pre_ref, z_ref, dtr_ref, y_ref, sp_ref, do_ref, bias_ref, alog_ref, dsk_ref, nw_ref = ins
        dpre_ref, dz_ref, ddt_ref, a128_ref, a512_ref = outs
        ds_ref = scr[0]

        @pl.when(c == nc - 1)
        def _():
            ds_ref[...] = jnp.zeros(ds_ref.shape, F32)

        pre_v = pre_ref[...]
        dtr = dtr_ref[...]
        act, dt, a_row, tril, cs, cs_last = _ssd_chunk_common(pre_v, dtr, bias_ref[...], alog_ref[...])
        x = act[:, :SSD_W]
        cs_t = cs.T
        e_col = jnp.exp(cs)
        decay_end = jnp.exp(cs_last - cs)
        w = decay_end * dt
        cd = jnp.exp(cs_last)
        expand = _expand_mat()
        reduce = _reduce_mat()
        w_x = _dot_hi(w, expand)
        dt_x = _dot_hi(dt, expand)
        e_x = _dot_hi(e_col, expand)
        d_x = _dot_hi(dsk_ref[...], expand)
        cd_x = _dot_hi(cd, expand)
        s_prev = sp_ref[...]
        d_s = ds_ref[...]
        xw = (x * w_x).astype(BF16)
        xd = (x * dt_x).astype(BF16)
        low = _head_mask()
        lane = _iota((q, LANES), 1)
        sub = _iota((q, LANES), 0)

        yv, zv, nw = y_ref[...], z_ref[...], nw_ref[...]
        d_out = do_ref[...]
        sz, yn, rss, _ = _ssd_gate_norm(yv, zv, nw)
        a512_ref[0:1, :] += jnp.sum(d_out * yn, axis=0, keepdims=True)
        dyn = d_out * nw
        half = SSD_W // 2
        dyg_parts = []
        for g in range(2):
            hs = slice(half * g, half * (g + 1))
            dyg_parts.append(rss[g] * (dyn[:, hs] - yn[:, hs] * jnp.mean(dyn[:, hs] * yn[:, hs], axis=-1,
                                                                          keepdims=True)))
        dyg = jnp.concatenate(dyg_parts, axis=1)
        dy = dyg * sz
        dz_ref[...] = dyg * yv * _dsilu(zv)

        a128_ref[2:3, :] += _dot_hi(jnp.sum(dy * x, axis=0, keepdims=True), reduce)
        dx = d_x * dy

        dy0 = e_x * dy
        dyb = dy.astype(BF16)
        dcs = jnp.zeros((q, LANES), F32)
        dcs_rows = jnp.zeros((q, LANES), F32)
        ddt = jnp.zeros((q, LANES), F32)
        ds_prev_parts, z_parts, db_parts, dc_parts, dxd_parts, y0_parts = [], [], [], [], [], []
        for g in range(2):
            bg = act[:, SSD_W + SSD_STATE * g:SSD_W + SSD_STATE * (g + 1)].astype(BF16)
            cg = act[:, SSD_W + 2 * SSD_STATE + SSD_STATE * g:SSD_W + 2 * SSD_STATE + SSD_STATE * (g + 1)].astype(BF16)
            gsl = slice(256 * g, 256 * (g + 1))
            spg = s_prev[:, gsl].astype(BF16)
            dsg = d_s[:, gsl].astype(BF16)
            dy0g = dy0[:, gsl].astype(BF16)
            gmat = _dot_nt(cg, bg)
            y0_parts.append(_dot(cg, spg))
            dc_g = _dot_nt(dy0g, spg)
            ds_prev_parts.append(_dot_tn(cg, dy0g))
            z_parts.append(_dot(bg, dsg))
            db_g = _dot_nt(xw[:, gsl], dsg)
            dg_acc = jnp.zeros((q, q), F32)
            for pp in range(2):
                pair = 2 * g + pp
                psl = slice(LANES * pair, LANES * (pair + 1))
                dxd_e = []
                for e in range(2):
                    h = 2 * pair + e
                    keep = low if e == 0 else jnp.logical_not(low)
                    lm = _ssd_lmat(cs, cs_t, h)
                    mh = gmat * lm
                    dm = _dot_nt(jnp.where(keep, dy[:, psl], 0.0).astype(BF16), xd[:, psl])
                    dxd_e.append(_dot_tn(mh.astype(BF16), dyb[:, psl]))
                    wm = dm * mh
                    dcs = dcs + jnp.where(lane == h, jnp.sum(wm, axis=1, keepdims=True), 0.0)
                    dcs_rows = dcs_rows - jnp.where(sub == h, jnp.sum(wm, axis=0, keepdims=True), 0.0)
                    dg_acc = dg_acc + dm * lm
                dxd_parts.append(jnp.where(low, dxd_e[0], dxd_e[1]))
            dgb = dg_acc.astype(BF16)
            dc_parts.append(dc_g + _dot(dgb, bg))
            db_parts.append(db_g + _dot_tn(dgb, cg))
        y0 = jnp.concatenate(y0_parts, axis=1)
        zmat = jnp.concatenate(z_parts, axis=1)
        dxd = jnp.concatenate(dxd_parts, axis=1)
        ds_prev = jnp.concatenate(ds_prev_parts, axis=1) + cd_x * d_s
        ds_ref[...] = ds_prev

        dcs = dcs + _dot_hi(dy * y0, reduce) * e_col
        dcd = _dot_hi(jnp.sum(d_s * s_prev, axis=0, keepdims=True), reduce)
        dlast = dcd * cd
        dx = dx + w_x * zmat + dxd * dt_x
        dw = _dot_hi(zmat * x, reduce)
        ddt = ddt + dw * decay_end + _dot_hi(dxd * x, reduce)
        dwl = dw * w
        dcs = dcs - dwl
        dlast = dlast + jnp.sum(dwl, axis=0, keepdims=True)
        dcs = dcs + dcs_rows.T + jnp.where(sub == q - 1, dlast, 0.0)
        dda = _dot_hi(tril.T, dcs)
        ddt = ddt + dda * a_row
        a128_ref[1:2, :] += jnp.sum(dda * dt, axis=0, keepdims=True) * a_row
        draw = jnp.where(lane < N_HEADS, ddt * _sigmoid(dtr + bias_ref[...]), 0.0)
        a128_ref[0:1, :] += jnp.sum(draw, axis=0, keepdims=True)
        ddt_ref[...] = draw
        dact = jnp.concatenate([dx] + db_parts + dc_parts, axis=1)
        dpre_ref[...] = dact * _dsilu(pre_v)

    return _rowwise("ssd_bwd", body, t, q,
                    [('row', pre), ('col', rest, SSD_W, 2), ('col', rest, LANES, 20), ('row', y),
                     ('row', s_prev_all), ('col', d_mix, SSD_W, 0), ('full', dt_bias), ('full', a_log),
                     ('full', d_skip), ('full', norm_w)],
                    [('row', SSD_CONV, F32), ('row', SSD_W, F32), ('row', LANES, F32),
                     ('acc', (8, LANES), F32), ('acc', (8, SSD_W), F32)],
                    scratch=[pltpu.VMEM((SSD_STATE, SSD_W), F32)], reverse=True)


LRU_TM = 256


def _lru_gates(xc, wa, ba, wx, bx, lam):
    xb = xc.astype(BF16)
    r = _sigmoid(_dot(xb, wa) + ba)
    i = _sigmoid(_dot(xb, wx) + bx)
    sp = _softplus(-lam)
    a = jnp.exp(-LRU_C * r * sp)
    mult = jnp.sqrt(1.0 - a * a)
    return r, i, sp, a, mult


def _lru_fwd(xc, rest, wa, ba, wx, bx, lam):
    def body(i, nt, ins, outs, scr):
        xc_ref, g_ref, wa_ref, ba_ref, wx_ref, bx_ref, lam_ref = ins
        carry = scr[0]

        @pl.when(i == 0)
        def _():
            carry[...] = jnp.zeros(carry.shape, F32)

        xv = xc_ref[...]
        r, ig, sp, a, mult = _lru_gates(xv, wa_ref[...], ba_ref[...], wx_ref[...], bx_ref[...], lam_ref[...])
        u = mult * (ig * xv)
        row = _iota(a.shape, 0)
        s = 1
        while s < LRU_TM:
            a_sh = jnp.where(row >= s, pltpu.roll(a, s, 0), 1.0)
            u_sh = jnp.where(row >= s, pltpu.roll(u, s, 0), 0.0)
            u = a * u_sh + u
            a = a * a_sh
            s *= 2
        h = u + a * carry[0:1, :]
        carry[0:1, :] = h[LRU_TM - 1:LRU_TM, :]
        outs[1][...] = h
        outs[0][...] = h * _gelu(g_ref[...])

    return _rowwise("lru_fwd", body, xc.shape[0], LRU_TM,
                    [('row', xc), ('col', rest, LRU_W, 3), ('full', wa), ('full', ba), ('full', wx),
                     ('full', bx), ('full', lam)],
                    [('row', LRU_W, F32), ('row', LRU_W, F32)], scratch=[pltpu.VMEM((8, LRU_W), F32)])


def _lru_bwd(xc, rest, h, d_mix, wa, ba, wx, bx, lam, wa_t, wx_t):
    def body(i, nt, ins, outs, scr):
        xc_ref, g_ref, h_ref, hp_ref, do_ref, wa_ref, ba_ref, wx_ref, bx_ref, lam_ref, wat_ref, wxt_ref = ins
        dxc_ref, dg_ref, dza_ref, dzi_ref, acc_ref = outs
        carry = scr[0]

        @pl.when(i == nt - 1)
        def _():
            carry[...] = jnp.zeros(carry.shape, F32)

        xv, gv, hv, d_out = xc_ref[...], g_ref[...], h_ref[...], do_ref[...]
        r, ig, sp, a, mult = _lru_gates(xv, wa_ref[...], ba_ref[...], wx_ref[...], bx_ref[...], lam_ref[...])
        dg_ref[...] = d_out * hv * _dgelu(gv)
        gsum = d_out * _gelu(gv)
        row = _iota(a.shape, 0)
        b = jnp.where(row < LRU_TM - 1, pltpu.roll(a, LRU_TM - 1, 0), 1.0)
        s = 1
        while s < LRU_TM:
            keep = row < LRU_TM - s
            b_sh = jnp.where(keep, pltpu.roll(b, LRU_TM - s, 0), 1.0)
            g_sh = jnp.where(keep, pltpu.roll(gsum, LRU_TM - s, 0), 0.0)
            gsum = gsum + b * g_sh
            b = b * b_sh
            s *= 2
        dh = gsum + b * carry[0:1, :]
        carry[0:1, :] = a[0:1, :] * dh[0:1, :]
        h_prev = _shift_down(hv, 1, jnp.where(i > 0, hp_ref[...], 0.0))
        du = dh
        dmult = du * ig * xv
        di = du * mult * xv
        dxc = du * mult * ig
        da = dh * h_prev - dmult * a / mult
        dlog = da * a
        dr = dlog * (-LRU_C) * sp
        acc_ref[2:3, :] += jnp.sum(dlog * (-LRU_C) * r, axis=0, keepdims=True)
        dza = dr * r * (1.0 - r)
        dzi = di * ig * (1.0 - ig)
        acc_ref[0:1, :] += jnp.sum(dza, axis=0, keepdims=True)
        acc_ref[1:2, :] += jnp.sum(dzi, axis=0, keepdims=True)
        dzab, dzib = dza.astype(BF16), dzi.astype(BF16)
        dza_ref[...] = dzab
        dzi_ref[...] = dzib
        dxc_ref[...] = dxc + _dot(dzab, wat_ref[...]) + _dot(dzib, wxt_ref[...])

    return _rowwise("lru_bwd", body, xc.shape[0], LRU_TM,
                    [('row', xc), ('col', rest, LRU_W, 3), ('row', h), ('prev8', h, LRU_W, 0),
                     ('col', d_mix, LRU_W, 1), ('full', wa), ('full', ba), ('full', wx), ('full', bx),
                     ('full', lam), ('full', wa_t), ('full', wx_t)],
                    [('row', LRU_W, F32), ('row', LRU_W, F32), ('row', LRU_W, BF16), ('row', LRU_W, BF16),
                     ('acc', (8, LRU_W), F32)],
                    scratch=[pltpu.VMEM((8, LRU_W), F32)], reverse=True)


FFN_TM = 512
FFN_TN = 1536


def _ffn_up(name, h2, w_gate, w_up):
    t, k = h2.shape

    def kern(a_ref, wg_ref, wu_ref, g_ref, u_ref, act_ref):
        a = a_ref[...].astype(BF16)
        gv = _dot(a, wg_ref[...])
        uv = _dot(a, wu_ref[...])
        g_ref[...] = gv.astype(BF16)
        u_ref[...] = uv.astype(BF16)
        act_ref[...] = (_silu(gv) * uv).astype(BF16)

    tile = pl.BlockSpec((FFN_TM, FFN_TN), lambda i, j: (i, j))
    return pl.pallas_call(
        kern, name=name, grid=(t // FFN_TM, D_FFP // FFN_TN),
        in_specs=[pl.BlockSpec((FFN_TM, k), lambda i, j: (i, 0)),
                  pl.BlockSpec((k, FFN_TN), lambda i, j: (0, j)),
                  pl.BlockSpec((k, FFN_TN), lambda i, j: (0, j))],
        out_specs=[tile, tile, tile],
        out_shape=[jax.ShapeDtypeStruct((t, D_FFP), BF16)] * 3,
        compiler_params=_params(2))(h2, w_gate, w_up)


def _ffn_down_bwd(name, dx, w_down_t, gate, up):
    t, k = dx.shape

    def kern(dx_ref, w_ref, g_ref, u_ref, dg_ref, du_ref):
        da = _dot(dx_ref[...].astype(BF16), w_ref[...])
        gv, uv = g_ref[...].astype(F32), u_ref[...].astype(F32)
        dg_ref[...] = (da * uv * _dsilu(gv)).astype(BF16)
        du_ref[...] = (da * _silu(gv)).astype(BF16)

    tile = pl.BlockSpec((FFN_TM, FFN_TN), lambda i, j: (i, j))
    shp = jax.ShapeDtypeStruct((t, D_FFP), BF16)
    return pl.pallas_call(
        kern, name=name, grid=(t // FFN_TM, D_FFP // FFN_TN),
        in_specs=[pl.BlockSpec((FFN_TM, k), lambda i, j: (i, 0)),
                  pl.BlockSpec((k, FFN_TN), lambda i, j: (0, j)), tile, tile],
        out_specs=[tile, tile], out_shape=[shp, shp], compiler_params=_params(2))(dx, w_down_t, gate, up)


def _layer_fwd(x, w, l):
    tag = "_l%d" % l
    h = _rmsnorm_fwd("norm_mix" + tag, x, w['norm_mix'][l])
    qkv = _mm("proj_qkv" + tag, [h], w['w_qkv'][l], tn=768, out_dtype=BF16)
    rest = _mm("proj_rest" + tag, [h], w['w_rest'][l], tn=896)
    att, lse = _attention_fwd(qkv)
    pre = _conv_fwd("ssd_conv" + tag, rest, SSD_CONV, 0, w['ssd_conv_w'][l], w['ssd_conv_b'][l])
    ssd, y, s_prev = _ssd_fwd(pre, rest, w['ssd_dt_bias'][l], w['ssd_a_log'][l], w['ssd_d'][l], w['ssd_norm'][l])
    xc = _conv_fwd("lru_conv" + tag, rest, LRU_W, 4, w['lru_conv_w'][l], w['lru_conv_b'][l])
    lru, hl = _lru_fwd(xc, rest, w['lru_wa'][l], w['lru_ba'][l], w['lru_wx'][l], w['lru_bx'][l], w['lru_lambda'][l])
    x_mid = _mm("proj_out" + tag, [att, ssd, lru], w['w_out'][l], res=x, tn=512)
    h2 = _rmsnorm_fwd("norm_ffn" + tag, x_mid, w['norm_ffn'][l])
    gate, up, act = _ffn_up("proj_gu" + tag, h2, w['w_gate'][l], w['w_up'][l])
    x_next = _mm("proj_down" + tag, [act], w['w_down'][l], res=x_mid, tn=512)
    saved = dict(x=x, h=h, qkv=qkv, rest=rest, att=att, lse=lse, pre=pre, ssd=ssd, y=y, s_prev=s_prev, xc=xc,
                 lru=lru, hl=hl, x_mid=x_mid, h2=h2, gate=gate, up=up, act=act)
    return x_next, saved


def _layer_bwd(dx_next, sv, w, l):
    tag = "_l%d_b" % l
    t = dx_next.shape[0]
    g = {}
    dgate, dup = _ffn_down_bwd("d_act" + tag, dx_next, w['w_down_t'][l], sv['gate'], sv['up'])
    g['w_down'] = _mm_tn("dw_down" + tag, sv['act'], dx_next, tk=1536)
    dh2 = _mm("d_h2" + tag, [dgate, dup], [w['w_gate_t'][l], w['w_up_t'][l]], tn=512)
    g['w_gate'] = _mm_tn("dw_gate" + tag, sv['h2'], dgate, tn=1536)
    g['w_up'] = _mm_tn("dw_up" + tag, sv['h2'], dup, tn=1536)
    dx_mid, acc = _rmsnorm_bwd("norm_ffn" + tag, dh2, sv['x_mid'], w['norm_ffn'][l], dx_next)
    g['norm_ffn'] = acc[0]
    d_att = _mm("d_att" + tag, [dx_mid], w['w_out_t'][l][:, :ATT_W], tn=512)
    d_mix = _mm("d_mix" + tag, [dx_mid], w['w_out_t'][l][:, ATT_W:], tn=512)
    g['w_out'] = jnp.concatenate([_mm_tn("dw_out%d" % k + tag, a, dx_mid)
                                  for k, a in enumerate((sv['att'], sv['ssd'], sv['lru']))], axis=0)
    dxc, dgl, dza, dzi, acc = _lru_bwd(sv['xc'], sv['rest'], sv['hl'], d_mix, w['lru_wa'][l], w['lru_ba'][l],
                                       w['lru_wx'][l], w['lru_bx'][l], w['lru_lambda'][l],
                                       w['lru_wa_t'][l], w['lru_wx_t'][l])
    g['lru_ba'], g['lru_bx'] = acc[0], acc[1]
    g['lru_lambda'] = acc[2] * (-_sigmoid(-w['lru_lambda'][l][0]))
    g['lru_wa'] = _diag_blocks(_mm_tn("dw_lru_a" + tag, sv['xc'], dza))
    g['lru_wx'] = _diag_blocks(_mm_tn("dw_lru_x" + tag, sv['xc'], dzi))
    dxl, acc = _conv_bwd("lru_conv" + tag, dxc, sv['rest'], LRU_W, 4, w['lru_conv_w'][l])
    g['lru_conv_w'], g['lru_conv_b'] = acc[:4], acc[4]
    dpre, dz, ddt, a128, a512 = _ssd_bwd(sv['pre'], sv['rest'], sv['y'], sv['s_prev'], d_mix, w['ssd_dt_bias'][l],
                                         w['ssd_a_log'][l], w['ssd_d'][l], w['ssd_norm'][l])
    g['ssd_dt_bias'], g['ssd_a_log'], g['ssd_d'] = a128[0, :N_HEADS], a128[1, :N_HEADS], a128[2, :N_HEADS]
    g['ssd_norm'] = a512[0]
    dxbc, acc = _conv_bwd("ssd_conv" + tag, dpre, sv['rest'], SSD_CONV, 0, w['ssd_conv_w'][l])
    g['ssd_conv_w'], g['ssd_conv_b'] = acc[:4], acc[4]
    dq, dk, dv = _attention_bwd(sv['qkv'], d_att, sv['att'], sv['lse'])
    pieces = [dq, dk, dv, dxbc, dz, dgl, dxl, ddt]
    dh = _mm("d_h" + tag, pieces, w['w_in_t'][l], tn=512)
    dws = [_mm_tn("dw_in%d" % k + tag, sv['h'], p) for k, p in enumerate(pieces)]
    g['w_in'] = jnp.concatenate([dws[0], dws[1], dws[2], dws[4], dws[3], _dt_tile_place(dws[7]), dws[5], dws[6]],
                                axis=1)
    dx, acc = _rmsnorm_bwd("norm_mix" + tag, dh, sv['x'], w['norm_mix'][l], dx_mid)
    g['norm_mix'] = acc[0]
    return dx, g


def _diag_blocks(m):
    return jnp.stack([m[64 * n:64 * (n + 1), 64 * n:64 * (n + 1)] for n in range(8)])


def _block_diag(w):
    eye = jnp.eye(8, dtype=w.dtype)
    return (w[:, :, None, :] * eye[:, None, :, None]).reshape(512, 512)


_ANY = pl.BlockSpec(memory_space=pl.ANY)
_MESH = pl.DeviceIdType.MESH


def _all_gather(name, xs):
    n = len(xs)

    def body(*refs):
        x_refs, out_refs = refs[:n], refs[n:2 * n]
        send_sems, recv_sems, local_sems = refs[2 * n:]
        x_, y_, c_ = lax.axis_index("x"), lax.axis_index("y"), lax.axis_index("c")
        me, sibling = (x_, y_, c_), (x_, y_, 1 - c_)
        chips = [(1 - x_, y_), (x_, 1 - y_), (1 - x_, 1 - y_)]

        def slot(a, px, py, pc):
            return out_refs[a].at[4 * px + 2 * py + pc]

        def copy(a, k, block, to, src=None):
            return pltpu.make_async_remote_copy(
                src_ref=slot(a, *block) if src is None else src, dst_ref=slot(a, *block),
                send_sem=send_sems.at[a, k], recv_sem=recv_sems.at[a, k], device_id=to, device_id_type=_MESH)

        mine = [pltpu.make_async_copy(x_refs[a], slot(a, *me), local_sems.at[a]) for a in range(n)]
        for cp in mine:
            cp.start()
        first = []
        for a in range(n):
            first.append(copy(a, 0, me, sibling, src=x_refs[a]))
            first += [copy(a, 1 + j, me, (*chip, c_), src=x_refs[a]) for j, chip in enumerate(chips)]
        for cp in first:
            cp.start()
        passed = []
        for j, chip in enumerate(chips):
            for a in range(n):
                copy(a, 1 + j, (*chip, c_), me).wait_recv()
                fwd = copy(a, 4 + j, (*chip, c_), sibling)
                fwd.start()
                passed.append(fwd)
        for a in range(n):
            copy(a, 0, sibling, me).wait_recv()
            for j, chip in enumerate(chips):
                copy(a, 4 + j, (*chip, 1 - c_), me).wait_recv()
        for cp in first + passed:
            cp.wait_send()
        for cp in mine:
            cp.wait()

    return pl.pallas_call(
        body, name=name, out_shape=[jax.ShapeDtypeStruct((N_DEV,) + x.shape, x.dtype) for x in xs],
        in_specs=[_ANY] * n, out_specs=[_ANY] * n,
        scratch_shapes=[pltpu.SemaphoreType.DMA((n, 7)), pltpu.SemaphoreType.DMA((n, 7)),
                        pltpu.SemaphoreType.DMA((n,))],
    )(*xs)


def _all_to_all(name, xs):
    n = len(xs)

    def body(*refs):
        x_refs, out_refs = refs[:n], refs[n:2 * n]
        send_sems, recv_sems, local_sems = refs[2 * n:]
        x_, y_, c_ = lax.axis_index("x"), lax.axis_index("y"), lax.axis_index("c")
        me = 4 * x_ + 2 * y_ + c_

        def peer(k):
            return ((1 - x_) if k & 4 else x_, (1 - y_) if k & 2 else y_, (1 - c_) if k & 1 else c_)

        def copy(a, k):
            px, py, pc = peer(k)
            return pltpu.make_async_remote_copy(
                src_ref=x_refs[a].at[4 * px + 2 * py + pc], dst_ref=out_refs[a].at[me],
                send_sem=send_sems.at[a, k - 1], recv_sem=recv_sems.at[a, k - 1],
                device_id=(px, py, pc), device_id_type=_MESH)

        def arrival(a, k):
            px, py, pc = peer(k)
            return pltpu.make_async_remote_copy(
                src_ref=x_refs[a].at[me], dst_ref=out_refs[a].at[4 * px + 2 * py + pc],
                send_sem=send_sems.at[a, k - 1], recv_sem=recv_sems.at[a, k - 1],
                device_id=(px, py, pc), device_id_type=_MESH)

        mine = [pltpu.make_async_copy(x_refs[a].at[me], out_refs[a].at[me], local_sems.at[a]) for a in range(n)]
        for cp in mine:
            cp.start()
        copies = [copy(a, k) for a in range(n) for k in range(1, N_DEV)]
        for cp in copies:
            cp.start()
        for a in range(n):
            for k in range(1, N_DEV):
                arrival(a, k).wait_recv()
        for cp in copies:
            cp.wait_send()
        for cp in mine:
            cp.wait()

    return pl.pallas_call(
        body, name=name, out_shape=[jax.ShapeDtypeStruct(x.shape, x.dtype) for x in xs],
        in_specs=[_ANY] * n, out_specs=[_ANY] * n,
        scratch_shapes=[pltpu.SemaphoreType.DMA((n, 7)), pltpu.SemaphoreType.DMA((n, 7)),
                        pltpu.SemaphoreType.DMA((n,))],
    )(*xs)


def _window_offset():
    me = 4 * lax.axis_index("x") + 2 * lax.axis_index("y") + lax.axis_index("c")
    return jnp.where(me < 6, me, me + 120)


def _place_w_in(shard):
    def body(i, nt, ins, outs, scr):
        outs[0][...] = pltpu.roll(ins[0][...], _window_offset(), 1).astype(BF16)

    return _rowwise("place_w_in", body, shard.shape[0], 256, [('row', shard)], [('row', IN_WIN, BF16)])[0]


def _adamw(name, w, m, v, g, tr, from_window=False):
    s_parts, r, c = g.shape
    assert r % tr == 0

    def kern(w_ref, m_ref, v_ref, g_ref, go_ref, d_ref, mo_ref, vo_ref):
        gs = g_ref[0].astype(F32)
        for s in range(1, s_parts):
            gs = gs + g_ref[s].astype(F32)
        if from_window:
            gs = pltpu.roll(gs, c - _window_offset(), 1)
        wv = w_ref[...]
        m2 = ADAM_B1 * m_ref[...] + (1.0 - ADAM_B1) * gs
        v2 = ADAM_B2 * v_ref[...] + (1.0 - ADAM_B2) * (gs * gs)
        m_hat = m2 / (1.0 - ADAM_B1 ** ADAM_STEP)
        v_hat = v2 / (1.0 - ADAM_B2 ** ADAM_STEP)
        go_ref[...] = gs
        d_ref[...] = -ADAM_LR * (m_hat / (jnp.sqrt(v_hat) + ADAM_EPS) + ADAM_WD * wv)
        mo_ref[...] = m2
        vo_ref[...] = v2

    spec = pl.BlockSpec((tr, c), lambda i: (i, 0))
    shp = jax.ShapeDtypeStruct((r, c), F32)
    return pl.pallas_call(kern, name=name, grid=(r // tr,),
                          in_specs=[spec, spec, spec, pl.BlockSpec((s_parts, tr, c), lambda i: (0, i, 0))],
                          out_specs=[spec] * 4, out_shape=[shp] * 4, compiler_params=_params(1))(w, m, v, g)


def _pack(arrs, rows, lead=0):
    parts = []
    for a in arrs:
        flat = a.reshape(a.shape[:lead] + (-1,))
        pad = (-flat.shape[-1]) % LANES
        if pad:
            flat = jnp.pad(flat, [(0, 0)] * lead + [(0, pad)])
        parts.append(flat)
    flat = jnp.concatenate(parts, axis=-1)
    pad = rows * LANES - flat.shape[-1]
    assert pad >= 0
    if pad:
        flat = jnp.pad(flat, [(0, 0)] * lead + [(0, pad)])
    return flat.reshape(flat.shape[:lead] + (rows, LANES))


def _unpack(buf, shapes, lead=0):
    flat = buf.reshape(buf.shape[:lead] + (-1,))
    out, off = [], 0
    for shp in shapes:
        n = math.prod(shp)
        out.append(flat[..., off:off + n].reshape(buf.shape[:lead] + tuple(shp)))
        off += n + ((-n) % LANES)
    return out


BIG = ('w_in', 'w_out', 'w_gate', 'w_up', 'w_down')
CONV =('ssd_conv_w', 'lru_conv_w')
CONV_SHARD_SHAPES = ((DEPTH, 4, SSD_CONV // N_DEV), (DEPTH, 4, LRU_W // N_DEV))
CONV_ROWS = 16
SMALL = ('norm_mix', 'ssd_conv_b', 'ssd_dt_bias', 'ssd_a_log', 'ssd_d', 'ssd_norm', 'lru_conv_b', 'lru_wa',
         'lru_ba', 'lru_wx', 'lru_bx', 'lru_lambda', 'norm_ffn', 'norm_final')
SMALL_ROWS = 1280
SMALL_TILE = 256
WEIGHTS = ('norm_mix', 'w_in', 'ssd_conv_w', 'ssd_conv_b', 'ssd_dt_bias', 'ssd_a_log', 'ssd_d', 'ssd_norm',
           'lru_conv_w', 'lru_conv_b', 'lru_wa', 'lru_ba', 'lru_wx', 'lru_bx', 'lru_lambda', 'w_out', 'norm_ffn',
           'w_gate', 'w_up', 'w_down', 'norm_final')


def _join_cols(a):
    return jnp.transpose(a, (1, 2, 0, 3)).reshape(a.shape[1], a.shape[2], -1)


def _join_rows(a):
    return jnp.transpose(a, (1, 0, 2, 3)).reshape(a.shape[1], -1, a.shape[3])


def _split_cols(a):
    l, r, c = a.shape
    return jnp.transpose(a.reshape(l, r, N_DEV, c // N_DEV), (2, 0, 1, 3)).reshape(N_DEV, l * r, c // N_DEV)


def _split_rows(a):
    l, r, c = a.shape
    return jnp.transpose(a.reshape(l, N_DEV, r // N_DEV, c), (1, 0, 2, 3)).reshape(N_DEV, l * r // N_DEV, c)


def _shard_form(k, a):
    if k == 'w_in':
        return jnp.pad(a.reshape(-1, IN_SHARD), ((0, 0), (0, IN_WIN - IN_SHARD)))
    if k in ('w_gate', 'w_up'):
        return jnp.pad(a.reshape(-1, FF_SHARD), ((0, 0), (0, FF_SHARD_P - FF_SHARD)))
    if k == 'w_down':
        return jnp.pad(a, ((0, 0), (0, FF_SHARD_P - FF_SHARD), (0, 0))).reshape(-1, D_MODEL)
    return a.reshape(-1, D_MODEL)


def _shard_back(k, a):
    if k == 'w_in':
        return a[:, :IN_SHARD].reshape(DEPTH, D_MODEL, IN_SHARD)
    if k in ('w_gate', 'w_up'):
        return a[:, :FF_SHARD].reshape(DEPTH, D_MODEL, FF_SHARD)
    if k == 'w_down':
        return a.reshape(DEPTH, FF_SHARD_P, D_MODEL)[:, :FF_SHARD]
    return a.reshape(DEPTH, D_MIX // N_DEV, D_MODEL)


def _dt_tile_place(a):
    zeros = jnp.zeros(a.shape[:-1] + (LANES - N_HEADS,), a.dtype)
    return jnp.concatenate([a[..., :6], zeros, a[..., 6:8]], axis=-1)


def _dt_tile_heads(tile):
    zeros = jnp.zeros(tile.shape[:-1] + (LANES - N_HEADS,), tile.dtype)
    return jnp.concatenate([tile[..., :6], tile[..., 126:128], zeros], axis=-1)


def _layout_from_windows(win):
    r = win.shape[1]
    main = jnp.concatenate([win[j][:, :512] for j in range(N_DEV)] + [jnp.zeros((r, LANES), win.dtype)], axis=1)
    gap = jnp.zeros((r, 384), win.dtype)
    tails = [jnp.zeros((r, 512), win.dtype)]
    for j in range(N_DEV - 1):
        tails += [win[j][:, 512:], gap]
    tails.append(win[N_DEV - 1][:, 512:])
    return main + jnp.concatenate(tails, axis=1)


def _prepare_weights(p, full):
    w = {}
    w_in = full['w_in']
    w_qkv = w_in[:, :, :D_MIX]
    dt_cols = _dt_tile_heads(w_in[:, :, 3072:3200])
    w_rest = jnp.concatenate([w_in[:, :, 2048:3072], w_in[:, :, 1536:2048], w_in[:, :, 3200:3712],
                              w_in[:, :, 3712:4224], dt_cols], axis=2)
    w['w_qkv'], w['w_rest'] = w_qkv, w_rest
    w['w_in_t'] = jnp.transpose(jnp.concatenate([w_qkv, w_rest], axis=2), (0, 2, 1))
    w['w_out'] = full['w_out']
    w['w_out_t'] = jnp.transpose(full['w_out'], (0, 2, 1))
    for k in ('w_gate', 'w_up'):
        w[k] = full[k]
        w[k + '_t'] = jnp.transpose(full[k], (0, 2, 1))
    w['w_down'] = full['w_down']
    w['w_down_t'] = jnp.transpose(full['w_down'], (0, 2, 1))
    for k in ('norm_mix', 'ssd_conv_b', 'ssd_norm', 'lru_conv_b', 'lru_ba', 'lru_bx', 'lru_lambda', 'norm_ffn'):
        w[k] = p[k][:, None, :]
    for k in ('ssd_dt_bias', 'ssd_a_log', 'ssd_d'):
        w[k] = jnp.pad(p[k], ((0, 0), (0, LANES - N_HEADS)))[:, None, :]
    for k in CONV:
        w[k] = jnp.pad(full[k], ((0, 0), (0, 4), (0, 0)))
    for k in ('lru_wa', 'lru_wx'):
        bd = jnp.stack([_block_diag(p[k][l]) for l in range(DEPTH)]).astype(BF16)
        w[k] = bd
        w[k + '_t'] = jnp.transpose(bd, (0, 2, 1))
    return w


def _local_step(x, target, w, norm_final):
    saved = []
    for l in range(DEPTH):
        x, sv = _layer_fwd(x, w, l)
        saved.append(sv)
    dx, loss_acc, dgf = _final_loss(x, norm_final[None, :], target)
    grads = [None] * DEPTH
    for l in reversed(range(DEPTH)):
        dx, grads[l] = _layer_bwd(dx, saved[l], w, l)
    g = {k: jnp.stack([grads[l][k] for l in range(DEPTH)]) for k in grads[0]}
    g['norm_final'] = dgf[0]
    return loss_acc[0, 0], dx, g


def kernel(x, norm_mix, w_in, ssd_conv_w, ssd_conv_b, ssd_dt_bias, ssd_a_log, ssd_d, ssd_norm, lru_conv_w, lru_conv_b, lru_wa, lru_ba, lru_wx, lru_bx, lru_lambda, w_out, norm_ffn, w_gate, w_up, w_down, norm_final, loss_target, m_norm_mix, m_w_in, m_ssd_conv_w, m_ssd_conv_b, m_ssd_dt_bias, m_ssd_a_log, m_ssd_d, m_ssd_norm, m_lru_conv_w, m_lru_conv_b, m_lru_wa, m_lru_ba, m_lru_wx, m_lru_bx, m_lru_lambda, m_w_out, m_norm_ffn, m_w_gate, m_w_up, m_w_down, m_norm_final, v_norm_mix, v_w_in, v_ssd_conv_w, v_ssd_conv_b, v_ssd_dt_bias, v_ssd_a_log, v_ssd_d, v_ssd_norm, v_lru_conv_w, v_lru_conv_b, v_lru_wa, v_lru_ba, v_lru_wx, v_lru_bx, v_lru_lambda, v_w_out, v_norm_ffn, v_w_gate, v_w_up, v_w_down, v_norm_final):
    args = (norm_mix, w_in, ssd_conv_w, ssd_conv_b, ssd_dt_bias, ssd_a_log, ssd_d, ssd_norm, lru_conv_w, lru_conv_b, lru_wa, lru_ba, lru_wx, lru_bx, lru_lambda, w_out, norm_ffn, w_gate, w_up, w_down, norm_final)
    margs = (m_norm_mix, m_w_in, m_ssd_conv_w, m_ssd_conv_b, m_ssd_dt_bias, m_ssd_a_log, m_ssd_d, m_ssd_norm, m_lru_conv_w, m_lru_conv_b, m_lru_wa, m_lru_ba, m_lru_wx, m_lru_bx, m_lru_lambda, m_w_out, m_norm_ffn, m_w_gate, m_w_up, m_w_down, m_norm_final)
    vargs = (v_norm_mix, v_w_in, v_ssd_conv_w, v_ssd_conv_b, v_ssd_dt_bias, v_ssd_a_log, v_ssd_d, v_ssd_norm, v_lru_conv_w, v_lru_conv_b, v_lru_wa, v_lru_ba, v_lru_wx, v_lru_bx, v_lru_lambda, v_w_out, v_norm_ffn, v_w_gate, v_w_up, v_w_down, v_norm_final)
    p = dict(zip(WEIGHTS, args))
    pm = dict(zip(WEIGHTS, margs))
    pv = dict(zip(WEIGHTS, vargs))

    forms = {k: _shard_form(k, p[k]) for k in BIG}
    send = [_place_w_in(forms['w_in'])] + [forms[k].astype(BF16) for k in BIG[1:]]
    got = _all_gather("gather_weights", send + [_pack([p[k] for k in CONV], CONV_ROWS)])
    full = {'w_in': _layout_from_windows(got[0]).reshape(DEPTH, D_MODEL, IN_COLS_P),
            'w_out': _join_rows(got[1].reshape(N_DEV, DEPTH, D_MIX // N_DEV, D_MODEL)),
            'w_gate': _join_cols(got[2].reshape(N_DEV, DEPTH, D_MODEL, FF_SHARD_P)),
            'w_up': _join_cols(got[3].reshape(N_DEV, DEPTH, D_MODEL, FF_SHARD_P)),
            'w_down': _join_rows(got[4].reshape(N_DEV, DEPTH, FF_SHARD_P, D_MODEL))}
    for k, a in zip(CONV, _unpack(got[5], CONV_SHARD_SHAPES, lead=1)):
        full[k] = _join_cols(a)
    w = _prepare_weights(p, full)

    loss_local, dx, g = _local_step(x[0], loss_target[0], w, norm_final)
    loss = lax.psum(loss_local, ("x", "y", "c"))

    small_g = _all_gather("gather_small_grads", [_pack([g[k] for k in SMALL + CONV], SMALL_ROWS)])[0]
    zeros = [jnp.zeros_like(g[k]) for k in CONV]
    res_small = _adamw("adamw_small", _pack([p[k] for k in SMALL] + zeros, SMALL_ROWS),
                       _pack([pm[k] for k in SMALL] + zeros, SMALL_ROWS),
                       _pack([pv[k] for k in SMALL] + zeros, SMALL_ROWS), small_g, SMALL_TILE)
    small_shapes = [g[k].shape for k in SMALL + CONV]
    out = {kind: {} for kind in range(4)}
    for kind in range(4):
        for k, a in zip(SMALL + CONV, _unpack(res_small[kind], small_shapes)):
            out[kind][k] = a
    me = 4 * lax.axis_index("x") + 2 * lax.axis_index("y") + lax.axis_index("c")
    conv_g = []
    for k, shp in zip(CONV, CONV_SHARD_SHAPES):
        conv_g.append(lax.dynamic_slice_in_dim(out[0][k], me * shp[2], shp[2], axis=2))
    res_conv = _adamw("adamw_conv", _pack([p[k] for k in CONV], CONV_ROWS), _pack([pm[k] for k in CONV], CONV_ROWS),
                      _pack([pv[k] for k in CONV], CONV_ROWS), _pack(conv_g, CONV_ROWS)[None], CONV_ROWS)
    for kind in range(4):
        for k, a in zip(CONV, _unpack(res_conv[kind], CONV_SHARD_SHAPES)):
            out[kind][k] = a

    g_in = g['w_in'].reshape(DEPTH * D_MODEL, IN_COLS_P)
    dest = [jnp.stack([g_in[:, 512 * j:512 * j + IN_WIN] for j in range(N_DEV)]), _split_rows(g['w_out']),
            _split_cols(g['w_gate']), _split_cols(g['w_up']), _split_rows(g['w_down'])]
    parts = _all_to_all("exchange_big_grads", [a.astype(BF16) for a in dest])
    tiles = {'w_in': 256, 'w_out': 128, 'w_gate': 512, 'w_up': 512, 'w_down': 256}
    for k, part in zip(BIG, parts):
        res = _adamw("adamw_" + k, forms[k], _shard_form(k, pm[k]), _shard_form(k, pv[k]), part, tiles[k],
                     from_window=(k == 'w_in'))
        for kind in range(4):
            out[kind][k] = _shard_back(k, res[kind])

    outs = [loss, dx[None]]
    for kind in range(4):
        outs += [out[kind][k] for k in WEIGHTS]
    return tuple(outs)
```

```python
import functools
import math

import jax
import jax.numpy as jnp
from jax import lax
from jax.experimental import pallas as pl
from jax.experimental.pallas import tpu as pltpu

F32 = jnp.float32
BF16 = jnp.bfloat16

N_DEV = 8
DEPTH = 2
D_MODEL = 1024
ATT_W = 512
HEAD_DIM = 64
N_HEADS = 8
ATT_BLOCK = 128
ATT_DILATIONS = (16, 4, 1)
SSD_W = 512
SSD_STATE = 128
SSD_CONV = 1024
SSD_CHUNK = 128
LRU_W = 512
LRU_C = 8.0
D_MIX = 1536
D_FF = 2816
FF_SHARD = D_FF // N_DEV
FF_SHARD_P = 384
D_FFP = N_DEV * FF_SHARD_P
IN_COLS = 4104
IN_SHARD = IN_COLS // N_DEV
IN_WIN = 640
IN_COLS_P = 4224
REST_COLS = 2688
NORM_EPS = 1e-6
SSD_NORM_EPS = 1e-5
NEG = -1e30

ADAM_LR = 0.001
ADAM_B1 = 0.9
ADAM_B2 = 0.999
ADAM_EPS = 1e-08
ADAM_WD = 0.01
ADAM_STEP = 10

LANES = 128
VMEM_LIMIT = 52 * 1024 * 1024
HI = lax.Precision.HIGHEST


def _sigmoid(x):
    return 1.0 / (1.0 + jnp.exp(-x))


def _silu(x):
    return x * _sigmoid(x)


def _dsilu(x):
    s = _sigmoid(x)
    return s * (1.0 + x * (1.0 - s))


def _softplus(x):
    return jnp.maximum(x, 0.0) + jnp.log(1.0 + jnp.exp(-jnp.abs(x)))


_GELU_C = math.sqrt(2.0 / math.pi)


def _gelu(x):
    return 0.5 * x * (1.0 + jnp.tanh(_GELU_C * (x + 0.044715 * x * x * x)))


def _dgelu(x):
    t = jnp.tanh(_GELU_C * (x + 0.044715 * x * x * x))
    return 0.5 * (1.0 + t) + 0.5 * x * (1.0 - t * t) * _GELU_C * (1.0 + 3.0 * 0.044715 * x * x)


def _dot(a, b):
    return jnp.dot(a, b, preferred_element_type=F32)


def _dot_nt(a, b):
    return lax.dot_general(a, b, (((1,), (1,)), ((), ())), preferred_element_type=F32)


def _dot_tn(a, b):
    return lax.dot_general(a, b, (((0,), (0,)), ((), ())), preferred_element_type=F32)


def _dot_hi(a, b):
    return jnp.dot(a, b, preferred_element_type=F32, precision=HI)


def _iota(shape, axis):
    return lax.broadcasted_iota(jnp.int32, shape, axis)


def _shift_down(x, s, prev8):
    xs = pltpu.roll(x, s, 0)
    ps = pltpu.roll(prev8, s, 0)
    top = jnp.concatenate([ps, x[8:]], axis=0)
    return jnp.where(_iota(x.shape, 0) < s, top, xs)


def _shift_up(x, s, next8):
    tm = x.shape[0]
    xs = pltpu.roll(x, tm - s, 0)
    ns = pltpu.roll(next8, 8 - s, 0)
    bottom = jnp.concatenate([x[:tm - 8], ns], axis=0)
    return jnp.where(_iota(x.shape, 0) >= tm - s, bottom, xs)


def _expand_mat():
    return jnp.where(_iota((LANES, SSD_W), 1) // HEAD_DIM == _iota((LANES, SSD_W), 0), 1.0, 0.0).astype(F32)


def _reduce_mat():
    return jnp.where(_iota((SSD_W, LANES), 0) // HEAD_DIM == _iota((SSD_W, LANES), 1), 1.0, 0.0).astype(F32)


def _params(n_grid):
    return pltpu.CompilerParams(dimension_semantics=("arbitrary",) * n_grid, vmem_limit_bytes=VMEM_LIMIT)


def _rowwise(name, body, n_rows, tm, ins, outs, scratch=(), reverse=False):
    nt = n_rows // tm
    assert nt * tm == n_rows and tm % 8 == 0
    r8 = tm // 8
    last8 = n_rows // 8 - 1

    def pos(s):
        return (nt - 1 - s) if reverse else s

    in_specs, args = [], []
    for spec in ins:
        kind, arr = spec[0], spec[1]
        args.append(arr)
        if kind == 'row':
            in_specs.append(pl.BlockSpec((tm, arr.shape[1]), lambda s: (pos(s), 0)))
        elif kind == 'col':
            in_specs.append(pl.BlockSpec((tm, spec[2]), functools.partial(lambda s, j: (pos(s), j), j=spec[3])))
        elif kind == 'full':
            in_specs.append(pl.BlockSpec(arr.shape, functools.partial(lambda s, n: (0,) * n, n=arr.ndim)))
        elif kind == 'prev8':
            in_specs.append(pl.BlockSpec((8, spec[2]), functools.partial(
                lambda s, j: (jnp.maximum(pos(s) * r8 - 1, 0), j), j=spec[3])))
        elif kind == 'next8':
            in_specs.append(pl.BlockSpec((8, spec[2]), functools.partial(
                lambda s, j: (jnp.minimum((pos(s) + 1) * r8, last8), j), j=spec[3])))
        else:
            raise ValueError(kind)
    out_specs, out_shape, acc_idx = [], [], []
    for k, spec in enumerate(outs):
        if spec[0] == 'row':
            out_specs.append(pl.BlockSpec((tm, spec[1]), lambda s: (pos(s), 0)))
            out_shape.append(jax.ShapeDtypeStruct((n_rows, spec[1]), spec[2]))
        else:
            out_specs.append(pl.BlockSpec(spec[1], lambda s: (0, 0)))
            out_shape.append(jax.ShapeDtypeStruct(spec[1], spec[2]))
            acc_idx.append(k)
    n_in, n_out = len(ins), len(outs)

    def kern(*refs):
        s = pl.program_id(0)
        in_refs, out_refs, scr = refs[:n_in], refs[n_in:n_in + n_out], refs[n_in + n_out:]

        @pl.when(s == 0)
        def _():
            for k in acc_idx:
                out_refs[k][...] = jnp.zeros(out_refs[k].shape, out_refs[k].dtype)

        body(pos(s), nt, in_refs, out_refs, scr)

    res = pl.pallas_call(kern, name=name, grid=(nt,), in_specs=in_specs, out_specs=out_specs,
                         out_shape=out_shape, scratch_shapes=list(scratch), compiler_params=_params(1))(*args)
    return res


def _mm(name, a_list, b, *, res=None, out_dtype=F32, tm=512, tn=None):
    n_rows = a_list[0].shape[0]
    b_list = list(b) if isinstance(b, (list, tuple)) else [b]
    n = b_list[0].shape[1]
    ks = [a.shape[1] for a in a_list]
    assert sum(ks) == sum(m.shape[0] for m in b_list)
    assert len(b_list) == 1 or [m.shape[0] for m in b_list] == ks
    tn = n if tn is None else tn
    assert n_rows % tm == 0 and n % tn == 0
    na, nbm = len(a_list), len(b_list)

    def kern(*refs):
        a_refs, b_refs, o_ref = refs[:na], refs[na:na + nbm], refs[-1]
        acc, off = None, 0
        for idx, (a_ref, kp) in enumerate(zip(a_refs, ks)):
            rhs = b_refs[idx][...] if nbm > 1 else b_refs[0][off:off + kp, :]
            part = _dot(a_ref[...].astype(BF16), rhs)
            acc = part if acc is None else acc + part
            off += kp
        if res is not None:
            acc = acc + refs[na + nbm][...]
        o_ref[...] = acc.astype(out_dtype)

    in_specs = [pl.BlockSpec((tm, kp), lambda i, j: (i, 0)) for kp in ks]
    in_specs += [pl.BlockSpec((m.shape[0], tn), lambda i, j: (0, j)) for m in b_list]
    args = list(a_list) + b_list
    if res is not None:
        in_specs.append(pl.BlockSpec((tm, tn), lambda i, j: (i, j)))
        args.append(res)
    return pl.pallas_call(kern, name=name, grid=(n_rows // tm, n // tn), in_specs=in_specs,
                          out_specs=pl.BlockSpec((tm, tn), lambda i, j: (i, j)),
                          out_shape=jax.ShapeDtypeStruct((n_rows, n), out_dtype),
                          compiler_params=_params(2))(*args)


def _mm_tn(name, a, g, *, a_col=None, g_col=None, tk=None, tn=None, tt=1024):
    n_rows = a.shape[0]
    k = a.shape[1] if a_col is None else a_col[0]
    a_j = 0 if a_col is None else a_col[1]
    n = g.shape[1] if g_col is None else g_col[0]
    g_j = 0 if g_col is None else g_col[1]
    tk = k if tk is None else tk
    tn = n if tn is None else tn
    assert k % tk == 0 and n % tn == 0 and n_rows % tt == 0
    kb = k // tk
    nbk = n // tn

    def kern(a_ref, g_ref, o_ref):
        t = pl.program_id(2)

        @pl.when(t == 0)
        def _():
            o_ref[...] = jnp.zeros(o_ref.shape, F32)

        o_ref[...] += _dot_tn(a_ref[...].astype(BF16), g_ref[...].astype(BF16))

    return pl.pallas_call(
        kern, name=name, grid=(kb, n // tn, n_rows // tt),
        in_specs=[pl.BlockSpec((tt, tk), lambda i, j, t: (t, a_j * kb + i)),
                  pl.BlockSpec((tt, tn), lambda i, j, t: (t, g_j * nbk + j))],
        out_specs=pl.BlockSpec((tk, tn), lambda i, j, t: (i, j)),
        out_shape=jax.ShapeDtypeStruct((k, n), F32), compiler_params=_params(3))(a, g)


def _rmsnorm_fwd(name, x, g):
    def body(i, nt, ins, outs, scr):
        xv = ins[0][...]
        rstd = lax.rsqrt(jnp.mean(xv * xv, axis=-1, keepdims=True) + NORM_EPS)
        outs[0][...] = (xv * rstd * ins[1][...]).astype(BF16)

    return _rowwise(name, body, x.shape[0], 512, [('row', x), ('full', g)], [('row', x.shape[1], BF16)])[0]


def _rmsnorm_bwd(name, dh, x, g, dres):
    d = x.shape[1]

    def body(i, nt, ins, outs, scr):
        dy, xv, gv, dr = ins[0][...], ins[1][...], ins[2][...], ins[3][...]
        rstd = lax.rsqrt(jnp.mean(xv * xv, axis=-1, keepdims=True) + NORM_EPS)
        xhat = xv * rstd
        outs[1][0:1, :] += jnp.sum(dy * xhat, axis=0, keepdims=True)
        dxh = dy * gv
        outs[0][...] = dr + rstd * (dxh - xhat * jnp.mean(dxh * xhat, axis=-1, keepdims=True))

    return _rowwise(name, body, x.shape[0], 512, [('row', dh), ('row', x), ('full', g), ('row', dres)],
                    [('row', d, F32), ('acc', (8, d), F32)])


def _final_loss(x, g, target):
    d = x.shape[1]

    def body(i, nt, ins, outs, scr):
        xv, gv, tv = ins[0][...], ins[1][...], ins[2][...]
        rstd = lax.rsqrt(jnp.mean(xv * xv, axis=-1, keepdims=True) + NORM_EPS)
        xhat = xv * rstd
        err = xhat * gv - tv
        row_loss = 0.5 * jnp.mean(err * err, axis=-1, keepdims=True)
        outs[1][...] += jnp.sum(row_loss, axis=0, keepdims=True)
        dy = err * (1.0 / d)
        outs[2][0:1, :] += jnp.sum(dy * xhat, axis=0, keepdims=True)
        dxh = dy * gv
        outs[0][...] = rstd * (dxh - xhat * jnp.mean(dxh * xhat, axis=-1, keepdims=True))

    return _rowwise("final_loss", body, x.shape[0], 512, [('row', x), ('full', g), ('row', target)],
                    [('row', d, F32), ('acc', (8, LANES), F32), ('acc', (8, d), F32)])


def _conv_fwd(name, src, width, idx, w, b):
    def body(i, nt, ins, outs, scr):
        xv = ins[0][...]
        prev = jnp.where(i > 0, ins[1][...], 0.0)
        wv = ins[2][...]
        y = ins[3][...] + wv[3:4, :] * xv
        for s in (1, 2, 3):
            y = y + wv[3 - s:4 - s, :] * _shift_down(xv, s, prev)
        outs[0][...] = y

    return _rowwise(name, body, src.shape[0], 512,
                    [('col', src, width, idx), ('prev8', src, width, idx), ('full', w), ('full', b)],
                    [('row', width, F32)])[0]


def _conv_bwd(name, dpre, src, width, idx, w):
    def body(i, nt, ins, outs, scr):
        dy = ins[0][...]
        nxt = jnp.where(i < nt - 1, ins[1][...], 0.0)
        xv = ins[2][...]
        prev = jnp.where(i > 0, ins[3][...], 0.0)
        wv = ins[4][...]
        dx = wv[3:4, :] * dy
        outs[1][3:4, :] += jnp.sum(dy * xv, axis=0, keepdims=True)
        outs[1][4:5, :] += jnp.sum(dy, axis=0, keepdims=True)
        for s in (1, 2, 3):
            dx = dx + wv[3 - s:4 - s, :] * _shift_up(dy, s, nxt)
            outs[1][3 - s:4 - s, :] += jnp.sum(dy * _shift_down(xv, s, prev), axis=0, keepdims=True)
        outs[0][...] = dx

    return _rowwise(name, body, src.shape[0], 512,
                    [('row', dpre), ('next8', dpre, width, 0), ('col', src, width, idx),
                     ('prev8', src, width, idx), ('full', w)],
                    [('row', width, F32), ('acc', (8, width), F32)])


ATT_STEP_BLOCKS = 4


def _att_bias(not_first, dil, head):
    qi = _iota((ATT_BLOCK, 2 * ATT_BLOCK), 0)
    ki = _iota((ATT_BLOCK, 2 * ATT_BLOCK), 1)
    dist = ATT_BLOCK + qi - ki
    valid = (dist >= 0) & (dist <= ATT_BLOCK) & (not_first | (ki >= ATT_BLOCK))
    slope = 2.0 ** (-(head + 1))
    return jnp.where(valid, (-slope * dil) * dist.astype(F32), NEG)


def _head_mask():
    lane = _iota((ATT_BLOCK, LANES), 1)
    return lane < HEAD_DIM


def _att_q_specs(dil, nb, bq):
    big = (bq * ATT_BLOCK, ATT_W)
    one = (ATT_BLOCK, ATT_W)
    specs = [pl.BlockSpec(big, lambda r, n: (n, 3 * r)),
             pl.BlockSpec(big, lambda r, n: (n, 3 * r + 1)),
             pl.BlockSpec(one, lambda r, n: (jnp.maximum(n * bq - 1, 0), 3 * r + 1)),
             pl.BlockSpec(big, lambda r, n: (n, 3 * r + 2)),
             pl.BlockSpec(one, lambda r, n: (jnp.maximum(n * bq - 1, 0), 3 * r + 2))]
    wide = pl.BlockSpec(big, lambda r, n: (n, r))
    stat = pl.BlockSpec((bq * ATT_BLOCK, LANES), lambda r, n: (n, r))
    return specs, wide, stat


def _att_fwd(dil, qkv_v):
    n_l = qkv_v.shape[0]
    nb = n_l // ATT_BLOCK
    bq = min(ATT_STEP_BLOCKS, nb)
    scale = HEAD_DIM ** -0.5

    def kern(q_ref, kc_ref, kp_ref, vc_ref, vp_ref, m_out, l_out, a_out):
        n = pl.program_id(1)
        low = _head_mask()
        lane = _iota((ATT_BLOCK, LANES), 1)
        for b in range(bq):
            rows = slice(ATT_BLOCK * b, ATT_BLOCK * (b + 1))
            prev = slice(ATT_BLOCK * (b - 1), ATT_BLOCK * b)
            not_first = (n * bq + b) > 0
            m_acc = jnp.zeros((ATT_BLOCK, LANES), F32)
            l_acc = jnp.zeros((ATT_BLOCK, LANES), F32)
            for p in range(N_HEADS // 2):
                sl = slice(LANES * p, LANES * (p + 1))
                q2 = q_ref[rows, sl].astype(F32)
                k_prev = kp_ref[:, sl] if b == 0 else kc_ref[prev, sl]
                v_prev = vp_ref[:, sl] if b == 0 else vc_ref[prev, sl]
                k2 = jnp.concatenate([k_prev, kc_ref[rows, sl]], axis=0).astype(BF16)
                v2 = jnp.concatenate([v_prev, vc_ref[rows, sl]], axis=0).astype(BF16)
                res = []
                for e in range(2):
                    h = 2 * p + e
                    keep = low if e == 0 else jnp.logical_not(low)
                    qe = jnp.where(keep, q2, 0.0).astype(BF16)
                    s = _dot_nt(qe, k2) * scale + _att_bias(not_first, dil, h)
                    m_new = jnp.max(s, axis=-1, keepdims=True)
                    pe = jnp.exp(s - m_new)
                    l_new = jnp.sum(pe, axis=-1, keepdims=True)
                    m_acc = jnp.where(lane == h, m_new, m_acc)
                    l_acc = jnp.where(lane == h, l_new, l_acc)
                    res.append(_dot(pe.astype(BF16), v2))
                a_out[rows, sl] = jnp.where(low, res[0], res[1])
            m_out[rows, :] = m_acc
            l_out[rows, :] = l_acc

    specs, wide, stat = _att_q_specs(dil, nb, bq)
    shp_s = jax.ShapeDtypeStruct((n_l, dil * LANES), F32)
    shp_a = jax.ShapeDtypeStruct((n_l, dil * ATT_W), F32)
    return pl.pallas_call(kern, name="att_fwd_d%d" % dil, grid=(dil, nb // bq), in_specs=specs,
                          out_specs=[stat, stat, wide], out_shape=[shp_s, shp_s, shp_a],
                          compiler_params=_params(2))(*([qkv_v] * 5))


def _att_merge(parts):
    n_pat = len(parts)

    def body(i, nt, ins, outs, scr):
        ms = [ins[3 * g][...] for g in range(n_pat)]
        m_all = functools.reduce(jnp.maximum, ms)
        expand = _expand_mat()
        num, den = None, None
        for g in range(n_pat):
            e = jnp.exp(ms[g] - m_all)
            d_g = ins[3 * g + 1][...] * e
            n_g = ins[3 * g + 2][...] * _dot_hi(e, expand)
            num = n_g if num is None else num + n_g
            den = d_g if den is None else den + d_g
        real = _iota(den.shape, 1) < N_HEADS
        outs[0][...] = num / _dot_hi(den, expand)
        outs[1][...] = jnp.where(real, m_all + jnp.log(jnp.where(real, den, 1.0)), 0.0)

    ins = []
    for m, l, acc in parts:
        ins += [('row', m), ('row', l), ('row', acc)]
    return _rowwise("att_merge", body, parts[0][0].shape[0], 512, ins,
                    [('row', ATT_W, F32), ('row', LANES, F32)])


def _att_delta(d_att, out):
    def body(i, nt, ins, outs, scr):
        outs[0][...] = _dot_hi(ins[0][...] * ins[1][...], _reduce_mat())

    return _rowwise("att_delta", body, out.shape[0], 512, [('row', d_att), ('row', out)],
                    [('row', LANES, F32)])[0]


def _att_bwd(dil, qkv_v, do_v, lse_v, delta_v, dkv_in):
    n_l = qkv_v.shape[0]
    nb = n_l // ATT_BLOCK
    bq = min(ATT_STEP_BLOCKS, nb)
    steps = nb // bq
    first = dkv_in is None
    scale = HEAD_DIM ** -0.5

    def kern(*refs):
        j = pl.program_id(1)
        k_ref, v_ref, qc_ref, qn_ref, doc_ref, don_ref, lc_ref, ln_ref, dc_ref, dn_ref = refs[:10]
        dk_out, dv_out, dq_out, carry = refs[-4:]
        low = _head_mask()
        row = _iota((2 * ATT_BLOCK, ATT_BLOCK), 0)
        key = _iota((2 * ATT_BLOCK, ATT_BLOCK), 1)
        dist = row - key
        low2 = _iota((2 * ATT_BLOCK, LANES), 1) < HEAD_DIM

        @pl.when(j == 0)
        def _():
            carry[...] = jnp.zeros(carry.shape, F32)

        dq_prev = [carry[:, LANES * p:LANES * (p + 1)] for p in range(N_HEADS // 2)]
        for b in range(bq):
            rows = slice(ATT_BLOCK * b, ATT_BLOCK * (b + 1))
            nrows = slice(ATT_BLOCK * (b + 1), ATT_BLOCK * (b + 2))
            inner = b < bq - 1
            has_next = True if inner else (j < steps - 1)
            valid = (dist >= 0) & (dist <= ATT_BLOCK) & ((row < ATT_BLOCK) | has_next)
            lse2 = jnp.concatenate([lc_ref[rows, :], lc_ref[nrows, :] if inner else ln_ref[...]], axis=0)
            dl2 = jnp.concatenate([dc_ref[rows, :], dc_ref[nrows, :] if inner else dn_ref[...]], axis=0)
            for p in range(N_HEADS // 2):
                sl = slice(LANES * p, LANES * (p + 1))
                k2 = k_ref[rows, sl].astype(F32)
                v2 = v_ref[rows, sl].astype(F32)
                q2 = jnp.concatenate([qc_ref[rows, sl], qc_ref[nrows, sl] if inner else qn_ref[:, sl]],
                                     axis=0).astype(F32)
                do2 = jnp.concatenate([doc_ref[rows, sl], doc_ref[nrows, sl] if inner else don_ref[:, sl]],
                                      axis=0).astype(F32)
                q2b = q2.astype(BF16)
                do2b = do2.astype(BF16)
                dks, dvs, dq2 = [], [], None
                for e in range(2):
                    h = 2 * p + e
                    keep = low if e == 0 else jnp.logical_not(low)
                    keep2 = low2 if e == 0 else jnp.logical_not(low2)
                    slope = 2.0 ** (-(h + 1))
                    bias = jnp.where(valid, (-slope * dil) * dist.astype(F32), NEG)
                    ke = jnp.where(keep, k2, 0.0).astype(BF16)
                    ve = jnp.where(keep, v2, 0.0).astype(BF16)
                    s = _dot_nt(q2b, ke) * scale + bias
                    pe = jnp.exp(s - lse2[:, h:h + 1])
                    dp = _dot_nt(do2b, ve)
                    dsb = (pe * (dp - dl2[:, h:h + 1])).astype(BF16)
                    dvs.append(_dot_tn(pe.astype(BF16), jnp.where(keep2, do2, 0.0).astype(BF16)))
                    dks.append(_dot_tn(dsb, jnp.where(keep2, q2, 0.0).astype(BF16)) * scale)
                    dqe = _dot(dsb, ke)
                    dq2 = dqe if dq2 is None else dq2 + dqe
                dk = jnp.where(low, dks[0], dks[1])
                dv = jnp.where(low, dvs[0], dvs[1])
                dq = dq_prev[p] + dq2[:ATT_BLOCK] * scale
                dq_prev[p] = dq2[ATT_BLOCK:] * scale
                if not first:
                    dk = dk + refs[10][rows, sl].astype(F32)
                    dv = dv + refs[11][rows, sl].astype(F32)
                    dq = dq + refs[12][rows, sl].astype(F32)
                dk_out[rows, sl] = dk.astype(BF16)
                dv_out[rows, sl] = dv.astype(BF16)
                dq_out[rows, sl] = dq.astype(BF16)
        for p in range(N_HEADS // 2):
            carry[:, LANES * p:LANES * (p + 1)] = dq_prev[p]

    big = (bq * ATT_BLOCK, ATT_W)
    one = (ATT_BLOCK, ATT_W)

    def nxt_idx(j):
        return jnp.minimum((j + 1) * bq, nb - 1)

    cur = pl.BlockSpec(big, lambda r, j: (j, r))
    nxt = pl.BlockSpec(one, lambda r, j: (nxt_idx(j), r))
    cur_s = pl.BlockSpec((bq * ATT_BLOCK, LANES), lambda r, j: (j, r))
    nxt_s = pl.BlockSpec((ATT_BLOCK, LANES), lambda r, j: (nxt_idx(j), r))
    in_specs = [pl.BlockSpec(big, lambda r, j: (j, 3 * r + 1)),
                pl.BlockSpec(big, lambda r, j: (j, 3 * r + 2)),
                pl.BlockSpec(big, lambda r, j: (j, 3 * r)),
                pl.BlockSpec(one, lambda r, j: (nxt_idx(j), 3 * r)),
                cur, nxt, cur_s, nxt_s, cur_s, nxt_s]
    args = [qkv_v] * 4 + [do_v, do_v, lse_v, lse_v, delta_v, delta_v]
    if not first:
        in_specs += [cur, cur, cur]
        args += list(dkv_in)
    shp = jax.ShapeDtypeStruct((n_l, dil * ATT_W), BF16)
    return pl.pallas_call(kern, name="att_bwd_d%d" % dil, grid=(dil, steps), in_specs=in_specs,
                          out_specs=[cur, cur, cur], out_shape=[shp, shp, shp],
                          scratch_shapes=[pltpu.VMEM((ATT_BLOCK, ATT_W), F32)], compiler_params=_params(2))(*args)


def _attention_fwd(qkv):
    t = qkv.shape[0]
    parts = [[s.reshape(t, -1) for s in _att_fwd(dil, qkv.reshape(t // dil, dil * D_MIX))]
             for dil in ATT_DILATIONS]
    return _att_merge(parts)


def _attention_bwd(qkv, d_att, out, lse):
    t = qkv.shape[0]
    delta = _att_delta(d_att, out)
    grads = None
    for dil in ATT_DILATIONS:
        view = lambda a: a.reshape(t // dil, -1)
        grads = _att_bwd(dil, view(qkv), view(d_att), view(lse), view(delta),
                         None if grads is None else [view(a) for a in grads])
    dk, dv, dq = [a.reshape(t, ATT_W) for a in grads]
    return dq, dk, dv


def _ssd_chunk_common(pre, dtraw, bias_row, alog_row):
    q = SSD_CHUNK
    act = _silu(pre)
    lane = _iota((q, LANES), 1)
    dt = jnp.where(lane < N_HEADS, _softplus(dtraw + bias_row), 0.0)
    a_row = -jnp.exp(alog_row)
    tril = jnp.where(_iota((q, q), 0) >= _iota((q, q), 1), 1.0, 0.0).astype(F32)
    cs = _dot_hi(tril, dt * a_row)
    cs_last = cs[q - 1:q, :]
    return act, dt, a_row, tril, cs, cs_last


def _ssd_lmat(cs, cs_t, h):
    q = SSD_CHUNK
    seg = cs[:, h:h + 1] - cs_t[h:h + 1, :]
    causal = _iota((q, q), 0) >= _iota((q, q), 1)
    return jnp.exp(jnp.where(causal, seg, NEG))


def _ssd_gate_norm(y, z, norm_w):
    sz = _silu(z)
    yg = y * sz
    half = SSD_W // 2
    outs, rss = [], []
    for g in range(2):
        part = yg[:, half * g:half * (g + 1)]
        rs = lax.rsqrt(jnp.mean(part * part, axis=-1, keepdims=True) + SSD_NORM_EPS)
        outs.append(part * rs)
        rss.append(rs)
    yn = jnp.concatenate(outs, axis=1)
    return sz, yn, rss, yn * norm_w


def _ssd_fwd(pre, rest, dt_bias, a_log, d_skip, norm_w):
    t = pre.shape[0]
    q = SSD_CHUNK

    def body(c, nc, ins, outs, scr):
        pre_ref, z_ref, dtr_ref, bias_ref, alog_ref, dsk_ref, nw_ref = ins
        out_ref, y_ref, sp_ref = outs
        s_ref = scr[0]

        @pl.when(c == 0)
        def _():
            s_ref[...] = jnp.zeros(s_ref.shape, F32)

        act, dt, a_row, tril, cs, cs_last = _ssd_chunk_common(pre_ref[...], dtr_ref[...], bias_ref[...],
                                                               alog_ref[...])
        x = act[:, :SSD_W]
        cs_t = cs.T
        e_col = jnp.exp(cs)
        w = jnp.exp(cs_last - cs) * dt
        expand = _expand_mat()
        w_x = _dot_hi(w, expand)
        dt_x = _dot_hi(dt, expand)
        e_x = _dot_hi(e_col, expand)
        d_x = _dot_hi(dsk_ref[...], expand)
        cd_x = _dot_hi(jnp.exp(cs_last), expand)
        s_prev = s_ref[...]
        sp_ref[...] = s_prev
        xw = (x * w_x).astype(BF16)
        xd = (x * dt_x).astype(BF16)
        low = _head_mask()
        y_parts, s_parts = [], []
        for g in range(2):
            bg = act[:, SSD_W + SSD_STATE * g:SSD_W + SSD_STATE * (g + 1)].astype(BF16)
            cg = act[:, SSD_W + 2 * SSD_STATE + SSD_STATE * g:SSD_W + 2 * SSD_STATE + SSD_STATE * (g + 1)].astype(BF16)
            gsl = slice(256 * g, 256 * (g + 1))
            gmat = _dot_nt(cg, bg)
            s_parts.append(_dot_tn(bg, xw[:, gsl]))
            y0 = _dot(cg, s_prev[:, gsl].astype(BF16))
            for pp in range(2):
                pair = 2 * g + pp
                psl = slice(LANES * pair, LANES * (pair + 1))
                yd = []
                for e in range(2):
                    h = 2 * pair + e
                    mh = (gmat * _ssd_lmat(cs, cs_t, h)).astype(BF16)
                    yd.append(_dot(mh, xd[:, psl]))
                y_parts.append(jnp.where(low, yd[0], yd[1]) + e_x[:, psl] * y0[:, LANES * pp:LANES * (pp + 1)])
        y = jnp.concatenate(y_parts, axis=1) + d_x * x
        s_ref[...] = cd_x * s_prev + jnp.concatenate(s_parts, axis=1)
        y_ref[...] = y
        out_ref[...] = _ssd_gate_norm(y, z_ref[...], nw_ref[...])[3]

    return _rowwise("ssd_fwd", body, t, q,
                    [('row', pre), ('col', rest, SSD_W, 2), ('col', rest, LANES, 20), ('full', dt_bias),
                     ('full', a_log), ('full', d_skip), ('full', norm_w)],
                    [('row', SSD_W, F32), ('row', SSD_W, F32), ('row', SSD_W, F32)],
                    scratch=[pltpu.VMEM((SSD_STATE, SSD_W), F32)])


def _ssd_bwd(pre, rest, y, s_prev_all, d_mix, dt_bias, a_log, d_skip, norm_w):
    t = pre.shape[0]
    q = SSD_CHUNK

    def body(c, nc, ins, outs, scr):
        pre_ref, z_ref, dtr_ref, y_ref, sp_ref, do_ref, bias_ref, alog_ref, dsk_ref, nw_ref = ins
        dpre_ref, dz_ref, ddt_ref, a128_ref, a512_ref = outs
        ds_ref = scr[0]

        @pl.when(c == nc - 1)
        def _():
            ds_ref[...] = jnp.zeros(ds_ref.shape, F32)

        pre_v = pre_ref[...]
        dtr = dtr_ref[...]
        act, dt, a_row, tril, cs, cs_last = _ssd_chunk_common(pre_v, dtr, bias_ref[...], alog_ref[...])
        x = act[:, :SSD_W]
        cs_t = cs.T
        e_col = jnp.exp(cs)
        decay_end = jnp.exp(cs_last - cs)
        w = decay_end * dt
        cd = jnp.exp(cs_last)
        expand = _expand_mat()
        reduce = _reduce_mat()
        w_x = _dot_hi(w, expand)
        dt_x = _dot_hi(dt, expand)
        e_x = _dot_hi(e_col, expand)
        d_x = _dot_hi(dsk_ref[...], expand)
        cd_x = _dot_hi(cd, expand)
        s_prev = sp_ref[...]
        d_s = ds_ref[...]
        xw = (x * w_x).astype(BF16)
        xd = (x * dt_x).astype(BF16)
        low = _head_mask()
        lane = _iota((q, LANES), 1)
        sub = _iota((q, LANES), 0)

        yv, zv, nw = y_ref[...], z_ref[...], nw_ref[...]
        d_out = do_ref[...]
        sz, yn, rss, _ = _ssd_gate_norm(yv, zv, nw)
        a512_ref[0:1, :] += jnp.sum(d_out * yn, axis=0, keepdims=True)
        dyn = d_out * nw
        half = SSD_W // 2
        dyg_parts = []
        for g in range(2):
            hs = slice(half * g, half * (g + 1))
            dyg_parts.append(rss[g] * (dyn[:, hs] - yn[:, hs] * jnp.mean(dyn[:, hs] * yn[:, hs], axis=-1,
                                                                          keepdims=True)))
        dyg = jnp.concatenate(dyg_parts, axis=1)
        dy = dyg * sz
        dz_ref[...] = dyg * yv * _dsilu(zv)

        a128_ref[2:3, :] += _dot_hi(jnp.sum(dy * x, axis=0, keepdims=True), reduce)
        dx = d_x * dy

        dy0 = e_x * dy
        dyb = dy.astype(BF16)
        dcs = jnp.zeros((q, LANES), F32)
        dcs_rows = jnp.zeros((q, LANES), F32)
        ddt = jnp.zeros((q, LANES), F32)
        ds_prev_parts, z_parts, db_parts, dc_parts, dxd_parts, y0_parts = [], [], [], [], [], []
        for g in range(2):
            bg = act[:, SSD_W + SSD_STATE * g:SSD_W + SSD_STATE * (g + 1)].astype(BF16)
            cg = act[:, SSD_W + 2 * SSD_STATE + SSD_STATE * g:SSD_W + 2 * SSD_STATE + SSD_STATE * (g + 1)].astype(BF16)
            gsl = slice(256 * g, 256 * (g + 1))
            spg = s_prev[:, gsl].astype(BF16)
            dsg = d_s[:, gsl].astype(BF16)
            dy0g = dy0[:, gsl].astype(BF16)
            gmat = _dot_nt(cg, bg)
            y0_parts.append(_dot(cg, spg))
            dc_g = _dot_nt(dy0g, spg)
            ds_prev_parts.append(_dot_tn(cg, dy0g))
            z_parts.append(_dot(bg, dsg))
            db_g = _dot_nt(xw[:, gsl], dsg)
            dg_acc = jnp.zeros((q, q), F32)
            for pp in range(2):
                pair = 2 * g + pp
                psl = slice(LANES * pair, LANES * (pair + 1))
                dxd_e = []
                for e in range(2):
                    h = 2 * pair + e
                    keep = low if e == 0 else jnp.logical_not(low)
                    lm = _ssd_lmat(cs, cs_t, h)
                    mh = gmat * lm
                    dm = _dot_nt(jnp.where(keep, dy[:, psl], 0.0).astype(BF16), xd[:, psl])
                    dxd_e.append(_dot_tn(mh.astype(BF16), dyb[:, psl]))
                    wm = dm * mh
                    dcs = dcs + jnp.where(lane == h, jnp.sum(wm, axis=1, keepdims=True), 0.0)
                    dcs_rows = dcs_rows - jnp.where(sub == h, jnp.sum(wm, axis=0, keepdims=True), 0.0)
                    dg_acc = dg_acc + dm * lm
                dxd_parts.append(jnp.where(low, dxd_e[0], dxd_e[1]))
            dgb = dg_acc.astype(BF16)
            dc_parts.append(dc_g + _dot(dgb, bg))
            db_parts.append(db_g + _dot_tn(dgb, cg))
        y0 = jnp.concatenate(y0_parts, axis=1)
        zmat = jnp.concatenate(z_parts, axis=1)
        dxd = jnp.concatenate(dxd_parts, axis=1)
        ds_prev = jnp.concatenate(ds_prev_parts, axis=1) + cd_x * d_s
        ds_ref[...] = ds_prev

        dcs = dcs + _dot_hi(dy * y0, reduce) * e_col
        dcd = _dot_hi(jnp.sum(d_s * s_prev, axis=0, keepdims=True), reduce)
        dlast = dcd * cd
        dx = dx + w_x * zmat + dxd * dt_x
        dw = _dot_hi(zmat * x, reduce)
        ddt = ddt + dw * decay_end + _dot_hi(dxd * x, reduce)
        dwl = dw * w
        dcs = dcs - dwl
        dlast = dlast + jnp.sum(dwl, axis=0, keepdims=True)
        dcs = dcs + dcs_rows.T + jnp.where(sub == q - 1, dlast, 0.0)
        dda = _dot_hi(tril.T, dcs)
        ddt = ddt + dda * a_row
        a128_ref[1:2, :] += jnp.sum(dda * dt, axis=0, keepdims=True) * a_row
        draw = jnp.where(lane < N_HEADS, ddt * _sigmoid(dtr + bias_ref[...]), 0.0)
        a128_ref[0:1, :] += jnp.sum(draw, axis=0, keepdims=True)
        ddt_ref[...] = draw
        dact = jnp.concatenate([dx] + db_parts + dc_parts, axis=1)
        dpre_ref[...] = dact * _dsilu(pre_v)

    return _rowwise("ssd_bwd", body, t, q,
                    [('row', pre), ('col', rest, SSD_W, 2), ('col', rest, LANES, 20), ('row', y),
                     ('row', s_prev_all), ('col', d_mix, SSD_W, 0), ('full', dt_bias), ('full', a_log),
                     ('full', d_skip), ('full', norm_w)],
                    [('row', SSD_CONV, F32), ('row', SSD_W, F32), ('row', LANES, F32),
                     ('acc', (8, LANES), F32), ('acc', (8, SSD_W), F32)],
                    scratch=[pltpu.VMEM((SSD_STATE, SSD_W), F32)], reverse=True)


LRU_TM = 256


def _lru_gates(xc, wa, ba, wx, bx, lam):
    xb = xc.astype(BF16)
    r = _sigmoid(_dot(xb, wa) + ba)
    i = _sigmoid(_dot(xb, wx) + bx)
    sp = _softplus(-lam)
    a = jnp.exp(-LRU_C * r * sp)
    mult = jnp.sqrt(1.0 - a * a)
    return r, i, sp, a, mult


def _lru_fwd(xc, rest, wa, ba, wx, bx, lam):
    def body(i, nt, ins, outs, scr):
        xc_ref, g_ref, wa_ref, ba_ref, wx_ref, bx_ref, lam_ref = ins
        carry = scr[0]

        @pl.when(i == 0)
        def _():
            carry[...] = jnp.zeros(carry.shape, F32)

        xv = xc_ref[...]
        r, ig, sp, a, mult = _lru_gates(xv, wa_ref[...], ba_ref[...], wx_ref[...], bx_ref[...], lam_ref[...])
        u = mult * (ig * xv)
        row = _iota(a.shape, 0)
        s = 1
        while s < LRU_TM:
            a_sh = jnp.where(row >= s, pltpu.roll(a, s, 0), 1.0)
            u_sh = jnp.where(row >= s, pltpu.roll(u, s, 0), 0.0)
            u = a * u_sh + u
            a = a * a_sh
            s *= 2
        h = u + a * carry[0:1, :]
        carry[0:1, :] = h[LRU_TM - 1:LRU_TM, :]
        outs[1][...] = h
        outs[0][...] = h * _gelu(g_ref[...])

    return _rowwise("lru_fwd", body, xc.shape[0], LRU_TM,
                    [('row', xc), ('col', rest, LRU_W, 3), ('full', wa), ('full', ba), ('full', wx),
                     ('full', bx), ('full', lam)],
                    [('row', LRU_W, F32), ('row', LRU_W, F32)], scratch=[pltpu.VMEM((8, LRU_W), F32)])


def _lru_bwd(xc, rest, h, d_mix, wa, ba, wx, bx, lam, wa_t, wx_t):
    def body(i, nt, ins, outs, scr):
        xc_ref, g_ref, h_ref, hp_ref, do_ref, wa_ref, ba_ref, wx_ref, bx_ref, lam_ref, wat_ref, wxt_ref = ins
        dxc_ref, dg_ref, dza_ref, dzi_ref, acc_ref = outs
        carry = scr[0]

        @pl.when(i == nt - 1)
        def _():
            carry[...] = jnp.zeros(carry.shape, F32)

        xv, gv, hv, d_out = xc_ref[...], g_ref[...], h_ref[...], do_ref[...]
        r, ig, sp, a, mult = _lru_gates(xv, wa_ref[...], ba_ref[...], wx_ref[...], bx_ref[...], lam_ref[...])
        dg_ref[...] = d_out * hv * _dgelu(gv)
        gsum = d_out * _gelu(gv)
        row = _iota(a.shape, 0)
        b = jnp.where(row < LRU_TM - 1, pltpu.roll(a, LRU_TM - 1, 0), 1.0)
        s = 1
        while s < LRU_TM:
            keep = row < LRU_TM - s
            b_sh = jnp.where(keep, pltpu.roll(b, LRU_TM - s, 0), 1.0)
            g_sh = jnp.where(keep, pltpu.roll(gsum, LRU_TM - s, 0), 0.0)
            gsum = gsum + b * g_sh
            b = b * b_sh
            s *= 2
        dh = gsum + b * carry[0:1, :]
        carry[0:1, :] = a[0:1, :] * dh[0:1, :]
        h_prev = _shift_down(hv, 1, jnp.where(i > 0, hp_ref[...], 0.0))
        du = dh
        dmult = du * ig * xv
        di = du * mult * xv
        dxc = du * mult * ig
        da = dh * h_prev - dmult * a / mult
        dlog = da * a
        dr = dlog * (-LRU_C) * sp
        acc_ref[2:3, :] += jnp.sum(dlog * (-LRU_C) * r, axis=0, keepdims=True)
        dza = dr * r * (1.0 - r)
        dzi = di * ig * (1.0 - ig)
        acc_ref[0:1, :] += jnp.sum(dza, axis=0, keepdims=True)
        acc_ref[1:2, :] += jnp.sum(dzi, axis=0, keepdims=True)
        dzab, dzib = dza.astype(BF16), dzi.astype(BF16)
        dza_ref[...] = dzab
        dzi_ref[...] = dzib
        dxc_ref[...] = dxc + _dot(dzab, wat_ref[...]) + _dot(dzib, wxt_ref[...])

    return _rowwise("lru_bwd", body, xc.shape[0], LRU_TM,
                    [('row', xc), ('col', rest, LRU_W, 3), ('row', h), ('prev8', h, LRU_W, 0),
                     ('col', d_mix, LRU_W, 1), ('full', wa), ('full', ba), ('full', wx), ('full', bx),
                     ('full', lam), ('full', wa_t), ('full', wx_t)],
                    [('row', LRU_W, F32), ('row', LRU_W, F32), ('row', LRU_W, BF16), ('row', LRU_W, BF16),
                     ('acc', (8, LRU_W), F32)],
                    scratch=[pltpu.VMEM((8, LRU_W), F32)], reverse=True)


FFN_TM = 512
FFN_TN = 1536


def _ffn_up(name, h2, w_gate, w_up):
    t, k = h2.shape

    def kern(a_ref, wg_ref, wu_ref, g_ref, u_ref, act_ref):
        a = a_ref[...].astype(BF16)
        gv = _dot(a, wg_ref[...])
        uv = _dot(a, wu_ref[...])
        g_ref[...] = gv.astype(BF16)
        u_ref[...] = uv.astype(BF16)
        act_ref[...] = (_silu(gv) * uv).astype(BF16)

    tile = pl.BlockSpec((FFN_TM, FFN_TN), lambda i, j: (i, j))
    return pl.pallas_call(
        kern, name=name, grid=(t // FFN_TM, D_FFP // FFN_TN),
        in_specs=[pl.BlockSpec((FFN_TM, k), lambda i, j: (i, 0)),
                  pl.BlockSpec((k, FFN_TN), lambda i, j: (0, j)),
                  pl.BlockSpec((k, FFN_TN), lambda i, j: (0, j))],
        out_specs=[tile, tile, tile],
        out_shape=[jax.ShapeDtypeStruct((t, D_FFP), BF16)] * 3,
        compiler_params=_params(2))(h2, w_gate, w_up)


def _ffn_down_bwd(name, dx, w_down_t, gate, up):
    t, k = dx.shape

    def kern(dx_ref, w_ref, g_ref, u_ref, dg_ref, du_ref):
        da = _dot(dx_ref[...].astype(BF16), w_ref[...])
        gv, uv = g_ref[...].astype(F32), u_ref[...].astype(F32)
        dg_ref[...] = (da * uv * _dsilu(gv)).astype(BF16)
        du_ref[...] = (da * _silu(gv)).astype(BF16)

    tile = pl.BlockSpec((FFN_TM, FFN_TN), lambda i, j: (i, j))
    shp = jax.ShapeDtypeStruct((t, D_FFP), BF16)
    return pl.pallas_call(
        kern, name=name, grid=(t // FFN_TM, D_FFP // FFN_TN),
        in_specs=[pl.BlockSpec((FFN_TM, k), lambda i, j: (i, 0)),
                  pl.BlockSpec((k, FFN_TN), lambda i, j: (0, j)), tile, tile],
        out_specs=[tile, tile], out_shape=[shp, shp], compiler_params=_params(2))(dx, w_down_t, gate, up)


def _layer_fwd(x, w, l):
    tag = "_l%d" % l
    h = _rmsnorm_fwd("norm_mix" + tag, x, w['norm_mix'][l])
    qkv = _mm("proj_qkv" + tag, [h], w['w_qkv'][l], tn=768, out_dtype=BF16)
    rest = _mm("proj_rest" + tag, [h], w['w_rest'][l], tn=896)
    att, lse = _attention_fwd(qkv)
    pre = _conv_fwd("ssd_conv" + tag, rest, SSD_CONV, 0, w['ssd_conv_w'][l], w['ssd_conv_b'][l])
    ssd, y, s_prev = _ssd_fwd(pre, rest, w['ssd_dt_bias'][l], w['ssd_a_log'][l], w['ssd_d'][l], w['ssd_norm'][l])
    xc = _conv_fwd("lru_conv" + tag, rest, LRU_W, 4, w['lru_conv_w'][l], w['lru_conv_b'][l])
    lru, hl = _lru_fwd(xc, rest, w['lru_wa'][l], w['lru_ba'][l], w['lru_wx'][l], w['lru_bx'][l], w['lru_lambda'][l])
    x_mid = _mm("proj_out" + tag, [att, ssd, lru], w['w_out'][l], res=x, tn=512)
    h2 = _rmsnorm_fwd("norm_ffn" + tag, x_mid, w['norm_ffn'][l])
    gate, up, act = _ffn_up("proj_gu" + tag, h2, w['w_gate'][l], w['w_up'][l])
    x_next = _mm("proj_down" + tag, [act], w['w_down'][l], res=x_mid, tn=512)
    saved = dict(x=x, h=h, qkv=qkv, rest=rest, att=att, lse=lse, pre=pre, ssd=ssd, y=y, s_prev=s_prev, xc=xc,
                 lru=lru, hl=hl, x_mid=x_mid, h2=h2, gate=gate, up=up, act=act)
    return x_next, saved


def _layer_bwd(dx_next, sv, w, l):
    tag = "_l%d_b" % l
    t = dx_next.shape[0]
    g = {}
    dgate, dup = _ffn_down_bwd("d_act" + tag, dx_next, w['w_down_t'][l], sv['gate'], sv['up'])
    g['w_down'] = _mm_tn("dw_down" + tag, sv['act'], dx_next, tk=1536)
    dh2 = _mm("d_h2" + tag, [dgate, dup], [w['w_gate_t'][l], w['w_up_t'][l]], tn=512)
    g['w_gate'] = _mm_tn("dw_gate" + tag, sv['h2'], dgate, tn=1536)
    g['w_up'] = _mm_tn("dw_up" + tag, sv['h2'], dup, tn=1536)
    dx_mid, acc = _rmsnorm_bwd("norm_ffn" + tag, dh2, sv['x_mid'], w['norm_ffn'][l], dx_next)
    g['norm_ffn'] = acc[0]
    d_att = _mm("d_att" + tag, [dx_mid], w['w_out_t'][l][:, :ATT_W], tn=512, out_dtype=BF16)
    d_mix = _mm("d_mix" + tag, [dx_mid], w['w_out_t'][l][:, ATT_W:], tn=512)
    g['w_out'] = jnp.concatenate([_mm_tn("dw_out%d" % k + tag, a, dx_mid)
                                  for k, a in enumerate((sv['att'], sv['ssd'], sv['lru']))], axis=0)
    dxc, dgl, dza, dzi, acc = _lru_bwd(sv['xc'], sv['rest'], sv['hl'], d_mix, w['lru_wa'][l], w['lru_ba'][l],
                                       w['lru_wx'][l], w['lru_bx'][l], w['lru_lambda'][l],
                                       w['lru_wa_t'][l], w['lru_wx_t'][l])
    g['lru_ba'], g['lru_bx'] = acc[0], acc[1]
    g['lru_lambda'] = acc[2] * (-_sigmoid(-w['lru_lambda'][l][0]))
    g['lru_wa'] = _diag_blocks(_mm_tn("dw_lru_a" + tag, sv['xc'], dza))
    g['lru_wx'] = _diag_blocks(_mm_tn("dw_lru_x" + tag, sv['xc'], dzi))
    dxl, acc = _conv_bwd("lru_conv" + tag, dxc, sv['rest'], LRU_W, 4, w['lru_conv_w'][l])
    g['lru_conv_w'], g['lru_conv_b'] = acc[:4], acc[4]
    dpre, dz, ddt, a128, a512 = _ssd_bwd(sv['pre'], sv['rest'], sv['y'], sv['s_prev'], d_mix, w['ssd_dt_bias'][l],
                                         w['ssd_a_log'][l], w['ssd_d'][l], w['ssd_norm'][l])
    g['ssd_dt_bias'], g['ssd_a_log'], g['ssd_d'] = a128[0, :N_HEADS], a128[1, :N_HEADS], a128[2, :N_HEADS]
    g['ssd_norm'] = a512[0]
    dxbc, acc = _conv_bwd("ssd_conv" + tag, dpre, sv['rest'], SSD_CONV, 0, w['ssd_conv_w'][l])
    g['ssd_conv_w'], g['ssd_conv_b'] = acc[:4], acc[4]
    dq, dk, dv = _attention_bwd(sv['qkv'], d_att, sv['att'], sv['lse'])
    pieces = [dq, dk, dv, dxbc, dz, dgl, dxl, ddt]
    dh = _mm("d_h" + tag, pieces, w['w_in_t'][l], tn=512)
    dws = [_mm_tn("dw_in%d" % k + tag, sv['h'], p) for k, p in enumerate(pieces)]
    g['w_in'] = jnp.concatenate([dws[0], dws[1], dws[2], dws[4], dws[3], _dt_tile_place(dws[7]), dws[5], dws[6]],
                                axis=1)
    dx, acc = _rmsnorm_bwd("norm_mix" + tag, dh, sv['x'], w['norm_mix'][l], dx_mid)
    g['norm_mix'] = acc[0]
    return dx, g


def _diag_blocks(m):
    return jnp.stack([m[64 * n:64 * (n + 1), 64 * n:64 * (n + 1)] for n in range(8)])


def _block_diag(w):
    eye = jnp.eye(8, dtype=w.dtype)
    return (w[:, :, None, :] * eye[:, None, :, None]).reshape(512, 512)


_ANY = pl.BlockSpec(memory_space=pl.ANY)
_MESH = pl.DeviceIdType.MESH


def _all_gather(name, xs):
    n = len(xs)

    def body(*refs):
        x_refs, out_refs = refs[:n], refs[n:2 * n]
        send_sems, recv_sems, local_sems = refs[2 * n:]
        x_, y_, c_ = lax.axis_index("x"), lax.axis_index("y"), lax.axis_index("c")
        me, sibling = (x_, y_, c_), (x_, y_, 1 - c_)
        chips = [(1 - x_, y_), (x_, 1 - y_), (1 - x_, 1 - y_)]

        def slot(a, px, py, pc):
            return out_refs[a].at[4 * px + 2 * py + pc]

        def copy(a, k, block, to, src=None):
            return pltpu.make_async_remote_copy(
                src_ref=slot(a, *block) if src is None else src, dst_ref=slot(a, *block),
                send_sem=send_sems.at[a, k], recv_sem=recv_sems.at[a, k], device_id=to, device_id_type=_MESH)

        mine = [pltpu.make_async_copy(x_refs[a], slot(a, *me), local_sems.at[a]) for a in range(n)]
        for cp in mine:
            cp.start()
        first = []
        for a in range(n):
            first.append(copy(a, 0, me, sibling, src=x_refs[a]))
            first += [copy(a, 1 + j, me, (*chip, c_), src=x_refs[a]) for j, chip in enumerate(chips)]
        for cp in first:
            cp.start()
        passed = []
        for j, chip in enumerate(chips):
            for a in range(n):
                copy(a, 1 + j, (*chip, c_), me).wait_recv()
                fwd = copy(a, 4 + j, (*chip, c_), sibling)
                fwd.start()
                passed.append(fwd)
        for a in range(n):
            copy(a, 0, sibling, me).wait_recv()
            for j, chip in enumerate(chips):
                copy(a, 4 + j, (*chip, 1 - c_), me).wait_recv()
        for cp in first + passed:
            cp.wait_send()
        for cp in mine:
            cp.wait()

    return pl.pallas_call(
        body, name=name, out_shape=[jax.ShapeDtypeStruct((N_DEV,) + x.shape, x.dtype) for x in xs],
        in_specs=[_ANY] * n, out_specs=[_ANY] * n,
        scratch_shapes=[pltpu.SemaphoreType.DMA((n, 7)), pltpu.SemaphoreType.DMA((n, 7)),
                        pltpu.SemaphoreType.DMA((n,))],
    )(*xs)


def _all_to_all(name, xs):
    n = len(xs)

    def body(*refs):
        x_refs, out_refs = refs[:n], refs[n:2 * n]
        send_sems, recv_sems, local_sems = refs[2 * n:]
        x_, y_, c_ = lax.axis_index("x"), lax.axis_index("y"), lax.axis_index("c")
        me = 4 * x_ + 2 * y_ + c_

        def peer(k):
            return ((1 - x_) if k & 4 else x_, (1 - y_) if k & 2 else y_, (1 - c_) if k & 1 else c_)

        def copy(a, k):
            px, py, pc = peer(k)
            return pltpu.make_async_remote_copy(
                src_ref=x_refs[a].at[4 * px + 2 * py + pc], dst_ref=out_refs[a].at[me],
                send_sem=send_sems.at[a, k - 1], recv_sem=recv_sems.at[a, k - 1],
                device_id=(px, py, pc), device_id_type=_MESH)

        def arrival(a, k):
            px, py, pc = peer(k)
            return pltpu.make_async_remote_copy(
                src_ref=x_refs[a].at[me], dst_ref=out_refs[a].at[4 * px + 2 * py + pc],
                send_sem=send_sems.at[a, k - 1], recv_sem=recv_sems.at[a, k - 1],
                device_id=(px, py, pc), device_id_type=_MESH)

        mine = [pltpu.make_async_copy(x_refs[a].at[me], out_refs[a].at[me], local_sems.at[a]) for a in range(n)]
        for cp in mine:
            cp.start()
        copies = [copy(a, k) for a in range(n) for k in range(1, N_DEV)]
        for cp in copies:
            cp.start()
        for a in range(n):
            for k in range(1, N_DEV):
                arrival(a, k).wait_recv()
        for cp in copies:
            cp.wait_send()
        for cp in mine:
            cp.wait()

    return pl.pallas_call(
        body, name=name, out_shape=[jax.ShapeDtypeStruct(x.shape, x.dtype) for x in xs],
        in_specs=[_ANY] * n, out_specs=[_ANY] * n,
        scratch_shapes=[pltpu.SemaphoreType.DMA((n, 7)), pltpu.SemaphoreType.DMA((n, 7)),
                        pltpu.SemaphoreType.DMA((n,))],
    )(*xs)


def _window_offset():
    me = 4 * lax.axis_index("x") + 2 * lax.axis_index("y") + lax.axis_index("c")
    return jnp.where(me < 6, me, me + 120)


def _place_w_in(shard):
    def body(i, nt, ins, outs, scr):
        outs[0][...] = pltpu.roll(ins[0][...], _window_offset(), 1).astype(BF16)

    return _rowwise("place_w_in", body, shard.shape[0], 256, [('row', shard)], [('row', IN_WIN, BF16)])[0]


def _adamw(name, w, m, v, g, tr, from_window=False):
    s_parts, r, c = g.shape
    assert r % tr == 0

    def kern(w_ref, m_ref, v_ref, g_ref, go_ref, d_ref, mo_ref, vo_ref):
        gs = g_ref[0].astype(F32)
        for s in range(1, s_parts):
            gs = gs + g_ref[s].astype(F32)
        if from_window:
            gs = pltpu.roll(gs, c - _window_offset(), 1)
        wv = w_ref[...]
        m2 = ADAM_B1 * m_ref[...] + (1.0 - ADAM_B1) * gs
        v2 = ADAM_B2 * v_ref[...] + (1.0 - ADAM_B2) * (gs * gs)
        m_hat = m2 / (1.0 - ADAM_B1 ** ADAM_STEP)
        v_hat = v2 / (1.0 - ADAM_B2 ** ADAM_STEP)
        go_ref[...] = gs
        d_ref[...] = -ADAM_LR * (m_hat / (jnp.sqrt(v_hat) + ADAM_EPS) + ADAM_WD * wv)
        mo_ref[...] = m2
        vo_ref[...] = v2

    spec = pl.BlockSpec((tr, c), lambda i: (i, 0))
    shp = jax.ShapeDtypeStruct((r, c), F32)
    return pl.pallas_call(kern, name=name, grid=(r // tr,),
                          in_specs=[spec, spec, spec, pl.BlockSpec((s_parts, tr, c), lambda i: (0, i, 0))],
                          out_specs=[spec] * 4, out_shape=[shp] * 4, compiler_params=_params(1))(w, m, v, g)


def _pack(arrs, rows, lead=0):
    parts = []
    for a in arrs:
        flat = a.reshape(a.shape[:lead] + (-1,))
        pad = (-flat.shape[-1]) % LANES
        if pad:
            flat = jnp.pad(flat, [(0, 0)] * lead + [(0, pad)])
        parts.append(flat)
    flat = jnp.concatenate(parts, axis=-1)
    pad = rows * LANES - flat.shape[-1]
    assert pad >= 0
    if pad:
        flat = jnp.pad(flat, [(0, 0)] * lead + [(0, pad)])
    return flat.reshape(flat.shape[:lead] + (rows, LANES))


def _unpack(buf, shapes, lead=0):
    flat = buf.reshape(buf.shape[:lead] + (-1,))
    out, off = [], 0
    for shp in shapes:
        n = math.prod(shp)
        out.append(flat[..., off:off + n].reshape(buf.shape[:lead] + tuple(shp)))
        off += n + ((-n) % LANES)
    return out


BIG = ('w_in', 'w_out', 'w_gate', 'w_up', 'w_down')
CONV =('ssd_conv_w', 'lru_conv_w')
CONV_SHARD_SHAPES = ((DEPTH, 4, SSD_CONV // N_DEV), (DEPTH, 4, LRU_W // N_DEV))
CONV_ROWS = 16
SMALL = ('norm_mix', 'ssd_conv_b', 'ssd_dt_bias', 'ssd_a_log', 'ssd_d', 'ssd_norm', 'lru_conv_b', 'lru_wa',
         'lru_ba', 'lru_wx', 'lru_bx', 'lru_lambda', 'norm_ffn', 'norm_final')
SMALL_ROWS = 1280
SMALL_TILE = 256
WEIGHTS = ('norm_mix', 'w_in', 'ssd_conv_w', 'ssd_conv_b', 'ssd_dt_bias', 'ssd_a_log', 'ssd_d', 'ssd_norm',
           'lru_conv_w', 'lru_conv_b', 'lru_wa', 'lru_ba', 'lru_wx', 'lru_bx', 'lru_lambda', 'w_out', 'norm_ffn',
           'w_gate', 'w_up', 'w_down', 'norm_final')


def _join_cols(a):
    return jnp.transpose(a, (1, 2, 0, 3)).reshape(a.shape[1], a.shape[2], -1)


def _join_rows(a):
    return jnp.transpose(a, (1, 0, 2, 3)).reshape(a.shape[1], -1, a.shape[3])


def _split_cols(a):
    l, r, c = a.shape
    return jnp.transpose(a.reshape(l, r, N_DEV, c // N_DEV), (2, 0, 1, 3)).reshape(N_DEV, l * r, c // N_DEV)


def _split_rows(a):
    l, r, c = a.shape
    return jnp.transpose(a.reshape(l, N_DEV, r // N_DEV, c), (1, 0, 2, 3)).reshape(N_DEV, l * r // N_DEV, c)


def _shard_form(k, a):
    if k == 'w_in':
        return jnp.pad(a.reshape(-1, IN_SHARD), ((0, 0), (0, IN_WIN - IN_SHARD)))
    if k in ('w_gate', 'w_up'):
        return jnp.pad(a.reshape(-1, FF_SHARD), ((0, 0), (0, FF_SHARD_P - FF_SHARD)))
    if k == 'w_down':
        return jnp.pad(a, ((0, 0), (0, FF_SHARD_P - FF_SHARD), (0, 0))).reshape(-1, D_MODEL)
    return a.reshape(-1, D_MODEL)


def _shard_back(k, a):
    if k == 'w_in':
        return a[:, :IN_SHARD].reshape(DEPTH, D_MODEL, IN_SHARD)
    if k in ('w_gate', 'w_up'):
        return a[:, :FF_SHARD].reshape(DEPTH, D_MODEL, FF_SHARD)
    if k == 'w_down':
        return a.reshape(DEPTH, FF_SHARD_P, D_MODEL)[:, :FF_SHARD]
    return a.reshape(DEPTH, D_MIX // N_DEV, D_MODEL)


def _dt_tile_place(a):
    zeros = jnp.zeros(a.shape[:-1] + (LANES - N_HEADS,), a.dtype)
    return jnp.concatenate([a[..., :6], zeros, a[..., 6:8]], axis=-1)


def _dt_tile_heads(tile):
    zeros = jnp.zeros(tile.shape[:-1] + (LANES - N_HEADS,), tile.dtype)
    return jnp.concatenate([tile[..., :6], tile[..., 126:128], zeros], axis=-1)


def _layout_from_windows(win):
    r = win.shape[1]
    main = jnp.concatenate([win[j][:, :512] for j in range(N_DEV)] + [jnp.zeros((r, LANES), win.dtype)], axis=1)
    gap = jnp.zeros((r, 384), win.dtype)
    tails = [jnp.zeros((r, 512), win.dtype)]
    for j in range(N_DEV - 1):
        tails += [win[j][:, 512:], gap]
    tails.append(win[N_DEV - 1][:, 512:])
    return main + jnp.concatenate(tails, axis=1)


def _prepare_weights(p, full):
    w = {}
    w_in = full['w_in']
    w_qkv = w_in[:, :, :D_MIX]
    dt_cols = _dt_tile_heads(w_in[:, :, 3072:3200])
    w_rest = jnp.concatenate([w_in[:, :, 2048:3072], w_in[:, :, 1536:2048], w_in[:, :, 3200:3712],
                              w_in[:, :, 3712:4224], dt_cols], axis=2)
    w['w_qkv'], w['w_rest'] = w_qkv, w_rest
    w['w_in_t'] = jnp.transpose(jnp.concatenate([w_qkv, w_rest], axis=2), (0, 2, 1))
    w['w_out'] = full['w_out']
    w['w_out_t'] = jnp.transpose(full['w_out'], (0, 2, 1))
    for k in ('w_gate', 'w_up'):
        w[k] = full[k]
        w[k + '_t'] = jnp.transpose(full[k], (0, 2, 1))
    w['w_down'] = full['w_down']
    w['w_down_t'] = jnp.transpose(full['w_down'], (0, 2, 1))
    for k in ('norm_mix', 'ssd_conv_b', 'ssd_norm', 'lru_conv_b', 'lru_ba', 'lru_bx', 'lru_lambda', 'norm_ffn'):
        w[k] = p[k][:, None, :]
    for k in ('ssd_dt_bias', 'ssd_a_log', 'ssd_d'):
        w[k] = jnp.pad(p[k], ((0, 0), (0, LANES - N_HEADS)))[:, None, :]
    for k in CONV:
        w[k] = jnp.pad(full[k], ((0, 0), (0, 4), (0, 0)))
    for k in ('lru_wa', 'lru_wx'):
        bd = jnp.stack([_block_diag(p[k][l]) for l in range(DEPTH)]).astype(BF16)
        w[k] = bd
        w[k + '_t'] = jnp.transpose(bd, (0, 2, 1))
    return w


def _local_step(x, target, w, norm_final):
    saved = []
    for l in range(DEPTH):
        x, sv = _layer_fwd(x, w, l)
        saved.append(sv)
    dx, loss_acc, dgf = _final_loss(x, norm_final[None, :], target)
    grads = [None] * DEPTH
    for l in reversed(range(DEPTH)):
        dx, grads[l] = _layer_bwd(dx, saved[l], w, l)
    g = {k: jnp.stack([grads[l][k] for l in range(DEPTH)]) for k in grads[0]}
    g['norm_final'] = dgf[0]
    return loss_acc[0, 0], dx, g


def kernel(x, norm_mix, w_in, ssd_conv_w, ssd_conv_b, ssd_dt_bias, ssd_a_log, ssd_d, ssd_norm, lru_conv_w, lru_conv_b, lru_wa, lru_ba, lru_wx, lru_bx, lru_lambda, w_out, norm_ffn, w_gate, w_up, w_down, norm_final, loss_target, m_norm_mix, m_w_in, m_ssd_conv_w, m_ssd_conv_b, m_ssd_dt_bias, m_ssd_a_log, m_ssd_d, m_ssd_norm, m_lru_conv_w, m_lru_conv_b, m_lru_wa, m_lru_ba, m_lru_wx, m_lru_bx, m_lru_lambda, m_w_out, m_norm_ffn, m_w_gate, m_w_up, m_w_down, m_norm_final, v_norm_mix, v_w_in, v_ssd_conv_w, v_ssd_conv_b, v_ssd_dt_bias, v_ssd_a_log, v_ssd_d, v_ssd_norm, v_lru_conv_w, v_lru_conv_b, v_lru_wa, v_lru_ba, v_lru_wx, v_lru_bx, v_lru_lambda, v_w_out, v_norm_ffn, v_w_gate, v_w_up, v_w_down, v_norm_final):
    args = (norm_mix, w_in, ssd_conv_w, ssd_conv_b, ssd_dt_bias, ssd_a_log, ssd_d, ssd_norm, lru_conv_w, lru_conv_b, lru_wa, lru_ba, lru_wx, lru_bx, lru_lambda, w_out, norm_ffn, w_gate, w_up, w_down, norm_final)
    margs = (m_norm_mix, m_w_in, m_ssd_conv_w, m_ssd_conv_b, m_ssd_dt_bias, m_ssd_a_log, m_ssd_d, m_ssd_norm, m_lru_conv_w, m_lru_conv_b, m_lru_wa, m_lru_ba, m_lru_wx, m_lru_bx, m_lru_lambda, m_w_out, m_norm_ffn, m_w_gate, m_w_up, m_w_down, m_norm_final)
    vargs = (v_norm_mix, v_w_in, v_ssd_conv_w, v_ssd_conv_b, v_ssd_dt_bias, v_ssd_a_log, v_ssd_d, v_ssd_norm, v_lru_conv_w, v_lru_conv_b, v_lru_wa, v_lru_ba, v_lru_wx, v_lru_bx, v_lru_lambda, v_w_out, v_norm_ffn, v_w_gate, v_w_up, v_w_down, v_norm_final)
    p = dict(zip(WEIGHTS, args))
    pm = dict(zip(WEIGHTS, margs))
    pv = dict(zip(WEIGHTS, vargs))

    forms = {k: _shard_form(k, p[k]) for k in BIG}
    send = [_place_w_in(forms['w_in'])] + [forms[k].astype(BF16) for k in BIG[1:]]
    got = _all_gather("gather_weights", send + [_pack([p[k] for k in CONV], CONV_ROWS)])
    full = {'w_in': _layout_from_windows(got[0]).reshape(DEPTH, D_MODEL, IN_COLS_P),
            'w_out': _join_rows(got[1].reshape(N_DEV, DEPTH, D_MIX // N_DEV, D_MODEL)),
            'w_gate': _join_cols(got[2].reshape(N_DEV, DEPTH, D_MODEL, FF_SHARD_P)),
            'w_up': _join_cols(got[3].reshape(N_DEV, DEPTH, D_MODEL, FF_SHARD_P)),
            'w_down': _join_rows(got[4].reshape(N_DEV, DEPTH, FF_SHARD_P, D_MODEL))}
    for k, a in zip(CONV, _unpack(got[5], CONV_SHARD_SHAPES, lead=1)):
        full[k] = _join_cols(a)
    w = _prepare_weights(p, full)

    loss_local, dx, g = _local_step(x[0], loss_target[0], w, norm_final)
    loss = lax.psum(loss_local, ("x", "y", "c"))

    small_g = _all_gather("gather_small_grads", [_pack([g[k] for k in SMALL + CONV], SMALL_ROWS)])[0]
    zeros = [jnp.zeros_like(g[k]) for k in CONV]
    res_small = _adamw("adamw_small", _pack([p[k] for k in SMALL] + zeros, SMALL_ROWS),
                       _pack([pm[k] for k in SMALL] + zeros, SMALL_ROWS),
                       _pack([pv[k] for k in SMALL] + zeros, SMALL_ROWS), small_g, SMALL_TILE)
    small_shapes = [g[k].shape for k in SMALL + CONV]
    out = {kind: {} for kind in range(4)}
    for kind in range(4):
        for k, a in zip(SMALL + CONV, _unpack(res_small[kind], small_shapes)):
            out[kind][k] = a
    me = 4 * lax.axis_index("x") + 2 * lax.axis_index("y") + lax.axis_index("c")
    conv_g = []
    for k, shp in zip(CONV, CONV_SHARD_SHAPES):
        conv_g.append(lax.dynamic_slice_in_dim(out[0][k], me * shp[2], shp[2], axis=2))
    res_conv = _adamw("adamw_conv", _pack([p[k] for k in CONV], CONV_ROWS), _pack([pm[k] for k in CONV], CONV_ROWS),
                      _pack([pv[k] for k in CONV], CONV_ROWS), _pack(conv_g, CONV_ROWS)[None], CONV_ROWS)
    for kind in range(4):
        for k, a in zip(CONV, _unpack(res_conv[kind], CONV_SHARD_SHAPES)):
            out[kind][k] = a

    g_in = g['w_in'].reshape(DEPTH * D_MODEL, IN_COLS_P)
    dest = [jnp.stack([g_in[:, 512 * j:512 * j + IN_WIN] for j in range(N_DEV)]), _split_rows(g['w_out']),
            _split_cols(g['w_gate']), _split_cols(g['w_up']), _split_rows(g['w_down'])]
    parts = _all_to_all("exchange_big_grads", [a.astype(BF16) for a in dest])
    tiles = {'w_in': 256, 'w_out': 128, 'w_gate': 512, 'w_up': 512, 'w_down': 256}
    for k, part in zip(BIG, parts):
        res = _adamw("adamw_" + k, forms[k], _shard_form(k, pm[k]), _shard_form(k, pv[k]), part, tiles[k],
                     from_window=(k == 'w_in'))
        for kind in range(4):
            out[kind][k] = _shard_back(k, res[kind])

    outs = [loss, dx[None]]
    for kind in range(4):
        outs += [out[kind][k] for k in WEIGHTS]
    return tuple(outs)
```
